```python
import jax, jax.numpy as jnp
from jax import lax
import numpy as np

D_MODEL = 1024
BATCH = 8
SEQ = 4096
DEPTH = 4
DEC_BATCH = 2
DEC_SEQ = 8192
PAST_LEN = 128

GRID_W = 64
Q_BLOCK = 128
ROPE_THETA = 10000.0
EPS = 1e-6

A_HEADS = 8
A_KV_HEADS = 2
A_HEAD_DIM = 64
B_HEADS = 8
B_Q_RANK = 384
B_KV_RANK = 256
B_NOPE = 64
B_ROPE = 32
B_V = 64
N_EXPERTS = 16
N_GROUPS = 4
EXPERTS_PER_GROUP = N_EXPERTS // N_GROUPS
GROUP_SCORE_K = 2
TOP_K = 2
D_EXPERT = 512

A_Q_COLS = A_HEADS * A_HEAD_DIM
A_KV_COLS = A_KV_HEADS * A_HEAD_DIM
B_QK_DIM = B_NOPE + B_ROPE
B_O_COLS = B_HEADS * B_V
IN_SPLITS = (A_Q_COLS, A_KV_COLS, A_KV_COLS, B_Q_RANK, B_KV_RANK, B_ROPE, D_MODEL, D_MODEL)
IN_COLS = A_Q_COLS + 2 * A_KV_COLS + B_Q_RANK + B_KV_RANK + B_ROPE + 2 * D_MODEL

kernel_name = "hybrid_gqa_mla_grouped_moe_encoder"


def rms_norm(x, g):
    xf = x.astype(jnp.float32)
    y = xf * lax.rsqrt(jnp.mean(xf * xf, axis=-1, keepdims=True) + EPS)
    return y.astype(x.dtype) * g


def split_cols(u, sizes):
    outs = []
    start = 0
    for s in sizes:
        outs.append(u[..., start:start + s])
        start += s
    return outs


def axial_rope_tables(seq_len, rot_dim):
    rows = seq_len // GRID_W
    row = jnp.broadcast_to(jnp.arange(rows)[:, None], (rows, GRID_W)).reshape(-1).astype(jnp.float32)
    col = jnp.broadcast_to(jnp.arange(GRID_W)[None, :], (rows, GRID_W)).reshape(-1).astype(jnp.float32)
    axis_dim = rot_dim // 2
    inv_freq = jnp.power(jnp.float32(ROPE_THETA), -jnp.arange(0, axis_dim, 2, dtype=jnp.float32) / axis_dim)
    ang = jnp.concatenate([row[:, None] * inv_freq[None, :], col[:, None] * inv_freq[None, :]], axis=-1)
    return jnp.cos(ang), jnp.sin(ang)


def apply_rope(x, cos, sin):
    half = x.shape[-1] // 2
    x1 = x[..., :half].astype(jnp.float32)
    x2 = x[..., half:].astype(jnp.float32)
    c = cos[None, :, None, :]
    s = sin[None, :, None, :]
    return jnp.concatenate([x1 * c - x2 * s, x1 * s + x2 * c], axis=-1).astype(x.dtype)


def block_attention(q, k, v, scale):
    b, s, h, dk = q.shape
    hk = k.shape[2]
    g = h // hk
    dv = v.shape[-1]
    nb = s // Q_BLOCK
    qb = q.reshape(b, nb, Q_BLOCK, hk, g, dk).transpose(1, 0, 2, 3, 4, 5)

    def one_block(q_blk):
        sc = jnp.einsum('bqhgd,bkhd->bhgqk', q_blk, k).astype(jnp.float32) * scale
        p = jax.nn.softmax(sc, axis=-1)
        return jnp.einsum('bhgqk,bkhd->bqhgd', p.astype(v.dtype), v)

    o = lax.map(one_block, qb)
    return o.transpose(1, 0, 2, 3, 4, 5).reshape(b, s, h, dv)


def mixer(h, rope_a, rope_b, w_in, a_q_norm, a_k_norm, b_q_norm, b_kv_norm,
          w_q_up, w_kv_up, w_branch_a, w_branch_b, w_out):
    b, s, _ = h.shape
    u = h @ w_in
    qa, ka, va, cq, ckv, kr, ga, gb = split_cols(u, IN_SPLITS)
    cos_a, sin_a = rope_a
    cos_b, sin_b = rope_b

    qa = apply_rope(rms_norm(qa.reshape(b, s, A_HEADS, A_HEAD_DIM), a_q_norm), cos_a, sin_a)
    ka = apply_rope(rms_norm(ka.reshape(b, s, A_KV_HEADS, A_HEAD_DIM), a_k_norm), cos_a, sin_a)
    va = va.reshape(b, s, A_KV_HEADS, A_HEAD_DIM)
    oa = block_attention(qa, ka, va, A_HEAD_DIM ** -0.5).reshape(b, s, A_Q_COLS)

    qb = (rms_norm(cq, b_q_norm) @ w_q_up).reshape(b, s, B_HEADS, B_QK_DIM)
    q_nope = qb[..., :B_NOPE]
    q_rope = apply_rope(qb[..., B_NOPE:], cos_b, sin_b)
    kv = (rms_norm(ckv, b_kv_norm) @ w_kv_up).reshape(b, s, B_HEADS, B_NOPE + B_V)
    k_nope = kv[..., :B_NOPE]
    vb = kv[..., B_NOPE:]
    k_rope = apply_rope(kr[:, :, None, :], cos_b, sin_b)
    qb = jnp.concatenate([q_nope, q_rope], axis=-1)
    kb = jnp.concatenate([k_nope, jnp.broadcast_to(k_rope, (b, s, B_HEADS, B_ROPE))], axis=-1)
    ob = block_attention(qb, kb, vb, B_QK_DIM ** -0.5).reshape(b, s, B_O_COLS)

    merged = jax.nn.sigmoid(ga) * (oa @ w_branch_a) + jax.nn.sigmoid(gb) * (ob @ w_branch_b)
    return merged @ w_out


def moe(h, w_router, b_router, w_gate, w_up, w_down):
    b, s, d = h.shape
    t = h.reshape(-1, d)
    scores = jax.nn.sigmoid((t @ w_router).astype(jnp.float32))
    biased = (scores + b_router.astype(jnp.float32)).reshape(-1, N_GROUPS, EXPERTS_PER_GROUP)
    group_score = lax.top_k(biased, GROUP_SCORE_K)[0].sum(axis=-1)
    grp = jnp.argmax(group_score, axis=-1)
    in_group = jnp.take_along_axis(biased, grp[:, None, None], axis=1)[:, 0]
    _, local = lax.top_k(in_group, TOP_K)
    idx = grp[:, None] * EXPERTS_PER_GROUP + local
    wts = jnp.take_along_axis(scores, idx, axis=-1)
    wts = wts / jnp.sum(wts, axis=-1, keepdims=True)
    gate = jnp.einsum('tk,tke->te', wts, jax.nn.one_hot(idx, N_EXPERTS, dtype=jnp.float32)).astype(h.dtype)
    y = jnp.zeros_like(t)
    for e in range(N_EXPERTS):
        a = jax.nn.silu(t @ w_gate[e]) * (t @ w_up[e])
        y = y + gate[:, e:e + 1] * (a @ w_down[e])
    return y.reshape(b, s, d)


def trunk(x, norm_mix, w_in, a_q_norm, a_k_norm, b_q_norm, b_kv_norm, w_q_up, w_kv_up,
          w_branch_a, w_branch_b, w_out, norm_ffn, w_router, b_router, w_gate, w_up, w_down, norm_final):
    seq_len = x.shape[1]
    rope_a = axial_rope_tables(seq_len, A_HEAD_DIM)
    rope_b = axial_rope_tables(seq_len, B_ROPE)
    for l in range(DEPTH):
        x = x + mixer(rms_norm(x, norm_mix[l]), rope_a, rope_b, w_in[l], a_q_norm[l], a_k_norm[l],
                      b_q_norm[l], b_kv_norm[l], w_q_up[l], w_kv_up[l], w_branch_a[l], w_branch_b[l], w_out[l])
        x = x + moe(rms_norm(x, norm_ffn[l]), w_router, b_router, w_gate[l], w_up[l], w_down[l])
    return rms_norm(x, norm_final)


def setup_inputs(seed: int = 0) -> dict:
    key = jax.random.key(seed)
    ks = jax.random.split(key, 24)
    f32 = jnp.float32
    out_scale = (2 * DEPTH) ** -0.5

    def w(k, shape, fan_in, extra=1.0):
        return jax.random.normal(k, shape, f32) * (fan_in ** -0.5) * extra

    def gain(k, shape):
        return 1.0 + 0.02 * jax.random.normal(k, shape, f32)

    return {
        "x_prompt": jax.random.normal(ks[0], (BATCH, SEQ, D_MODEL), f32),
        "x_sample": jax.random.normal(ks[1], (DEC_BATCH, DEC_SEQ, D_MODEL), f32),
        "norm_mix": gain(ks[2], (DEPTH, D_MODEL)),
        "w_in": w(ks[3], (DEPTH, D_MODEL, IN_COLS), D_MODEL),
        "a_q_norm": gain(ks[4], (DEPTH, A_HEAD_DIM)),
        "a_k_norm": gain(ks[5], (DEPTH, A_HEAD_DIM)),
        "b_q_norm": gain(ks[6], (DEPTH, B_Q_RANK)),
        "b_kv_norm": gain(ks[7], (DEPTH, B_KV_RANK)),
        "w_q_up": w(ks[8], (DEPTH, B_Q_RANK, B_HEADS * B_QK_DIM), B_Q_RANK),
        "w_kv_up": w(ks[9], (DEPTH, B_KV_RANK, B_HEADS * (B_NOPE + B_V)), B_KV_RANK),
        "w_branch_a": w(ks[10], (DEPTH, A_Q_COLS, D_MODEL), A_Q_COLS),
        "w_branch_b": w(ks[11], (DEPTH, B_O_COLS, D_MODEL), B_O_COLS),
        "w_out": w(ks[12], (DEPTH, D_MODEL, D_MODEL), D_MODEL, out_scale),
        "norm_ffn": gain(ks[13], (DEPTH, D_MODEL)),
        "w_router": w(ks[14], (D_MODEL, N_EXPERTS), D_MODEL),
        "b_router": 0.01 * jax.random.normal(ks[15], (N_EXPERTS,), f32),
        "w_gate": w(ks[16], (DEPTH, N_EXPERTS, D_MODEL, D_EXPERT), D_MODEL),
        "w_up": w(ks[17], (DEPTH, N_EXPERTS, D_MODEL, D_EXPERT), D_MODEL),
        "w_down": w(ks[18], (DEPTH, N_EXPERTS, D_EXPERT, D_MODEL), D_EXPERT, out_scale),
        "norm_final": gain(ks[19], (D_MODEL,)),
    }


def reference(x_prompt, x_sample, norm_mix, w_in, a_q_norm, a_k_norm, b_q_norm, b_kv_norm,
              w_q_up, w_kv_up, w_branch_a, w_branch_b, w_out, norm_ffn, w_router, b_router,
              w_gate, w_up, w_down, norm_final):
    y_prompt = trunk(x_prompt, norm_mix, w_in, a_q_norm, a_k_norm, b_q_norm, b_kv_norm, w_q_up, w_kv_up,
                     w_branch_a, w_branch_b, w_out, norm_ffn, w_router, b_router, w_gate, w_up, w_down, norm_final)
    y_sample = trunk(x_sample, norm_mix, w_in, a_q_norm, a_k_norm, b_q_norm, b_kv_norm, w_q_up, w_kv_up,
                     w_branch_a, w_branch_b, w_out, norm_ffn, w_router, b_router, w_gate, w_up, w_down, norm_final)
    return (y_prompt, y_sample)
```

```python
import functools

import jax
import jax.numpy as jnp
from jax import lax
from jax.experimental import pallas as pl
from jax.experimental.pallas import tpu as pltpu

F32 = jnp.float32
BF16 = jnp.bfloat16

D_MODEL = 1024
GRID_W = 64
ROPE_THETA = 10000.0
EPS = 1e-6
A_HEADS = 8
A_KV_HEADS = 2
A_HEAD_DIM = 64
B_HEADS = 8
B_Q_RANK = 384
B_KV_RANK = 256
B_NOPE = 64
B_ROPE = 32
B_V = 64
N_EXPERTS = 16
N_GROUPS = 4
EXPERTS_PER_GROUP = N_EXPERTS // N_GROUPS
D_EXPERT = 512
A_Q_COLS = A_HEADS * A_HEAD_DIM
A_KV_COLS = A_KV_HEADS * A_HEAD_DIM
B_QK_DIM = B_NOPE + B_ROPE
B_O_COLS = B_HEADS * B_V

LANES = 128

C_QA = 0
C_KA = C_QA + A_Q_COLS
C_VA = C_KA + A_KV_COLS
C_CQ = C_VA + A_KV_COLS
C_CKV = C_CQ + B_Q_RANK
C_KR = C_CKV + B_KV_RANK
C_GA = C_KR + LANES
C_GB = C_GA + D_MODEL
C_END = C_GB + D_MODEL
B_PAD_COLS = B_HEADS * LANES

A_HEAD_ORDER = (0, 4, 1, 5, 2, 6, 3, 7)

VMEM_LIMIT = 56 * 1024 * 1024


def _cparams(sem):
    return pltpu.CompilerParams(dimension_semantics=sem, vmem_limit_bytes=VMEM_LIMIT)


def _dot(a, b):
    return jnp.dot(a, b, preferred_element_type=F32)


def _dot_nt(a, b):
    return lax.dot_general(a, b, (((1,), (1,)), ((), ())), preferred_element_type=F32)


def _rms(x):
    return x * lax.rsqrt(jnp.mean(x * x, axis=-1, keepdims=True) + EPS)


def _split_bf16(x):
    hi = x.astype(BF16)
    lo = (x - hi.astype(F32)).astype(BF16)
    return hi, lo


def _group_mean_sq(v, bd):
    hi, lo = _split_bf16(v * v)
    return (_dot(hi, bd) + _dot(lo, bd)) * (1.0 / A_HEAD_DIM)


def _rope_a(v, cos, sin):
    n = v.shape[-1]
    lane = lax.broadcasted_iota(jnp.int32, v.shape, 1)
    low = (lane % A_HEAD_DIM) < (A_HEAD_DIM // 2)
    swapped = jnp.where(low, pltpu.roll(v, n - A_HEAD_DIM // 2, 1), pltpu.roll(v, A_HEAD_DIM // 2, 1))
    return v * cos + swapped * sin


def _rope_b(v, cos, sin):
    n = v.shape[-1]
    lane = lax.broadcasted_iota(jnp.int32, v.shape, 1)
    low = (lane % LANES) < (B_NOPE + B_ROPE // 2)
    swapped = jnp.where(low, pltpu.roll(v, n - B_ROPE // 2, 1), pltpu.roll(v, B_ROPE // 2, 1))
    return v * cos + swapped * sin


def _mixer_in_body(x_ref, gmix_ref, win_ref, ca_ref, sa_ref, cb_ref, sb_ref, gq_ref, gk_ref, gbq_ref, gbkv_ref,
                   wq_ref, wk_ref, wv_ref, bd_ref,
                   qa_ref, ka_ref, va_ref, qb_ref, kb_ref, vb_ref, ga_ref, gb_ref):
    hb = (_rms(x_ref[...]) * gmix_ref[...]).astype(BF16)
    u = _dot(hb, win_ref[:, C_QA:C_GA])
    bd = bd_ref[...]
    ca, sa = ca_ref[...], sa_ref[...]
    cb, sb = cb_ref[...], sb_ref[...]

    qa = u[:, C_QA:C_KA]
    qa = qa * lax.rsqrt(_group_mean_sq(qa, bd) + EPS) * gq_ref[...]
    qa_ref[...] = _rope_a(qa, jnp.concatenate([ca] * (A_Q_COLS // LANES), axis=1),
                          jnp.concatenate([sa] * (A_Q_COLS // LANES), axis=1)).astype(BF16)
    ka = u[:, C_KA:C_VA]
    ka = ka * lax.rsqrt(_group_mean_sq(ka, bd[:A_KV_COLS, :A_KV_COLS]) + EPS) * gk_ref[...]
    ka_ref[...] = _rope_a(ka, ca, sa).astype(BF16)
    va_ref[...] = u[:, C_VA:C_CQ].astype(BF16)

    cq = (_rms(u[:, C_CQ:C_CKV]) * gbq_ref[...]).astype(BF16)
    qb = _dot(cq, wq_ref[...])
    qb_ref[...] = (_rope_b(qb, jnp.concatenate([cb] * B_HEADS, axis=1), jnp.concatenate([sb] * B_HEADS, axis=1))
                   * (B_QK_DIM ** -0.5)).astype(BF16)
    ckv = (_rms(u[:, C_CKV:C_KR]) * gbkv_ref[...]).astype(BF16)
    kr = _rope_b(u[:, C_KR:C_GA], cb, sb)
    kb_ref[...] = (_dot(ckv, wk_ref[...]) + jnp.concatenate([kr] * B_HEADS, axis=1)).astype(BF16)
    vb_ref[...] = _dot(ckv, wv_ref[...]).astype(BF16)

    g = jax.nn.sigmoid(_dot(hb, win_ref[:, C_GA:C_END]))
    ga_ref[...] = g[:, :D_MODEL].astype(BF16)
    gb_ref[...] = g[:, D_MODEL:].astype(BF16)


def _mixer_in(x, lw, l, tabs, tm):
    t = x.shape[0]
    row = lambda n: pl.BlockSpec((tm, n), lambda i: (i, 0))
    full = lambda a: pl.BlockSpec((None,) + a.shape[1:], lambda i: (l,) + (0,) * (a.ndim - 1))
    const = lambda a: pl.BlockSpec(a.shape, lambda i: (0,) * a.ndim)
    ca, sa, cb, sb = tabs
    ins = [x, lw["gmix"], lw["win"], ca, sa, cb, sb, lw["gq"], lw["gk"], lw["gbq"], lw["gbkv"],
           lw["wq"], lw["wk"], lw["wv"], lw["bd"]]
    in_specs = [row(D_MODEL), full(lw["gmix"]), full(lw["win"]), row(LANES), row(LANES), row(LANES), row(LANES),
                full(lw["gq"]), full(lw["gk"]), full(lw["gbq"]), full(lw["gbkv"]),
                full(lw["wq"]), full(lw["wk"]), full(lw["wv"]), const(lw["bd"])]
    widths = [A_Q_COLS, A_KV_COLS, A_KV_COLS, B_PAD_COLS, B_PAD_COLS, B_O_COLS, D_MODEL, D_MODEL]
    return pl.pallas_call(
        _mixer_in_body,
        grid=(t // tm,),
        in_specs=in_specs,
        out_specs=[row(n) for n in widths],
        out_shape=[jax.ShapeDtypeStruct((t, n), BF16) for n in widths],
        compiler_params=_cparams(("parallel",)),
        name="mixer_in",
    )(*ins)


def _flash_pair(q0, q1, k0_ref, k1_ref, v_ref, tk):
    tq = q0.shape[0]
    nk = v_ref.shape[0] // tk
    q = jnp.concatenate([q0, q1], axis=0)

    def step(j, carry):
        m, l, acc = carry
        off = pl.multiple_of(j * tk, tk)
        if k1_ref is None:
            s = _dot_nt(q, k0_ref[pl.ds(off, tk), :])
        else:
            s = jnp.concatenate([_dot_nt(q0, k0_ref[pl.ds(off, tk), :]), _dot_nt(q1, k1_ref[pl.ds(off, tk), :])], axis=0)
        m_new = jnp.maximum(m, jnp.max(s, axis=-1, keepdims=True))
        alpha = jnp.exp(m - m_new)
        p = jnp.exp(s - m_new)
        l = alpha * l + jnp.sum(p, axis=-1, keepdims=True)
        acc = alpha * acc + _dot(p.astype(BF16), v_ref[pl.ds(off, tk), :])
        return m_new, l, acc

    init = (jnp.full((2 * tq, 1), -jnp.inf, F32), jnp.zeros((2 * tq, 1), F32), jnp.zeros((2 * tq, LANES), F32))
    _, l, acc = lax.fori_loop(0, nk, step, init)
    o = acc / l
    lane = lax.broadcasted_iota(jnp.int32, (tq, LANES), 1)
    return jnp.where(lane < B_V, o[:tq], o[tq:])


def _attn_a_body(q_ref, k_ref, v_ref, o_ref, *, tk):
    q = q_ref[...]
    lane = lax.broadcasted_iota(jnp.int32, q.shape, 1)
    zero = jnp.zeros_like(q)
    q0 = jnp.where(lane < A_HEAD_DIM, q, zero)
    q1 = jnp.where(lane < A_HEAD_DIM, zero, q)
    o_ref[...] = _flash_pair(q0, q1, k_ref, None, v_ref, tk).astype(BF16)


def _attn_b_body(q0_ref, q1_ref, k0_ref, k1_ref, v_ref, o_ref, *, tk):
    o_ref[...] = _flash_pair(q0_ref[...], q1_ref[...], k0_ref, k1_ref, v_ref, tk).astype(BF16)


def _attention_a(qa, ka, va, nseq, s, tq, tk):
    nq = s // tq
    qspec = pl.BlockSpec((tq, LANES), lambda b, h, i: (b * nq + i, h))
    kvspec = pl.BlockSpec((s, LANES), lambda b, h, i: (b, 0))
    return pl.pallas_call(
        functools.partial(_attn_a_body, tk=tk),
        grid=(nseq, A_Q_COLS // LANES, nq),
        in_specs=[qspec, kvspec, kvspec],
        out_specs=qspec,
        out_shape=jax.ShapeDtypeStruct(qa.shape, BF16),
        compiler_params=_cparams(("parallel", "parallel", "parallel")),
        name="attn_a",
    )(qa, ka, va)


def _attention_b(qb, kb, vb, nseq, s, tq, tk):
    nq = s // tq
    q0 = pl.BlockSpec((tq, LANES), lambda b, h, i: (b * nq + i, 2 * h))
    q1 = pl.BlockSpec((tq, LANES), lambda b, h, i: (b * nq + i, 2 * h + 1))
    k0 = pl.BlockSpec((s, LANES), lambda b, h, i: (b, 2 * h))
    k1 = pl.BlockSpec((s, LANES), lambda b, h, i: (b, 2 * h + 1))
    v = pl.BlockSpec((s, LANES), lambda b, h, i: (b, h))
    o = pl.BlockSpec((tq, LANES), lambda b, h, i: (b * nq + i, h))
    return pl.pallas_call(
        functools.partial(_attn_b_body, tk=tk),
        grid=(nseq, B_O_COLS // LANES, nq),
        in_specs=[q0, q1, k0, k1, v],
        out_specs=o,
        out_shape=jax.ShapeDtypeStruct(vb.shape, BF16),
        compiler_params=_cparams(("parallel", "parallel", "parallel")),
        name="attn_b",
    )(qb, qb, kb, kb, vb)


def _within(x, d, period, n):
    row = lax.broadcasted_iota(jnp.int32, x.shape, 0)
    return jnp.where((row % period) + d < period, pltpu.roll(x, n - d, 0), pltpu.roll(x, period - d, 0))


def _route(logits_t, bias):
    n = N_EXPERTS
    scores = jax.nn.sigmoid(logits_t)
    biased = scores + bias
    row = lax.broadcasted_iota(jnp.int32, biased.shape, 0)
    pos = row % EXPERTS_PER_GROUP
    rank = jnp.zeros(biased.shape, jnp.int32)
    for d in range(1, EXPERTS_PER_GROUP):
        other = _within(biased, d, EXPERTS_PER_GROUP, n)
        other_pos = (pos + d) % EXPERTS_PER_GROUP
        ahead = (other > biased) | ((other == biased) & (other_pos < pos))
        rank = rank + ahead.astype(jnp.int32)
    top2 = rank < 2
    kept = jnp.where(top2, biased, 0.0)
    gscore = kept
    for d in range(1, EXPERTS_PER_GROUP):
        gscore = gscore + _within(kept, d, EXPERTS_PER_GROUP, n)
    grp = row // EXPERTS_PER_GROUP
    win = jnp.ones(biased.shape, jnp.bool_)
    for d in range(1, N_GROUPS):
        other = pltpu.roll(gscore, n - d * EXPERTS_PER_GROUP, 0)
        other_grp = (grp + d) % N_GROUPS
        win = win & ((other < gscore) | ((other == gscore) & (other_grp > grp)))
    w = jnp.where(top2 & win, scores, 0.0)
    return w / jnp.sum(w, axis=0, keepdims=True)


def _mixer_out_body(oa_ref, ob_ref, ga_ref, gb_ref, x_ref, wba_ref, wbb_ref, wout_ref, gffn_ref, wrh_ref, wrl_ref,
                    br_ref, h_ref, t_ref, g_ref):
    ma = _dot(oa_ref[...], wba_ref[...])
    mb = _dot(ob_ref[...], wbb_ref[...])
    merged = ga_ref[...].astype(F32) * ma + gb_ref[...].astype(F32) * mb
    h = x_ref[...] + _dot(merged.astype(BF16), wout_ref[...])
    h_ref[...] = h
    t = _rms(h) * gffn_ref[...]
    t_hi, t_lo = _split_bf16(t)
    t_ref[...] = t_hi
    wrh, wrl = wrh_ref[...], wrl_ref[...]
    logits_t = _dot_nt(wrh, t_hi) + _dot_nt(wrh, t_lo) + _dot_nt(wrl, t_hi)
    gates_t = _route(logits_t, br_ref[...])
    tm = gates_t.shape[1]
    padded = jnp.concatenate([gates_t, jnp.zeros((LANES - N_EXPERTS, tm), F32)], axis=0)
    g_ref[...] = padded.T


def _mixer_out(oa, ob, ga, gb, x, lw, l, tm):
    t = x.shape[0]
    row = lambda n: pl.BlockSpec((tm, n), lambda i: (i, 0))
    full = lambda a: pl.BlockSpec((None,) + a.shape[1:], lambda i: (l,) + (0,) * (a.ndim - 1))
    const = lambda a: pl.BlockSpec(a.shape, lambda i: (0,) * a.ndim)
    ins = [oa, ob, ga, gb, x, lw["wba"], lw["wbb"], lw["wout"], lw["gffn"], lw["wrh"], lw["wrl"], lw["br"]]
    in_specs = [row(A_Q_COLS), row(B_O_COLS), row(D_MODEL), row(D_MODEL), row(D_MODEL),
                full(lw["wba"]), full(lw["wbb"]), full(lw["wout"]), full(lw["gffn"]),
                const(lw["wrh"]), const(lw["wrl"]), const(lw["br"])]
    return pl.pallas_call(
        _mixer_out_body,
        grid=(t // tm,),
        in_specs=in_specs,
        out_specs=[row(D_MODEL), row(D_MODEL), row(LANES)],
        out_shape=[jax.ShapeDtypeStruct((t, D_MODEL), F32), jax.ShapeDtypeStruct((t, D_MODEL), BF16),
                   jax.ShapeDtypeStruct((t, LANES), F32)],
        compiler_params=_cparams(("parallel",)),
        name="mixer_out",
    )(*ins)


def _moe_body(t_ref, g_ref, h_ref, wg_ref, wu_ref, wd_ref, o_ref, acc_ref):
    e = pl.program_id(1)

    @pl.when(e == 0)
    def _():
        acc_ref[...] = jnp.zeros_like(acc_ref)

    tb = t_ref[...]
    a = jax.nn.silu(_dot(tb, wg_ref[...])) * _dot(tb, wu_ref[...])
    y = _dot(a.astype(BF16), wd_ref[...])
    g = g_ref[...]
    lane = lax.broadcasted_iota(jnp.int32, g.shape, 1)
    gcol = jnp.sum(jnp.where(lane == e, g, 0.0), axis=-1, keepdims=True)
    acc_ref[...] += gcol * y

    @pl.when(e == N_EXPERTS - 1)
    def _():
        o_ref[...] = h_ref[...] + acc_ref[...]


def _moe(tn, gates, h, lw, l, tm):
    t = h.shape[0]
    row = lambda n: pl.BlockSpec((tm, n), lambda i, e: (i, 0))
    wspec = lambda a: pl.BlockSpec((None, None) + a.shape[2:], lambda i, e: (l, e, 0, 0))
    return pl.pallas_call(
        _moe_body,
        grid=(t // tm, N_EXPERTS),
        in_specs=[row(D_MODEL), row(LANES), row(D_MODEL), wspec(lw["wg"]), wspec(lw["wu"]), wspec(lw["wd"])],
        out_specs=row(D_MODEL),
        out_shape=jax.ShapeDtypeStruct((t, D_MODEL), F32),
        scratch_shapes=[pltpu.VMEM((tm, D_MODEL), F32)],
        compiler_params=_cparams(("parallel", "arbitrary")),
        name="moe",
    )(tn, gates, h, lw["wg"], lw["wu"], lw["wd"])


def _final_norm_body(x_ref, g_ref, o_ref):
    o_ref[...] = _rms(x_ref[...]) * g_ref[...]


def _final_norm(x, g, tm):
    t = x.shape[0]
    row = pl.BlockSpec((tm, D_MODEL), lambda i: (i, 0))
    return pl.pallas_call(
        _final_norm_body,
        grid=(t // tm,),
        in_specs=[row, pl.BlockSpec(g.shape, lambda i: (0, 0))],
        out_specs=row,
        out_shape=jax.ShapeDtypeStruct(x.shape, F32),
        compiler_params=_cparams(("parallel",)),
        name="final_norm",
    )(x, g)


def _rope_angles(seq_len, rot_dim):
    rows = seq_len // GRID_W
    row = jnp.broadcast_to(jnp.arange(rows)[:, None], (rows, GRID_W)).reshape(-1).astype(F32)
    col = jnp.broadcast_to(jnp.arange(GRID_W)[None, :], (rows, GRID_W)).reshape(-1).astype(F32)
    axis_dim = rot_dim // 2
    inv_freq = jnp.power(jnp.float32(ROPE_THETA), -jnp.arange(0, axis_dim, 2, dtype=F32) / axis_dim)
    ang = jnp.concatenate([row[:, None] * inv_freq[None, :], col[:, None] * inv_freq[None, :]], axis=-1)
    return jnp.cos(ang), jnp.sin(ang)


def _rope_tables(nseq, seq_len):
    c, s = _rope_angles(seq_len, A_HEAD_DIM)
    ca = jnp.concatenate([c, c, c, c], axis=-1)
    sa = jnp.concatenate([-s, s, -s, s], axis=-1)
    c, s = _rope_angles(seq_len, B_ROPE)
    ones = jnp.ones((seq_len, B_NOPE), F32)
    zeros = jnp.zeros((seq_len, B_NOPE), F32)
    tail = LANES - B_NOPE - B_ROPE
    cb = jnp.concatenate([ones, c, c, ones[:, :tail]], axis=-1)
    sb = jnp.concatenate([zeros, -s, s, zeros[:, :tail]], axis=-1)
    return tuple(jnp.tile(a, (nseq, 1)) for a in (ca, sa, cb, sb))


def _prepare_weights(norm_mix, w_in, a_q_norm, a_k_norm, b_q_norm, b_kv_norm, w_q_up, w_kv_up, w_branch_a,
                     w_branch_b, w_out, norm_ffn, w_router, b_router, w_gate, w_up, w_down):
    depth = w_in.shape[0]
    order = jnp.array(A_HEAD_ORDER)
    parts = []
    start = 0
    for n in (A_Q_COLS, A_KV_COLS, A_KV_COLS, B_Q_RANK, B_KV_RANK, B_ROPE, D_MODEL, D_MODEL):
        parts.append(w_in[..., start:start + n])
        start += n
    qa, ka, va, cq, ckv, kr, ga, gb = parts
    qa = qa.reshape(depth, D_MODEL, A_HEADS, A_HEAD_DIM)[:, :, order].reshape(depth, D_MODEL, A_Q_COLS)
    kr = jnp.pad(kr, ((0, 0), (0, 0), (B_NOPE, LANES - B_NOPE - B_ROPE)))
    win = jnp.concatenate([qa, ka, va, cq, ckv, kr, ga, gb], axis=-1).astype(BF16)

    wq = w_q_up.reshape(depth, B_Q_RANK, B_HEADS, B_QK_DIM)
    wq = jnp.pad(wq, ((0, 0), (0, 0), (0, 0), (0, LANES - B_QK_DIM))).reshape(depth, B_Q_RANK, B_PAD_COLS)
    wkv = w_kv_up.reshape(depth, B_KV_RANK, B_HEADS, B_NOPE + B_V)
    wk = jnp.pad(wkv[..., :B_NOPE], ((0, 0), (0, 0), (0, 0), (0, LANES - B_NOPE))).reshape(depth, B_KV_RANK, B_PAD_COLS)
    wv = wkv[..., B_NOPE:].reshape(depth, B_KV_RANK, B_O_COLS)
    wba = w_branch_a.reshape(depth, A_HEADS, A_HEAD_DIM, D_MODEL)[:, order].reshape(depth, A_Q_COLS, D_MODEL)

    group = jnp.arange(A_Q_COLS) // A_HEAD_DIM
    bd = (group[:, None] == group[None, :]).astype(BF16)
    wr_t = w_router.T
    wrh, wrl = _split_bf16(wr_t)
    vec = lambda a: a[:, None, :]
    return dict(
        gmix=vec(norm_mix), win=win,
        gq=vec(jnp.tile(a_q_norm, (1, A_HEADS)) * (A_HEAD_DIM ** -0.5)), gk=vec(jnp.tile(a_k_norm, (1, A_KV_HEADS))),
        gbq=vec(b_q_norm), gbkv=vec(b_kv_norm),
        wq=wq.astype(BF16), wk=wk.astype(BF16), wv=wv.astype(BF16), bd=bd,
        wba=wba.astype(BF16), wbb=w_branch_b.astype(BF16), wout=w_out.astype(BF16), gffn=vec(norm_ffn),
        wrh=wrh, wrl=wrl, br=b_router[:, None].astype(F32),
        wg=w_gate.astype(BF16), wu=w_up.astype(BF16), wd=w_down.astype(BF16),
    )


def _pick(n, candidates):
    for c in candidates:
        if n % c == 0:
            return c
    raise ValueError(f"no tile in {candidates} divides {n}")


def _trunk(x3, lw, norm_final):
    nseq, s, _ = x3.shape
    t = nseq * s
    x = x3.reshape(t, D_MODEL)
    tabs = _rope_tables(nseq, s)
    depth = lw["win"].shape[0]
    tm = _pick(t, (256,))
    tm_moe = _pick(t, (1024, 512))
    tq = _pick(s, (256,))
    tk = _pick(s, (512,))
    for l in range(depth):
        qa, ka, va, qb, kb, vb, ga, gb = _mixer_in(x, lw, l, tabs, tm)
        oa = _attention_a(qa, ka, va, nseq, s, tq, tk)
        ob = _attention_b(qb, kb, vb, nseq, s, tq, tk)
        h, tn, gates = _mixer_out(oa, ob, ga, gb, x, lw, l, tm)
        x = _moe(tn, gates, h, lw, l, tm_moe)
    return _final_norm(x, norm_final[None, :], tm).reshape(x3.shape)


def kernel(x_prompt, x_sample, norm_mix, w_in, a_q_norm, a_k_norm, b_q_norm, b_kv_norm, w_q_up, w_kv_up, w_branch_a,
           w_branch_b, w_out, norm_ffn, w_router, b_router, w_gate, w_up, w_down, norm_final):
    lw = _prepare_weights(norm_mix, w_in, a_q_norm, a_k_norm, b_q_norm, b_kv_norm, w_q_up, w_kv_up, w_branch_a,
                          w_branch_b, w_out, norm_ffn, w_router, b_router, w_gate, w_up, w_down)
    return _trunk(x_prompt, lw, norm_final), _trunk(x_sample, lw, norm_final)
```

```python
import functools

import jax
import jax.numpy as jnp
from jax import lax
from jax.experimental import pallas as pl
from jax.experimental.pallas import tpu as pltpu

F32 = jnp.float32
BF16 = jnp.bfloat16

D_MODEL = 1024
GRID_W = 64
ROPE_THETA = 10000.0
EPS = 1e-6
A_HEADS = 8
A_KV_HEADS = 2
A_HEAD_DIM = 64
B_HEADS = 8
B_Q_RANK = 384
B_KV_RANK = 256
B_NOPE = 64
B_ROPE = 32
B_V = 64
N_EXPERTS = 16
N_GROUPS = 4
EXPERTS_PER_GROUP = N_EXPERTS // N_GROUPS
D_EXPERT = 512
A_Q_COLS = A_HEADS * A_HEAD_DIM
A_KV_COLS = A_KV_HEADS * A_HEAD_DIM
B_QK_DIM = B_NOPE + B_ROPE
B_O_COLS = B_HEADS * B_V

LANES = 128
LOG2E = 1.4426950408889634

C_QA = 0
C_KA = C_QA + A_Q_COLS
C_VA = C_KA + A_KV_COLS
C_CQ = C_VA + A_KV_COLS
C_CKV = C_CQ + B_Q_RANK
C_KR = C_CKV + B_KV_RANK
C_GA = C_KR + LANES
C_GB = C_GA + D_MODEL
C_END = C_GB + D_MODEL
B_PAD_COLS = B_HEADS * LANES

A_HEAD_ORDER = (0, 4, 1, 5, 2, 6, 3, 7)

VMEM_LIMIT = 56 * 1024 * 1024

ATTN_GROUP = 8
ATTN_S_BUFS = 3
ATTN_P_BUFS = 2
ATTN_SLAB = 64


def _cparams(sem):
    return pltpu.CompilerParams(dimension_semantics=sem, vmem_limit_bytes=VMEM_LIMIT)


def _dot(a, b):
    return jnp.dot(a, b, preferred_element_type=F32)


def _dot_nt(a, b):
    return lax.dot_general(a, b, (((1,), (1,)), ((), ())), preferred_element_type=F32)


def _dot_tn(a, b):
    return lax.dot_general(a, b, (((0,), (0,)), ((), ())), preferred_element_type=F32)


def _rms(x):
    return x * lax.rsqrt(jnp.mean(x * x, axis=-1, keepdims=True) + EPS)


def _split_bf16(x):
    hi = x.astype(BF16)
    lo = (x - hi.astype(F32)).astype(BF16)
    return hi, lo


def _group_mean_sq(v, bd):
    hi, lo = _split_bf16(v * v)
    return (_dot(hi, bd) + _dot(lo, bd)) * (1.0 / A_HEAD_DIM)


def _rope_a(v, cos, sin):
    n = v.shape[-1]
    lane = lax.broadcasted_iota(jnp.int32, v.shape, 1)
    low = (lane % A_HEAD_DIM) < (A_HEAD_DIM // 2)
    swapped = jnp.where(low, pltpu.roll(v, n - A_HEAD_DIM // 2, 1), pltpu.roll(v, A_HEAD_DIM // 2, 1))
    return v * cos + swapped * sin


def _rope_b(v, cos, sin):
    n = v.shape[-1]
    lane = lax.broadcasted_iota(jnp.int32, v.shape, 1)
    low = (lane % LANES) < (B_NOPE + B_ROPE // 2)
    swapped = jnp.where(low, pltpu.roll(v, n - B_ROPE // 2, 1), pltpu.roll(v, B_ROPE // 2, 1))
    return v * cos + swapped * sin


def _mixer_in_body(x_ref, gmix_ref, win_ref, ca_ref, sa_ref, cb_ref, sb_ref, gq_ref, gk_ref, gbq_ref, gbkv_ref,
                   wq_ref, wk_ref, wv_ref, bd_ref,
                   qat_ref, ka_ref, vat_ref, qbt_ref, kb_ref, vbt_ref, ga_ref, gb_ref):
    hb = (_rms(x_ref[...]) * gmix_ref[...]).astype(BF16)
    u = _dot(hb, win_ref[:, C_QA:C_GA])
    bd = bd_ref[...]
    ca, sa = ca_ref[...], sa_ref[...]
    cb, sb = cb_ref[...], sb_ref[...]

    qa = u[:, C_QA:C_KA]
    qa = qa * lax.rsqrt(_group_mean_sq(qa, bd) + EPS) * gq_ref[...]
    qa = _rope_a(qa, jnp.concatenate([ca] * (A_Q_COLS // LANES), axis=1),
                 jnp.concatenate([sa] * (A_Q_COLS // LANES), axis=1))
    qat_ref[...] = qa.T.astype(BF16)
    ka = u[:, C_KA:C_VA]
    ka = ka * lax.rsqrt(_group_mean_sq(ka, bd[:A_KV_COLS, :A_KV_COLS]) + EPS) * gk_ref[...]
    ka_ref[...] = _rope_a(ka, ca, sa).astype(BF16)
    vat_ref[...] = u[:, C_VA:C_CQ].T.astype(BF16)

    cq = (_rms(u[:, C_CQ:C_CKV]) * gbq_ref[...]).astype(BF16)
    qb = _dot(cq, wq_ref[...])
    qb = _rope_b(qb, jnp.concatenate([cb] * B_HEADS, axis=1), jnp.concatenate([sb] * B_HEADS, axis=1))
    qbt_ref[...] = (qb * (B_QK_DIM ** -0.5 * LOG2E)).T.astype(BF16)
    ckv = (_rms(u[:, C_CKV:C_KR]) * gbkv_ref[...]).astype(BF16)
    kr = _rope_b(u[:, C_KR:C_GA], cb, sb)
    kb_ref[...] = (_dot(ckv, wk_ref[...]) + jnp.concatenate([kr] * B_HEADS, axis=1)).astype(BF16)
    vbt_ref[...] = _dot(ckv, wv_ref[...]).T.astype(BF16)

    g = jax.nn.sigmoid(_dot(hb, win_ref[:, C_GA:C_END]))
    ga_ref[...] = g[:, :D_MODEL].astype(BF16)
    gb_ref[...] = g[:, D_MODEL:].astype(BF16)


def _mixer_in(x, lw, l, tabs, tm):
    t = x.shape[0]
    row = lambda n: pl.BlockSpec((tm, n), lambda i: (i, 0))
    full = lambda a: pl.BlockSpec((None,) + a.shape[1:], lambda i: (l,) + (0,) * (a.ndim - 1))
    const = lambda a: pl.BlockSpec(a.shape, lambda i: (0,) * a.ndim)
    ca, sa, cb, sb = tabs
    ins = [x, lw["gmix"], lw["win"], ca, sa, cb, sb, lw["gq"], lw["gk"], lw["gbq"], lw["gbkv"],
           lw["wq"], lw["wk"], lw["wv"], lw["bd"]]
    in_specs = [row(D_MODEL), full(lw["gmix"]), full(lw["win"]), row(LANES), row(LANES), row(LANES), row(LANES),
                full(lw["gq"]), full(lw["gk"]), full(lw["gbq"]), full(lw["gbkv"]),
                full(lw["wq"]), full(lw["wk"]), full(lw["wv"]), const(lw["bd"])]
    col = lambda n: pl.BlockSpec((n, tm), lambda i: (0, i))
    widths = [A_Q_COLS, A_KV_COLS, A_KV_COLS, B_PAD_COLS, B_PAD_COLS, B_O_COLS, D_MODEL, D_MODEL]
    transposed = [True, False, True, True, False, True, False, False]
    return pl.pallas_call(
        _mixer_in_body,
        grid=(t // tm,),
        in_specs=in_specs,
        out_specs=[col(n) if tr else row(n) for n, tr in zip(widths, transposed)],
        out_shape=[jax.ShapeDtypeStruct((n, t) if tr else (t, n), BF16) for n, tr in zip(widths, transposed)],
        compiler_params=_cparams(("parallel",)),
        name="mixer_in",
    )(*ins)


def _flash_pair(q0t, q1t, k0_ref, k1_ref, vt_ref, s_refs, p_refs, acc_ref, tk):
    tq = q0t.shape[1]
    nk = vt_ref.shape[1] // tk
    group = min(nk, ATTN_GROUP)
    ns = len(s_refs)
    qt2 = jnp.concatenate([q0t, q1t], axis=1)

    def scores(j, s_ref):
        off = pl.multiple_of(j * tk, tk)
        if k1_ref is None:
            s_ref[...] = _dot(k0_ref[pl.ds(off, tk), :], qt2)
        else:
            s_ref[:, :tq] = _dot(k0_ref[pl.ds(off, tk), :], q0t)
            s_ref[:, tq:] = _dot(k1_ref[pl.ds(off, tk), :], q1t)

    def softmax(s_ref, p_ref, m, l):
        slabs = [pl.ds(r, ATTN_SLAB) for r in range(0, tk, ATTN_SLAB)]
        fold = lambda x: x.reshape(ATTN_SLAB // 8, 8, 2 * tq)
        mx = fold(s_ref[slabs[0], :]).max(axis=0)
        for sl in slabs[1:]:
            mx = jnp.maximum(mx, fold(s_ref[sl, :]).max(axis=0))
        m_new = jnp.maximum(m, jnp.max(mx, axis=0, keepdims=True))
        alpha = jnp.exp2(m - m_new)
        l = alpha * l
        for sl in slabs:
            p = jnp.exp2(s_ref[sl, :] - m_new)
            p_ref[sl, :] = p.astype(BF16)
            l = l + fold(p).sum(axis=0)
        return m_new, l, alpha

    def values(j, p_ref, alpha):
        off = pl.multiple_of(j * tk, tk)
        acc_ref[...] = alpha * acc_ref[...] + _dot(vt_ref[:, pl.ds(off, tk)], p_ref[...])

    def one_group(g, carry):
        m, l = carry
        base = g * group
        for c in range(min(ns - 1, group)):
            scores(base + c, s_refs[c % ns])
        for c in range(group):
            if c + ns - 1 < group:
                scores(base + c + ns - 1, s_refs[(c + ns - 1) % ns])
            m, l, alpha = softmax(s_refs[c % ns], p_refs[c % len(p_refs)], m, l)
            values(base + c, p_refs[c % len(p_refs)], alpha)
        return m, l

    acc_ref[...] = jnp.zeros_like(acc_ref)
    carry = (jnp.full((1, 2 * tq), -jnp.inf, F32), jnp.zeros((8, 2 * tq), F32))
    if nk == group:
        _, l = one_group(0, carry)
    else:
        _, l = lax.fori_loop(0, nk // group, one_group, carry)
    o = acc_ref[...] / jnp.sum(l, axis=0, keepdims=True)
    return jnp.concatenate([o[:B_V, :tq], o[B_V:, tq:]], axis=0)


def _attn_a_body(qt_ref, k_ref, vt_ref, o_ref, *scratch, tk):
    qt = qt_ref[...]
    zero = jnp.zeros((A_HEAD_DIM, qt.shape[1]), BF16)
    q0t = jnp.concatenate([qt[:A_HEAD_DIM], zero], axis=0)
    q1t = jnp.concatenate([zero, qt[A_HEAD_DIM:]], axis=0)
    s_refs, p_refs, acc_ref = scratch[:ATTN_S_BUFS], scratch[ATTN_S_BUFS:-1], scratch[-1]
    o_ref[...] = _flash_pair(q0t, q1t, k_ref, None, vt_ref, s_refs, p_refs, acc_ref, tk).astype(BF16)


def _attn_b_body(q0t_ref, q1t_ref, k0_ref, k1_ref, vt_ref, o_ref, *scratch, tk):
    s_refs, p_refs, acc_ref = scratch[:ATTN_S_BUFS], scratch[ATTN_S_BUFS:-1], scratch[-1]
    o_ref[...] = _flash_pair(q0t_ref[...], q1t_ref[...], k0_ref, k1_ref, vt_ref, s_refs, p_refs, acc_ref,
                             tk).astype(BF16)


def _attn_scratch(tq, tk):
    return ([pltpu.VMEM((tk, 2 * tq), F32)] * ATTN_S_BUFS + [pltpu.VMEM((tk, 2 * tq), BF16)] * ATTN_P_BUFS
            + [pltpu.VMEM((LANES, 2 * tq), F32)])


def _attention_a(qat, ka, vat, nseq, s, tq, tk):
    nq = s // tq
    qspec = pl.BlockSpec((LANES, tq), lambda b, h, i: (h, b * nq + i))
    kspec = pl.BlockSpec((s, LANES), lambda b, h, i: (b, 0))
    vspec = pl.BlockSpec((LANES, s), lambda b, h, i: (0, b))
    return pl.pallas_call(
        functools.partial(_attn_a_body, tk=tk),
        grid=(nseq, A_Q_COLS // LANES, nq),
        in_specs=[qspec, kspec, vspec],
        out_specs=qspec,
        out_shape=jax.ShapeDtypeStruct(qat.shape, BF16),
        scratch_shapes=_attn_scratch(tq, tk),
        compiler_params=_cparams(("parallel", "parallel", "parallel")),
        name="attn_a",
    )(qat, ka, vat)


def _attention_b(qbt, kb, vbt, nseq, s, tq, tk):
    nq = s // tq
    q0 = pl.BlockSpec((LANES, tq), lambda b, h, i: (2 * h, b * nq + i))
    q1 = pl.BlockSpec((LANES, tq), lambda b, h, i: (2 * h + 1, b * nq + i))
    k0 = pl.BlockSpec((s, LANES), lambda b, h, i: (b, 2 * h))
    k1 = pl.BlockSpec((s, LANES), lambda b, h, i: (b, 2 * h + 1))
    v = pl.BlockSpec((LANES, s), lambda b, h, i: (h, b))
    o = pl.BlockSpec((LANES, tq), lambda b, h, i: (h, b * nq + i))
    return pl.pallas_call(
        functools.partial(_attn_b_body, tk=tk),
        grid=(nseq, B_O_COLS // LANES, nq),
        in_specs=[q0, q1, k0, k1, v],
        out_specs=o,
        out_shape=jax.ShapeDtypeStruct(vbt.shape, BF16),
        scratch_shapes=_attn_scratch(tq, tk),
        compiler_params=_cparams(("parallel", "parallel", "parallel")),
        name="attn_b",
    )(qbt, qbt, kb, kb, vbt)


def _within(x, d, period, n):
    row = lax.broadcasted_iota(jnp.int32, x.shape, 0)
    return jnp.where((row % period) + d < period, pltpu.roll(x, n - d, 0), pltpu.roll(x, period - d, 0))


def _route(logits_t, bias):
    n = N_EXPERTS
    scores = jax.nn.sigmoid(logits_t)
    biased = scores + bias
    row = lax.broadcasted_iota(jnp.int32, biased.shape, 0)
    pos = row % EXPERTS_PER_GROUP
    rank = jnp.zeros(biased.shape, jnp.int32)
    for d in range(1, EXPERTS_PER_GROUP):
        other = _within(biased, d, EXPERTS_PER_GROUP, n)
        other_pos = (pos + d) % EXPERTS_PER_GROUP
        ahead = (other > biased) | ((other == biased) & (other_pos < pos))
        rank = rank + ahead.astype(jnp.int32)
    top2 = rank < 2
    kept = jnp.where(top2, biased, 0.0)
    gscore = kept
    for d in range(1, EXPERTS_PER_GROUP):
        gscore = gscore + _within(kept, d, EXPERTS_PER_GROUP, n)
    grp = row // EXPERTS_PER_GROUP
    win = jnp.ones(biased.shape, jnp.bool_)
    for d in range(1, N_GROUPS):
        other = pltpu.roll(gscore, n - d * EXPERTS_PER_GROUP, 0)
        other_grp = (grp + d) % N_GROUPS
        win = win & ((other < gscore) | ((other == gscore) & (other_grp > grp)))
    w = jnp.where(top2 & win, scores, 0.0)
    return w / jnp.sum(w, axis=0, keepdims=True)


def _mixer_out_body(oa_ref, ob_ref, ga_ref, gb_ref, x_ref, wba_ref, wbb_ref, wout_ref, gffn_ref, wrh_ref, wrl_ref,
                    br_ref, h_ref, t_ref, g_ref):
    ma = _dot_tn(oa_ref[...], wba_ref[...])
    mb = _dot_tn(ob_ref[...], wbb_ref[...])
    merged = ga_ref[...].astype(F32) * ma + gb_ref[...].astype(F32) * mb
    h = x_ref[...] + _dot(merged.astype(BF16), wout_ref[...])
    h_ref[...] = h
    t = _rms(h) * gffn_ref[...]
    t_hi, t_lo = _split_bf16(t)
    t_ref[...] = t_hi
    wrh, wrl = wrh_ref[...], wrl_ref[...]
    logits_t = _dot_nt(wrh, t_hi) + _dot_nt(wrh, t_lo) + _dot_nt(wrl, t_hi)
    gates_t = _route(logits_t, br_ref[...])
    tm = gates_t.shape[1]
    padded = jnp.concatenate([gates_t, jnp.zeros((LANES - N_EXPERTS, tm), F32)], axis=0)
    g_ref[...] = padded.T


def _mixer_out(oa, ob, ga, gb, x, lw, l, tm):
    t = x.shape[0]
    row = lambda n: pl.BlockSpec((tm, n), lambda i: (i, 0))
    full = lambda a: pl.BlockSpec((None,) + a.shape[1:], lambda i: (l,) + (0,) * (a.ndim - 1))
    const = lambda a: pl.BlockSpec(a.shape, lambda i: (0,) * a.ndim)
    ins = [oa, ob, ga, gb, x, lw["wba"], lw["wbb"], lw["wout"], lw["gffn"], lw["wrh"], lw["wrl"], lw["br"]]
    col = lambda n: pl.BlockSpec((n, tm), lambda i: (0, i))
    in_specs = [col(A_Q_COLS), col(B_O_COLS), row(D_MODEL), row(D_MODEL), row(D_MODEL),
                full(lw["wba"]), full(lw["wbb"]), full(lw["wout"]), full(lw["gffn"]),
                const(lw["wrh"]), const(lw["wrl"]), const(lw["br"])]
    return pl.pallas_call(
        _mixer_out_body,
        grid=(t // tm,),
        in_specs=in_specs,
        out_specs=[row(D_MODEL), row(D_MODEL), row(LANES)],
        out_shape=[jax.ShapeDtypeStruct((t, D_MODEL), F32), jax.ShapeDtypeStruct((t, D_MODEL), BF16),
                   jax.ShapeDtypeStruct((t, LANES), F32)],
        compiler_params=_cparams(("parallel",)),
        name="mixer_out",
    )(*ins)


def _moe_body(t_ref, g_ref, h_ref, wg_ref, wu_ref, wd_ref, o_ref, acc_ref):
    e = pl.program_id(1)

    @pl.when(e == 0)
    def _():
        acc_ref[...] = jnp.zeros_like(acc_ref)

    tb = t_ref[...]
    a = jax.nn.silu(_dot(tb, wg_ref[...])) * _dot(tb, wu_ref[...])
    y = _dot(a.astype(BF16), wd_ref[...])
    g = g_ref[...]
    lane = lax.broadcasted_iota(jnp.int32, g.shape, 1)
    gcol = jnp.sum(jnp.where(lane == e, g, 0.0), axis=-1, keepdims=True)
    acc_ref[...] += gcol * y

    @pl.when(e == N_EXPERTS - 1)
    def _():
        o_ref[...] = h_ref[...] + acc_ref[...]


def _moe(tn, gates, h, lw, l, tm):
    t = h.shape[0]
    row = lambda n: pl.BlockSpec((tm, n), lambda i, e: (i, 0))
    wspec = lambda a: pl.BlockSpec((None, None) + a.shape[2:], lambda i, e: (l, e, 0, 0))
    return pl.pallas_call(
        _moe_body,
        grid=(t // tm, N_EXPERTS),
        in_specs=[row(D_MODEL), row(LANES), row(D_MODEL), wspec(lw["wg"]), wspec(lw["wu"]), wspec(lw["wd"])],
        out_specs=row(D_MODEL),
        out_shape=jax.ShapeDtypeStruct((t, D_MODEL), F32),
        scratch_shapes=[pltpu.VMEM((tm, D_MODEL), F32)],
        compiler_params=_cparams(("parallel", "arbitrary")),
        name="moe",
    )(tn, gates, h, lw["wg"], lw["wu"], lw["wd"])


def _final_norm_body(x_ref, g_ref, o_ref):
    o_ref[...] = _rms(x_ref[...]) * g_ref[...]


def _final_norm(x, g, tm):
    t = x.shape[0]
    row = pl.BlockSpec((tm, D_MODEL), lambda i: (i, 0))
    return pl.pallas_call(
        _final_norm_body,
        grid=(t // tm,),
        in_specs=[row, pl.BlockSpec(g.shape, lambda i: (0, 0))],
        out_specs=row,
        out_shape=jax.ShapeDtypeStruct(x.shape, F32),
        compiler_params=_cparams(("parallel",)),
        name="final_norm",
    )(x, g)


def _rope_angles(seq_len, rot_dim):
    rows = seq_len // GRID_W
    row = jnp.broadcast_to(jnp.arange(rows)[:, None], (rows, GRID_W)).reshape(-1).astype(F32)
    col = jnp.broadcast_to(jnp.arange(GRID_W)[None, :], (rows, GRID_W)).reshape(-1).astype(F32)
    axis_dim = rot_dim // 2
    inv_freq = jnp.power(jnp.float32(ROPE_THETA), -jnp.arange(0, axis_dim, 2, dtype=F32) / axis_dim)
    ang = jnp.concatenate([row[:, None] * inv_freq[None, :], col[:, None] * inv_freq[None, :]], axis=-1)
    return jnp.cos(ang), jnp.sin(ang)


def _rope_tables(nseq, seq_len):
    c, s = _rope_angles(seq_len, A_HEAD_DIM)
    ca = jnp.concatenate([c, c, c, c], axis=-1)
    sa = jnp.concatenate([-s, s, -s, s], axis=-1)
    c, s = _rope_angles(seq_len, B_ROPE)
    ones = jnp.ones((seq_len, B_NOPE), F32)
    zeros = jnp.zeros((seq_len, B_NOPE), F32)
    tail = LANES - B_NOPE - B_ROPE
    cb = jnp.concatenate([ones, c, c, ones[:, :tail]], axis=-1)
    sb = jnp.concatenate([zeros, -s, s, zeros[:, :tail]], axis=-1)
    return tuple(jnp.tile(a, (nseq, 1)) for a in (ca, sa, cb, sb))


def _prepare_weights(norm_mix, w_in, a_q_norm, a_k_norm, b_q_norm, b_kv_norm, w_q_up, w_kv_up, w_branch_a,
                     w_branch_b, w_out, norm_ffn, w_router, b_router, w_gate, w_up, w_down):
    depth = w_in.shape[0]
    order = jnp.array(A_HEAD_ORDER)
    parts = []
    start = 0
    for n in (A_Q_COLS, A_KV_COLS, A_KV_COLS, B_Q_RANK, B_KV_RANK, B_ROPE, D_MODEL, D_MODEL):
        parts.append(w_in[..., start:start + n])
        start += n
    qa, ka, va, cq, ckv, kr, ga, gb = parts
    qa = qa.reshape(depth, D_MODEL, A_HEADS, A_HEAD_DIM)[:, :, order].reshape(depth, D_MODEL, A_Q_COLS)
    kr = jnp.pad(kr, ((0, 0), (0, 0), (B_NOPE, LANES - B_NOPE - B_ROPE)))
    win = jnp.concatenate([qa, ka, va, cq, ckv, kr, ga, gb], axis=-1).astype(BF16)

    wq = w_q_up.reshape(depth, B_Q_RANK, B_HEADS, B_QK_DIM)
    wq = jnp.pad(wq, ((0, 0), (0, 0), (0, 0), (0, LANES - B_QK_DIM))).reshape(depth, B_Q_RANK, B_PAD_COLS)
    wkv = w_kv_up.reshape(depth, B_KV_RANK, B_HEADS, B_NOPE + B_V)
    wk = jnp.pad(wkv[..., :B_NOPE], ((0, 0), (0, 0), (0, 0), (0, LANES - B_NOPE))).reshape(depth, B_KV_RANK, B_PAD_COLS)
    wv = wkv[..., B_NOPE:].reshape(depth, B_KV_RANK, B_O_COLS)
    wba = w_branch_a.reshape(depth, A_HEADS, A_HEAD_DIM, D_MODEL)[:, order].reshape(depth, A_Q_COLS, D_MODEL)

    group = jnp.arange(A_Q_COLS) // A_HEAD_DIM
    bd = (group[:, None] == group[None, :]).astype(BF16)
    wr_t = w_router.T
    wrh, wrl = _split_bf16(wr_t)
    vec = lambda a: a[:, None, :]
    return dict(
        gmix=vec(norm_mix), win=win,
        gq=vec(jnp.tile(a_q_norm, (1, A_HEADS)) * (A_HEAD_DIM ** -0.5 * LOG2E)), gk=vec(jnp.tile(a_k_norm, (1, A_KV_HEADS))),
        gbq=vec(b_q_norm), gbkv=vec(b_kv_norm),
        wq=wq.astype(BF16), wk=wk.astype(BF16), wv=wv.astype(BF16), bd=bd,
        wba=wba.astype(BF16), wbb=w_branch_b.astype(BF16), wout=w_out.astype(BF16), gffn=vec(norm_ffn),
        wrh=wrh, wrl=wrl, br=b_router[:, None].astype(F32),
        wg=w_gate.astype(BF16), wu=w_up.astype(BF16), wd=w_down.astype(BF16),
    )


def _pick(n, candidates):
    for c in candidates:
        if n % c == 0:
            return c
    raise ValueError(f"no tile in {candidates} divides {n}")


def _trunk(x3, lw, norm_final):
    nseq, s, _ = x3.shape
    t = nseq * s
    x = x3.reshape(t, D_MODEL)
    tabs = _rope_tables(nseq, s)
    depth = lw["win"].shape[0]
    tm = _pick(t, (256,))
    tm_moe = _pick(t, (1024, 512))
    tq = _pick(s, (256,))
    tk = _pick(s, (512,))
    for l in range(depth):
        qa, ka, va, qb, kb, vb, ga, gb = _mixer_in(x, lw, l, tabs, tm)
        oa = _attention_a(qa, ka, va, nseq, s, tq, tk)
        ob = _attention_b(qb, kb, vb, nseq, s, tq, tk)
        h, tn, gates = _mixer_out(oa, ob, ga, gb, x, lw, l, tm)
        x = _moe(tn, gates, h, lw, l, tm_moe)
    return _final_norm(x, norm_final[None, :], tm).reshape(x3.shape)


def kernel(x_prompt, x_sample, norm_mix, w_in, a_q_norm, a_k_norm, b_q_norm, b_kv_norm, w_q_up, w_kv_up, w_branch_a,
           w_branch_b, w_out, norm_ffn, w_router, b_router, w_gate, w_up, w_down, norm_final):
    lw = _prepare_weights(norm_mix, w_in, a_q_norm, a_k_norm, b_q_norm, b_kv_norm, w_q_up, w_kv_up, w_branch_a,
                          w_branch_b, w_out, norm_ffn, w_router, b_router, w_gate, w_up, w_down)
    return _trunk(x_prompt, lw, norm_final), _trunk(x_sample, lw, norm_final)
```

```python
import functools

import jax
import jax.numpy as jnp
from jax import lax
from jax.experimental import pallas as pl
from jax.experimental.pallas import tpu as pltpu
from jax.experimental.pallas import tpu_sc as plsc

F32 = jnp.float32
BF16 = jnp.bfloat16

D_MODEL = 1024
GRID_W = 64
ROPE_THETA = 10000.0
EPS = 1e-6
A_HEADS = 8
A_KV_HEADS = 2
A_HEAD_DIM = 64
B_HEADS = 8
B_Q_RANK = 384
B_KV_RANK = 256
B_NOPE = 64
B_ROPE = 32
B_V = 64
N_EXPERTS = 16
N_GROUPS = 4
EXPERTS_PER_GROUP = N_EXPERTS // N_GROUPS
D_EXPERT = 512
A_Q_COLS = A_HEADS * A_HEAD_DIM
A_KV_COLS = A_KV_HEADS * A_HEAD_DIM
B_QK_DIM = B_NOPE + B_ROPE
B_O_COLS = B_HEADS * B_V

LANES = 128
LOG2E = 1.4426950408889634

C_QA = 0
C_KA = C_QA + A_Q_COLS
C_VA = C_KA + A_KV_COLS
C_CQ = C_VA + A_KV_COLS
C_CKV = C_CQ + B_Q_RANK
C_KR = C_CKV + B_KV_RANK
C_GA = C_KR + LANES
C_GB = C_GA + D_MODEL
C_END = C_GB + D_MODEL
B_PAD_COLS = B_HEADS * LANES

A_HEAD_ORDER = (0, 4, 1, 5, 2, 6, 3, 7)

VMEM_LIMIT = 56 * 1024 * 1024

ATTN_GROUP = 8
ATTN_S_BUFS = 3
ATTN_P_BUFS = 2
EXPERT_ROWS = 256
SC_CHUNK = 128
ATTN_SLAB = 64


def _cparams(sem):
    return pltpu.CompilerParams(dimension_semantics=sem, vmem_limit_bytes=VMEM_LIMIT)


def _dot(a, b):
    return jnp.dot(a, b, preferred_element_type=F32)


def _dot_nt(a, b):
    return lax.dot_general(a, b, (((1,), (1,)), ((), ())), preferred_element_type=F32)


def _dot_tn(a, b):
    return lax.dot_general(a, b, (((0,), (0,)), ((), ())), preferred_element_type=F32)


def _rms(x):
    return x * lax.rsqrt(jnp.mean(x * x, axis=-1, keepdims=True) + EPS)


def _split_bf16(x):
    hi = x.astype(BF16)
    lo = (x - hi.astype(F32)).astype(BF16)
    return hi, lo


def _group_mean_sq(v, bd):
    hi, lo = _split_bf16(v * v)
    return (_dot(hi, bd) + _dot(lo, bd)) * (1.0 / A_HEAD_DIM)


def _rope_a(v, cos, sin):
    n = v.shape[-1]
    lane = lax.broadcasted_iota(jnp.int32, v.shape, 1)
    low = (lane % A_HEAD_DIM) < (A_HEAD_DIM // 2)
    swapped = jnp.where(low, pltpu.roll(v, n - A_HEAD_DIM // 2, 1), pltpu.roll(v, A_HEAD_DIM // 2, 1))
    return v * cos + swapped * sin


def _rope_b(v, cos, sin):
    n = v.shape[-1]
    lane = lax.broadcasted_iota(jnp.int32, v.shape, 1)
    low = (lane % LANES) < (B_NOPE + B_ROPE // 2)
    swapped = jnp.where(low, pltpu.roll(v, n - B_ROPE // 2, 1), pltpu.roll(v, B_ROPE // 2, 1))
    return v * cos + swapped * sin


def _mixer_in_body(x_ref, gmix_ref, win_ref, ca_ref, sa_ref, cb_ref, sb_ref, gq_ref, gk_ref, gbq_ref, gbkv_ref,
                   wq_ref, wk_ref, wv_ref, bd_ref,
                   qat_ref, ka_ref, vat_ref, qbt_ref, kb_ref, vbt_ref, ga_ref, gb_ref):
    hb = (_rms(x_ref[...]) * gmix_ref[...]).astype(BF16)
    u = _dot(hb, win_ref[:, C_QA:C_GA])
    bd = bd_ref[...]
    ca, sa = ca_ref[...], sa_ref[...]
    cb, sb = cb_ref[...], sb_ref[...]

    qa = u[:, C_QA:C_KA]
    qa = qa * lax.rsqrt(_group_mean_sq(qa, bd) + EPS) * gq_ref[...]
    qa = _rope_a(qa, jnp.concatenate([ca] * (A_Q_COLS // LANES), axis=1),
                 jnp.concatenate([sa] * (A_Q_COLS // LANES), axis=1))
    qat_ref[...] = qa.T.astype(BF16)
    ka = u[:, C_KA:C_VA]
    ka = ka * lax.rsqrt(_group_mean_sq(ka, bd[:A_KV_COLS, :A_KV_COLS]) + EPS) * gk_ref[...]
    ka_ref[...] = _rope_a(ka, ca, sa).astype(BF16)
    vat_ref[...] = u[:, C_VA:C_CQ].T.astype(BF16)

    cq = (_rms(u[:, C_CQ:C_CKV]) * gbq_ref[...]).astype(BF16)
    qb = _dot(cq, wq_ref[...])
    qb = _rope_b(qb, jnp.concatenate([cb] * B_HEADS, axis=1), jnp.concatenate([sb] * B_HEADS, axis=1))
    qbt_ref[...] = (qb * (B_QK_DIM ** -0.5 * LOG2E)).T.astype(BF16)
    ckv = (_rms(u[:, C_CKV:C_KR]) * gbkv_ref[...]).astype(BF16)
    kr = _rope_b(u[:, C_KR:C_GA], cb, sb)
    kb_ref[...] = (_dot(ckv, wk_ref[...]) + jnp.concatenate([kr] * B_HEADS, axis=1)).astype(BF16)
    vbt_ref[...] = _dot(ckv, wv_ref[...]).T.astype(BF16)

    g = jax.nn.sigmoid(_dot(hb, win_ref[:, C_GA:C_END]))
    ga_ref[...] = g[:, :D_MODEL].astype(BF16)
    gb_ref[...] = g[:, D_MODEL:].astype(BF16)


def _mixer_in(x, lw, l, tabs, tm):
    t = x.shape[0]
    row = lambda n: pl.BlockSpec((tm, n), lambda i: (i, 0))
    full = lambda a: pl.BlockSpec((None,) + a.shape[1:], lambda i: (l,) + (0,) * (a.ndim - 1))
    const = lambda a: pl.BlockSpec(a.shape, lambda i: (0,) * a.ndim)
    ca, sa, cb, sb = tabs
    ins = [x, lw["gmix"], lw["win"], ca, sa, cb, sb, lw["gq"], lw["gk"], lw["gbq"], lw["gbkv"],
           lw["wq"], lw["wk"], lw["wv"], lw["bd"]]
    in_specs = [row(D_MODEL), full(lw["gmix"]), full(lw["win"]), row(LANES), row(LANES), row(LANES), row(LANES),
                full(lw["gq"]), full(lw["gk"]), full(lw["gbq"]), full(lw["gbkv"]),
                full(lw["wq"]), full(lw["wk"]), full(lw["wv"]), const(lw["bd"])]
    col = lambda n: pl.BlockSpec((n, tm), lambda i: (0, i))
    widths = [A_Q_COLS, A_KV_COLS, A_KV_COLS, B_PAD_COLS, B_PAD_COLS, B_O_COLS, D_MODEL, D_MODEL]
    transposed = [True, False, True, True, False, True, False, False]
    return pl.pallas_call(
        _mixer_in_body,
        grid=(t // tm,),
        in_specs=in_specs,
        out_specs=[col(n) if tr else row(n) for n, tr in zip(widths, transposed)],
        out_shape=[jax.ShapeDtypeStruct((n, t) if tr else (t, n), BF16) for n, tr in zip(widths, transposed)],
        compiler_params=_cparams(("parallel",)),
        name="mixer_in",
    )(*ins)


def _flash_pair(q0t, q1t, k0_ref, k1_ref, vt_ref, s_refs, p_refs, acc_ref, tk):
    tq = q0t.shape[1]
    nk = vt_ref.shape[1] // tk
    group = min(nk, ATTN_GROUP)
    ns = len(s_refs)
    qt2 = jnp.concatenate([q0t, q1t], axis=1)

    def scores(j, s_ref):
        off = pl.multiple_of(j * tk, tk)
        if k1_ref is None:
            s_ref[...] = _dot(k0_ref[pl.ds(off, tk), :], qt2)
        else:
            s_ref[:, :tq] = _dot(k0_ref[pl.ds(off, tk), :], q0t)
            s_ref[:, tq:] = _dot(k1_ref[pl.ds(off, tk), :], q1t)

    def softmax(s_ref, p_ref, m, l):
        slabs = [pl.ds(r, ATTN_SLAB) for r in range(0, tk, ATTN_SLAB)]
        fold = lambda x: x.reshape(ATTN_SLAB // 8, 8, 2 * tq)
        mx = fold(s_ref[slabs[0], :]).max(axis=0)
        for sl in slabs[1:]:
            mx = jnp.maximum(mx, fold(s_ref[sl, :]).max(axis=0))
        m_new = jnp.maximum(m, jnp.max(mx, axis=0, keepdims=True))
        alpha = jnp.exp2(m - m_new)
        l = alpha * l
        for sl in slabs:
            p = jnp.exp2(s_ref[sl, :] - m_new)
            p_ref[sl, :] = p.astype(BF16)
            l = l + fold(p).sum(axis=0)
        return m_new, l, alpha

    def values(j, p_ref, alpha):
        off = pl.multiple_of(j * tk, tk)
        acc_ref[...] = alpha * acc_ref[...] + _dot(vt_ref[:, pl.ds(off, tk)], p_ref[...])

    def one_group(g, carry):
        m, l = carry
        base = g * group
        for c in range(min(ns - 1, group)):
            scores(base + c, s_refs[c % ns])
        for c in range(group):
            if c + ns - 1 < group:
                scores(base + c + ns - 1, s_refs[(c + ns - 1) % ns])
            m, l, alpha = softmax(s_refs[c % ns], p_refs[c % len(p_refs)], m, l)
            values(base + c, p_refs[c % len(p_refs)], alpha)
        return m, l

    acc_ref[...] = jnp.zeros_like(acc_ref)
    carry = (jnp.full((1, 2 * tq), -jnp.inf, F32), jnp.zeros((8, 2 * tq), F32))
    if nk == group:
        _, l = one_group(0, carry)
    else:
        _, l = lax.fori_loop(0, nk // group, one_group, carry)
    o = acc_ref[...] / jnp.sum(l, axis=0, keepdims=True)
    return jnp.concatenate([o[:B_V, :tq], o[B_V:, tq:]], axis=0)


def _attn_a_body(qt_ref, k_ref, vt_ref, o_ref, *scratch, tk):
    qt = qt_ref[...]
    zero = jnp.zeros((A_HEAD_DIM, qt.shape[1]), BF16)
    q0t = jnp.concatenate([qt[:A_HEAD_DIM], zero], axis=0)
    q1t = jnp.concatenate([zero, qt[A_HEAD_DIM:]], axis=0)
    s_refs, p_refs, acc_ref = scratch[:ATTN_S_BUFS], scratch[ATTN_S_BUFS:-1], scratch[-1]
    o_ref[...] = _flash_pair(q0t, q1t, k_ref, None, vt_ref, s_refs, p_refs, acc_ref, tk).astype(BF16)


def _attn_b_body(q0t_ref, q1t_ref, k0_ref, k1_ref, vt_ref, o_ref, *scratch, tk):
    s_refs, p_refs, acc_ref = scratch[:ATTN_S_BUFS], scratch[ATTN_S_BUFS:-1], scratch[-1]
    o_ref[...] = _flash_pair(q0t_ref[...], q1t_ref[...], k0_ref, k1_ref, vt_ref, s_refs, p_refs, acc_ref,
                             tk).astype(BF16)


def _attn_scratch(tq, tk):
    return ([pltpu.VMEM((tk, 2 * tq), F32)] * ATTN_S_BUFS + [pltpu.VMEM((tk, 2 * tq), BF16)] * ATTN_P_BUFS
            + [pltpu.VMEM((LANES, 2 * tq), F32)])


def _attention_a(qat, ka, vat, nseq, s, tq, tk):
    nq = s // tq
    qspec = pl.BlockSpec((LANES, tq), lambda b, h, i: (h, b * nq + i))
    kspec = pl.BlockSpec((s, LANES), lambda b, h, i: (b, 0))
    vspec = pl.BlockSpec((LANES, s), lambda b, h, i: (0, b))
    return pl.pallas_call(
        functools.partial(_attn_a_body, tk=tk),
        grid=(nseq, A_Q_COLS // LANES, nq),
        in_specs=[qspec, kspec, vspec],
        out_specs=qspec,
        out_shape=jax.ShapeDtypeStruct(qat.shape, BF16),
        scratch_shapes=_attn_scratch(tq, tk),
        compiler_params=_cparams(("parallel", "parallel", "parallel")),
        name="attn_a",
    )(qat, ka, vat)


def _attention_b(qbt, kb, vbt, nseq, s, tq, tk):
    nq = s // tq
    q0 = pl.BlockSpec((LANES, tq), lambda b, h, i: (2 * h, b * nq + i))
    q1 = pl.BlockSpec((LANES, tq), lambda b, h, i: (2 * h + 1, b * nq + i))
    k0 = pl.BlockSpec((s, LANES), lambda b, h, i: (b, 2 * h))
    k1 = pl.BlockSpec((s, LANES), lambda b, h, i: (b, 2 * h + 1))
    v = pl.BlockSpec((LANES, s), lambda b, h, i: (h, b))
    o = pl.BlockSpec((LANES, tq), lambda b, h, i: (h, b * nq + i))
    return pl.pallas_call(
        functools.partial(_attn_b_body, tk=tk),
        grid=(nseq, B_O_COLS // LANES, nq),
        in_specs=[q0, q1, k0, k1, v],
        out_specs=o,
        out_shape=jax.ShapeDtypeStruct(vbt.shape, BF16),
        scratch_shapes=_attn_scratch(tq, tk),
        compiler_params=_cparams(("parallel", "parallel", "parallel")),
        name="attn_b",
    )(qbt, qbt, kb, kb, vbt)


def _within(x, d, period, n):
    row = lax.broadcasted_iota(jnp.int32, x.shape, 0)
    return jnp.where((row % period) + d < period, pltpu.roll(x, n - d, 0), pltpu.roll(x, period - d, 0))


def _route(logits_t, bias):
    n = N_EXPERTS
    scores = jax.nn.sigmoid(logits_t)
    biased = scores + bias
    row = lax.broadcasted_iota(jnp.int32, biased.shape, 0)
    pos = row % EXPERTS_PER_GROUP
    rank = jnp.zeros(biased.shape, jnp.int32)
    for d in range(1, EXPERTS_PER_GROUP):
        other = _within(biased, d, EXPERTS_PER_GROUP, n)
        other_pos = (pos + d) % EXPERTS_PER_GROUP
        ahead = (other > biased) | ((other == biased) & (other_pos < pos))
        rank = rank + ahead.astype(jnp.int32)
    top2 = rank < 2
    kept = jnp.where(top2, biased, 0.0)
    gscore = kept
    for d in range(1, EXPERTS_PER_GROUP):
        gscore = gscore + _within(kept, d, EXPERTS_PER_GROUP, n)
    grp = row // EXPERTS_PER_GROUP
    win = jnp.ones(biased.shape, jnp.bool_)
    for d in range(1, N_GROUPS):
        other = pltpu.roll(gscore, n - d * EXPERTS_PER_GROUP, 0)
        other_grp = (grp + d) % N_GROUPS
        win = win & ((other < gscore) | ((other == gscore) & (other_grp > grp)))
    sel = top2 & win
    w = jnp.where(sel, scores, 0.0)
    return w / jnp.sum(w, axis=0, keepdims=True), sel


def _pack_pairs(v):
    n = v.shape[1] // 2
    vb = v.astype(BF16).astype(F32)
    hi = pltpu.bitcast(vb[:, :n], jnp.int32)
    lo = pltpu.bitcast(vb[:, n:], jnp.int32)
    return hi | lax.shift_right_logical(lo, 16)


def _unpack_pairs(w):
    hi = pltpu.bitcast(w & jnp.int32(-65536), F32)
    lo = pltpu.bitcast(lax.shift_left(w, 16), F32)
    return jnp.concatenate([hi, lo], axis=1)


R_E0, R_E1, R_RANK0, R_RANK1, R_W0, R_W1 = range(6)


def _mixer_out_body(oa_ref, ob_ref, ga_ref, gb_ref, x_ref, wba_ref, wbb_ref, wout_ref, gffn_ref, wrh_ref, wrl_ref,
                    br_ref, tri_ref, h_ref, tp_ref, route_ref, count_ref):
    @pl.when(pl.program_id(0) == 0)
    def _():
        count_ref[...] = jnp.zeros_like(count_ref)

    ma = _dot_tn(oa_ref[...], wba_ref[...])
    mb = _dot_tn(ob_ref[...], wbb_ref[...])
    merged = ga_ref[...].astype(F32) * ma + gb_ref[...].astype(F32) * mb
    h = x_ref[...] + _dot(merged.astype(BF16), wout_ref[...])
    h_ref[...] = h
    t = _rms(h) * gffn_ref[...]
    t_hi, t_lo = _split_bf16(t)
    tp_ref[...] = _pack_pairs(t)
    wrh, wrl = wrh_ref[...], wrl_ref[...]
    logits_t = _dot_nt(wrh, t_hi) + _dot_nt(wrh, t_lo) + _dot_nt(wrl, t_hi)
    gates_t, sel = _route(logits_t, br_ref[...])

    tm = gates_t.shape[1]
    onehot = jnp.where(sel, 1.0, 0.0)
    before = _dot(onehot.astype(BF16), tri_ref[...])
    rank_t = count_ref[:, 0:1] + before
    count_ref[...] = count_ref[...] + jnp.sum(onehot, axis=1, keepdims=True)

    row = lax.broadcasted_iota(jnp.int32, sel.shape, 0).astype(F32)
    e0 = jnp.min(jnp.where(sel, row, float(N_EXPERTS)), axis=0, keepdims=True)
    e1 = jnp.max(jnp.where(sel, row, -1.0), axis=0, keepdims=True)
    pick = lambda v, e: jnp.sum(jnp.where(sel & (row == e), v, 0.0), axis=0, keepdims=True)
    rec = jnp.concatenate([e0, e1, pick(rank_t, e0), pick(rank_t, e1), pick(gates_t, e0), pick(gates_t, e1),
                           jnp.zeros((LANES - 6, tm), F32)], axis=0)
    route_ref[...] = rec.T


def _mixer_out(oa, ob, ga, gb, x, lw, l, tm):
    t = x.shape[0]
    row = lambda n: pl.BlockSpec((tm, n), lambda i: (i, 0))
    full = lambda a: pl.BlockSpec((None,) + a.shape[1:], lambda i: (l,) + (0,) * (a.ndim - 1))
    const = lambda a: pl.BlockSpec(a.shape, lambda i: (0,) * a.ndim)
    col = lambda n: pl.BlockSpec((n, tm), lambda i: (0, i))
    tri = (jnp.arange(tm)[:, None] < jnp.arange(tm)[None, :]).astype(BF16)
    ins = [oa, ob, ga, gb, x, lw["wba"], lw["wbb"], lw["wout"], lw["gffn"], lw["wrh"], lw["wrl"], lw["br"], tri]
    in_specs = [col(A_Q_COLS), col(B_O_COLS), row(D_MODEL), row(D_MODEL), row(D_MODEL),
                full(lw["wba"]), full(lw["wbb"]), full(lw["wout"]), full(lw["gffn"]),
                const(lw["wrh"]), const(lw["wrl"]), const(lw["br"]), const(tri)]
    return pl.pallas_call(
        _mixer_out_body,
        grid=(t // tm,),
        in_specs=in_specs,
        out_specs=[row(D_MODEL), row(D_MODEL // 2), row(LANES), pl.BlockSpec((N_EXPERTS, LANES), lambda i: (0, 0))],
        out_shape=[jax.ShapeDtypeStruct((t, D_MODEL), F32), jax.ShapeDtypeStruct((t, D_MODEL // 2), jnp.int32),
                   jax.ShapeDtypeStruct((t, LANES), F32), jax.ShapeDtypeStruct((N_EXPERTS, LANES), F32)],
        compiler_params=_cparams(("arbitrary",)),
        name="mixer_out",
    )(*ins)


def _sc_plan(nrows):
    info = plsc.get_sparse_core_info()
    workers = info.num_cores * info.num_subcores
    per_worker = nrows // workers
    chunk = min(SC_CHUNK, per_worker)
    assert per_worker * workers == nrows and per_worker % chunk == 0 and chunk % 8 == 0, (nrows, workers, chunk)
    return info.num_cores, per_worker, chunk


def _sc_mesh():
    return plsc.VectorSubcoreMesh(core_axis_name="core", subcore_axis_name="subcore")


def _scatter_rows(x, idx0, idx1, nrows):
    t, d = x.shape
    ncores, per_worker, chunk = _sc_plan(t)

    @functools.partial(
        pl.kernel, out_type=jax.ShapeDtypeStruct((nrows, d), x.dtype), mesh=_sc_mesh(), name="moe_dispatch",
        scratch_types=[pltpu.VMEM((chunk,), jnp.int32), pltpu.VMEM((chunk,), jnp.int32), pltpu.VMEM((chunk, d), x.dtype)])
    def run(x_hbm, i0_hbm, i1_hbm, o_hbm, i0_v, i1_v, rows_v):
        worker = lax.axis_index("subcore") * ncores + lax.axis_index("core")

        @pl.loop(0, per_worker // chunk)
        def _(c):
            base = pl.multiple_of(worker * per_worker + c * chunk, chunk)
            pltpu.sync_copy(x_hbm.at[pl.ds(base, chunk)], rows_v)
            pltpu.sync_copy(i0_hbm.at[pl.ds(base, chunk)], i0_v)
            pltpu.sync_copy(i1_hbm.at[pl.ds(base, chunk)], i1_v)
            pltpu.sync_copy(rows_v, o_hbm.at[i0_v])
            pltpu.sync_copy(rows_v, o_hbm.at[i1_v])

    return run(x, idx0, idx1)


def _gather_rows(table, idx):
    m = idx.shape[0]
    d = table.shape[1]
    ncores, per_worker, chunk = _sc_plan(m)

    @functools.partial(
        pl.kernel, out_type=jax.ShapeDtypeStruct((m, d), table.dtype), mesh=_sc_mesh(), name="moe_collect",
        scratch_types=[pltpu.VMEM((chunk,), jnp.int32), pltpu.VMEM((chunk, d), table.dtype)])
    def run(x_hbm, i_hbm, o_hbm, i_v, rows_v):
        worker = lax.axis_index("subcore") * ncores + lax.axis_index("core")

        @pl.loop(0, per_worker // chunk)
        def _(c):
            base = pl.multiple_of(worker * per_worker + c * chunk, chunk)
            pltpu.sync_copy(i_hbm.at[pl.ds(base, chunk)], i_v)
            pltpu.sync_copy(x_hbm.at[i_v], rows_v)
            pltpu.sync_copy(rows_v, o_hbm.at[pl.ds(base, chunk)])

    return run(table, idx)


def _routing_tables(route, counts, tr):
    t = route.shape[0]
    cnt = counts[:, 0].astype(jnp.int32)
    seg_end = jnp.cumsum(cnt)
    seg_start = seg_end - cnt
    e0 = route[:, R_E0].astype(jnp.int32)
    e1 = route[:, R_E1].astype(jnp.int32)
    pos0 = seg_start[e0] + route[:, R_RANK0].astype(jnp.int32)
    pos1 = seg_start[e1] + route[:, R_RANK1].astype(jnp.int32)

    n_tiles = 2 * t // tr
    n_visits = n_tiles + N_EXPERTS - 1
    first_tile = seg_start // tr
    last_tile = jnp.maximum(seg_end - 1, 0) // tr
    visits = jnp.where(cnt > 0, last_tile - first_tile + 1, 0)
    visit_end = jnp.cumsum(visits)
    visit_start = visit_end - visits
    g = jnp.arange(n_visits, dtype=jnp.int32)
    valid = g < visit_end[-1]
    ex = jnp.minimum(jnp.searchsorted(visit_end, g, side="right").astype(jnp.int32), N_EXPERTS - 1)
    tile = first_tile[ex] + g - visit_start[ex]
    lo = jnp.clip(seg_start[ex] - tile * tr, 0, tr)
    hi = jnp.clip(seg_end[ex] - tile * tr, 0, tr)
    last_ex = jnp.max(jnp.where(cnt > 0, jnp.arange(N_EXPERTS), 0)).astype(jnp.int32)
    tile = jnp.where(valid, tile, n_tiles - 1)
    ex = jnp.where(valid, ex, last_ex)
    lo = jnp.where(valid, lo, 0)
    hi = jnp.where(valid, hi, 0)
    first = jnp.concatenate([jnp.ones((1,), jnp.int32), (tile[1:] != tile[:-1]).astype(jnp.int32)])
    first = jnp.where(valid, first, 0)
    return pos0, pos1, (tile, ex, lo, hi, first)


def _experts_body(tile_ref, ex_ref, lo_ref, hi_ref, first_ref, xs_ref, wg_ref, wu_ref, wd_ref, ys_ref, acc_ref):
    g = pl.program_id(0)
    lo, hi = lo_ref[g], hi_ref[g]

    @pl.when(hi > lo)
    def _():
        x = _unpack_pairs(xs_ref[...]).astype(BF16)
        a = jax.nn.silu(_dot(x, wg_ref[...])) * _dot(x, wu_ref[...])
        row = lax.broadcasted_iota(jnp.int32, (a.shape[0], 1), 0)
        a = jnp.where((row >= lo) & (row < hi), a, 0.0)
        y = _dot(a.astype(BF16), wd_ref[...])

        @pl.when(first_ref[g] == 1)
        def _():
            acc_ref[...] = y

        @pl.when(first_ref[g] == 0)
        def _():
            acc_ref[...] += y

        ys_ref[...] = _pack_pairs(acc_ref[...])


def _experts(xs, visits, lw, l, tr):
    n_visits = visits[0].shape[0]
    rows = pl.BlockSpec((tr, D_MODEL // 2), lambda g, tile, ex, lo, hi, first: (tile[g], 0))
    wspec = lambda a: pl.BlockSpec((None, None) + a.shape[2:], lambda g, tile, ex, lo, hi, first: (l, ex[g], 0, 0))
    return pl.pallas_call(
        _experts_body,
        grid_spec=pltpu.PrefetchScalarGridSpec(
            num_scalar_prefetch=5,
            grid=(n_visits,),
            in_specs=[rows, wspec(lw["wg"]), wspec(lw["wu"]), wspec(lw["wd"])],
            out_specs=rows,
            scratch_shapes=[pltpu.VMEM((tr, D_MODEL), F32)],
        ),
        out_shape=jax.ShapeDtypeStruct(xs.shape, jnp.int32),
        compiler_params=_cparams(("arbitrary",)),
        name="experts",
    )(*visits, xs, lw["wg"], lw["wu"], lw["wd"])


def _combine_body(h_ref, y0_ref, y1_ref, route_ref, o_ref):
    r = route_ref[...]
    o_ref[...] = (h_ref[...] + r[:, R_W0:R_W0 + 1] * _unpack_pairs(y0_ref[...])
                  + r[:, R_W1:R_W1 + 1] * _unpack_pairs(y1_ref[...]))


def _combine(h, y, route, tm):
    t = h.shape[0]
    row = lambda n: pl.BlockSpec((tm, n), lambda i: (i, 0))
    second = pl.BlockSpec((tm, D_MODEL // 2), lambda i: (i + t // tm, 0))
    return pl.pallas_call(
        _combine_body,
        grid=(t // tm,),
        in_specs=[row(D_MODEL), row(D_MODEL // 2), second, row(LANES)],
        out_specs=row(D_MODEL),
        out_shape=jax.ShapeDtypeStruct(h.shape, F32),
        compiler_params=_cparams(("parallel",)),
        name="combine",
    )(h, y, y, route)


def _moe(h, tp, route, counts, lw, l, tm, tr):
    t = h.shape[0]
    pos0, pos1, visits = _routing_tables(route, counts, tr)
    xs = _scatter_rows(tp, pos0, pos1, 2 * t)
    ys = _experts(xs, visits, lw, l, tr)
    y = _gather_rows(ys, jnp.concatenate([pos0, pos1]))
    return _combine(h, y, route, tm)


def _final_norm_body(x_ref, g_ref, o_ref):
    o_ref[...] = _rms(x_ref[...]) * g_ref[...]


def _final_norm(x, g, tm):
    t = x.shape[0]
    row = pl.BlockSpec((tm, D_MODEL), lambda i: (i, 0))
    return pl.pallas_call(
        _final_norm_body,
        grid=(t // tm,),
        in_specs=[row, pl.BlockSpec(g.shape, lambda i: (0, 0))],
        out_specs=row,
        out_shape=jax.ShapeDtypeStruct(x.shape, F32),
        compiler_params=_cparams(("parallel",)),
        name="final_norm",
    )(x, g)


def _rope_angles(seq_len, rot_dim):
    rows = seq_len // GRID_W
    row = jnp.broadcast_to(jnp.arange(rows)[:, None], (rows, GRID_W)).reshape(-1).astype(F32)
    col = jnp.broadcast_to(jnp.arange(GRID_W)[None, :], (rows, GRID_W)).reshape(-1).astype(F32)
    axis_dim = rot_dim // 2
    inv_freq = jnp.power(jnp.float32(ROPE_THETA), -jnp.arange(0, axis_dim, 2, dtype=F32) / axis_dim)
    ang = jnp.concatenate([row[:, None] * inv_freq[None, :], col[:, None] * inv_freq[None, :]], axis=-1)
    return jnp.cos(ang), jnp.sin(ang)


def _rope_tables(nseq, seq_len):
    c, s = _rope_angles(seq_len, A_HEAD_DIM)
    ca = jnp.concatenate([c, c, c, c], axis=-1)
    sa = jnp.concatenate([-s, s, -s, s], axis=-1)
    c, s = _rope_angles(seq_len, B_ROPE)
    ones = jnp.ones((seq_len, B_NOPE), F32)
    zeros = jnp.zeros((seq_len, B_NOPE), F32)
    tail = LANES - B_NOPE - B_ROPE
    cb = jnp.concatenate([ones, c, c, ones[:, :tail]], axis=-1)
    sb = jnp.concatenate([zeros, -s, s, zeros[:, :tail]], axis=-1)
    return tuple(jnp.tile(a, (nseq, 1)) for a in (ca, sa, cb, sb))


def _prepare_weights(norm_mix, w_in, a_q_norm, a_k_norm, b_q_norm, b_kv_norm, w_q_up, w_kv_up, w_branch_a,
                     w_branch_b, w_out, norm_ffn, w_router, b_router, w_gate, w_up, w_down):
    depth = w_in.shape[0]
    order = jnp.array(A_HEAD_ORDER)
    parts = []
    start = 0
    for n in (A_Q_COLS, A_KV_COLS, A_KV_COLS, B_Q_RANK, B_KV_RANK, B_ROPE, D_MODEL, D_MODEL):
        parts.append(w_in[..., start:start + n])
        start += n
    qa, ka, va, cq, ckv, kr, ga, gb = parts
    qa = qa.reshape(depth, D_MODEL, A_HEADS, A_HEAD_DIM)[:, :, order].reshape(depth, D_MODEL, A_Q_COLS)
    kr = jnp.pad(kr, ((0, 0), (0, 0), (B_NOPE, LANES - B_NOPE - B_ROPE)))
    win = jnp.concatenate([qa, ka, va, cq, ckv, kr, ga, gb], axis=-1).astype(BF16)

    wq = w_q_up.reshape(depth, B_Q_RANK, B_HEADS, B_QK_DIM)
    wq = jnp.pad(wq, ((0, 0), (0, 0), (0, 0), (0, LANES - B_QK_DIM))).reshape(depth, B_Q_RANK, B_PAD_COLS)
    wkv = w_kv_up.reshape(depth, B_KV_RANK, B_HEADS, B_NOPE + B_V)
    wk = jnp.pad(wkv[..., :B_NOPE], ((0, 0), (0, 0), (0, 0), (0, LANES - B_NOPE))).reshape(depth, B_KV_RANK, B_PAD_COLS)
    wv = wkv[..., B_NOPE:].reshape(depth, B_KV_RANK, B_O_COLS)
    wba = w_branch_a.reshape(depth, A_HEADS, A_HEAD_DIM, D_MODEL)[:, order].reshape(depth, A_Q_COLS, D_MODEL)

    group = jnp.arange(A_Q_COLS) // A_HEAD_DIM
    bd = (group[:, None] == group[None, :]).astype(BF16)
    wr_t = w_router.T
    wrh, wrl = _split_bf16(wr_t)
    vec = lambda a: a[:, None, :]
    return dict(
        gmix=vec(norm_mix), win=win,
        gq=vec(jnp.tile(a_q_norm, (1, A_HEADS)) * (A_HEAD_DIM ** -0.5 * LOG2E)), gk=vec(jnp.tile(a_k_norm, (1, A_KV_HEADS))),
        gbq=vec(b_q_norm), gbkv=vec(b_kv_norm),
        wq=wq.astype(BF16), wk=wk.astype(BF16), wv=wv.astype(BF16), bd=bd,
        wba=wba.astype(BF16), wbb=w_branch_b.astype(BF16), wout=w_out.astype(BF16), gffn=vec(norm_ffn),
        wrh=wrh, wrl=wrl, br=b_router[:, None].astype(F32),
        wg=w_gate.astype(BF16), wu=w_up.astype(BF16), wd=w_down.astype(BF16),
    )


def _pick(n, candidates):
    for c in candidates:
        if n % c == 0:
            return c
    raise ValueError(f"no tile in {candidates} divides {n}")


def _trunk(x3, lw, norm_final):
    nseq, s, _ = x3.shape
    t = nseq * s
    x = x3.reshape(t, D_MODEL)
    tabs = _rope_tables(nseq, s)
    depth = lw["win"].shape[0]
    tm = _pick(t, (256,))
    tq = _pick(s, (256,))
    tk = _pick(s, (512,))
    for l in range(depth):
        qa, ka, va, qb, kb, vb, ga, gb = _mixer_in(x, lw, l, tabs, tm)
        oa = _attention_a(qa, ka, va, nseq, s, tq, tk)
        ob = _attention_b(qb, kb, vb, nseq, s, tq, tk)
        h, tp, route, counts = _mixer_out(oa, ob, ga, gb, x, lw, l, tm)
        x = _moe(h, tp, route, counts, lw, l, tm, EXPERT_ROWS)
    return _final_norm(x, norm_final[None, :], tm).reshape(x3.shape)


def kernel(x_prompt, x_sample, norm_mix, w_in, a_q_norm, a_k_norm, b_q_norm, b_kv_norm, w_q_up, w_kv_up, w_branch_a,
           w_branch_b, w_out, norm_ffn, w_router, b_router, w_gate, w_up, w_down, norm_final):
    lw = _prepare_weights(norm_mix, w_in, a_q_norm, a_k_norm, b_q_norm, b_kv_norm, w_q_up, w_kv_up, w_branch_a,
                          w_branch_b, w_out, norm_ffn, w_router, b_router, w_gate, w_up, w_down)
    return _trunk(x_prompt, lw, norm_final), _trunk(x_sample, lw, norm_final)
```

```python
import functools

import jax
import jax.numpy as jnp
from jax import lax
from jax.experimental import pallas as pl
from jax.experimental.pallas import tpu as pltpu
from jax.experimental.pallas import tpu_sc as plsc

F32 = jnp.float32
BF16 = jnp.bfloat16

D_MODEL = 1024
GRID_W = 64
ROPE_THETA = 10000.0
EPS = 1e-6
A_HEADS = 8
A_KV_HEADS = 2
A_HEAD_DIM = 64
B_HEADS = 8
B_Q_RANK = 384
B_KV_RANK = 256
B_NOPE = 64
B_ROPE = 32
B_V = 64
N_EXPERTS = 16
N_GROUPS = 4
EXPERTS_PER_GROUP = N_EXPERTS // N_GROUPS
D_EXPERT = 512
A_Q_COLS = A_HEADS * A_HEAD_DIM
A_KV_COLS = A_KV_HEADS * A_HEAD_DIM
B_QK_DIM = B_NOPE + B_ROPE
B_O_COLS = B_HEADS * B_V

LANES = 128
LOG2E = 1.4426950408889634

C_QA = 0
C_KA = C_QA + A_Q_COLS
C_VA = C_KA + A_KV_COLS
C_CQ = C_VA + A_KV_COLS
C_CKV = C_CQ + B_Q_RANK
C_KR = C_CKV + B_KV_RANK
C_GA = C_KR + LANES
C_GB = C_GA + D_MODEL
C_END = C_GB + D_MODEL
B_PAD_COLS = B_HEADS * LANES

A_HEAD_ORDER = (0, 4, 1, 5, 2, 6, 3, 7)

VMEM_LIMIT = 56 * 1024 * 1024

ATTN_GROUP = 8
ATTN_S_BUFS = 3
ATTN_P_BUFS = 2
ATTN_SLAB = 64
EXPERT_ROWS = 512
SC_CHUNK = 128


def _cparams(sem):
    return pltpu.CompilerParams(dimension_semantics=sem, vmem_limit_bytes=VMEM_LIMIT)


def _dot(a, b):
    return jnp.dot(a, b, preferred_element_type=F32)


def _dot_nt(a, b):
    return lax.dot_general(a, b, (((1,), (1,)), ((), ())), preferred_element_type=F32)


def _dot_tn(a, b):
    return lax.dot_general(a, b, (((0,), (0,)), ((), ())), preferred_element_type=F32)


def _rms(x):
    return x * lax.rsqrt(jnp.mean(x * x, axis=-1, keepdims=True) + EPS)


def _split_bf16(x):
    hi = x.astype(BF16)
    lo = (x - hi.astype(F32)).astype(BF16)
    return hi, lo


def _group_mean_sq(v, bd):
    hi, lo = _split_bf16(v * v)
    return (_dot(hi, bd) + _dot(lo, bd)) * (1.0 / A_HEAD_DIM)


def _rope_a(v, cos, sin):
    n = v.shape[-1]
    lane = lax.broadcasted_iota(jnp.int32, v.shape, 1)
    low = (lane % A_HEAD_DIM) < (A_HEAD_DIM // 2)
    swapped = jnp.where(low, pltpu.roll(v, n - A_HEAD_DIM // 2, 1), pltpu.roll(v, A_HEAD_DIM // 2, 1))
    return v * cos + swapped * sin


def _rope_b(v, cos, sin):
    n = v.shape[-1]
    lane = lax.broadcasted_iota(jnp.int32, v.shape, 1)
    low = (lane % LANES) < (B_NOPE + B_ROPE // 2)
    swapped = jnp.where(low, pltpu.roll(v, n - B_ROPE // 2, 1), pltpu.roll(v, B_ROPE // 2, 1))
    return v * cos + swapped * sin


def _mixer_in_body(x_ref, gmix_ref, win_ref, ca_ref, sa_ref, cb_ref, sb_ref, gq_ref, gk_ref, gbq_ref, gbkv_ref,
                   wq_ref, wk_ref, wv_ref, bd_ref,
                   qat_ref, ka_ref, vat_ref, qbt_ref, kb_ref, vbt_ref, ga_ref, gb_ref):
    hb = (_rms(x_ref[...]) * gmix_ref[...]).astype(BF16)
    u = _dot(hb, win_ref[:, C_QA:C_GA])
    bd = bd_ref[...]
    ca, sa = ca_ref[...], sa_ref[...]
    cb, sb = cb_ref[...], sb_ref[...]

    qa = u[:, C_QA:C_KA]
    qa = qa * lax.rsqrt(_group_mean_sq(qa, bd) + EPS) * gq_ref[...]
    qa = _rope_a(qa, jnp.concatenate([ca] * (A_Q_COLS // LANES), axis=1),
                 jnp.concatenate([sa] * (A_Q_COLS // LANES), axis=1))
    qat_ref[...] = qa.T.astype(BF16)
    ka = u[:, C_KA:C_VA]
    ka = ka * lax.rsqrt(_group_mean_sq(ka, bd[:A_KV_COLS, :A_KV_COLS]) + EPS) * gk_ref[...]
    ka_ref[...] = _rope_a(ka, ca, sa).astype(BF16)
    vat_ref[...] = u[:, C_VA:C_CQ].T.astype(BF16)

    cq = (_rms(u[:, C_CQ:C_CKV]) * gbq_ref[...]).astype(BF16)
    qb = _dot(cq, wq_ref[...])
    qb = _rope_b(qb, jnp.concatenate([cb] * B_HEADS, axis=1), jnp.concatenate([sb] * B_HEADS, axis=1))
    qbt_ref[...] = (qb * (B_QK_DIM ** -0.5 * LOG2E)).T.astype(BF16)
    ckv = (_rms(u[:, C_CKV:C_KR]) * gbkv_ref[...]).astype(BF16)
    kr = _rope_b(u[:, C_KR:C_GA], cb, sb)
    kb_ref[...] = (_dot(ckv, wk_ref[...]) + jnp.concatenate([kr] * B_HEADS, axis=1)).astype(BF16)
    vbt_ref[...] = _dot(ckv, wv_ref[...]).T.astype(BF16)

    g = jax.nn.sigmoid(_dot(hb, win_ref[:, C_GA:C_END]))
    ga_ref[...] = g[:, :D_MODEL].astype(BF16)
    gb_ref[...] = g[:, D_MODEL:].astype(BF16)


def _mixer_in(x, lw, l, tabs, tm):
    t = x.shape[0]
    row = lambda n: pl.BlockSpec((tm, n), lambda i: (i, 0))
    full = lambda a: pl.BlockSpec((None,) + a.shape[1:], lambda i: (l,) + (0,) * (a.ndim - 1))
    const = lambda a: pl.BlockSpec(a.shape, lambda i: (0,) * a.ndim)
    ca, sa, cb, sb = tabs
    ins = [x, lw["gmix"], lw["win"], ca, sa, cb, sb, lw["gq"], lw["gk"], lw["gbq"], lw["gbkv"],
           lw["wq"], lw["wk"], lw["wv"], lw["bd"]]
    in_specs = [row(D_MODEL), full(lw["gmix"]), full(lw["win"]), row(LANES), row(LANES), row(LANES), row(LANES),
                full(lw["gq"]), full(lw["gk"]), full(lw["gbq"]), full(lw["gbkv"]),
                full(lw["wq"]), full(lw["wk"]), full(lw["wv"]), const(lw["bd"])]
    col = lambda n: pl.BlockSpec((n, tm), lambda i: (0, i))
    widths = [A_Q_COLS, A_KV_COLS, A_KV_COLS, B_PAD_COLS, B_PAD_COLS, B_O_COLS, D_MODEL, D_MODEL]
    transposed = [True, False, True, True, False, True, False, False]
    return pl.pallas_call(
        _mixer_in_body,
        grid=(t // tm,),
        in_specs=in_specs,
        out_specs=[col(n) if tr else row(n) for n, tr in zip(widths, transposed)],
        out_shape=[jax.ShapeDtypeStruct((n, t) if tr else (t, n), BF16) for n, tr in zip(widths, transposed)],
        compiler_params=_cparams(("parallel",)),
        name="mixer_in",
    )(*ins)


def _flash_pair(q0t, q1t, k0_ref, k1_ref, vt_ref, s_refs, p_refs, acc_ref, tk):
    tq = q0t.shape[1]
    nk = vt_ref.shape[1] // tk
    group = min(nk, ATTN_GROUP)
    ns = len(s_refs)
    qt2 = jnp.concatenate([q0t, q1t], axis=1)

    def scores(j, s_ref):
        off = pl.multiple_of(j * tk, tk)
        if k1_ref is None:
            s_ref[...] = _dot(k0_ref[pl.ds(off, tk), :], qt2)
        else:
            s_ref[:, :tq] = _dot(k0_ref[pl.ds(off, tk), :], q0t)
            s_ref[:, tq:] = _dot(k1_ref[pl.ds(off, tk), :], q1t)

    def softmax(s_ref, p_ref, m, l):
        slabs = [pl.ds(r, ATTN_SLAB) for r in range(0, tk, ATTN_SLAB)]
        fold = lambda x: x.reshape(ATTN_SLAB // 8, 8, 2 * tq)
        mx = fold(s_ref[slabs[0], :]).max(axis=0)
        for sl in slabs[1:]:
            mx = jnp.maximum(mx, fold(s_ref[sl, :]).max(axis=0))
        m_new = jnp.maximum(m, jnp.max(mx, axis=0, keepdims=True))
        alpha = jnp.exp2(m - m_new)
        l = alpha * l
        for sl in slabs:
            p = jnp.exp2(s_ref[sl, :] - m_new)
            p_ref[sl, :] = p.astype(BF16)
            l = l + fold(p).sum(axis=0)
        return m_new, l, alpha

    def values(j, p_ref, alpha):
        off = pl.multiple_of(j * tk, tk)
        acc_ref[...] = alpha * acc_ref[...] + _dot(vt_ref[:, pl.ds(off, tk)], p_ref[...])

    def one_group(g, carry):
        m, l = carry
        base = g * group
        for c in range(min(ns - 1, group)):
            scores(base + c, s_refs[c % ns])
        for c in range(group):
            if c + ns - 1 < group:
                scores(base + c + ns - 1, s_refs[(c + ns - 1) % ns])
            m, l, alpha = softmax(s_refs[c % ns], p_refs[c % len(p_refs)], m, l)
            values(base + c, p_refs[c % len(p_refs)], alpha)
        return m, l

    acc_ref[...] = jnp.zeros_like(acc_ref)
    carry = (jnp.full((1, 2 * tq), -jnp.inf, F32), jnp.zeros((8, 2 * tq), F32))
    if nk == group:
        _, l = one_group(0, carry)
    else:
        _, l = lax.fori_loop(0, nk // group, one_group, carry)
    o = acc_ref[...] / jnp.sum(l, axis=0, keepdims=True)
    return jnp.concatenate([o[:B_V, :tq], o[B_V:, tq:]], axis=0)


def _attn_a_body(qt_ref, k_ref, vt_ref, o_ref, *scratch, tk):
    qt = qt_ref[...]
    zero = jnp.zeros((A_HEAD_DIM, qt.shape[1]), BF16)
    q0t = jnp.concatenate([qt[:A_HEAD_DIM], zero], axis=0)
    q1t = jnp.concatenate([zero, qt[A_HEAD_DIM:]], axis=0)
    s_refs, p_refs, acc_ref = scratch[:ATTN_S_BUFS], scratch[ATTN_S_BUFS:-1], scratch[-1]
    o_ref[...] = _flash_pair(q0t, q1t, k_ref, None, vt_ref, s_refs, p_refs, acc_ref, tk).astype(BF16)


def _attn_b_body(q0t_ref, q1t_ref, k0_ref, k1_ref, vt_ref, o_ref, *scratch, tk):
    s_refs, p_refs, acc_ref = scratch[:ATTN_S_BUFS], scratch[ATTN_S_BUFS:-1], scratch[-1]
    o_ref[...] = _flash_pair(q0t_ref[...], q1t_ref[...], k0_ref, k1_ref, vt_ref, s_refs, p_refs, acc_ref,
                             tk).astype(BF16)


def _attn_scratch(tq, tk):
    return ([pltpu.VMEM((tk, 2 * tq), F32)] * ATTN_S_BUFS + [pltpu.VMEM((tk, 2 * tq), BF16)] * ATTN_P_BUFS
            + [pltpu.VMEM((LANES, 2 * tq), F32)])


def _attention_a(qat, ka, vat, nseq, s, tq, tk):
    nq = s // tq
    qspec = pl.BlockSpec((LANES, tq), lambda b, h, i: (h, b * nq + i))
    kspec = pl.BlockSpec((s, LANES), lambda b, h, i: (b, 0))
    vspec = pl.BlockSpec((LANES, s), lambda b, h, i: (0, b))
    return pl.pallas_call(
        functools.partial(_attn_a_body, tk=tk),
        grid=(nseq, A_Q_COLS // LANES, nq),
        in_specs=[qspec, kspec, vspec],
        out_specs=qspec,
        out_shape=jax.ShapeDtypeStruct(qat.shape, BF16),
        scratch_shapes=_attn_scratch(tq, tk),
        compiler_params=_cparams(("parallel", "parallel", "parallel")),
        name="attn_a",
    )(qat, ka, vat)


def _attention_b(qbt, kb, vbt, nseq, s, tq, tk):
    nq = s // tq
    q0 = pl.BlockSpec((LANES, tq), lambda b, h, i: (2 * h, b * nq + i))
    q1 = pl.BlockSpec((LANES, tq), lambda b, h, i: (2 * h + 1, b * nq + i))
    k0 = pl.BlockSpec((s, LANES), lambda b, h, i: (b, 2 * h))
    k1 = pl.BlockSpec((s, LANES), lambda b, h, i: (b, 2 * h + 1))
    v = pl.BlockSpec((LANES, s), lambda b, h, i: (h, b))
    o = pl.BlockSpec((LANES, tq), lambda b, h, i: (h, b * nq + i))
    return pl.pallas_call(
        functools.partial(_attn_b_body, tk=tk),
        grid=(nseq, B_O_COLS // LANES, nq),
        in_specs=[q0, q1, k0, k1, v],
        out_specs=o,
        out_shape=jax.ShapeDtypeStruct(vbt.shape, BF16),
        scratch_shapes=_attn_scratch(tq, tk),
        compiler_params=_cparams(("parallel", "parallel", "parallel")),
        name="attn_b",
    )(qbt, qbt, kb, kb, vbt)


def _within(x, d, period, n):
    row = lax.broadcasted_iota(jnp.int32, x.shape, 0)
    return jnp.where((row % period) + d < period, pltpu.roll(x, n - d, 0), pltpu.roll(x, period - d, 0))


def _route(logits_t, bias):
    n = N_EXPERTS
    scores = jax.nn.sigmoid(logits_t)
    biased = scores + bias
    row = lax.broadcasted_iota(jnp.int32, biased.shape, 0)
    pos = row % EXPERTS_PER_GROUP
    rank = jnp.zeros(biased.shape, jnp.int32)
    for d in range(1, EXPERTS_PER_GROUP):
        other = _within(biased, d, EXPERTS_PER_GROUP, n)
        other_pos = (pos + d) % EXPERTS_PER_GROUP
        ahead = (other > biased) | ((other == biased) & (other_pos < pos))
        rank = rank + ahead.astype(jnp.int32)
    top2 = rank < 2
    kept = jnp.where(top2, biased, 0.0)
    gscore = kept
    for d in range(1, EXPERTS_PER_GROUP):
        gscore = gscore + _within(kept, d, EXPERTS_PER_GROUP, n)
    grp = row // EXPERTS_PER_GROUP
    win = jnp.ones(biased.shape, jnp.bool_)
    for d in range(1, N_GROUPS):
        other = pltpu.roll(gscore, n - d * EXPERTS_PER_GROUP, 0)
        other_grp = (grp + d) % N_GROUPS
        win = win & ((other < gscore) | ((other == gscore) & (other_grp > grp)))
    sel = top2 & win
    w = jnp.where(sel, scores, 0.0)
    return w / jnp.sum(w, axis=0, keepdims=True), sel


def _pack_pairs(v):
    n = v.shape[1] // 2
    vb = v.astype(BF16).astype(F32)
    hi = pltpu.bitcast(vb[:, :n], jnp.int32)
    lo = pltpu.bitcast(vb[:, n:], jnp.int32)
    return hi | lax.shift_right_logical(lo, 16)


def _unpack_pairs(w):
    hi = pltpu.bitcast(w & jnp.int32(-65536), F32)
    lo = pltpu.bitcast(lax.shift_left(w, 16), F32)
    return jnp.concatenate([hi, lo], axis=1)


R_E0, R_E1, R_RANK0, R_RANK1, R_W0, R_W1 = range(6)


def _mixer_out_body(oa_ref, ob_ref, ga_ref, gb_ref, x_ref, wba_ref, wbb_ref, wout_ref, gffn_ref, wrh_ref, wrl_ref,
                    br_ref, tri_ref, h_ref, tp_ref, route_ref, route_t_ref, count_ref):
    @pl.when(pl.program_id(0) == 0)
    def _():
        count_ref[...] = jnp.zeros_like(count_ref)

    ma = _dot_tn(oa_ref[...], wba_ref[...])
    mb = _dot_tn(ob_ref[...], wbb_ref[...])
    merged = ga_ref[...].astype(F32) * ma + gb_ref[...].astype(F32) * mb
    h = x_ref[...] + _dot(merged.astype(BF16), wout_ref[...])
    h_ref[...] = h
    t = _rms(h) * gffn_ref[...]
    t_hi, t_lo = _split_bf16(t)
    tp_ref[...] = _pack_pairs(t)
    wrh, wrl = wrh_ref[...], wrl_ref[...]
    logits_t = _dot_nt(wrh, t_hi) + _dot_nt(wrh, t_lo) + _dot_nt(wrl, t_hi)
    gates_t, sel = _route(logits_t, br_ref[...])

    tm = gates_t.shape[1]
    onehot = jnp.where(sel, 1.0, 0.0)
    before = _dot(onehot.astype(BF16), tri_ref[...])
    rank_t = count_ref[:, 0:1] + before
    count_ref[...] = count_ref[...] + jnp.sum(onehot, axis=1, keepdims=True)

    row = lax.broadcasted_iota(jnp.int32, sel.shape, 0).astype(F32)
    e0 = jnp.min(jnp.where(sel, row, float(N_EXPERTS)), axis=0, keepdims=True)
    e1 = jnp.max(jnp.where(sel, row, -1.0), axis=0, keepdims=True)
    pick = lambda v, e: jnp.sum(jnp.where(sel & (row == e), v, 0.0), axis=0, keepdims=True)
    rec = jnp.concatenate([e0, e1, pick(rank_t, e0), pick(rank_t, e1), pick(gates_t, e0), pick(gates_t, e1),
                           jnp.zeros((LANES - 6, tm), F32)], axis=0)
    route_ref[...] = rec.T
    route_t_ref[...] = rec[:8]


def _mixer_out(oa, ob, ga, gb, x, lw, l, tm):
    t = x.shape[0]
    row = lambda n: pl.BlockSpec((tm, n), lambda i: (i, 0))
    full = lambda a: pl.BlockSpec((None,) + a.shape[1:], lambda i: (l,) + (0,) * (a.ndim - 1))
    const = lambda a: pl.BlockSpec(a.shape, lambda i: (0,) * a.ndim)
    col = lambda n: pl.BlockSpec((n, tm), lambda i: (0, i))
    tri = (jnp.arange(tm)[:, None] < jnp.arange(tm)[None, :]).astype(BF16)
    ins = [oa, ob, ga, gb, x, lw["wba"], lw["wbb"], lw["wout"], lw["gffn"], lw["wrh"], lw["wrl"], lw["br"], tri]
    in_specs = [col(A_Q_COLS), col(B_O_COLS), row(D_MODEL), row(D_MODEL), row(D_MODEL),
                full(lw["wba"]), full(lw["wbb"]), full(lw["wout"]), full(lw["gffn"]),
                const(lw["wrh"]), const(lw["wrl"]), const(lw["br"]), const(tri)]
    return pl.pallas_call(
        _mixer_out_body,
        grid=(t // tm,),
        in_specs=in_specs,
        out_specs=[row(D_MODEL), row(D_MODEL // 2), row(LANES), col(8),
                   pl.BlockSpec((N_EXPERTS, LANES), lambda i: (0, 0))],
        out_shape=[jax.ShapeDtypeStruct((t, D_MODEL), F32), jax.ShapeDtypeStruct((t, D_MODEL // 2), jnp.int32),
                   jax.ShapeDtypeStruct((t, LANES), F32), jax.ShapeDtypeStruct((8, t), F32),
                   jax.ShapeDtypeStruct((N_EXPERTS, LANES), F32)],
        compiler_params=_cparams(("arbitrary",)),
        name="mixer_out",
    )(*ins)


def _sc_plan(nrows):
    info = plsc.get_sparse_core_info()
    workers = info.num_cores * info.num_subcores
    per_worker = nrows // workers
    chunk = min(SC_CHUNK, per_worker)
    assert per_worker * workers == nrows and per_worker % chunk == 0 and chunk % 8 == 0, (nrows, workers, chunk)
    return info.num_cores, per_worker, chunk


def _sc_mesh():
    return plsc.VectorSubcoreMesh(core_axis_name="core", subcore_axis_name="subcore")


def _scatter_rows(x, idx0, idx1, nrows):
    t, d = x.shape
    ncores, per_worker, chunk = _sc_plan(t)

    @functools.partial(
        pl.kernel, out_type=jax.ShapeDtypeStruct((nrows, d), x.dtype), mesh=_sc_mesh(), name="moe_dispatch",
        scratch_types=[pltpu.VMEM((chunk,), jnp.int32), pltpu.VMEM((chunk,), jnp.int32), pltpu.VMEM((chunk, d), x.dtype)])
    def run(x_hbm, i0_hbm, i1_hbm, o_hbm, i0_v, i1_v, rows_v):
        worker = lax.axis_index("subcore") * ncores + lax.axis_index("core")

        @pl.loop(0, per_worker // chunk)
        def _(c):
            base = pl.multiple_of(worker * per_worker + c * chunk, chunk)
            pltpu.sync_copy(x_hbm.at[pl.ds(base, chunk)], rows_v)
            pltpu.sync_copy(i0_hbm.at[pl.ds(base, chunk)], i0_v)
            pltpu.sync_copy(i1_hbm.at[pl.ds(base, chunk)], i1_v)
            pltpu.sync_copy(rows_v, o_hbm.at[i0_v])
            pltpu.sync_copy(rows_v, o_hbm.at[i1_v])

    return run(x, idx0, idx1)


def _gather_rows(table, idx):
    m = idx.shape[0]
    d = table.shape[1]
    ncores, per_worker, chunk = _sc_plan(m)

    @functools.partial(
        pl.kernel, out_type=jax.ShapeDtypeStruct((m, d), table.dtype), mesh=_sc_mesh(), name="moe_collect",
        scratch_types=[pltpu.VMEM((chunk,), jnp.int32), pltpu.VMEM((chunk, d), table.dtype)])
    def run(x_hbm, i_hbm, o_hbm, i_v, rows_v):
        worker = lax.axis_index("subcore") * ncores + lax.axis_index("core")

        @pl.loop(0, per_worker // chunk)
        def _(c):
            base = pl.multiple_of(worker * per_worker + c * chunk, chunk)
            pltpu.sync_copy(i_hbm.at[pl.ds(base, chunk)], i_v)
            pltpu.sync_copy(x_hbm.at[i_v], rows_v)
            pltpu.sync_copy(rows_v, o_hbm.at[pl.ds(base, chunk)])

    return run(table, idx)


def _routing_tables(route_t, counts, tr):
    t = route_t.shape[1]
    experts = jnp.arange(N_EXPERTS, dtype=jnp.int32)
    cnt = counts[:, 0].astype(jnp.int32)
    seg_end = jnp.cumsum(cnt)
    seg_start = seg_end - cnt
    lookup = lambda table, idx: jnp.sum(jnp.where(idx[None, :] == experts[:, None], table[:, None], 0), axis=0)
    e = route_t[R_E0:R_E1 + 1].astype(jnp.int32).reshape(2 * t)
    rank = route_t[R_RANK0:R_RANK1 + 1].astype(jnp.int32).reshape(2 * t)
    pos = lookup(seg_start, e) + rank

    n_tiles = 2 * t // tr
    n_visits = n_tiles + N_EXPERTS - 1
    first_tile = seg_start // tr
    last_tile = jnp.maximum(seg_end - 1, 0) // tr
    visits = jnp.where(cnt > 0, last_tile - first_tile + 1, 0)
    visit_end = jnp.cumsum(visits)
    visit_start = visit_end - visits
    g = jnp.arange(n_visits, dtype=jnp.int32)
    valid = g < visit_end[-1]
    ex = jnp.minimum(jnp.sum((g[None, :] >= visit_end[:, None]).astype(jnp.int32), axis=0), N_EXPERTS - 1)
    tile = lookup(first_tile - visit_start, ex) + g
    lo = jnp.clip(lookup(seg_start, ex) - tile * tr, 0, tr)
    hi = jnp.clip(lookup(seg_end, ex) - tile * tr, 0, tr)
    last_ex = jnp.max(jnp.where(cnt > 0, experts, 0))
    tile = jnp.where(valid, tile, n_tiles - 1)
    ex = jnp.where(valid, ex, last_ex)
    lo = jnp.where(valid, lo, 0)
    hi = jnp.where(valid, hi, 0)
    first = jnp.concatenate([jnp.ones((1,), jnp.int32), (tile[1:] != tile[:-1]).astype(jnp.int32)])
    first = jnp.where(valid, first, 0)
    return pos, (tile, ex, lo, hi, first)


def _experts_body(tile_ref, ex_ref, lo_ref, hi_ref, first_ref, xs_ref, wg_ref, wu_ref, wd_ref, ys_ref, acc_ref):
    g = pl.program_id(0)
    lo, hi = lo_ref[g], hi_ref[g]

    @pl.when(hi > lo)
    def _():
        x = _unpack_pairs(xs_ref[...]).astype(BF16)
        a = jax.nn.silu(_dot(x, wg_ref[...])) * _dot(x, wu_ref[...])
        row = lax.broadcasted_iota(jnp.int32, (a.shape[0], 1), 0)
        a = jnp.where((row >= lo) & (row < hi), a, 0.0)
        y = _dot(a.astype(BF16), wd_ref[...])

        @pl.when(first_ref[g] == 1)
        def _():
            acc_ref[...] = y

        @pl.when(first_ref[g] == 0)
        def _():
            acc_ref[...] += y

        ys_ref[...] = _pack_pairs(acc_ref[...])


def _experts(xs, visits, lw, l, tr):
    n_visits = visits[0].shape[0]
    rows = pl.BlockSpec((tr, D_MODEL // 2), lambda g, tile, ex, lo, hi, first: (tile[g], 0))
    wspec = lambda a: pl.BlockSpec((None, None) + a.shape[2:], lambda g, tile, ex, lo, hi, first: (l, ex[g], 0, 0))
    return pl.pallas_call(
        _experts_body,
        grid_spec=pltpu.PrefetchScalarGridSpec(
            num_scalar_prefetch=5,
            grid=(n_visits,),
            in_specs=[rows, wspec(lw["wg"]), wspec(lw["wu"]), wspec(lw["wd"])],
            out_specs=rows,
            scratch_shapes=[pltpu.VMEM((tr, D_MODEL), F32)],
        ),
        out_shape=jax.ShapeDtypeStruct(xs.shape, jnp.int32),
        compiler_params=_cparams(("arbitrary",)),
        name="experts",
    )(*visits, xs, lw["wg"], lw["wu"], lw["wd"])


def _combine_body(h_ref, y0_ref, y1_ref, route_ref, gfin_ref, o_ref, *, final):
    r = route_ref[...]
    x = (h_ref[...] + r[:, R_W0:R_W0 + 1] * _unpack_pairs(y0_ref[...])
         + r[:, R_W1:R_W1 + 1] * _unpack_pairs(y1_ref[...]))
    o_ref[...] = _rms(x) * gfin_ref[...] if final else x


def _combine(h, y, route, gfin, tm, final):
    t = h.shape[0]
    row = lambda n: pl.BlockSpec((tm, n), lambda i: (i, 0))
    second = pl.BlockSpec((tm, D_MODEL // 2), lambda i: (i + t // tm, 0))
    return pl.pallas_call(
        functools.partial(_combine_body, final=final),
        grid=(t // tm,),
        in_specs=[row(D_MODEL), row(D_MODEL // 2), second, row(LANES), pl.BlockSpec(gfin.shape, lambda i: (0, 0))],
        out_specs=row(D_MODEL),
        out_shape=jax.ShapeDtypeStruct(h.shape, F32),
        compiler_params=_cparams(("parallel",)),
        name="combine",
    )(h, y, y, route, gfin)


def _moe(h, tp, route, route_t, counts, lw, l, gfin, tm, tr, final):
    t = h.shape[0]
    pos, visits = _routing_tables(route_t, counts, tr)
    xs = _scatter_rows(tp, pos[:t], pos[t:], 2 * t)
    ys = _experts(xs, visits, lw, l, tr)
    y = _gather_rows(ys, pos)
    return _combine(h, y, route, gfin, tm, final)


def _rope_angles(seq_len, rot_dim):
    rows = seq_len // GRID_W
    row = jnp.broadcast_to(jnp.arange(rows)[:, None], (rows, GRID_W)).reshape(-1).astype(F32)
    col = jnp.broadcast_to(jnp.arange(GRID_W)[None, :], (rows, GRID_W)).reshape(-1).astype(F32)
    axis_dim = rot_dim // 2
    inv_freq = jnp.power(jnp.float32(ROPE_THETA), -jnp.arange(0, axis_dim, 2, dtype=F32) / axis_dim)
    ang = jnp.concatenate([row[:, None] * inv_freq[None, :], col[:, None] * inv_freq[None, :]], axis=-1)
    return jnp.cos(ang), jnp.sin(ang)


def _rope_tables(nseq, seq_len):
    c, s = _rope_angles(seq_len, A_HEAD_DIM)
    ca = jnp.concatenate([c, c, c, c], axis=-1)
    sa = jnp.concatenate([-s, s, -s, s], axis=-1)
    c, s = _rope_angles(seq_len, B_ROPE)
    ones = jnp.ones((seq_len, B_NOPE), F32)
    zeros = jnp.zeros((seq_len, B_NOPE), F32)
    tail = LANES - B_NOPE - B_ROPE
    cb = jnp.concatenate([ones, c, c, ones[:, :tail]], axis=-1)
    sb = jnp.concatenate([zeros, -s, s, zeros[:, :tail]], axis=-1)
    return tuple(jnp.tile(a, (nseq, 1)) for a in (ca, sa, cb, sb))


def _prepare_weights(norm_mix, w_in, a_q_norm, a_k_norm, b_q_norm, b_kv_norm, w_q_up, w_kv_up, w_branch_a,
                     w_branch_b, w_out, norm_ffn, w_router, b_router, w_gate, w_up, w_down):
    depth = w_in.shape[0]
    order = jnp.array(A_HEAD_ORDER)
    parts = []
    start = 0
    for n in (A_Q_COLS, A_KV_COLS, A_KV_COLS, B_Q_RANK, B_KV_RANK, B_ROPE, D_MODEL, D_MODEL):
        parts.append(w_in[..., start:start + n])
        start += n
    qa, ka, va, cq, ckv, kr, ga, gb = parts
    qa = qa.reshape(depth, D_MODEL, A_HEADS, A_HEAD_DIM)[:, :, order].reshape(depth, D_MODEL, A_Q_COLS)
    kr = jnp.pad(kr, ((0, 0), (0, 0), (B_NOPE, LANES - B_NOPE - B_ROPE)))
    win = jnp.concatenate([qa, ka, va, cq, ckv, kr, ga, gb], axis=-1).astype(BF16)

    wq = w_q_up.reshape(depth, B_Q_RANK, B_HEADS, B_QK_DIM)
    wq = jnp.pad(wq, ((0, 0), (0, 0), (0, 0), (0, LANES - B_QK_DIM))).reshape(depth, B_Q_RANK, B_PAD_COLS)
    wkv = w_kv_up.reshape(depth, B_KV_RANK, B_HEADS, B_NOPE + B_V)
    wk = jnp.pad(wkv[..., :B_NOPE], ((0, 0), (0, 0), (0, 0), (0, LANES - B_NOPE))).reshape(depth, B_KV_RANK, B_PAD_COLS)
    wv = wkv[..., B_NOPE:].reshape(depth, B_KV_RANK, B_O_COLS)
    wba = w_branch_a.reshape(depth, A_HEADS, A_HEAD_DIM, D_MODEL)[:, order].reshape(depth, A_Q_COLS, D_MODEL)

    group = jnp.arange(A_Q_COLS) // A_HEAD_DIM
    bd = (group[:, None] == group[None, :]).astype(BF16)
    wr_t = w_router.T
    wrh, wrl = _split_bf16(wr_t)
    vec = lambda a: a[:, None, :]
    return dict(
        gmix=vec(norm_mix), win=win,
        gq=vec(jnp.tile(a_q_norm, (1, A_HEADS)) * (A_HEAD_DIM ** -0.5 * LOG2E)), gk=vec(jnp.tile(a_k_norm, (1, A_KV_HEADS))),
        gbq=vec(b_q_norm), gbkv=vec(b_kv_norm),
        wq=wq.astype(BF16), wk=wk.astype(BF16), wv=wv.astype(BF16), bd=bd,
        wba=wba.astype(BF16), wbb=w_branch_b.astype(BF16), wout=w_out.astype(BF16), gffn=vec(norm_ffn),
        wrh=wrh, wrl=wrl, br=b_router[:, None].astype(F32),
        wg=w_gate.astype(BF16), wu=w_up.astype(BF16), wd=w_down.astype(BF16),
    )


def _pick(n, candidates):
    for c in candidates:
        if n % c == 0:
            return c
    raise ValueError(f"no tile in {candidates} divides {n}")


def _trunk(x3, lw, norm_final):
    nseq, s, _ = x3.shape
    t = nseq * s
    x = x3.reshape(t, D_MODEL)
    tabs = _rope_tables(nseq, s)
    depth = lw["win"].shape[0]
    tm = _pick(t, (256,))
    tm_out = _pick(t, (512, 256))
    tq = _pick(s, (512, 256))
    tk = _pick(s, (512,))
    for l in range(depth):
        qa, ka, va, qb, kb, vb, ga, gb = _mixer_in(x, lw, l, tabs, tm_out)
        oa = _attention_a(qa, ka, va, nseq, s, tq, tk)
        ob = _attention_b(qb, kb, vb, nseq, s, tq, tk)
        h, tp, route, route_t, counts = _mixer_out(oa, ob, ga, gb, x, lw, l, tm_out)
        x = _moe(h, tp, route, route_t, counts, lw, l, norm_final[None, :], tm, EXPERT_ROWS, final=l == depth - 1)
    return x.reshape(x3.shape)


def kernel(x_prompt, x_sample, norm_mix, w_in, a_q_norm, a_k_norm, b_q_norm, b_kv_norm, w_q_up, w_kv_up, w_branch_a,
           w_branch_b, w_out, norm_ffn, w_router, b_router, w_gate, w_up, w_down, norm_final):
    lw = _prepare_weights(norm_mix, w_in, a_q_norm, a_k_norm, b_q_norm, b_kv_norm, w_q_up, w_kv_up, w_branch_a,
                          w_branch_b, w_out, norm_ffn, w_router, b_router, w_gate, w_up, w_down)
    return _trunk(x_prompt, lw, norm_final), _trunk(x_sample, lw, norm_final)
```

```python
import functools

import jax
import jax.numpy as jnp
from jax import lax
from jax.experimental import pallas as pl
from jax.experimental.pallas import tpu as pltpu
from jax.experimental.pallas import tpu_sc as plsc

F32 = jnp.float32
BF16 = jnp.bfloat16

D_MODEL = 1024
GRID_W = 64
ROPE_THETA = 10000.0
EPS = 1e-6
A_HEADS = 8
A_KV_HEADS = 2
A_HEAD_DIM = 64
B_HEADS = 8
B_Q_RANK = 384
B_KV_RANK = 256
B_NOPE = 64
B_ROPE = 32
B_V = 64
N_EXPERTS = 16
N_GROUPS = 4
EXPERTS_PER_GROUP = N_EXPERTS // N_GROUPS
D_EXPERT = 512
A_Q_COLS = A_HEADS * A_HEAD_DIM
A_KV_COLS = A_KV_HEADS * A_HEAD_DIM
B_QK_DIM = B_NOPE + B_ROPE
B_O_COLS = B_HEADS * B_V

LANES = 128
LOG2E = 1.4426950408889634

C_QA = 0
C_KA = C_QA + A_Q_COLS
C_VA = C_KA + A_KV_COLS
C_CQ = C_VA + A_KV_COLS
C_CKV = C_CQ + B_Q_RANK
C_KR = C_CKV + B_KV_RANK
C_GA = C_KR + LANES
C_GB = C_GA + D_MODEL
C_END = C_GB + D_MODEL
B_PAD_COLS = B_HEADS * LANES

A_HEAD_ORDER = (0, 4, 1, 5, 2, 6, 3, 7)

VMEM_LIMIT = 56 * 1024 * 1024

ATTN_GROUP = 8
ATTN_S_BUFS = 3
ATTN_P_BUFS = 2
ATTN_SLAB = 32
ATTN_FIXED_SHIFT_MAX = 48.0
EXPERT_ROWS = 512
SC_CHUNK = 128


def _cparams(sem):
    return pltpu.CompilerParams(dimension_semantics=sem, vmem_limit_bytes=VMEM_LIMIT)


def _dot(a, b):
    return jnp.dot(a, b, preferred_element_type=F32)


def _dot_nt(a, b):
    return lax.dot_general(a, b, (((1,), (1,)), ((), ())), preferred_element_type=F32)


def _dot_tn(a, b):
    return lax.dot_general(a, b, (((0,), (0,)), ((), ())), preferred_element_type=F32)


def _rms(x):
    return x * lax.rsqrt(jnp.mean(x * x, axis=-1, keepdims=True) + EPS)


def _split_bf16(x):
    hi = x.astype(BF16)
    lo = (x - hi.astype(F32)).astype(BF16)
    return hi, lo


def _group_mean_sq(v, bd):
    hi, lo = _split_bf16(v * v)
    return (_dot(hi, bd) + _dot(lo, bd)) * (1.0 / A_HEAD_DIM)


def _rope_a(v, cos, sin):
    n = v.shape[-1]
    lane = lax.broadcasted_iota(jnp.int32, v.shape, 1)
    low = (lane % A_HEAD_DIM) < (A_HEAD_DIM // 2)
    swapped = jnp.where(low, pltpu.roll(v, n - A_HEAD_DIM // 2, 1), pltpu.roll(v, A_HEAD_DIM // 2, 1))
    return v * cos + swapped * sin


def _rope_b(v, cos, sin):
    n = v.shape[-1]
    lane = lax.broadcasted_iota(jnp.int32, v.shape, 1)
    low = (lane % LANES) < (B_NOPE + B_ROPE // 2)
    swapped = jnp.where(low, pltpu.roll(v, n - B_ROPE // 2, 1), pltpu.roll(v, B_ROPE // 2, 1))
    return v * cos + swapped * sin


def _mixer_in_body(x_ref, gmix_ref, win_ref, ca_ref, sa_ref, cb_ref, sb_ref, gq_ref, gk_ref, gbq_ref, gbkv_ref,
                   wq_ref, wk_ref, wv_ref, bd_ref,
                   qat_ref, ka_ref, vat_ref, qbt_ref, kb_ref, vbt_ref, ga_ref, gb_ref):
    hb = (_rms(x_ref[...]) * gmix_ref[...]).astype(BF16)
    u = _dot(hb, win_ref[:, C_QA:C_GA])
    bd = bd_ref[...]
    ca, sa = ca_ref[...], sa_ref[...]
    cb, sb = cb_ref[...], sb_ref[...]

    qa = u[:, C_QA:C_KA]
    qa = qa * lax.rsqrt(_group_mean_sq(qa, bd) + EPS) * gq_ref[...]
    qa = _rope_a(qa, jnp.concatenate([ca] * (A_Q_COLS // LANES), axis=1),
                 jnp.concatenate([sa] * (A_Q_COLS // LANES), axis=1))
    qat_ref[...] = qa.T.astype(BF16)
    ka = u[:, C_KA:C_VA]
    ka = ka * lax.rsqrt(_group_mean_sq(ka, bd[:A_KV_COLS, :A_KV_COLS]) + EPS) * gk_ref[...]
    ka_ref[...] = _rope_a(ka, ca, sa).astype(BF16)
    vat_ref[...] = u[:, C_VA:C_CQ].T.astype(BF16)

    cq = (_rms(u[:, C_CQ:C_CKV]) * gbq_ref[...]).astype(BF16)
    qb = _dot(cq, wq_ref[...])
    qb = _rope_b(qb, jnp.concatenate([cb] * B_HEADS, axis=1), jnp.concatenate([sb] * B_HEADS, axis=1))
    qbt_ref[...] = (qb * (B_QK_DIM ** -0.5 * LOG2E)).T.astype(BF16)
    ckv = (_rms(u[:, C_CKV:C_KR]) * gbkv_ref[...]).astype(BF16)
    kr = _rope_b(u[:, C_KR:C_GA], cb, sb)
    kb_ref[...] = (_dot(ckv, wk_ref[...]) + jnp.concatenate([kr] * B_HEADS, axis=1)).astype(BF16)
    vbt_ref[...] = _dot(ckv, wv_ref[...]).T.astype(BF16)

    g = jax.nn.sigmoid(_dot(hb, win_ref[:, C_GA:C_END]))
    ga_ref[...] = g[:, :D_MODEL].astype(BF16)
    gb_ref[...] = g[:, D_MODEL:].astype(BF16)


def _mixer_in(x, lw, l, tabs, tm):
    t = x.shape[0]
    row = lambda n: pl.BlockSpec((tm, n), lambda i: (i, 0))
    full = lambda a: pl.BlockSpec((None,) + a.shape[1:], lambda i: (l,) + (0,) * (a.ndim - 1))
    const = lambda a: pl.BlockSpec(a.shape, lambda i: (0,) * a.ndim)
    ca, sa, cb, sb = tabs
    ins = [x, lw["gmix"], lw["win"], ca, sa, cb, sb, lw["gq"], lw["gk"], lw["gbq"], lw["gbkv"],
           lw["wq"], lw["wk"], lw["wv"], lw["bd"]]
    in_specs = [row(D_MODEL), full(lw["gmix"]), full(lw["win"]), row(LANES), row(LANES), row(LANES), row(LANES),
                full(lw["gq"]), full(lw["gk"]), full(lw["gbq"]), full(lw["gbkv"]),
                full(lw["wq"]), full(lw["wk"]), full(lw["wv"]), const(lw["bd"])]
    col = lambda n: pl.BlockSpec((n, tm), lambda i: (0, i))
    widths = [A_Q_COLS, A_KV_COLS, A_KV_COLS, B_PAD_COLS, B_PAD_COLS, B_O_COLS, D_MODEL, D_MODEL]
    transposed = [True, False, True, True, False, True, False, False]
    return pl.pallas_call(
        _mixer_in_body,
        grid=(t // tm,),
        in_specs=in_specs,
        out_specs=[col(n) if tr else row(n) for n, tr in zip(widths, transposed)],
        out_shape=[jax.ShapeDtypeStruct((n, t) if tr else (t, n), BF16) for n, tr in zip(widths, transposed)],
        compiler_params=_cparams(("parallel",)),
        name="mixer_in",
    )(*ins)


def _flash_pair(q0t, q1t, k0_ref, k1_ref, vt_ref, o_ref, s_refs, p_refs, acc_ref, kmax_ref, tk):
    tq = q0t.shape[1]
    nk = vt_ref.shape[1] // tk
    group = min(nk, ATTN_GROUP)
    ns = len(s_refs)
    qt2 = jnp.concatenate([q0t, q1t], axis=1)

    def score_values(j):
        off = pl.multiple_of(j * tk, tk)
        if k1_ref is None:
            return _dot(k0_ref[pl.ds(off, tk), :], qt2)
        return jnp.concatenate([_dot(k0_ref[pl.ds(off, tk), :], q0t), _dot(k1_ref[pl.ds(off, tk), :], q1t)], axis=1)

    def add_values(j, p, alpha):
        off = pl.multiple_of(j * tk, tk)
        for rows, lanes in ((slice(0, B_V), slice(0, tq)), (slice(B_V, LANES), slice(tq, 2 * tq))):
            prev = acc_ref[rows, :] if alpha is None else alpha[:, lanes] * acc_ref[rows, :]
            acc_ref[rows, :] = prev + _dot(vt_ref[rows, pl.ds(off, tk)], p[:, lanes])

    def finish(l):
        l = jnp.sum(l, axis=0, keepdims=True)
        o_ref[:B_V, :] = (acc_ref[:B_V, :] / l[:, :tq]).astype(o_ref.dtype)
        o_ref[B_V:, :] = (acc_ref[B_V:, :] / l[:, tq:]).astype(o_ref.dtype)

    def run_groups(one_group, carry):
        acc_ref[...] = jnp.zeros_like(acc_ref)
        return one_group(0, carry) if nk == group else lax.fori_loop(0, nk // group, one_group, carry)

    def fixed_shift(shift):
        def one_group(g, l):
            for c in range(group):
                p = jnp.exp2(score_values(g * group + c) - shift)
                l = l + p.reshape(tk // 8, 8, 2 * tq).sum(axis=0)
                add_values(g * group + c, p.astype(BF16), None)
            return l

        finish(run_groups(one_group, jnp.zeros((8, 2 * tq), F32)))

    def softmax(s_ref, p_ref, m, l):
        slabs = [pl.ds(r, ATTN_SLAB) for r in range(0, tk, ATTN_SLAB)]
        fold = lambda x: x.reshape(ATTN_SLAB // 8, 8, 2 * tq)
        mx = fold(s_ref[slabs[0], :]).max(axis=0)
        for sl in slabs[1:]:
            mx = jnp.maximum(mx, fold(s_ref[sl, :]).max(axis=0))
        m_new = jnp.maximum(m, jnp.max(mx, axis=0, keepdims=True))
        alpha = jnp.exp2(m - m_new)
        l = alpha * l
        for sl in slabs:
            p = jnp.exp2(s_ref[sl, :] - m_new)
            p_ref[sl, :] = p.astype(BF16)
            l = l + fold(p).sum(axis=0)
        return m_new, l, alpha

    def running_max():
        def one_group(g, carry):
            m, l = carry
            base = g * group
            for c in range(min(ns - 1, group)):
                s_refs[c % ns][...] = score_values(base + c)
            for c in range(group):
                if c + ns - 1 < group:
                    s_refs[(c + ns - 1) % ns][...] = score_values(base + c + ns - 1)
                p_ref = p_refs[c % len(p_refs)]
                m, l, alpha = softmax(s_refs[c % ns], p_ref, m, l)
                add_values(base + c, p_ref[...], alpha)
            return m, l

        carry = (jnp.full((1, 2 * tq), -jnp.inf, F32), jnp.zeros((8, 2 * tq), F32))
        finish(run_groups(one_group, carry)[1])

    lane = lax.broadcasted_iota(jnp.int32, (1, 2 * tq), 1)
    qsq = jnp.sum(jnp.square(qt2.astype(F32)), axis=0, keepdims=True)
    bound = jnp.sqrt(qsq * jnp.where(lane < tq, kmax_ref[0], kmax_ref[1]))
    small = jnp.max(bound) <= ATTN_FIXED_SHIFT_MAX
    pl.when(small)(lambda: fixed_shift(bound))
    pl.when(jnp.logical_not(small))(running_max)


def _max_sq_norm(k_ref, ones_ref, lanes):
    k = k_ref[...].astype(F32)
    n = _dot((k * k).astype(BF16), ones_ref[...])
    return jnp.max(n[:, lanes])


def _attn_a_body(qt_ref, k_ref, vt_ref, ones_ref, o_ref, *scratch, tk):
    s_refs, p_refs = scratch[:ATTN_S_BUFS], scratch[ATTN_S_BUFS:-2]
    acc_ref, kmax_ref = scratch[-2:]

    @pl.when(pl.program_id(2) == 0)
    def _():
        kmax_ref[0] = _max_sq_norm(k_ref, ones_ref, slice(0, A_HEAD_DIM))
        kmax_ref[1] = _max_sq_norm(k_ref, ones_ref, slice(A_HEAD_DIM, LANES))

    qt = qt_ref[...]
    zero = jnp.zeros((A_HEAD_DIM, qt.shape[1]), BF16)
    q0t = jnp.concatenate([qt[:A_HEAD_DIM], zero], axis=0)
    q1t = jnp.concatenate([zero, qt[A_HEAD_DIM:]], axis=0)
    _flash_pair(q0t, q1t, k_ref, None, vt_ref, o_ref, s_refs, p_refs, acc_ref, kmax_ref, tk)


def _attn_b_body(q0t_ref, q1t_ref, k0_ref, k1_ref, vt_ref, ones_ref, o_ref, *scratch, tk):
    s_refs, p_refs = scratch[:ATTN_S_BUFS], scratch[ATTN_S_BUFS:-2]
    acc_ref, kmax_ref = scratch[-2:]

    @pl.when(pl.program_id(2) == 0)
    def _():
        kmax_ref[0] = _max_sq_norm(k0_ref, ones_ref, slice(0, LANES))
        kmax_ref[1] = _max_sq_norm(k1_ref, ones_ref, slice(0, LANES))

    _flash_pair(q0t_ref[...], q1t_ref[...], k0_ref, k1_ref, vt_ref, o_ref, s_refs, p_refs, acc_ref, kmax_ref, tk)


def _attn_scratch(tq, tk):
    return ([pltpu.VMEM((tk, 2 * tq), F32)] * ATTN_S_BUFS + [pltpu.VMEM((tk, 2 * tq), BF16)] * ATTN_P_BUFS
            + [pltpu.VMEM((LANES, tq), F32), pltpu.SMEM((2,), F32)])


def _head_ones(head_dim):
    group = jnp.arange(LANES) // head_dim
    return (group[:, None] == group[None, :]).astype(BF16)


def _attention_a(qat, ka, vat, nseq, s, tq, tk):
    nq = s // tq
    qspec = pl.BlockSpec((LANES, tq), lambda b, h, i: (h, b * nq + i))
    kspec = pl.BlockSpec((s, LANES), lambda b, h, i: (b, 0))
    vspec = pl.BlockSpec((LANES, s), lambda b, h, i: (0, b))
    ones = pl.BlockSpec((LANES, LANES), lambda b, h, i: (0, 0))
    return pl.pallas_call(
        functools.partial(_attn_a_body, tk=tk),
        grid=(nseq, A_Q_COLS // LANES, nq),
        in_specs=[qspec, kspec, vspec, ones],
        out_specs=qspec,
        out_shape=jax.ShapeDtypeStruct(qat.shape, BF16),
        scratch_shapes=_attn_scratch(tq, tk),
        compiler_params=_cparams(("parallel", "parallel", "arbitrary")),
        name="attn_a",
    )(qat, ka, vat, _head_ones(A_HEAD_DIM))


def _attention_b(qbt, kb, vbt, nseq, s, tq, tk):
    nq = s // tq
    q0 = pl.BlockSpec((LANES, tq), lambda b, h, i: (2 * h, b * nq + i))
    q1 = pl.BlockSpec((LANES, tq), lambda b, h, i: (2 * h + 1, b * nq + i))
    k0 = pl.BlockSpec((s, LANES), lambda b, h, i: (b, 2 * h))
    k1 = pl.BlockSpec((s, LANES), lambda b, h, i: (b, 2 * h + 1))
    v = pl.BlockSpec((LANES, s), lambda b, h, i: (h, b))
    o = pl.BlockSpec((LANES, tq), lambda b, h, i: (h, b * nq + i))
    ones = pl.BlockSpec((LANES, LANES), lambda b, h, i: (0, 0))
    return pl.pallas_call(
        functools.partial(_attn_b_body, tk=tk),
        grid=(nseq, B_O_COLS // LANES, nq),
        in_specs=[q0, q1, k0, k1, v, ones],
        out_specs=o,
        out_shape=jax.ShapeDtypeStruct(vbt.shape, BF16),
        scratch_shapes=_attn_scratch(tq, tk),
        compiler_params=_cparams(("parallel", "parallel", "arbitrary")),
        name="attn_b",
    )(qbt, qbt, kb, kb, vbt, _head_ones(LANES))


def _within(x, d, period, n):
    row = lax.broadcasted_iota(jnp.int32, x.shape, 0)
    return jnp.where((row % period) + d < period, pltpu.roll(x, n - d, 0), pltpu.roll(x, period - d, 0))


def _route(logits_t, bias):
    n = N_EXPERTS
    scores = jax.nn.sigmoid(logits_t)
    biased = scores + bias
    row = lax.broadcasted_iota(jnp.int32, biased.shape, 0)
    pos = row % EXPERTS_PER_GROUP
    rank = jnp.zeros(biased.shape, jnp.int32)
    for d in range(1, EXPERTS_PER_GROUP):
        other = _within(biased, d, EXPERTS_PER_GROUP, n)
        other_pos = (pos + d) % EXPERTS_PER_GROUP
        ahead = (other > biased) | ((other == biased) & (other_pos < pos))
        rank = rank + ahead.astype(jnp.int32)
    top2 = rank < 2
    kept = jnp.where(top2, biased, 0.0)
    gscore = kept
    for d in range(1, EXPERTS_PER_GROUP):
        gscore = gscore + _within(kept, d, EXPERTS_PER_GROUP, n)
    grp = row // EXPERTS_PER_GROUP
    win = jnp.ones(biased.shape, jnp.bool_)
    for d in range(1, N_GROUPS):
        other = pltpu.roll(gscore, n - d * EXPERTS_PER_GROUP, 0)
        other_grp = (grp + d) % N_GROUPS
        win = win & ((other < gscore) | ((other == gscore) & (other_grp > grp)))
    sel = top2 & win
    w = jnp.where(sel, scores, 0.0)
    return w / jnp.sum(w, axis=0, keepdims=True), sel


def _pack_pairs(v):
    n = v.shape[1] // 2
    vb = v.astype(BF16).astype(F32)
    hi = pltpu.bitcast(vb[:, :n], jnp.int32)
    lo = pltpu.bitcast(vb[:, n:], jnp.int32)
    return hi | lax.shift_right_logical(lo, 16)


def _unpack_pairs(w):
    hi = pltpu.bitcast(w & jnp.int32(-65536), F32)
    lo = pltpu.bitcast(lax.shift_left(w, 16), F32)
    return jnp.concatenate([hi, lo], axis=1)


R_E0, R_E1, R_RANK0, R_RANK1, R_W0, R_W1 = range(6)


def _mixer_out_body(oa_ref, ob_ref, ga_ref, gb_ref, x_ref, wba_ref, wbb_ref, wout_ref, gffn_ref, wrh_ref, wrl_ref,
                    br_ref, tri_ref, h_ref, tp_ref, route_ref, route_t_ref, count_ref):
    @pl.when(pl.program_id(0) == 0)
    def _():
        count_ref[...] = jnp.zeros_like(count_ref)

    ma = _dot_tn(oa_ref[...], wba_ref[...])
    mb = _dot_tn(ob_ref[...], wbb_ref[...])
    merged = ga_ref[...].astype(F32) * ma + gb_ref[...].astype(F32) * mb
    h = x_ref[...] + _dot(merged.astype(BF16), wout_ref[...])
    h_ref[...] = h
    t = _rms(h) * gffn_ref[...]
    t_hi, t_lo = _split_bf16(t)
    tp_ref[...] = _pack_pairs(t)
    wrh, wrl = wrh_ref[...], wrl_ref[...]
    logits_t = _dot_nt(wrh, t_hi) + _dot_nt(wrh, t_lo) + _dot_nt(wrl, t_hi)
    gates_t, sel = _route(logits_t, br_ref[...])

    tm = gates_t.shape[1]
    onehot = jnp.where(sel, 1.0, 0.0)
    before = _dot(onehot.astype(BF16), tri_ref[...])
    rank_t = count_ref[:, 0:1] + before
    count_ref[...] = count_ref[...] + jnp.sum(onehot, axis=1, keepdims=True)

    row = lax.broadcasted_iota(jnp.int32, sel.shape, 0).astype(F32)
    e0 = jnp.min(jnp.where(sel, row, float(N_EXPERTS)), axis=0, keepdims=True)
    e1 = jnp.max(jnp.where(sel, row, -1.0), axis=0, keepdims=True)
    pick = lambda v, e: jnp.sum(jnp.where(sel & (row == e), v, 0.0), axis=0, keepdims=True)
    rec = jnp.concatenate([e0, e1, pick(rank_t, e0), pick(rank_t, e1), pick(gates_t, e0), pick(gates_t, e1),
                           jnp.zeros((LANES - 6, tm), F32)], axis=0)
    route_ref[...] = rec.T
    route_t_ref[...] = rec[:8]


def _mixer_out(oa, ob, ga, gb, x, lw, l, tm):
    t = x.shape[0]
    row = lambda n: pl.BlockSpec((tm, n), lambda i: (i, 0))
    full = lambda a: pl.BlockSpec((None,) + a.shape[1:], lambda i: (l,) + (0,) * (a.ndim - 1))
    const = lambda a: pl.BlockSpec(a.shape, lambda i: (0,) * a.ndim)
    col = lambda n: pl.BlockSpec((n, tm), lambda i: (0, i))
    tri = (jnp.arange(tm)[:, None] < jnp.arange(tm)[None, :]).astype(BF16)
    ins = [oa, ob, ga, gb, x, lw["wba"], lw["wbb"], lw["wout"], lw["gffn"], lw["wrh"], lw["wrl"], lw["br"], tri]
    in_specs = [col(A_Q_COLS), col(B_O_COLS), row(D_MODEL), row(D_MODEL), row(D_MODEL),
                full(lw["wba"]), full(lw["wbb"]), full(lw["wout"]), full(lw["gffn"]),
                const(lw["wrh"]), const(lw["wrl"]), const(lw["br"]), const(tri)]
    return pl.pallas_call(
        _mixer_out_body,
        grid=(t // tm,),
        in_specs=in_specs,
        out_specs=[row(D_MODEL), row(D_MODEL // 2), row(LANES), col(8),
                   pl.BlockSpec((N_EXPERTS, LANES), lambda i: (0, 0))],
        out_shape=[jax.ShapeDtypeStruct((t, D_MODEL), F32), jax.ShapeDtypeStruct((t, D_MODEL // 2), jnp.int32),
                   jax.ShapeDtypeStruct((t, LANES), F32), jax.ShapeDtypeStruct((8, t), F32),
                   jax.ShapeDtypeStruct((N_EXPERTS, LANES), F32)],
        compiler_params=_cparams(("arbitrary",)),
        name="mixer_out",
    )(*ins)


def _sc_plan(nrows):
    info = plsc.get_sparse_core_info()
    workers = info.num_cores * info.num_subcores
    per_worker = nrows // workers
    chunk = min(SC_CHUNK, per_worker)
    assert per_worker * workers == nrows and per_worker % chunk == 0 and chunk % 8 == 0, (nrows, workers, chunk)
    return info.num_cores, per_worker, chunk


def _sc_mesh():
    return plsc.VectorSubcoreMesh(core_axis_name="core", subcore_axis_name="subcore")


def _scatter_rows(x, idx0, idx1, nrows):
    t, d = x.shape
    ncores, per_worker, chunk = _sc_plan(t)

    @functools.partial(
        pl.kernel, out_type=jax.ShapeDtypeStruct((nrows, d), x.dtype), mesh=_sc_mesh(), name="moe_dispatch",
        scratch_types=[pltpu.VMEM((chunk,), jnp.int32), pltpu.VMEM((chunk,), jnp.int32), pltpu.VMEM((chunk, d), x.dtype)])
    def run(x_hbm, i0_hbm, i1_hbm, o_hbm, i0_v, i1_v, rows_v):
        worker = lax.axis_index("subcore") * ncores + lax.axis_index("core")

        @pl.loop(0, per_worker // chunk)
        def _(c):
            base = pl.multiple_of(worker * per_worker + c * chunk, chunk)
            pltpu.sync_copy(x_hbm.at[pl.ds(base, chunk)], rows_v)
            pltpu.sync_copy(i0_hbm.at[pl.ds(base, chunk)], i0_v)
            pltpu.sync_copy(i1_hbm.at[pl.ds(base, chunk)], i1_v)
            pltpu.sync_copy(rows_v, o_hbm.at[i0_v])
            pltpu.sync_copy(rows_v, o_hbm.at[i1_v])

    return run(x, idx0, idx1)


def _gather_rows(table, idx):
    m = idx.shape[0]
    d = table.shape[1]
    ncores, per_worker, chunk = _sc_plan(m)

    @functools.partial(
        pl.kernel, out_type=jax.ShapeDtypeStruct((m, d), table.dtype), mesh=_sc_mesh(), name="moe_collect",
        scratch_types=[pltpu.VMEM((chunk,), jnp.int32), pltpu.VMEM((chunk, d), table.dtype)])
    def run(x_hbm, i_hbm, o_hbm, i_v, rows_v):
        worker = lax.axis_index("subcore") * ncores + lax.axis_index("core")

        @pl.loop(0, per_worker // chunk)
        def _(c):
            base = pl.multiple_of(worker * per_worker + c * chunk, chunk)
            pltpu.sync_copy(i_hbm.at[pl.ds(base, chunk)], i_v)
            pltpu.sync_copy(x_hbm.at[i_v], rows_v)
            pltpu.sync_copy(rows_v, o_hbm.at[pl.ds(base, chunk)])

    return run(table, idx)


def _routing_tables(route_t, counts, tr):
    t = route_t.shape[1]
    experts = jnp.arange(N_EXPERTS, dtype=jnp.int32)
    cnt = counts[:, 0].astype(jnp.int32)
    seg_end = jnp.cumsum(cnt)
    seg_start = seg_end - cnt
    lookup = lambda table, idx: jnp.sum(jnp.where(idx[None, :] == experts[:, None], table[:, None], 0), axis=0)
    e = route_t[R_E0:R_E1 + 1].astype(jnp.int32).reshape(2 * t)
    rank = route_t[R_RANK0:R_RANK1 + 1].astype(jnp.int32).reshape(2 * t)
    pos = lookup(seg_start, e) + rank

    n_tiles = 2 * t // tr
    n_visits = n_tiles + N_EXPERTS - 1
    first_tile = seg_start // tr
    last_tile = jnp.maximum(seg_end - 1, 0) // tr
    visits = jnp.where(cnt > 0, last_tile - first_tile + 1, 0)
    visit_end = jnp.cumsum(visits)
    visit_start = visit_end - visits
    g = jnp.arange(n_visits, dtype=jnp.int32)
    valid = g < visit_end[-1]
    ex = jnp.minimum(jnp.sum((g[None, :] >= visit_end[:, None]).astype(jnp.int32), axis=0), N_EXPERTS - 1)
    tile = lookup(first_tile - visit_start, ex) + g
    lo = jnp.clip(lookup(seg_start, ex) - tile * tr, 0, tr)
    hi = jnp.clip(lookup(seg_end, ex) - tile * tr, 0, tr)
    last_ex = jnp.max(jnp.where(cnt > 0, experts, 0))
    tile = jnp.where(valid, tile, n_tiles - 1)
    ex = jnp.where(valid, ex, last_ex)
    lo = jnp.where(valid, lo, 0)
    hi = jnp.where(valid, hi, 0)
    first = jnp.concatenate([jnp.ones((1,), jnp.int32), (tile[1:] != tile[:-1]).astype(jnp.int32)])
    first = jnp.where(valid, first, 0)
    return pos, (tile, ex, lo, hi, first)


def _experts_body(tile_ref, ex_ref, lo_ref, hi_ref, first_ref, xs_ref, wg_ref, wu_ref, wd_ref, ys_ref, acc_ref):
    g = pl.program_id(0)
    lo, hi = lo_ref[g], hi_ref[g]

    @pl.when(hi > lo)
    def _():
        x = _unpack_pairs(xs_ref[...]).astype(BF16)
        a = jax.nn.silu(_dot(x, wg_ref[...])) * _dot(x, wu_ref[...])
        row = lax.broadcasted_iota(jnp.int32, (a.shape[0], 1), 0)
        a = jnp.where((row >= lo) & (row < hi), a, 0.0)
        y = _dot(a.astype(BF16), wd_ref[...])

        @pl.when(first_ref[g] == 1)
        def _():
            acc_ref[...] = y

        @pl.when(first_ref[g] == 0)
        def _():
            acc_ref[...] += y

        ys_ref[...] = _pack_pairs(acc_ref[...])


def _experts(xs, visits, lw, l, tr):
    n_visits = visits[0].shape[0]
    rows = pl.BlockSpec((tr, D_MODEL // 2), lambda g, tile, ex, lo, hi, first: (tile[g], 0))
    wspec = lambda a: pl.BlockSpec((None, None) + a.shape[2:], lambda g, tile, ex, lo, hi, first: (l, ex[g], 0, 0))
    return pl.pallas_call(
        _experts_body,
        grid_spec=pltpu.PrefetchScalarGridSpec(
            num_scalar_prefetch=5,
            grid=(n_visits,),
            in_specs=[rows, wspec(lw["wg"]), wspec(lw["wu"]), wspec(lw["wd"])],
            out_specs=rows,
            scratch_shapes=[pltpu.VMEM((tr, D_MODEL), F32)],
        ),
        out_shape=jax.ShapeDtypeStruct(xs.shape, jnp.int32),
        compiler_params=_cparams(("arbitrary",)),
        name="experts",
    )(*visits, xs, lw["wg"], lw["wu"], lw["wd"])


def _combine_body(h_ref, y0_ref, y1_ref, route_ref, gfin_ref, o_ref, *, final):
    r = route_ref[...]
    x = (h_ref[...] + r[:, R_W0:R_W0 + 1] * _unpack_pairs(y0_ref[...])
         + r[:, R_W1:R_W1 + 1] * _unpack_pairs(y1_ref[...]))
    o_ref[...] = _rms(x) * gfin_ref[...] if final else x


def _combine(h, y, route, gfin, tm, final):
    t = h.shape[0]
    row = lambda n: pl.BlockSpec((tm, n), lambda i: (i, 0))
    second = pl.BlockSpec((tm, D_MODEL // 2), lambda i: (i + t // tm, 0))
    return pl.pallas_call(
        functools.partial(_combine_body, final=final),
        grid=(t // tm,),
        in_specs=[row(D_MODEL), row(D_MODEL // 2), second, row(LANES), pl.BlockSpec(gfin.shape, lambda i: (0, 0))],
        out_specs=row(D_MODEL),
        out_shape=jax.ShapeDtypeStruct(h.shape, F32),
        compiler_params=_cparams(("parallel",)),
        name="combine",
    )(h, y, y, route, gfin)


def _moe(h, tp, route, route_t, counts, lw, l, gfin, tm, tr, final):
    t = h.shape[0]
    pos, visits = _routing_tables(route_t, counts, tr)
    xs = _scatter_rows(tp, pos[:t], pos[t:], 2 * t)
    ys = _experts(xs, visits, lw, l, tr)
    y = _gather_rows(ys, pos)
    return _combine(h, y, route, gfin, tm, final)


def _rope_angles(seq_len, rot_dim):
    rows = seq_len // GRID_W
    row = jnp.broadcast_to(jnp.arange(rows)[:, None], (rows, GRID_W)).reshape(-1).astype(F32)
    col = jnp.broadcast_to(jnp.arange(GRID_W)[None, :], (rows, GRID_W)).reshape(-1).astype(F32)
    axis_dim = rot_dim // 2
    inv_freq = jnp.power(jnp.float32(ROPE_THETA), -jnp.arange(0, axis_dim, 2, dtype=F32) / axis_dim)
    ang = jnp.concatenate([row[:, None] * inv_freq[None, :], col[:, None] * inv_freq[None, :]], axis=-1)
    return jnp.cos(ang), jnp.sin(ang)


def _rope_tables(nseq, seq_len):
    c, s = _rope_angles(seq_len, A_HEAD_DIM)
    ca = jnp.concatenate([c, c, c, c], axis=-1)
    sa = jnp.concatenate([-s, s, -s, s], axis=-1)
    c, s = _rope_angles(seq_len, B_ROPE)
    ones = jnp.ones((seq_len, B_NOPE), F32)
    zeros = jnp.zeros((seq_len, B_NOPE), F32)
    tail = LANES - B_NOPE - B_ROPE
    cb = jnp.concatenate([ones, c, c, ones[:, :tail]], axis=-1)
    sb = jnp.concatenate([zeros, -s, s, zeros[:, :tail]], axis=-1)
    return tuple(jnp.tile(a, (nseq, 1)) for a in (ca, sa, cb, sb))


def _prepare_weights(norm_mix, w_in, a_q_norm, a_k_norm, b_q_norm, b_kv_norm, w_q_up, w_kv_up, w_branch_a,
                     w_branch_b, w_out, norm_ffn, w_router, b_router, w_gate, w_up, w_down):
    depth = w_in.shape[0]
    order = jnp.array(A_HEAD_ORDER)
    parts = []
    start = 0
    for n in (A_Q_COLS, A_KV_COLS, A_KV_COLS, B_Q_RANK, B_KV_RANK, B_ROPE, D_MODEL, D_MODEL):
        parts.append(w_in[..., start:start + n])
        start += n
    qa, ka, va, cq, ckv, kr, ga, gb = parts
    qa = qa.reshape(depth, D_MODEL, A_HEADS, A_HEAD_DIM)[:, :, order].reshape(depth, D_MODEL, A_Q_COLS)
    kr = jnp.pad(kr, ((0, 0), (0, 0), (B_NOPE, LANES - B_NOPE - B_ROPE)))
    win = jnp.concatenate([qa, ka, va, cq, ckv, kr, ga, gb], axis=-1).astype(BF16)

    wq = w_q_up.reshape(depth, B_Q_RANK, B_HEADS, B_QK_DIM)
    wq = jnp.pad(wq, ((0, 0), (0, 0), (0, 0), (0, LANES - B_QK_DIM))).reshape(depth, B_Q_RANK, B_PAD_COLS)
    wkv = w_kv_up.reshape(depth, B_KV_RANK, B_HEADS, B_NOPE + B_V)
    wk = jnp.pad(wkv[..., :B_NOPE], ((0, 0), (0, 0), (0, 0), (0, LANES - B_NOPE))).reshape(depth, B_KV_RANK, B_PAD_COLS)
    wv = wkv[..., B_NOPE:].reshape(depth, B_KV_RANK, B_O_COLS)
    wba = w_branch_a.reshape(depth, A_HEADS, A_HEAD_DIM, D_MODEL)[:, order].reshape(depth, A_Q_COLS, D_MODEL)

    group = jnp.arange(A_Q_COLS) // A_HEAD_DIM
    bd = (group[:, None] == group[None, :]).astype(BF16)
    wr_t = w_router.T
    wrh, wrl = _split_bf16(wr_t)
    vec = lambda a: a[:, None, :]
    return dict(
        gmix=vec(norm_mix), win=win,
        gq=vec(jnp.tile(a_q_norm, (1, A_HEADS)) * (A_HEAD_DIM ** -0.5 * LOG2E)), gk=vec(jnp.tile(a_k_norm, (1, A_KV_HEADS))),
        gbq=vec(b_q_norm), gbkv=vec(b_kv_norm),
        wq=wq.astype(BF16), wk=wk.astype(BF16), wv=wv.astype(BF16), bd=bd,
        wba=wba.astype(BF16), wbb=w_branch_b.astype(BF16), wout=w_out.astype(BF16), gffn=vec(norm_ffn),
        wrh=wrh, wrl=wrl, br=b_router[:, None].astype(F32),
        wg=w_gate.astype(BF16), wu=w_up.astype(BF16), wd=w_down.astype(BF16),
    )


def _pick(n, candidates):
    for c in candidates:
        if n % c == 0:
            return c
    raise ValueError(f"no tile in {candidates} divides {n}")


def _trunk(x3, lw, norm_final):
    nseq, s, _ = x3.shape
    t = nseq * s
    x = x3.reshape(t, D_MODEL)
    tabs = _rope_tables(nseq, s)
    depth = lw["win"].shape[0]
    tm = _pick(t, (256,))
    tm_out = _pick(t, (512, 256))
    tq = _pick(s, (512, 256))
    tk = _pick(s, (512,))
    for l in range(depth):
        qa, ka, va, qb, kb, vb, ga, gb = _mixer_in(x, lw, l, tabs, tm_out)
        oa = _attention_a(qa, ka, va, nseq, s, tq, tk)
        ob = _attention_b(qb, kb, vb, nseq, s, tq, tk)
        h, tp, route, route_t, counts = _mixer_out(oa, ob, ga, gb, x, lw, l, tm_out)
        x = _moe(h, tp, route, route_t, counts, lw, l, norm_final[None, :], tm, EXPERT_ROWS, final=l == depth - 1)
    return x.reshape(x3.shape)


def kernel(x_prompt, x_sample, norm_mix, w_in, a_q_norm, a_k_norm, b_q_norm, b_kv_norm, w_q_up, w_kv_up, w_branch_a,
           w_branch_b, w_out, norm_ffn, w_router, b_router, w_gate, w_up, w_down, norm_final):
    lw = _prepare_weights(norm_mix, w_in, a_q_norm, a_k_norm, b_q_norm, b_kv_norm, w_q_up, w_kv_up, w_branch_a,
                          w_branch_b, w_out, norm_ffn, w_router, b_router, w_gate, w_up, w_down)
    return _trunk(x_prompt, lw, norm_final), _trunk(x_sample, lw, norm_final)
```

```python
import functools

import jax
import jax.numpy as jnp
from jax import lax
from jax.experimental import pallas as pl
from jax.experimental.pallas import tpu as pltpu
from jax.experimental.pallas import tpu_sc as plsc

F32 = jnp.float32
BF16 = jnp.bfloat16

D_MODEL = 1024
GRID_W = 64
ROPE_THETA = 10000.0
EPS = 1e-6
A_HEADS = 8
A_KV_HEADS = 2
A_HEAD_DIM = 64
B_HEADS = 8
B_Q_RANK = 384
B_KV_RANK = 256
B_NOPE = 64
B_ROPE = 32
B_V = 64
N_EXPERTS = 16
N_GROUPS = 4
EXPERTS_PER_GROUP = N_EXPERTS // N_GROUPS
D_EXPERT = 512
A_Q_COLS = A_HEADS * A_HEAD_DIM
A_KV_COLS = A_KV_HEADS * A_HEAD_DIM
B_QK_DIM = B_NOPE + B_ROPE
B_O_COLS = B_HEADS * B_V

LANES = 128
LOG2E = 1.4426950408889634

C_QA = 0
C_KA = C_QA + A_Q_COLS
C_VA = C_KA + A_KV_COLS
C_CQ = C_VA + A_KV_COLS
C_CKV = C_CQ + B_Q_RANK
C_KR = C_CKV + B_KV_RANK
C_GA = C_KR + LANES
C_GB = C_GA + D_MODEL
C_END = C_GB + D_MODEL
B_PAD_COLS = B_HEADS * LANES

A_HEAD_ORDER = (0, 4, 1, 5, 2, 6, 3, 7)

VMEM_LIMIT = 56 * 1024 * 1024

ATTN_Q_TILES = 2
ATTN_GROUP = 8
ATTN_S_BUFS = 3
ATTN_P_BUFS = 2
ATTN_SLAB = 32
ATTN_FIXED_SHIFT_MAX = 48.0
EXPERT_ROWS = 512
SC_CHUNK = 128


def _cparams(sem):
    return pltpu.CompilerParams(dimension_semantics=sem, vmem_limit_bytes=VMEM_LIMIT)


def _dot(a, b):
    return jnp.dot(a, b, preferred_element_type=F32)


def _dot_nt(a, b):
    return lax.dot_general(a, b, (((1,), (1,)), ((), ())), preferred_element_type=F32)


def _dot_tn(a, b):
    return lax.dot_general(a, b, (((0,), (0,)), ((), ())), preferred_element_type=F32)


def _rms(x):
    return x * lax.rsqrt(jnp.mean(x * x, axis=-1, keepdims=True) + EPS)


def _split_bf16(x):
    hi = x.astype(BF16)
    lo = (x - hi.astype(F32)).astype(BF16)
    return hi, lo


def _group_mean_sq(v, bd):
    return _dot((v * v).astype(BF16), bd) * (1.0 / A_HEAD_DIM)


def _rope_a(v, cos, sin):
    n = v.shape[-1]
    lane = lax.broadcasted_iota(jnp.int32, v.shape, 1)
    low = (lane % A_HEAD_DIM) < (A_HEAD_DIM // 2)
    swapped = jnp.where(low, pltpu.roll(v, n - A_HEAD_DIM // 2, 1), pltpu.roll(v, A_HEAD_DIM // 2, 1))
    return v * cos + swapped * sin


def _rope_b(v, cos, sin):
    n = v.shape[-1]
    lane = lax.broadcasted_iota(jnp.int32, v.shape, 1)
    low = (lane % LANES) < (B_NOPE + B_ROPE // 2)
    swapped = jnp.where(low, pltpu.roll(v, n - B_ROPE // 2, 1), pltpu.roll(v, B_ROPE // 2, 1))
    return v * cos + swapped * sin


def _mixer_in_body(*refs, fused):
    if fused:
        h_ref, y0_ref, y1_ref, route_ref, *refs = refs
    else:
        x_ref, *refs = refs
    (gmix_ref, win_ref, ca_ref, sa_ref, cb_ref, sb_ref, gq_ref, gk_ref, gbq_ref, gbkv_ref, wq_ref, wk_ref, wv_ref,
     bd_ref, *outs) = refs
    if fused:
        x_out_ref, *outs = outs
        x = _moe_residual(h_ref, y0_ref, y1_ref, route_ref)
        x_out_ref[...] = x
    else:
        x = x_ref[...]
    qat_ref, ka_ref, vat_ref, qbt_ref, kb_ref, vbt_ref, ga_ref, gb_ref = outs
    hb = (_rms(x) * gmix_ref[...]).astype(BF16)
    u = _dot(hb, win_ref[:, C_QA:C_GA])
    bd = bd_ref[...]
    ca, sa = ca_ref[...], sa_ref[...]
    cb, sb = cb_ref[...], sb_ref[...]

    qa = u[:, C_QA:C_KA]
    qa = qa * lax.rsqrt(_group_mean_sq(qa, bd) + EPS) * gq_ref[...]
    qa = _rope_a(qa, jnp.concatenate([ca] * (A_Q_COLS // LANES), axis=1),
                 jnp.concatenate([sa] * (A_Q_COLS // LANES), axis=1))
    qat_ref[...] = qa.T.astype(BF16)
    ka = u[:, C_KA:C_VA]
    ka = ka * lax.rsqrt(_group_mean_sq(ka, bd[:A_KV_COLS, :A_KV_COLS]) + EPS) * gk_ref[...]
    ka_ref[...] = _rope_a(ka, ca, sa).astype(BF16)
    vat_ref[...] = u[:, C_VA:C_CQ].T.astype(BF16)

    cq = (_rms(u[:, C_CQ:C_CKV]) * gbq_ref[...]).astype(BF16)
    qb = _dot(cq, wq_ref[...])
    qb = _rope_b(qb, jnp.concatenate([cb] * B_HEADS, axis=1), jnp.concatenate([sb] * B_HEADS, axis=1))
    qbt_ref[...] = (qb * (B_QK_DIM ** -0.5 * LOG2E)).T.astype(BF16)
    ckv = (_rms(u[:, C_CKV:C_KR]) * gbkv_ref[...]).astype(BF16)
    kr = _rope_b(u[:, C_KR:C_GA], cb, sb)
    kb_ref[...] = (_dot(ckv, wk_ref[...]) + jnp.concatenate([kr] * B_HEADS, axis=1)).astype(BF16)
    vbt_ref[...] = _dot(ckv, wv_ref[...]).T.astype(BF16)

    g = jax.nn.sigmoid(_dot(hb, win_ref[:, C_GA:C_END]))
    ga_ref[...] = g[:, :D_MODEL].astype(BF16)
    gb_ref[...] = g[:, D_MODEL:].astype(BF16)


def _mixer_in(x, moe_out, lw, l, tabs, tm):
    fused = x is None
    t = moe_out[0].shape[0] if fused else x.shape[0]
    row = lambda n: pl.BlockSpec((tm, n), lambda i: (i, 0))
    full = lambda a: pl.BlockSpec((None,) + a.shape[1:], lambda i: (l,) + (0,) * (a.ndim - 1))
    const = lambda a: pl.BlockSpec(a.shape, lambda i: (0,) * a.ndim)
    col = lambda n: pl.BlockSpec((n, tm), lambda i: (0, i))
    ca, sa, cb, sb = tabs
    if fused:
        h, y, route = moe_out
        second = pl.BlockSpec((tm, D_MODEL // 2), lambda i: (i + t // tm, 0))
        ins, in_specs = [h, y, y, route], [row(D_MODEL), row(D_MODEL // 2), second, row(LANES)]
    else:
        ins, in_specs = [x], [row(D_MODEL)]
    ins += [lw["gmix"], lw["win"], ca, sa, cb, sb, lw["gq"], lw["gk"], lw["gbq"], lw["gbkv"],
            lw["wq"], lw["wk"], lw["wv"], lw["bd"]]
    in_specs += [full(lw["gmix"]), full(lw["win"]), row(LANES), row(LANES), row(LANES), row(LANES),
                 full(lw["gq"]), full(lw["gk"]), full(lw["gbq"]), full(lw["gbkv"]),
                 full(lw["wq"]), full(lw["wk"]), full(lw["wv"]), const(lw["bd"])]
    widths = [A_Q_COLS, A_KV_COLS, A_KV_COLS, B_PAD_COLS, B_PAD_COLS, B_O_COLS, D_MODEL, D_MODEL]
    transposed = [True, False, True, True, False, True, False, False]
    out_specs = [col(n) if tr else row(n) for n, tr in zip(widths, transposed)]
    out_shape = [jax.ShapeDtypeStruct((n, t) if tr else (t, n), BF16) for n, tr in zip(widths, transposed)]
    if fused:
        out_specs.insert(0, row(D_MODEL))
        out_shape.insert(0, jax.ShapeDtypeStruct((t, D_MODEL), F32))
    outs = pl.pallas_call(
        functools.partial(_mixer_in_body, fused=fused),
        grid=(t // tm,),
        in_specs=in_specs,
        out_specs=out_specs,
        out_shape=out_shape,
        compiler_params=_cparams(("parallel",)),
        name="mixer_in",
    )(*ins)
    return tuple(outs) if fused else (None,) + tuple(outs)


def _flash_pair(q0t, q1t, k0_ref, k1_ref, vt_ref, o_ref, s_refs, p_refs, acc_ref, kmax_ref, tk):
    tq = q0t.shape[1]
    nk = vt_ref.shape[1] // tk
    group = min(nk, ATTN_GROUP)
    ns = len(s_refs)
    qt2 = jnp.concatenate([q0t, q1t], axis=1)

    def score_values(j):
        off = pl.multiple_of(j * tk, tk)
        if k1_ref is None:
            return _dot(k0_ref[pl.ds(off, tk), :], qt2)
        return jnp.concatenate([_dot(k0_ref[pl.ds(off, tk), :], q0t), _dot(k1_ref[pl.ds(off, tk), :], q1t)], axis=1)

    def add_values(j, p, alpha):
        off = pl.multiple_of(j * tk, tk)
        for rows, lanes in ((slice(0, B_V), slice(0, tq)), (slice(B_V, LANES), slice(tq, 2 * tq))):
            prev = acc_ref[rows, :] if alpha is None else alpha[:, lanes] * acc_ref[rows, :]
            acc_ref[rows, :] = prev + _dot(vt_ref[rows, pl.ds(off, tk)], p[:, lanes])

    def finish(l):
        l = jnp.sum(l, axis=0, keepdims=True)
        o_ref[:B_V, :] = (acc_ref[:B_V, :] / l[:, :tq]).astype(o_ref.dtype)
        o_ref[B_V:, :] = (acc_ref[B_V:, :] / l[:, tq:]).astype(o_ref.dtype)

    def run_groups(one_group, carry):
        acc_ref[...] = jnp.zeros_like(acc_ref)
        return one_group(0, carry) if nk == group else lax.fori_loop(0, nk // group, one_group, carry)

    def fixed_shift(shift):
        def one_group(g, l):
            for c in range(group):
                p = jnp.exp2(score_values(g * group + c) - shift)
                l = l + p.reshape(tk // 8, 8, 2 * tq).sum(axis=0)
                add_values(g * group + c, p.astype(BF16), None)
            return l

        finish(run_groups(one_group, jnp.zeros((8, 2 * tq), F32)))

    def softmax(s_ref, p_ref, m, l):
        slabs = [pl.ds(r, ATTN_SLAB) for r in range(0, tk, ATTN_SLAB)]
        fold = lambda x: x.reshape(ATTN_SLAB // 8, 8, 2 * tq)
        mx = fold(s_ref[slabs[0], :]).max(axis=0)
        for sl in slabs[1:]:
            mx = jnp.maximum(mx, fold(s_ref[sl, :]).max(axis=0))
        m_new = jnp.maximum(m, jnp.max(mx, axis=0, keepdims=True))
        alpha = jnp.exp2(m - m_new)
        l = alpha * l
        for sl in slabs:
            p = jnp.exp2(s_ref[sl, :] - m_new)
            p_ref[sl, :] = p.astype(BF16)
            l = l + fold(p).sum(axis=0)
        return m_new, l, alpha

    def running_max():
        def one_group(g, carry):
            m, l = carry
            base = g * group
            for c in range(min(ns - 1, group)):
                s_refs[c % ns][...] = score_values(base + c)
            for c in range(group):
                if c + ns - 1 < group:
                    s_refs[(c + ns - 1) % ns][...] = score_values(base + c + ns - 1)
                p_ref = p_refs[c % len(p_refs)]
                m, l, alpha = softmax(s_refs[c % ns], p_ref, m, l)
                add_values(base + c, p_ref[...], alpha)
            return m, l

        carry = (jnp.full((1, 2 * tq), -jnp.inf, F32), jnp.zeros((8, 2 * tq), F32))
        finish(run_groups(one_group, carry)[1])

    lane = lax.broadcasted_iota(jnp.int32, (1, 2 * tq), 1)
    qsq = jnp.sum(jnp.square(qt2.astype(F32)), axis=0, keepdims=True)
    bound = jnp.sqrt(qsq * jnp.where(lane < tq, kmax_ref[0], kmax_ref[1]))
    small = jnp.max(bound) <= ATTN_FIXED_SHIFT_MAX
    pl.when(small)(lambda: fixed_shift(bound))
    pl.when(jnp.logical_not(small))(running_max)


def _max_sq_norm(k_ref, ones_ref, lanes):
    k = k_ref[...].astype(F32)
    n = _dot((k * k).astype(BF16), ones_ref[...])
    return jnp.max(n[:, lanes])


def _attn_a_body(qt_ref, k_ref, vt_ref, ones_ref, o_ref, *scratch, tq, tk):
    s_refs, p_refs = scratch[:ATTN_S_BUFS], scratch[ATTN_S_BUFS:-2]
    acc_ref, kmax_ref = scratch[-2:]

    @pl.when(pl.program_id(2) == 0)
    def _():
        kmax_ref[0] = _max_sq_norm(k_ref, ones_ref, slice(0, A_HEAD_DIM))
        kmax_ref[1] = _max_sq_norm(k_ref, ones_ref, slice(A_HEAD_DIM, LANES))

    @pl.loop(0, qt_ref.shape[1] // tq)
    def _(j):
        lanes = pl.ds(pl.multiple_of(j * tq, tq), tq)
        qt = qt_ref[:, lanes]
        zero = jnp.zeros((A_HEAD_DIM, tq), BF16)
        q0t = jnp.concatenate([qt[:A_HEAD_DIM], zero], axis=0)
        q1t = jnp.concatenate([zero, qt[A_HEAD_DIM:]], axis=0)
        _flash_pair(q0t, q1t, k_ref, None, vt_ref, o_ref.at[:, lanes], s_refs, p_refs, acc_ref, kmax_ref, tk)


def _attn_b_body(q0t_ref, q1t_ref, k0_ref, k1_ref, vt_ref, ones_ref, o_ref, *scratch, tq, tk):
    s_refs, p_refs = scratch[:ATTN_S_BUFS], scratch[ATTN_S_BUFS:-2]
    acc_ref, kmax_ref = scratch[-2:]

    @pl.when(pl.program_id(2) == 0)
    def _():
        kmax_ref[0] = _max_sq_norm(k0_ref, ones_ref, slice(0, LANES))
        kmax_ref[1] = _max_sq_norm(k1_ref, ones_ref, slice(0, LANES))

    @pl.loop(0, q0t_ref.shape[1] // tq)
    def _(j):
        lanes = pl.ds(pl.multiple_of(j * tq, tq), tq)
        _flash_pair(q0t_ref[:, lanes], q1t_ref[:, lanes], k0_ref, k1_ref, vt_ref, o_ref.at[:, lanes], s_refs, p_refs,
                    acc_ref, kmax_ref, tk)


def _attn_scratch(tq, tk):
    return ([pltpu.VMEM((tk, 2 * tq), F32)] * ATTN_S_BUFS + [pltpu.VMEM((tk, 2 * tq), BF16)] * ATTN_P_BUFS
            + [pltpu.VMEM((LANES, tq), F32), pltpu.SMEM((2,), F32)])


def _q_block(s, tq):
    return tq * ATTN_Q_TILES if s % (tq * ATTN_Q_TILES) == 0 else tq


def _head_ones(head_dim):
    group = jnp.arange(LANES) // head_dim
    return (group[:, None] == group[None, :]).astype(BF16)


def _attention_a(qat, ka, vat, nseq, s, tq, tk):
    tile = _q_block(s, tq)
    nq = s // tile
    qspec = pl.BlockSpec((LANES, tile), lambda b, h, i: (h, b * nq + i))
    kspec = pl.BlockSpec((s, LANES), lambda b, h, i: (b, 0))
    vspec = pl.BlockSpec((LANES, s), lambda b, h, i: (0, b))
    ones = pl.BlockSpec((LANES, LANES), lambda b, h, i: (0, 0))
    return pl.pallas_call(
        functools.partial(_attn_a_body, tq=tq, tk=tk),
        grid=(nseq, A_Q_COLS // LANES, nq),
        in_specs=[qspec, kspec, vspec, ones],
        out_specs=qspec,
        out_shape=jax.ShapeDtypeStruct(qat.shape, BF16),
        scratch_shapes=_attn_scratch(tq, tk),
        compiler_params=_cparams(("parallel", "parallel", "arbitrary")),
        name="attn_a",
    )(qat, ka, vat, _head_ones(A_HEAD_DIM))


def _attention_b(qbt, kb, vbt, nseq, s, tq, tk):
    tile = _q_block(s, tq)
    nq = s // tile
    q0 = pl.BlockSpec((LANES, tile), lambda b, h, i: (2 * h, b * nq + i))
    q1 = pl.BlockSpec((LANES, tile), lambda b, h, i: (2 * h + 1, b * nq + i))
    k0 = pl.BlockSpec((s, LANES), lambda b, h, i: (b, 2 * h))
    k1 = pl.BlockSpec((s, LANES), lambda b, h, i: (b, 2 * h + 1))
    v = pl.BlockSpec((LANES, s), lambda b, h, i: (h, b))
    o = pl.BlockSpec((LANES, tile), lambda b, h, i: (h, b * nq + i))
    ones = pl.BlockSpec((LANES, LANES), lambda b, h, i: (0, 0))
    return pl.pallas_call(
        functools.partial(_attn_b_body, tq=tq, tk=tk),
        grid=(nseq, B_O_COLS // LANES, nq),
        in_specs=[q0, q1, k0, k1, v, ones],
        out_specs=o,
        out_shape=jax.ShapeDtypeStruct(vbt.shape, BF16),
        scratch_shapes=_attn_scratch(tq, tk),
        compiler_params=_cparams(("parallel", "parallel", "arbitrary")),
        name="attn_b",
    )(qbt, qbt, kb, kb, vbt, _head_ones(LANES))


def _within(x, d, period, n):
    row = lax.broadcasted_iota(jnp.int32, x.shape, 0)
    return jnp.where((row % period) + d < period, pltpu.roll(x, n - d, 0), pltpu.roll(x, period - d, 0))


def _route(logits_t, bias):
    n = N_EXPERTS
    scores = jax.nn.sigmoid(logits_t)
    biased = scores + bias
    row = lax.broadcasted_iota(jnp.int32, biased.shape, 0)
    pos = row % EXPERTS_PER_GROUP
    rank = jnp.zeros(biased.shape, jnp.int32)
    for d in range(1, EXPERTS_PER_GROUP):
        other = _within(biased, d, EXPERTS_PER_GROUP, n)
        other_pos = (pos + d) % EXPERTS_PER_GROUP
        ahead = (other > biased) | ((other == biased) & (other_pos < pos))
        rank = rank + ahead.astype(jnp.int32)
    top2 = rank < 2
    kept = jnp.where(top2, biased, 0.0)
    gscore = kept
    for d in range(1, EXPERTS_PER_GROUP):
        gscore = gscore + _within(kept, d, EXPERTS_PER_GROUP, n)
    grp = row // EXPERTS_PER_GROUP
    win = jnp.ones(biased.shape, jnp.bool_)
    for d in range(1, N_GROUPS):
        other = pltpu.roll(gscore, n - d * EXPERTS_PER_GROUP, 0)
        other_grp = (grp + d) % N_GROUPS
        win = win & ((other < gscore) | ((other == gscore) & (other_grp > grp)))
    sel = top2 & win
    w = jnp.where(sel, scores, 0.0)
    return w / jnp.sum(w, axis=0, keepdims=True), sel


def _pack_pairs(v):
    n = v.shape[1] // 2
    vb = v.astype(BF16).astype(F32)
    hi = pltpu.bitcast(vb[:, :n], jnp.int32)
    lo = pltpu.bitcast(vb[:, n:], jnp.int32)
    return hi | lax.shift_right_logical(lo, 16)


def _unpack_pairs(w):
    hi = pltpu.bitcast(w & jnp.int32(-65536), F32)
    lo = pltpu.bitcast(lax.shift_left(w, 16), F32)
    return jnp.concatenate([hi, lo], axis=1)


R_E0, R_E1, R_RANK0, R_RANK1, R_W0, R_W1 = range(6)


def _mixer_out_body(oa_ref, ob_ref, ga_ref, gb_ref, x_ref, wba_ref, wbb_ref, wout_ref, gffn_ref, wrh_ref, wrl_ref,
                    br_ref, tri_ref, h_ref, tp_ref, route_ref, route_t_ref, count_ref):
    @pl.when(pl.program_id(0) == 0)
    def _():
        count_ref[...] = jnp.zeros_like(count_ref)

    ma = _dot_tn(oa_ref[...], wba_ref[...])
    mb = _dot_tn(ob_ref[...], wbb_ref[...])
    merged = ga_ref[...].astype(F32) * ma + gb_ref[...].astype(F32) * mb
    h = x_ref[...] + _dot(merged.astype(BF16), wout_ref[...])
    h_ref[...] = h
    t = _rms(h) * gffn_ref[...]
    t_hi, t_lo = _split_bf16(t)
    tp_ref[...] = _pack_pairs(t)
    wrh, wrl = wrh_ref[...], wrl_ref[...]
    logits_t = _dot_nt(wrh, t_hi) + _dot_nt(wrh, t_lo) + _dot_nt(wrl, t_hi)
    gates_t, sel = _route(logits_t, br_ref[...])

    tm = gates_t.shape[1]
    onehot = jnp.where(sel, 1.0, 0.0)
    before = _dot(onehot.astype(BF16), tri_ref[...])
    rank_t = count_ref[:, 0:1] + before
    count_ref[...] = count_ref[...] + jnp.sum(onehot, axis=1, keepdims=True)

    row = lax.broadcasted_iota(jnp.int32, sel.shape, 0).astype(F32)
    e0 = jnp.min(jnp.where(sel, row, float(N_EXPERTS)), axis=0, keepdims=True)
    e1 = jnp.max(jnp.where(sel, row, -1.0), axis=0, keepdims=True)
    pick = lambda v, e: jnp.sum(jnp.where(sel & (row == e), v, 0.0), axis=0, keepdims=True)
    rec = jnp.concatenate([e0, e1, pick(rank_t, e0), pick(rank_t, e1), pick(gates_t, e0), pick(gates_t, e1),
                           jnp.zeros((LANES - 6, tm), F32)], axis=0)
    route_ref[...] = rec.T
    route_t_ref[...] = rec[:8]


def _mixer_out(oa, ob, ga, gb, x, lw, l, tm):
    t = x.shape[0]
    row = lambda n: pl.BlockSpec((tm, n), lambda i: (i, 0))
    full = lambda a: pl.BlockSpec((None,) + a.shape[1:], lambda i: (l,) + (0,) * (a.ndim - 1))
    const = lambda a: pl.BlockSpec(a.shape, lambda i: (0,) * a.ndim)
    col = lambda n: pl.BlockSpec((n, tm), lambda i: (0, i))
    tri = (jnp.arange(tm)[:, None] < jnp.arange(tm)[None, :]).astype(BF16)
    ins = [oa, ob, ga, gb, x, lw["wba"], lw["wbb"], lw["wout"], lw["gffn"], lw["wrh"], lw["wrl"], lw["br"], tri]
    in_specs = [col(A_Q_COLS), col(B_O_COLS), row(D_MODEL), row(D_MODEL), row(D_MODEL),
                full(lw["wba"]), full(lw["wbb"]), full(lw["wout"]), full(lw["gffn"]),
                const(lw["wrh"]), const(lw["wrl"]), const(lw["br"]), const(tri)]
    return pl.pallas_call(
        _mixer_out_body,
        grid=(t // tm,),
        in_specs=in_specs,
        out_specs=[row(D_MODEL), row(D_MODEL // 2), row(LANES), col(8),
                   pl.BlockSpec((N_EXPERTS, LANES), lambda i: (0, 0))],
        out_shape=[jax.ShapeDtypeStruct((t, D_MODEL), F32), jax.ShapeDtypeStruct((t, D_MODEL // 2), jnp.int32),
                   jax.ShapeDtypeStruct((t, LANES), F32), jax.ShapeDtypeStruct((8, t), F32),
                   jax.ShapeDtypeStruct((N_EXPERTS, LANES), F32)],
        compiler_params=_cparams(("arbitrary",)),
        name="mixer_out",
    )(*ins)


def _sc_plan(nrows):
    info = plsc.get_sparse_core_info()
    workers = info.num_cores * info.num_subcores
    per_worker = nrows // workers
    chunk = min(SC_CHUNK, per_worker)
    assert per_worker * workers == nrows and per_worker % chunk == 0 and chunk % 8 == 0, (nrows, workers, chunk)
    return info.num_cores, per_worker, chunk


def _sc_mesh():
    return plsc.VectorSubcoreMesh(core_axis_name="core", subcore_axis_name="subcore")


def _scatter_rows(x, idx0, idx1, nrows):
    t, d = x.shape
    ncores, per_worker, chunk = _sc_plan(t)

    @functools.partial(
        pl.kernel, out_type=jax.ShapeDtypeStruct((nrows, d), x.dtype), mesh=_sc_mesh(), name="moe_dispatch",
        scratch_types=[pltpu.VMEM((chunk,), jnp.int32), pltpu.VMEM((chunk,), jnp.int32), pltpu.VMEM((chunk, d), x.dtype)])
    def run(x_hbm, i0_hbm, i1_hbm, o_hbm, i0_v, i1_v, rows_v):
        worker = lax.axis_index("subcore") * ncores + lax.axis_index("core")

        @pl.loop(0, per_worker // chunk)
        def _(c):
            base = pl.multiple_of(worker * per_worker + c * chunk, chunk)
            pltpu.sync_copy(x_hbm.at[pl.ds(base, chunk)], rows_v)
            pltpu.sync_copy(i0_hbm.at[pl.ds(base, chunk)], i0_v)
            pltpu.sync_copy(i1_hbm.at[pl.ds(base, chunk)], i1_v)
            pltpu.sync_copy(rows_v, o_hbm.at[i0_v])
            pltpu.sync_copy(rows_v, o_hbm.at[i1_v])

    return run(x, idx0, idx1)


def _gather_rows(table, idx):
    m = idx.shape[0]
    d = table.shape[1]
    ncores, per_worker, chunk = _sc_plan(m)

    @functools.partial(
        pl.kernel, out_type=jax.ShapeDtypeStruct((m, d), table.dtype), mesh=_sc_mesh(), name="moe_collect",
        scratch_types=[pltpu.VMEM((chunk,), jnp.int32), pltpu.VMEM((chunk, d), table.dtype)])
    def run(x_hbm, i_hbm, o_hbm, i_v, rows_v):
        worker = lax.axis_index("subcore") * ncores + lax.axis_index("core")

        @pl.loop(0, per_worker // chunk)
        def _(c):
            base = pl.multiple_of(worker * per_worker + c * chunk, chunk)
            pltpu.sync_copy(i_hbm.at[pl.ds(base, chunk)], i_v)
            pltpu.sync_copy(x_hbm.at[i_v], rows_v)
            pltpu.sync_copy(rows_v, o_hbm.at[pl.ds(base, chunk)])

    return run(table, idx)


def _routing_tables(route_t, counts, tr):
    t = route_t.shape[1]
    experts = jnp.arange(N_EXPERTS, dtype=jnp.int32)
    cnt = counts[:, 0].astype(jnp.int32)
    seg_end = jnp.cumsum(cnt)
    seg_start = seg_end - cnt
    lookup = lambda table, idx: jnp.sum(jnp.where(idx[None, :] == experts[:, None], table[:, None], 0), axis=0)
    e = route_t[R_E0:R_E1 + 1].astype(jnp.int32).reshape(2 * t)
    rank = route_t[R_RANK0:R_RANK1 + 1].astype(jnp.int32).reshape(2 * t)
    pos = lookup(seg_start, e) + rank

    n_tiles = 2 * t // tr
    n_visits = n_tiles + N_EXPERTS - 1
    first_tile = seg_start // tr
    last_tile = jnp.maximum(seg_end - 1, 0) // tr
    visits = jnp.where(cnt > 0, last_tile - first_tile + 1, 0)
    visit_end = jnp.cumsum(visits)
    visit_start = visit_end - visits
    g = jnp.arange(n_visits, dtype=jnp.int32)
    valid = g < visit_end[-1]
    ex = jnp.minimum(jnp.sum((g[None, :] >= visit_end[:, None]).astype(jnp.int32), axis=0), N_EXPERTS - 1)
    tile = lookup(first_tile - visit_start, ex) + g
    lo = jnp.clip(lookup(seg_start, ex) - tile * tr, 0, tr)
    hi = jnp.clip(lookup(seg_end, ex) - tile * tr, 0, tr)
    last_ex = jnp.max(jnp.where(cnt > 0, experts, 0))
    tile = jnp.where(valid, tile, n_tiles - 1)
    ex = jnp.where(valid, ex, last_ex)
    lo = jnp.where(valid, lo, 0)
    hi = jnp.where(valid, hi, 0)
    first = jnp.concatenate([jnp.ones((1,), jnp.int32), (tile[1:] != tile[:-1]).astype(jnp.int32)])
    first = jnp.where(valid, first, 0)
    return pos, (tile, ex, lo, hi, first)


def _experts_body(tile_ref, ex_ref, lo_ref, hi_ref, first_ref, xs_ref, wg_ref, wu_ref, wd_ref, ys_ref, acc_ref):
    g = pl.program_id(0)
    lo, hi = lo_ref[g], hi_ref[g]

    @pl.when(hi > lo)
    def _():
        x = _unpack_pairs(xs_ref[...]).astype(BF16)
        a = jax.nn.silu(_dot(x, wg_ref[...])) * _dot(x, wu_ref[...])
        row = lax.broadcasted_iota(jnp.int32, (a.shape[0], 1), 0)
        a = jnp.where((row >= lo) & (row < hi), a, 0.0)
        y = _dot(a.astype(BF16), wd_ref[...])

        @pl.when(first_ref[g] == 1)
        def _():
            acc_ref[...] = y

        @pl.when(first_ref[g] == 0)
        def _():
            acc_ref[...] += y

        ys_ref[...] = _pack_pairs(acc_ref[...])


def _experts(xs, visits, lw, l, tr):
    n_visits = visits[0].shape[0]
    rows = pl.BlockSpec((tr, D_MODEL // 2), lambda g, tile, ex, lo, hi, first: (tile[g], 0))
    wspec = lambda a: pl.BlockSpec((None, None) + a.shape[2:], lambda g, tile, ex, lo, hi, first: (l, ex[g], 0, 0))
    return pl.pallas_call(
        _experts_body,
        grid_spec=pltpu.PrefetchScalarGridSpec(
            num_scalar_prefetch=5,
            grid=(n_visits,),
            in_specs=[rows, wspec(lw["wg"]), wspec(lw["wu"]), wspec(lw["wd"])],
            out_specs=rows,
            scratch_shapes=[pltpu.VMEM((tr, D_MODEL), F32)],
        ),
        out_shape=jax.ShapeDtypeStruct(xs.shape, jnp.int32),
        compiler_params=_cparams(("arbitrary",)),
        name="experts",
    )(*visits, xs, lw["wg"], lw["wu"], lw["wd"])


def _moe_residual(h_ref, y0_ref, y1_ref, route_ref):
    r = route_ref[...]
    return (h_ref[...] + r[:, R_W0:R_W0 + 1] * _unpack_pairs(y0_ref[...])
            + r[:, R_W1:R_W1 + 1] * _unpack_pairs(y1_ref[...]))


def _final_body(h_ref, y0_ref, y1_ref, route_ref, gfin_ref, o_ref):
    o_ref[...] = _rms(_moe_residual(h_ref, y0_ref, y1_ref, route_ref)) * gfin_ref[...]


def _final(h, y, route, gfin, tm):
    t = h.shape[0]
    row = lambda n: pl.BlockSpec((tm, n), lambda i: (i, 0))
    second = pl.BlockSpec((tm, D_MODEL // 2), lambda i: (i + t // tm, 0))
    return pl.pallas_call(
        _final_body,
        grid=(t // tm,),
        in_specs=[row(D_MODEL), row(D_MODEL // 2), second, row(LANES), pl.BlockSpec(gfin.shape, lambda i: (0, 0))],
        out_specs=row(D_MODEL),
        out_shape=jax.ShapeDtypeStruct(h.shape, F32),
        compiler_params=_cparams(("parallel",)),
        name="final",
    )(h, y, y, route, gfin)


def _moe_rows(tp, route_t, counts, lw, l, tr):
    t = tp.shape[0]
    pos, visits = _routing_tables(route_t, counts, tr)
    xs = _scatter_rows(tp, pos[:t], pos[t:], 2 * t)
    ys = _experts(xs, visits, lw, l, tr)
    return _gather_rows(ys, pos)


def _rope_angles(seq_len, rot_dim):
    rows = seq_len // GRID_W
    row = jnp.broadcast_to(jnp.arange(rows)[:, None], (rows, GRID_W)).reshape(-1).astype(F32)
    col = jnp.broadcast_to(jnp.arange(GRID_W)[None, :], (rows, GRID_W)).reshape(-1).astype(F32)
    axis_dim = rot_dim // 2
    inv_freq = jnp.power(jnp.float32(ROPE_THETA), -jnp.arange(0, axis_dim, 2, dtype=F32) / axis_dim)
    ang = jnp.concatenate([row[:, None] * inv_freq[None, :], col[:, None] * inv_freq[None, :]], axis=-1)
    return jnp.cos(ang), jnp.sin(ang)


def _rope_tables(nseq, seq_len):
    c, s = _rope_angles(seq_len, A_HEAD_DIM)
    ca = jnp.concatenate([c, c, c, c], axis=-1)
    sa = jnp.concatenate([-s, s, -s, s], axis=-1)
    c, s = _rope_angles(seq_len, B_ROPE)
    ones = jnp.ones((seq_len, B_NOPE), F32)
    zeros = jnp.zeros((seq_len, B_NOPE), F32)
    tail = LANES - B_NOPE - B_ROPE
    cb = jnp.concatenate([ones, c, c, ones[:, :tail]], axis=-1)
    sb = jnp.concatenate([zeros, -s, s, zeros[:, :tail]], axis=-1)
    return tuple(jnp.tile(a, (nseq, 1)) for a in (ca, sa, cb, sb))


def _prepare_weights(norm_mix, w_in, a_q_norm, a_k_norm, b_q_norm, b_kv_norm, w_q_up, w_kv_up, w_branch_a,
                     w_branch_b, w_out, norm_ffn, w_router, b_router, w_gate, w_up, w_down):
    depth = w_in.shape[0]
    order = jnp.array(A_HEAD_ORDER)
    parts = []
    start = 0
    for n in (A_Q_COLS, A_KV_COLS, A_KV_COLS, B_Q_RANK, B_KV_RANK, B_ROPE, D_MODEL, D_MODEL):
        parts.append(w_in[..., start:start + n])
        start += n
    qa, ka, va, cq, ckv, kr, ga, gb = parts
    qa = qa.reshape(depth, D_MODEL, A_HEADS, A_HEAD_DIM)[:, :, order].reshape(depth, D_MODEL, A_Q_COLS)
    kr = jnp.pad(kr, ((0, 0), (0, 0), (B_NOPE, LANES - B_NOPE - B_ROPE)))
    win = jnp.concatenate([qa, ka, va, cq, ckv, kr, ga, gb], axis=-1).astype(BF16)

    wq = w_q_up.reshape(depth, B_Q_RANK, B_HEADS, B_QK_DIM)
    wq = jnp.pad(wq, ((0, 0), (0, 0), (0, 0), (0, LANES - B_QK_DIM))).reshape(depth, B_Q_RANK, B_PAD_COLS)
    wkv = w_kv_up.reshape(depth, B_KV_RANK, B_HEADS, B_NOPE + B_V)
    wk = jnp.pad(wkv[..., :B_NOPE], ((0, 0), (0, 0), (0, 0), (0, LANES - B_NOPE))).reshape(depth, B_KV_RANK, B_PAD_COLS)
    wv = wkv[..., B_NOPE:].reshape(depth, B_KV_RANK, B_O_COLS)
    wba = w_branch_a.reshape(depth, A_HEADS, A_HEAD_DIM, D_MODEL)[:, order].reshape(depth, A_Q_COLS, D_MODEL)

    group = jnp.arange(A_Q_COLS) // A_HEAD_DIM
    bd = (group[:, None] == group[None, :]).astype(BF16)
    wr_t = w_router.T
    wrh, wrl = _split_bf16(wr_t)
    vec = lambda a: a[:, None, :]
    return dict(
        gmix=vec(norm_mix), win=win,
        gq=vec(jnp.tile(a_q_norm, (1, A_HEADS)) * (A_HEAD_DIM ** -0.5 * LOG2E)), gk=vec(jnp.tile(a_k_norm, (1, A_KV_HEADS))),
        gbq=vec(b_q_norm), gbkv=vec(b_kv_norm),
        wq=wq.astype(BF16), wk=wk.astype(BF16), wv=wv.astype(BF16), bd=bd,
        wba=wba.astype(BF16), wbb=w_branch_b.astype(BF16), wout=w_out.astype(BF16), gffn=vec(norm_ffn),
        wrh=wrh, wrl=wrl, br=b_router[:, None].astype(F32),
        wg=w_gate.astype(BF16), wu=w_up.astype(BF16), wd=w_down.astype(BF16),
    )


def _pick(n, candidates):
    for c in candidates:
        if n % c == 0:
            return c
    raise ValueError(f"no tile in {candidates} divides {n}")


def _trunk(x3, lw, norm_final):
    nseq, s, _ = x3.shape
    t = nseq * s
    x = x3.reshape(t, D_MODEL)
    tabs = _rope_tables(nseq, s)
    depth = lw["win"].shape[0]
    tm = _pick(t, (256,))
    tm_out = _pick(t, (512, 256))
    tq = _pick(s, (512, 256))
    tk = _pick(s, (512,))
    moe_out = None
    for l in range(depth):
        x_new, qa, ka, va, qb, kb, vb, ga, gb = _mixer_in(x, moe_out, lw, l, tabs, tm_out)
        x = x if x_new is None else x_new
        oa = _attention_a(qa, ka, va, nseq, s, tq, tk)
        ob = _attention_b(qb, kb, vb, nseq, s, tq, tk)
        h, tp, route, route_t, counts = _mixer_out(oa, ob, ga, gb, x, lw, l, tm_out)
        moe_out = (h, _moe_rows(tp, route_t, counts, lw, l, EXPERT_ROWS), route)
        x = None
    return _final(*moe_out, norm_final[None, :], tm).reshape(x3.shape)


def kernel(x_prompt, x_sample, norm_mix, w_in, a_q_norm, a_k_norm, b_q_norm, b_kv_norm, w_q_up, w_kv_up, w_branch_a,
           w_branch_b, w_out, norm_ffn, w_router, b_router, w_gate, w_up, w_down, norm_final):
    lw = _prepare_weights(norm_mix, w_in, a_q_norm, a_k_norm, b_q_norm, b_kv_norm, w_q_up, w_kv_up, w_branch_a,
                          w_branch_b, w_out, norm_ffn, w_router, b_router, w_gate, w_up, w_down)
    return _trunk(x_prompt, lw, norm_final), _trunk(x_sample, lw, norm_final)
```

```python
import functools

import jax
import jax.numpy as jnp
from jax import lax
from jax.experimental import pallas as pl
from jax.experimental.pallas import tpu as pltpu
from jax.experimental.pallas import tpu_sc as plsc

F32 = jnp.float32
BF16 = jnp.bfloat16

D_MODEL = 1024
GRID_W = 64
ROPE_THETA = 10000.0
EPS = 1e-6
A_HEADS = 8
A_KV_HEADS = 2
A_HEAD_DIM = 64
B_HEADS = 8
B_Q_RANK = 384
B_KV_RANK = 256
B_NOPE = 64
B_ROPE = 32
B_V = 64
N_EXPERTS = 16
N_GROUPS = 4
EXPERTS_PER_GROUP = N_EXPERTS // N_GROUPS
D_EXPERT = 512
A_Q_COLS = A_HEADS * A_HEAD_DIM
A_KV_COLS = A_KV_HEADS * A_HEAD_DIM
B_QK_DIM = B_NOPE + B_ROPE
B_O_COLS = B_HEADS * B_V

LANES = 128
LOG2E = 1.4426950408889634

C_QA = 0
C_KA = C_QA + A_Q_COLS
C_VA = C_KA + A_KV_COLS
C_CQ = C_VA + A_KV_COLS
C_CKV = C_CQ + B_Q_RANK
C_KR = C_CKV + B_KV_RANK
C_GA = C_KR + LANES
C_GB = C_GA + D_MODEL
C_END = C_GB + D_MODEL
B_PAD_COLS = B_HEADS * LANES

A_HEAD_ORDER = (0, 4, 1, 5, 2, 6, 3, 7)

VMEM_LIMIT = 56 * 1024 * 1024

ATTN_GROUP = 8
ATTN_S_BUFS = 3
ATTN_P_BUFS = 2
ATTN_SLAB = 32
ATTN_FIXED_SHIFT_MAX = 48.0
EXPERT_ROWS = 512
SC_CHUNK = 128


def _cparams(sem):
    return pltpu.CompilerParams(dimension_semantics=sem, vmem_limit_bytes=VMEM_LIMIT)


def _dot(a, b):
    return jnp.dot(a, b, preferred_element_type=F32)


def _dot_nt(a, b):
    return lax.dot_general(a, b, (((1,), (1,)), ((), ())), preferred_element_type=F32)


def _dot_tn(a, b):
    return lax.dot_general(a, b, (((0,), (0,)), ((), ())), preferred_element_type=F32)


def _rms(x):
    return x * lax.rsqrt(jnp.mean(x * x, axis=-1, keepdims=True) + EPS)


def _split_bf16(x):
    hi = x.astype(BF16)
    lo = (x - hi.astype(F32)).astype(BF16)
    return hi, lo


def _group_mean_sq(v, bd):
    return _dot((v * v).astype(BF16), bd) * (1.0 / A_HEAD_DIM)


def _rope_a(v, cos, sin):
    n = v.shape[-1]
    lane = lax.broadcasted_iota(jnp.int32, v.shape, 1)
    low = (lane % A_HEAD_DIM) < (A_HEAD_DIM // 2)
    swapped = jnp.where(low, pltpu.roll(v, n - A_HEAD_DIM // 2, 1), pltpu.roll(v, A_HEAD_DIM // 2, 1))
    return v * cos + swapped * sin


def _rope_b(v, cos, sin):
    n = v.shape[-1]
    lane = lax.broadcasted_iota(jnp.int32, v.shape, 1)
    low = (lane % LANES) < (B_NOPE + B_ROPE // 2)
    swapped = jnp.where(low, pltpu.roll(v, n - B_ROPE // 2, 1), pltpu.roll(v, B_ROPE // 2, 1))
    return v * cos + swapped * sin


def _mixer_in_body(*refs, fused):
    if fused:
        h_ref, y0_ref, y1_ref, route_ref, *refs = refs
    else:
        x_ref, *refs = refs
    (gmix_ref, win_ref, ca_ref, sa_ref, cb_ref, sb_ref, gq_ref, gk_ref, gbq_ref, gbkv_ref, wq_ref, wk_ref, wv_ref,
     bd_ref, *outs) = refs
    if fused:
        x_out_ref, *outs = outs
        x = _moe_residual(h_ref, y0_ref, y1_ref, route_ref)
        x_out_ref[...] = x
    else:
        x = x_ref[...]
    qat_ref, ka_ref, vat_ref, qbt_ref, kb_ref, vbt_ref, ga_ref, gb_ref = outs
    hb = (_rms(x) * gmix_ref[...]).astype(BF16)
    u = _dot(hb, win_ref[:, C_QA:C_GA])
    bd = bd_ref[...]
    ca, sa = ca_ref[...], sa_ref[...]
    cb, sb = cb_ref[...], sb_ref[...]

    qa = u[:, C_QA:C_KA]
    qa = qa * lax.rsqrt(_group_mean_sq(qa, bd) + EPS) * gq_ref[...]
    qa = _rope_a(qa, jnp.concatenate([ca] * (A_Q_COLS // LANES), axis=1),
                 jnp.concatenate([sa] * (A_Q_COLS // LANES), axis=1))
    qat_ref[...] = qa.T.astype(BF16)
    ka = u[:, C_KA:C_VA]
    ka = ka * lax.rsqrt(_group_mean_sq(ka, bd[:A_KV_COLS, :A_KV_COLS]) + EPS) * gk_ref[...]
    ka_ref[...] = _rope_a(ka, ca, sa).astype(BF16)
    vat_ref[...] = u[:, C_VA:C_CQ].T.astype(BF16)

    cq = (_rms(u[:, C_CQ:C_CKV]) * gbq_ref[...]).astype(BF16)
    qb = _dot(cq, wq_ref[...])
    qb = _rope_b(qb, jnp.concatenate([cb] * B_HEADS, axis=1), jnp.concatenate([sb] * B_HEADS, axis=1))
    qbt_ref[...] = (qb * (B_QK_DIM ** -0.5 * LOG2E)).T.astype(BF16)
    ckv = (_rms(u[:, C_CKV:C_KR]) * gbkv_ref[...]).astype(BF16)
    kr = _rope_b(u[:, C_KR:C_GA], cb, sb)
    kb_ref[...] = (_dot(ckv, wk_ref[...]) + jnp.concatenate([kr] * B_HEADS, axis=1)).astype(BF16)
    vbt_ref[...] = _dot(ckv, wv_ref[...]).T.astype(BF16)

    g = jax.nn.sigmoid(_dot(hb, win_ref[:, C_GA:C_END]))
    ga_ref[...] = g[:, :D_MODEL].astype(BF16)
    gb_ref[...] = g[:, D_MODEL:].astype(BF16)


def _mixer_in(x, moe_out, lw, l, tabs, tm):
    fused = x is None
    t = moe_out[0].shape[0] if fused else x.shape[0]
    row = lambda n: pl.BlockSpec((tm, n), lambda i: (i, 0))
    full = lambda a: pl.BlockSpec((None,) + a.shape[1:], lambda i: (l,) + (0,) * (a.ndim - 1))
    const = lambda a: pl.BlockSpec(a.shape, lambda i: (0,) * a.ndim)
    col = lambda n: pl.BlockSpec((n, tm), lambda i: (0, i))
    ca, sa, cb, sb = tabs
    if fused:
        h, y, route = moe_out
        second = pl.BlockSpec((tm, D_MODEL // 2), lambda i: (i + t // tm, 0))
        ins, in_specs = [h, y, y, route], [row(D_MODEL), row(D_MODEL // 2), second, row(LANES)]
    else:
        ins, in_specs = [x], [row(D_MODEL)]
    ins += [lw["gmix"], lw["win"], ca, sa, cb, sb, lw["gq"], lw["gk"], lw["gbq"], lw["gbkv"],
            lw["wq"], lw["wk"], lw["wv"], lw["bd"]]
    in_specs += [full(lw["gmix"]), full(lw["win"]), row(LANES), row(LANES), row(LANES), row(LANES),
                 full(lw["gq"]), full(lw["gk"]), full(lw["gbq"]), full(lw["gbkv"]),
                 full(lw["wq"]), full(lw["wk"]), full(lw["wv"]), const(lw["bd"])]
    widths = [A_Q_COLS, A_KV_COLS, A_KV_COLS, B_PAD_COLS, B_PAD_COLS, B_O_COLS, D_MODEL, D_MODEL]
    transposed = [True, False, True, True, False, True, False, False]
    out_specs = [col(n) if tr else row(n) for n, tr in zip(widths, transposed)]
    out_shape = [jax.ShapeDtypeStruct((n, t) if tr else (t, n), BF16) for n, tr in zip(widths, transposed)]
    if fused:
        out_specs.insert(0, row(D_MODEL))
        out_shape.insert(0, jax.ShapeDtypeStruct((t, D_MODEL), F32))
    outs = pl.pallas_call(
        functools.partial(_mixer_in_body, fused=fused),
        grid=(t // tm,),
        in_specs=in_specs,
        out_specs=out_specs,
        out_shape=out_shape,
        compiler_params=_cparams(("parallel",)),
        name="mixer_in",
    )(*ins)
    return tuple(outs) if fused else (None,) + tuple(outs)


def _flash_pair(q0t, q1t, k0_ref, k1_ref, vt_ref, o_ref, s_refs, p_refs, acc_ref, kmax2, fixed, tk):
    tq = q0t.shape[1]
    nk = vt_ref.shape[1] // tk
    group = min(nk, ATTN_GROUP)
    ns = len(s_refs)
    qt2 = jnp.concatenate([q0t, q1t], axis=1)

    def score_values(j):
        off = pl.multiple_of(j * tk, tk)
        if k1_ref is None:
            return _dot(k0_ref[pl.ds(off, tk), :], qt2)
        return jnp.concatenate([_dot(k0_ref[pl.ds(off, tk), :], q0t), _dot(k1_ref[pl.ds(off, tk), :], q1t)], axis=1)

    def add_values(j, p, alpha):
        off = pl.multiple_of(j * tk, tk)
        for rows, lanes in ((slice(0, B_V), slice(0, tq)), (slice(B_V, LANES), slice(tq, 2 * tq))):
            prev = acc_ref[rows, :] if alpha is None else alpha[:, lanes] * acc_ref[rows, :]
            acc_ref[rows, :] = prev + _dot(vt_ref[rows, pl.ds(off, tk)], p[:, lanes])

    def finish(l):
        l = jnp.sum(l, axis=0, keepdims=True)
        o_ref[:B_V, :] = (acc_ref[:B_V, :] / l[:, :tq]).astype(o_ref.dtype)
        o_ref[B_V:, :] = (acc_ref[B_V:, :] / l[:, tq:]).astype(o_ref.dtype)

    def run_groups(one_group, carry):
        acc_ref[...] = jnp.zeros_like(acc_ref)
        return one_group(0, carry) if nk == group else lax.fori_loop(0, nk // group, one_group, carry)

    def fixed_shift(shift):
        def one_group(g, l):
            for c in range(group):
                p = jnp.exp2(score_values(g * group + c) - shift)
                l = l + p.reshape(tk // 8, 8, 2 * tq).sum(axis=0)
                add_values(g * group + c, p.astype(BF16), None)
            return l

        finish(run_groups(one_group, jnp.zeros((8, 2 * tq), F32)))

    def softmax(s_ref, p_ref, m, l):
        slabs = [pl.ds(r, ATTN_SLAB) for r in range(0, tk, ATTN_SLAB)]
        fold = lambda x: x.reshape(ATTN_SLAB // 8, 8, 2 * tq)
        mx = fold(s_ref[slabs[0], :]).max(axis=0)
        for sl in slabs[1:]:
            mx = jnp.maximum(mx, fold(s_ref[sl, :]).max(axis=0))
        m_new = jnp.maximum(m, jnp.max(mx, axis=0, keepdims=True))
        alpha = jnp.exp2(m - m_new)
        l = alpha * l
        for sl in slabs:
            p = jnp.exp2(s_ref[sl, :] - m_new)
            p_ref[sl, :] = p.astype(BF16)
            l = l + fold(p).sum(axis=0)
        return m_new, l, alpha

    def running_max():
        def one_group(g, carry):
            m, l = carry
            base = g * group
            for c in range(min(ns - 1, group)):
                s_refs[c % ns][...] = score_values(base + c)
            for c in range(group):
                if c + ns - 1 < group:
                    s_refs[(c + ns - 1) % ns][...] = score_values(base + c + ns - 1)
                p_ref = p_refs[c % len(p_refs)]
                m, l, alpha = softmax(s_refs[c % ns], p_ref, m, l)
                add_values(base + c, p_ref[...], alpha)
            return m, l

        carry = (jnp.full((1, 2 * tq), -jnp.inf, F32), jnp.zeros((8, 2 * tq), F32))
        finish(run_groups(one_group, carry)[1])

    if fixed:
        lane = lax.broadcasted_iota(jnp.int32, (1, 2 * tq), 1)
        qsq = jnp.sum(jnp.square(qt2.astype(F32)), axis=0, keepdims=True)
        fixed_shift(jnp.sqrt(qsq * jnp.where(lane < tq, kmax2[0], kmax2[1])))
    else:
        running_max()


def _max_sq_norm(k_ref, ones_ref, lanes):
    k = k_ref[...].astype(F32)
    n = _dot((k * k).astype(BF16), ones_ref[...])
    return jnp.max(n[:, lanes])


def _max_query_sq_norm(q):
    q = q.astype(F32)
    return jnp.max(jnp.sum(q * q, axis=0, keepdims=True))


def _attention_tiles(q_pair, k0_ref, k1_ref, vt_ref, o_ref, scratch, qmax2, kmax2, tq, tk):
    s_refs, p_refs, acc_ref = scratch[:ATTN_S_BUFS], scratch[ATTN_S_BUFS:-1], scratch[-1]
    bound2 = jnp.maximum(qmax2[0] * kmax2[0], qmax2[1] * kmax2[1])
    small = bound2 <= ATTN_FIXED_SHIFT_MAX * ATTN_FIXED_SHIFT_MAX

    def walk(fixed):
        @pl.loop(0, o_ref.shape[1] // tq)
        def _(j):
            lanes = pl.ds(pl.multiple_of(j * tq, tq), tq)
            q0t, q1t = q_pair(lanes)
            _flash_pair(q0t, q1t, k0_ref, k1_ref, vt_ref, o_ref.at[:, lanes], s_refs, p_refs, acc_ref, kmax2, fixed, tk)

    pl.when(small)(lambda: walk(True))
    pl.when(jnp.logical_not(small))(lambda: walk(False))


def _attn_a_body(qt_ref, k_ref, vt_ref, ones_ref, o_ref, *scratch, tq, tk):
    kmax2 = (_max_sq_norm(k_ref, ones_ref, slice(0, A_HEAD_DIM)), _max_sq_norm(k_ref, ones_ref, slice(A_HEAD_DIM, LANES)))
    qmax2 = (_max_query_sq_norm(qt_ref[:A_HEAD_DIM, :]), _max_query_sq_norm(qt_ref[A_HEAD_DIM:, :]))

    def q_pair(lanes):
        qt = qt_ref[:, lanes]
        zero = jnp.zeros((A_HEAD_DIM, tq), BF16)
        return jnp.concatenate([qt[:A_HEAD_DIM], zero], axis=0), jnp.concatenate([zero, qt[A_HEAD_DIM:]], axis=0)

    _attention_tiles(q_pair, k_ref, None, vt_ref, o_ref, scratch, qmax2, kmax2, tq, tk)


def _attn_b_body(q0t_ref, q1t_ref, k0_ref, k1_ref, vt_ref, ones_ref, o_ref, *scratch, tq, tk):
    kmax2 = (_max_sq_norm(k0_ref, ones_ref, slice(0, LANES)), _max_sq_norm(k1_ref, ones_ref, slice(0, LANES)))
    qmax2 = (_max_query_sq_norm(q0t_ref[...]), _max_query_sq_norm(q1t_ref[...]))
    q_pair = lambda lanes: (q0t_ref[:, lanes], q1t_ref[:, lanes])
    _attention_tiles(q_pair, k0_ref, k1_ref, vt_ref, o_ref, scratch, qmax2, kmax2, tq, tk)


def _attn_scratch(tq, tk):
    return ([pltpu.VMEM((tk, 2 * tq), F32)] * ATTN_S_BUFS + [pltpu.VMEM((tk, 2 * tq), BF16)] * ATTN_P_BUFS
            + [pltpu.VMEM((LANES, tq), F32)])


def _head_ones(head_dim):
    group = jnp.arange(LANES) // head_dim
    return (group[:, None] == group[None, :]).astype(BF16)


def _attention_a(qat, ka, vat, nseq, s, tq, tk):
    qspec = pl.BlockSpec((LANES, s), lambda b, h: (h, b))
    kspec = pl.BlockSpec((s, LANES), lambda b, h: (b, 0))
    vspec = pl.BlockSpec((LANES, s), lambda b, h: (0, b))
    ones = pl.BlockSpec((LANES, LANES), lambda b, h: (0, 0))
    return pl.pallas_call(
        functools.partial(_attn_a_body, tq=tq, tk=tk),
        grid=(nseq, A_Q_COLS // LANES),
        in_specs=[qspec, kspec, vspec, ones],
        out_specs=qspec,
        out_shape=jax.ShapeDtypeStruct(qat.shape, BF16),
        scratch_shapes=_attn_scratch(tq, tk),
        compiler_params=_cparams(("parallel", "parallel")),
        name="attn_a",
    )(qat, ka, vat, _head_ones(A_HEAD_DIM))


def _attention_b(qbt, kb, vbt, nseq, s, tq, tk):
    q0 = pl.BlockSpec((LANES, s), lambda b, h: (2 * h, b))
    q1 = pl.BlockSpec((LANES, s), lambda b, h: (2 * h + 1, b))
    k0 = pl.BlockSpec((s, LANES), lambda b, h: (b, 2 * h))
    k1 = pl.BlockSpec((s, LANES), lambda b, h: (b, 2 * h + 1))
    v = pl.BlockSpec((LANES, s), lambda b, h: (h, b))
    o = pl.BlockSpec((LANES, s), lambda b, h: (h, b))
    ones = pl.BlockSpec((LANES, LANES), lambda b, h: (0, 0))
    return pl.pallas_call(
        functools.partial(_attn_b_body, tq=tq, tk=tk),
        grid=(nseq, B_O_COLS // LANES),
        in_specs=[q0, q1, k0, k1, v, ones],
        out_specs=o,
        out_shape=jax.ShapeDtypeStruct(vbt.shape, BF16),
        scratch_shapes=_attn_scratch(tq, tk),
        compiler_params=_cparams(("parallel", "parallel")),
        name="attn_b",
    )(qbt, qbt, kb, kb, vbt, _head_ones(LANES))


def _within(x, d, period, n):
    row = lax.broadcasted_iota(jnp.int32, x.shape, 0)
    return jnp.where((row % period) + d < period, pltpu.roll(x, n - d, 0), pltpu.roll(x, period - d, 0))


def _route(logits_t, bias):
    n = N_EXPERTS
    scores = jax.nn.sigmoid(logits_t)
    biased = scores + bias
    row = lax.broadcasted_iota(jnp.int32, biased.shape, 0)
    pos = row % EXPERTS_PER_GROUP
    rank = jnp.zeros(biased.shape, jnp.int32)
    for d in range(1, EXPERTS_PER_GROUP):
        other = _within(biased, d, EXPERTS_PER_GROUP, n)
        other_pos = (pos + d) % EXPERTS_PER_GROUP
        ahead = (other > biased) | ((other == biased) & (other_pos < pos))
        rank = rank + ahead.astype(jnp.int32)
    top2 = rank < 2
    kept = jnp.where(top2, biased, 0.0)
    gscore = kept
    for d in range(1, EXPERTS_PER_GROUP):
        gscore = gscore + _within(kept, d, EXPERTS_PER_GROUP, n)
    grp = row // EXPERTS_PER_GROUP
    win = jnp.ones(biased.shape, jnp.bool_)
    for d in range(1, N_GROUPS):
        other = pltpu.roll(gscore, n - d * EXPERTS_PER_GROUP, 0)
        other_grp = (grp + d) % N_GROUPS
        win = win & ((other < gscore) | ((other == gscore) & (other_grp > grp)))
    sel = top2 & win
    w = jnp.where(sel, scores, 0.0)
    return w / jnp.sum(w, axis=0, keepdims=True), sel


def _pack_pairs(v):
    n = v.shape[1] // 2
    vb = v.astype(BF16).astype(F32)
    hi = pltpu.bitcast(vb[:, :n], jnp.int32)
    lo = pltpu.bitcast(vb[:, n:], jnp.int32)
    return hi | lax.shift_right_logical(lo, 16)


def _unpack_pairs(w):
    hi = pltpu.bitcast(w & jnp.int32(-65536), F32)
    lo = pltpu.bitcast(lax.shift_left(w, 16), F32)
    return jnp.concatenate([hi, lo], axis=1)


R_E0, R_E1, R_RANK0, R_RANK1, R_W0, R_W1 = range(6)


def _mixer_out_body(oa_ref, ob_ref, ga_ref, gb_ref, x_ref, wba_ref, wbb_ref, wout_ref, gffn_ref, wrh_ref, wrl_ref,
                    br_ref, tri_ref, h_ref, tp_ref, route_ref, route_t_ref, count_ref):
    @pl.when(pl.program_id(0) == 0)
    def _():
        count_ref[...] = jnp.zeros_like(count_ref)

    ma = _dot_tn(oa_ref[...], wba_ref[...])
    mb = _dot_tn(ob_ref[...], wbb_ref[...])
    merged = ga_ref[...].astype(F32) * ma + gb_ref[...].astype(F32) * mb
    h = x_ref[...] + _dot(merged.astype(BF16), wout_ref[...])
    h_ref[...] = h
    t = _rms(h) * gffn_ref[...]
    t_hi, t_lo = _split_bf16(t)
    tp_ref[...] = _pack_pairs(t)
    wrh, wrl = wrh_ref[...], wrl_ref[...]
    logits_t = _dot_nt(wrh, t_hi) + _dot_nt(wrh, t_lo) + _dot_nt(wrl, t_hi)
    gates_t, sel = _route(logits_t, br_ref[...])

    tm = gates_t.shape[1]
    onehot = jnp.where(sel, 1.0, 0.0)
    before = _dot(onehot.astype(BF16), tri_ref[...])
    rank_t = count_ref[:, 0:1] + before
    count_ref[...] = count_ref[...] + jnp.sum(onehot, axis=1, keepdims=True)

    row = lax.broadcasted_iota(jnp.int32, sel.shape, 0).astype(F32)
    e0 = jnp.min(jnp.where(sel, row, float(N_EXPERTS)), axis=0, keepdims=True)
    e1 = jnp.max(jnp.where(sel, row, -1.0), axis=0, keepdims=True)
    pick = lambda v, e: jnp.sum(jnp.where(sel & (row == e), v, 0.0), axis=0, keepdims=True)
    rec = jnp.concatenate([e0, e1, pick(rank_t, e0), pick(rank_t, e1), pick(gates_t, e0), pick(gates_t, e1),
                           jnp.zeros((LANES - 6, tm), F32)], axis=0)
    route_ref[...] = rec.T
    route_t_ref[...] = rec[:8]


def _mixer_out(oa, ob, ga, gb, x, lw, l, tm):
    t = x.shape[0]
    row = lambda n: pl.BlockSpec((tm, n), lambda i: (i, 0))
    full = lambda a: pl.BlockSpec((None,) + a.shape[1:], lambda i: (l,) + (0,) * (a.ndim - 1))
    const = lambda a: pl.BlockSpec(a.shape, lambda i: (0,) * a.ndim)
    col = lambda n: pl.BlockSpec((n, tm), lambda i: (0, i))
    tri = (jnp.arange(tm)[:, None] < jnp.arange(tm)[None, :]).astype(BF16)
    ins = [oa, ob, ga, gb, x, lw["wba"], lw["wbb"], lw["wout"], lw["gffn"], lw["wrh"], lw["wrl"], lw["br"], tri]
    in_specs = [col(A_Q_COLS), col(B_O_COLS), row(D_MODEL), row(D_MODEL), row(D_MODEL),
                full(lw["wba"]), full(lw["wbb"]), full(lw["wout"]), full(lw["gffn"]),
                const(lw["wrh"]), const(lw["wrl"]), const(lw["br"]), const(tri)]
    return pl.pallas_call(
        _mixer_out_body,
        grid=(t // tm,),
        in_specs=in_specs,
        out_specs=[row(D_MODEL), row(D_MODEL // 2), row(LANES), col(8),
                   pl.BlockSpec((N_EXPERTS, LANES), lambda i: (0, 0))],
        out_shape=[jax.ShapeDtypeStruct((t, D_MODEL), F32), jax.ShapeDtypeStruct((t, D_MODEL // 2), jnp.int32),
                   jax.ShapeDtypeStruct((t, LANES), F32), jax.ShapeDtypeStruct((8, t), F32),
                   jax.ShapeDtypeStruct((N_EXPERTS, LANES), F32)],
        compiler_params=_cparams(("arbitrary",)),
        name="mixer_out",
    )(*ins)


def _sc_plan(nrows):
    info = plsc.get_sparse_core_info()
    workers = info.num_cores * info.num_subcores
    per_worker = nrows // workers
    chunk = min(SC_CHUNK, per_worker)
    assert per_worker * workers == nrows and per_worker % chunk == 0 and chunk % 8 == 0, (nrows, workers, chunk)
    return info.num_cores, per_worker, chunk


def _sc_mesh():
    return plsc.VectorSubcoreMesh(core_axis_name="core", subcore_axis_name="subcore")


def _scatter_rows(x, idx0, idx1, nrows):
    t, d = x.shape
    ncores, per_worker, chunk = _sc_plan(t)

    @functools.partial(
        pl.kernel, out_type=jax.ShapeDtypeStruct((nrows, d), x.dtype), mesh=_sc_mesh(), name="moe_dispatch",
        scratch_types=[pltpu.VMEM((chunk,), jnp.int32), pltpu.VMEM((chunk,), jnp.int32), pltpu.VMEM((chunk, d), x.dtype)])
    def run(x_hbm, i0_hbm, i1_hbm, o_hbm, i0_v, i1_v, rows_v):
        worker = lax.axis_index("subcore") * ncores + lax.axis_index("core")

        @pl.loop(0, per_worker // chunk)
        def _(c):
            base = pl.multiple_of(worker * per_worker + c * chunk, chunk)
            pltpu.sync_copy(x_hbm.at[pl.ds(base, chunk)], rows_v)
            pltpu.sync_copy(i0_hbm.at[pl.ds(base, chunk)], i0_v)
            pltpu.sync_copy(i1_hbm.at[pl.ds(base, chunk)], i1_v)
            pltpu.sync_copy(rows_v, o_hbm.at[i0_v])
            pltpu.sync_copy(rows_v, o_hbm.at[i1_v])

    return run(x, idx0, idx1)


def _gather_rows(table, idx):
    m = idx.shape[0]
    d = table.shape[1]
    ncores, per_worker, chunk = _sc_plan(m)

    @functools.partial(
        pl.kernel, out_type=jax.ShapeDtypeStruct((m, d), table.dtype), mesh=_sc_mesh(), name="moe_collect",
        scratch_types=[pltpu.VMEM((chunk,), jnp.int32), pltpu.VMEM((chunk, d), table.dtype)])
    def run(x_hbm, i_hbm, o_hbm, i_v, rows_v):
        worker = lax.axis_index("subcore") * ncores + lax.axis_index("core")

        @pl.loop(0, per_worker // chunk)
        def _(c):
            base = pl.multiple_of(worker * per_worker + c * chunk, chunk)
            pltpu.sync_copy(i_hbm.at[pl.ds(base, chunk)], i_v)
            pltpu.sync_copy(x_hbm.at[i_v], rows_v)
            pltpu.sync_copy(rows_v, o_hbm.at[pl.ds(base, chunk)])

    return run(table, idx)


def _routing_tables(route_t, counts, tr):
    t = route_t.shape[1]
    experts = jnp.arange(N_EXPERTS, dtype=jnp.int32)
    cnt = counts[:, 0].astype(jnp.int32)
    seg_end = jnp.cumsum(cnt)
    seg_start = seg_end - cnt
    lookup = lambda table, idx: jnp.sum(jnp.where(idx[None, :] == experts[:, None], table[:, None], 0), axis=0)
    e = route_t[R_E0:R_E1 + 1].astype(jnp.int32).reshape(2 * t)
    rank = route_t[R_RANK0:R_RANK1 + 1].astype(jnp.int32).reshape(2 * t)
    pos = lookup(seg_start, e) + rank

    n_tiles = 2 * t // tr
    n_visits = n_tiles + N_EXPERTS - 1
    first_tile = seg_start // tr
    last_tile = jnp.maximum(seg_end - 1, 0) // tr
    visits = jnp.where(cnt > 0, last_tile - first_tile + 1, 0)
    visit_end = jnp.cumsum(visits)
    visit_start = visit_end - visits
    g = jnp.arange(n_visits, dtype=jnp.int32)
    valid = g < visit_end[-1]
    ex = jnp.minimum(jnp.sum((g[None, :] >= visit_end[:, None]).astype(jnp.int32), axis=0), N_EXPERTS - 1)
    tile = lookup(first_tile - visit_start, ex) + g
    lo = jnp.clip(lookup(seg_start, ex) - tile * tr, 0, tr)
    hi = jnp.clip(lookup(seg_end, ex) - tile * tr, 0, tr)
    last_ex = jnp.max(jnp.where(cnt > 0, experts, 0))
    tile = jnp.where(valid, tile, n_tiles - 1)
    ex = jnp.where(valid, ex, last_ex)
    lo = jnp.where(valid, lo, 0)
    hi = jnp.where(valid, hi, 0)
    first = jnp.concatenate([jnp.ones((1,), jnp.int32), (tile[1:] != tile[:-1]).astype(jnp.int32)])
    first = jnp.where(valid, first, 0)
    return pos, (tile, ex, lo, hi, first)


def _experts_body(tile_ref, ex_ref, lo_ref, hi_ref, first_ref, xs_ref, wg_ref, wu_ref, wd_ref, ys_ref, acc_ref):
    g = pl.program_id(0)
    lo, hi = lo_ref[g], hi_ref[g]

    @pl.when(hi > lo)
    def _():
        x = _unpack_pairs(xs_ref[...]).astype(BF16)
        a = jax.nn.silu(_dot(x, wg_ref[...])) * _dot(x, wu_ref[...])
        row = lax.broadcasted_iota(jnp.int32, (a.shape[0], 1), 0)
        a = jnp.where((row >= lo) & (row < hi), a, 0.0)
        y = _dot(a.astype(BF16), wd_ref[...])

        @pl.when(first_ref[g] == 1)
        def _():
            acc_ref[...] = y

        @pl.when(first_ref[g] == 0)
        def _():
            acc_ref[...] += y

        ys_ref[...] = _pack_pairs(acc_ref[...])


def _experts(xs, visits, lw, l, tr):
    n_visits = visits[0].shape[0]
    rows = pl.BlockSpec((tr, D_MODEL // 2), lambda g, tile, ex, lo, hi, first: (tile[g], 0))
    wspec = lambda a: pl.BlockSpec((None, None) + a.shape[2:], lambda g, tile, ex, lo, hi, first: (l, ex[g], 0, 0))
    return pl.pallas_call(
        _experts_body,
        grid_spec=pltpu.PrefetchScalarGridSpec(
            num_scalar_prefetch=5,
            grid=(n_visits,),
            in_specs=[rows, wspec(lw["wg"]), wspec(lw["wu"]), wspec(lw["wd"])],
            out_specs=rows,
            scratch_shapes=[pltpu.VMEM((tr, D_MODEL), F32)],
        ),
        out_shape=jax.ShapeDtypeStruct(xs.shape, jnp.int32),
        compiler_params=_cparams(("arbitrary",)),
        name="experts",
    )(*visits, xs, lw["wg"], lw["wu"], lw["wd"])


def _moe_residual(h_ref, y0_ref, y1_ref, route_ref):
    r = route_ref[...]
    return (h_ref[...] + r[:, R_W0:R_W0 + 1] * _unpack_pairs(y0_ref[...])
            + r[:, R_W1:R_W1 + 1] * _unpack_pairs(y1_ref[...]))


def _final_body(h_ref, y0_ref, y1_ref, route_ref, gfin_ref, o_ref):
    o_ref[...] = _rms(_moe_residual(h_ref, y0_ref, y1_ref, route_ref)) * gfin_ref[...]


def _final(h, y, route, gfin, tm):
    t = h.shape[0]
    row = lambda n: pl.BlockSpec((tm, n), lambda i: (i, 0))
    second = pl.BlockSpec((tm, D_MODEL // 2), lambda i: (i + t // tm, 0))
    return pl.pallas_call(
        _final_body,
        grid=(t // tm,),
        in_specs=[row(D_MODEL), row(D_MODEL // 2), second, row(LANES), pl.BlockSpec(gfin.shape, lambda i: (0, 0))],
        out_specs=row(D_MODEL),
        out_shape=jax.ShapeDtypeStruct(h.shape, F32),
        compiler_params=_cparams(("parallel",)),
        name="final",
    )(h, y, y, route, gfin)


def _moe_rows(tp, route_t, counts, lw, l, tr):
    t = tp.shape[0]
    pos, visits = _routing_tables(route_t, counts, tr)
    xs = _scatter_rows(tp, pos[:t], pos[t:], 2 * t)
    ys = _experts(xs, visits, lw, l, tr)
    return _gather_rows(ys, pos)


def _rope_angles(seq_len, rot_dim):
    rows = seq_len // GRID_W
    row = jnp.broadcast_to(jnp.arange(rows)[:, None], (rows, GRID_W)).reshape(-1).astype(F32)
    col = jnp.broadcast_to(jnp.arange(GRID_W)[None, :], (rows, GRID_W)).reshape(-1).astype(F32)
    axis_dim = rot_dim // 2
    inv_freq = jnp.power(jnp.float32(ROPE_THETA), -jnp.arange(0, axis_dim, 2, dtype=F32) / axis_dim)
    ang = jnp.concatenate([row[:, None] * inv_freq[None, :], col[:, None] * inv_freq[None, :]], axis=-1)
    return jnp.cos(ang), jnp.sin(ang)


def _rope_tables(nseq, seq_len):
    c, s = _rope_angles(seq_len, A_HEAD_DIM)
    ca = jnp.concatenate([c, c, c, c], axis=-1)
    sa = jnp.concatenate([-s, s, -s, s], axis=-1)
    c, s = _rope_angles(seq_len, B_ROPE)
    ones = jnp.ones((seq_len, B_NOPE), F32)
    zeros = jnp.zeros((seq_len, B_NOPE), F32)
    tail = LANES - B_NOPE - B_ROPE
    cb = jnp.concatenate([ones, c, c, ones[:, :tail]], axis=-1)
    sb = jnp.concatenate([zeros, -s, s, zeros[:, :tail]], axis=-1)
    return tuple(jnp.tile(a, (nseq, 1)) for a in (ca, sa, cb, sb))


def _prepare_weights(norm_mix, w_in, a_q_norm, a_k_norm, b_q_norm, b_kv_norm, w_q_up, w_kv_up, w_branch_a,
                     w_branch_b, w_out, norm_ffn, w_router, b_router, w_gate, w_up, w_down):
    depth = w_in.shape[0]
    order = jnp.array(A_HEAD_ORDER)
    parts = []
    start = 0
    for n in (A_Q_COLS, A_KV_COLS, A_KV_COLS, B_Q_RANK, B_KV_RANK, B_ROPE, D_MODEL, D_MODEL):
        parts.append(w_in[..., start:start + n])
        start += n
    qa, ka, va, cq, ckv, kr, ga, gb = parts
    qa = qa.reshape(depth, D_MODEL, A_HEADS, A_HEAD_DIM)[:, :, order].reshape(depth, D_MODEL, A_Q_COLS)
    kr = jnp.pad(kr, ((0, 0), (0, 0), (B_NOPE, LANES - B_NOPE - B_ROPE)))
    win = jnp.concatenate([qa, ka, va, cq, ckv, kr, ga, gb], axis=-1).astype(BF16)

    wq = w_q_up.reshape(depth, B_Q_RANK, B_HEADS, B_QK_DIM)
    wq = jnp.pad(wq, ((0, 0), (0, 0), (0, 0), (0, LANES - B_QK_DIM))).reshape(depth, B_Q_RANK, B_PAD_COLS)
    wkv = w_kv_up.reshape(depth, B_KV_RANK, B_HEADS, B_NOPE + B_V)
    wk = jnp.pad(wkv[..., :B_NOPE], ((0, 0), (0, 0), (0, 0), (0, LANES - B_NOPE))).reshape(depth, B_KV_RANK, B_PAD_COLS)
    wv = wkv[..., B_NOPE:].reshape(depth, B_KV_RANK, B_O_COLS)
    wba = w_branch_a.reshape(depth, A_HEADS, A_HEAD_DIM, D_MODEL)[:, order].reshape(depth, A_Q_COLS, D_MODEL)

    group = jnp.arange(A_Q_COLS) // A_HEAD_DIM
    bd = (group[:, None] == group[None, :]).astype(BF16)
    wr_t = w_router.T
    wrh, wrl = _split_bf16(wr_t)
    vec = lambda a: a[:, None, :]
    return dict(
        gmix=vec(norm_mix), win=win,
        gq=vec(jnp.tile(a_q_norm, (1, A_HEADS)) * (A_HEAD_DIM ** -0.5 * LOG2E)), gk=vec(jnp.tile(a_k_norm, (1, A_KV_HEADS))),
        gbq=vec(b_q_norm), gbkv=vec(b_kv_norm),
        wq=wq.astype(BF16), wk=wk.astype(BF16), wv=wv.astype(BF16), bd=bd,
        wba=wba.astype(BF16), wbb=w_branch_b.astype(BF16), wout=w_out.astype(BF16), gffn=vec(norm_ffn),
        wrh=wrh, wrl=wrl, br=b_router[:, None].astype(F32),
        wg=w_gate.astype(BF16), wu=w_up.astype(BF16), wd=w_down.astype(BF16),
    )


def _pick(n, candidates):
    for c in candidates:
        if n % c == 0:
            return c
    raise ValueError(f"no tile in {candidates} divides {n}")


def _trunk(x3, lw, norm_final):
    nseq, s, _ = x3.shape
    t = nseq * s
    x = x3.reshape(t, D_MODEL)
    tabs = _rope_tables(nseq, s)
    depth = lw["win"].shape[0]
    tm = _pick(t, (256,))
    tm_out = _pick(t, (512, 256))
    tq = _pick(s, (512, 256))
    tk = _pick(s, (512,))
    moe_out = None
    for l in range(depth):
        x_new, qa, ka, va, qb, kb, vb, ga, gb = _mixer_in(x, moe_out, lw, l, tabs, tm_out)
        x = x if x_new is None else x_new
        oa = _attention_a(qa, ka, va, nseq, s, tq, tk)
        ob = _attention_b(qb, kb, vb, nseq, s, tq, tk)
        h, tp, route, route_t, counts = _mixer_out(oa, ob, ga, gb, x, lw, l, tm_out)
        moe_out = (h, _moe_rows(tp, route_t, counts, lw, l, EXPERT_ROWS), route)
        x = None
    return _final(*moe_out, norm_final[None, :], tm).reshape(x3.shape)


def kernel(x_prompt, x_sample, norm_mix, w_in, a_q_norm, a_k_norm, b_q_norm, b_kv_norm, w_q_up, w_kv_up, w_branch_a,
           w_branch_b, w_out, norm_ffn, w_router, b_router, w_gate, w_up, w_down, norm_final):
    lw = _prepare_weights(norm_mix, w_in, a_q_norm, a_k_norm, b_q_norm, b_kv_norm, w_q_up, w_kv_up, w_branch_a,
                          w_branch_b, w_out, norm_ffn, w_router, b_router, w_gate, w_up, w_down)
    return _trunk(x_prompt, lw, norm_final), _trunk(x_sample, lw, norm_final)
```

```python
import functools

import jax
import jax.numpy as jnp
from jax import lax
from jax.experimental import pallas as pl
from jax.experimental.pallas import tpu as pltpu
from jax.experimental.pallas import tpu_sc as plsc

F32 = jnp.float32
BF16 = jnp.bfloat16

D_MODEL = 1024
GRID_W = 64
ROPE_THETA = 10000.0
EPS = 1e-6
A_HEADS = 8
A_KV_HEADS = 2
A_HEAD_DIM = 64
B_HEADS = 8
B_Q_RANK = 384
B_KV_RANK = 256
B_NOPE = 64
B_ROPE = 32
B_V = 64
N_EXPERTS = 16
N_GROUPS = 4
EXPERTS_PER_GROUP = N_EXPERTS // N_GROUPS
D_EXPERT = 512
A_Q_COLS = A_HEADS * A_HEAD_DIM
A_KV_COLS = A_KV_HEADS * A_HEAD_DIM
B_QK_DIM = B_NOPE + B_ROPE
B_O_COLS = B_HEADS * B_V

LANES = 128
LOG2E = 1.4426950408889634

C_QA = 0
C_KA = C_QA + A_Q_COLS
C_VA = C_KA + A_KV_COLS
C_CQ = C_VA + A_KV_COLS
C_CKV = C_CQ + B_Q_RANK
C_KR = C_CKV + B_KV_RANK
C_GA = C_KR + LANES
C_GB = C_GA + D_MODEL
C_END = C_GB + D_MODEL
B_PAD_COLS = B_HEADS * LANES

A_HEAD_ORDER = (0, 4, 1, 5, 2, 6, 3, 7)

VMEM_LIMIT = 56 * 1024 * 1024

ATTN_GROUP = 8
ATTN_S_BUFS = 3
ATTN_P_BUFS = 2
ATTN_SLAB = 32
ATTN_FIXED_SHIFT_MAX = 48.0
EXPERT_ROWS = 512
SC_CHUNK = 128


def _cparams(sem):
    return pltpu.CompilerParams(dimension_semantics=sem, vmem_limit_bytes=VMEM_LIMIT)


def _dot(a, b):
    return jnp.dot(a, b, preferred_element_type=F32)


def _dot_nt(a, b):
    return lax.dot_general(a, b, (((1,), (1,)), ((), ())), preferred_element_type=F32)


def _dot_tn(a, b):
    return lax.dot_general(a, b, (((0,), (0,)), ((), ())), preferred_element_type=F32)


def _rms(x):
    return x * lax.rsqrt(jnp.mean(x * x, axis=-1, keepdims=True) + EPS)


def _split_bf16(x):
    hi = x.astype(BF16)
    lo = (x - hi.astype(F32)).astype(BF16)
    return hi, lo


def _group_mean_sq(v, bd):
    return _dot((v * v).astype(BF16), bd) * (1.0 / A_HEAD_DIM)


def _rope_a(v, cos, sin):
    n = v.shape[-1]
    lane = lax.broadcasted_iota(jnp.int32, v.shape, 1)
    low = (lane % A_HEAD_DIM) < (A_HEAD_DIM // 2)
    swapped = jnp.where(low, pltpu.roll(v, n - A_HEAD_DIM // 2, 1), pltpu.roll(v, A_HEAD_DIM // 2, 1))
    return v * cos + swapped * sin


def _rope_b(v, cos, sin):
    n = v.shape[-1]
    lane = lax.broadcasted_iota(jnp.int32, v.shape, 1)
    low = (lane % LANES) < (B_NOPE + B_ROPE // 2)
    swapped = jnp.where(low, pltpu.roll(v, n - B_ROPE // 2, 1), pltpu.roll(v, B_ROPE // 2, 1))
    return v * cos + swapped * sin


def _mixer_in_body(*refs, fused):
    if fused:
        h_ref, y0_ref, y1_ref, route_ref, *refs = refs
    else:
        x_ref, *refs = refs
    (gmix_ref, win_ref, ca_ref, sa_ref, cb_ref, sb_ref, gq_ref, gk_ref, gbq_ref, gbkv_ref, wq_ref, wk_ref, wv_ref,
     bd_ref, *outs) = refs
    if fused:
        x_out_ref, *outs = outs
        x = _moe_residual(h_ref, y0_ref, y1_ref, route_ref)
        x_out_ref[...] = x
    else:
        x = x_ref[...]
    qat_ref, ka_ref, vat_ref, qbt_ref, kb_ref, vbt_ref, ga_ref, gb_ref = outs
    hb = (_rms(x) * gmix_ref[...]).astype(BF16)
    u = _dot(hb, win_ref[:, C_QA:C_GA])
    bd = bd_ref[...]
    ca, sa = ca_ref[...], sa_ref[...]
    cb, sb = cb_ref[...], sb_ref[...]

    qa = u[:, C_QA:C_KA]
    qa = qa * lax.rsqrt(_group_mean_sq(qa, bd) + EPS) * gq_ref[...]
    qa = _rope_a(qa, jnp.concatenate([ca] * (A_Q_COLS // LANES), axis=1),
                 jnp.concatenate([sa] * (A_Q_COLS // LANES), axis=1))
    qat_ref[...] = qa.T.astype(BF16)
    ka = u[:, C_KA:C_VA]
    ka = ka * lax.rsqrt(_group_mean_sq(ka, bd[:A_KV_COLS, :A_KV_COLS]) + EPS) * gk_ref[...]
    ka_ref[...] = _rope_a(ka, ca, sa).astype(BF16)
    vat_ref[...] = u[:, C_VA:C_CQ].T.astype(BF16)

    cq = (_rms(u[:, C_CQ:C_CKV]) * gbq_ref[...]).astype(BF16)
    qb = _dot(cq, wq_ref[...])
    qb = _rope_b(qb, jnp.concatenate([cb] * B_HEADS, axis=1), jnp.concatenate([sb] * B_HEADS, axis=1))
    qbt_ref[...] = (qb * (B_QK_DIM ** -0.5 * LOG2E)).T.astype(BF16)
    ckv = (_rms(u[:, C_CKV:C_KR]) * gbkv_ref[...]).astype(BF16)
    kr = _rope_b(u[:, C_KR:C_GA], cb, sb)
    kb_ref[...] = (_dot(ckv, wk_ref[...]) + jnp.concatenate([kr] * B_HEADS, axis=1)).astype(BF16)
    vbt_ref[...] = _dot(ckv, wv_ref[...]).T.astype(BF16)

    g = jax.nn.sigmoid(_dot(hb, win_ref[:, C_GA:C_END]))
    ga_ref[...] = g[:, :D_MODEL].astype(BF16)
    gb_ref[...] = g[:, D_MODEL:].astype(BF16)


def _mixer_in(x, moe_out, lw, l, tabs, tm):
    fused = x is None
    t = moe_out[0].shape[0] if fused else x.shape[0]
    row = lambda n: pl.BlockSpec((tm, n), lambda i: (i, 0))
    full = lambda a: pl.BlockSpec((None,) + a.shape[1:], lambda i: (l,) + (0,) * (a.ndim - 1))
    const = lambda a: pl.BlockSpec(a.shape, lambda i: (0,) * a.ndim)
    col = lambda n: pl.BlockSpec((n, tm), lambda i: (0, i))
    ca, sa, cb, sb = tabs
    if fused:
        h, y, route = moe_out
        second = pl.BlockSpec((tm, D_MODEL // 2), lambda i: (i + t // tm, 0))
        ins, in_specs = [h, y, y, route], [row(D_MODEL), row(D_MODEL // 2), second, row(LANES)]
    else:
        ins, in_specs = [x], [row(D_MODEL)]
    ins += [lw["gmix"], lw["win"], ca, sa, cb, sb, lw["gq"], lw["gk"], lw["gbq"], lw["gbkv"],
            lw["wq"], lw["wk"], lw["wv"], lw["bd"]]
    in_specs += [full(lw["gmix"]), full(lw["win"]), row(LANES), row(LANES), row(LANES), row(LANES),
                 full(lw["gq"]), full(lw["gk"]), full(lw["gbq"]), full(lw["gbkv"]),
                 full(lw["wq"]), full(lw["wk"]), full(lw["wv"]), const(lw["bd"])]
    widths = [A_Q_COLS, A_KV_COLS, A_KV_COLS, B_PAD_COLS, B_PAD_COLS, B_O_COLS, D_MODEL, D_MODEL]
    transposed = [True, False, True, True, False, True, False, False]
    out_specs = [col(n) if tr else row(n) for n, tr in zip(widths, transposed)]
    out_shape = [jax.ShapeDtypeStruct((n, t) if tr else (t, n), BF16) for n, tr in zip(widths, transposed)]
    if fused:
        out_specs.insert(0, row(D_MODEL))
        out_shape.insert(0, jax.ShapeDtypeStruct((t, D_MODEL), F32))
    outs = pl.pallas_call(
        functools.partial(_mixer_in_body, fused=fused),
        grid=(t // tm,),
        in_specs=in_specs,
        out_specs=out_specs,
        out_shape=out_shape,
        compiler_params=_cparams(("parallel",)),
        name="mixer_in",
    )(*ins)
    return tuple(outs) if fused else (None,) + tuple(outs)


def _flash_pair(q0t, q1t, k0_ref, k1_ref, vt_ref, o_ref, s_refs, p_refs, acc_ref, kmax2, fixed, tk):
    tq = q0t.shape[1]
    nk = vt_ref.shape[1] // tk
    group = min(nk, ATTN_GROUP)
    ns = len(s_refs)
    qt2 = jnp.concatenate([q0t, q1t], axis=1)

    def score_values(j):
        off = pl.multiple_of(j * tk, tk)
        if k1_ref is None:
            return _dot(k0_ref[pl.ds(off, tk), :], qt2)
        return jnp.concatenate([_dot(k0_ref[pl.ds(off, tk), :], q0t), _dot(k1_ref[pl.ds(off, tk), :], q1t)], axis=1)

    def add_values(j, p, alpha):
        off = pl.multiple_of(j * tk, tk)
        for rows, lanes in ((slice(0, B_V), slice(0, tq)), (slice(B_V, LANES), slice(tq, 2 * tq))):
            prev = acc_ref[rows, :] if alpha is None else alpha[:, lanes] * acc_ref[rows, :]
            acc_ref[rows, :] = prev + _dot(vt_ref[rows, pl.ds(off, tk)], p[:, lanes])

    def finish(l):
        l = jnp.sum(l, axis=0, keepdims=True)
        o_ref[:B_V, :] = (acc_ref[:B_V, :] / l[:, :tq]).astype(o_ref.dtype)
        o_ref[B_V:, :] = (acc_ref[B_V:, :] / l[:, tq:]).astype(o_ref.dtype)

    def run_groups(one_group, carry):
        acc_ref[...] = jnp.zeros_like(acc_ref)
        return one_group(0, carry) if nk == group else lax.fori_loop(0, nk // group, one_group, carry)

    def fixed_shift(shift):
        def one_group(g, l):
            for c in range(group):
                p = jnp.exp2(score_values(g * group + c) - shift)
                l = l + p.reshape(tk // 8, 8, 2 * tq).sum(axis=0)
                add_values(g * group + c, p.astype(BF16), None)
            return l

        finish(run_groups(one_group, jnp.zeros((8, 2 * tq), F32)))

    def softmax(s_ref, p_ref, m, l):
        slabs = [pl.ds(r, ATTN_SLAB) for r in range(0, tk, ATTN_SLAB)]
        fold = lambda x: x.reshape(ATTN_SLAB // 8, 8, 2 * tq)
        mx = fold(s_ref[slabs[0], :]).max(axis=0)
        for sl in slabs[1:]:
            mx = jnp.maximum(mx, fold(s_ref[sl, :]).max(axis=0))
        m_new = jnp.maximum(m, jnp.max(mx, axis=0, keepdims=True))
        alpha = jnp.exp2(m - m_new)
        l = alpha * l
        for sl in slabs:
            p = jnp.exp2(s_ref[sl, :] - m_new)
            p_ref[sl, :] = p.astype(BF16)
            l = l + fold(p).sum(axis=0)
        return m_new, l, alpha

    def running_max():
        def one_group(g, carry):
            m, l = carry
            base = g * group
            for c in range(min(ns - 1, group)):
                s_refs[c % ns][...] = score_values(base + c)
            for c in range(group):
                if c + ns - 1 < group:
                    s_refs[(c + ns - 1) % ns][...] = score_values(base + c + ns - 1)
                p_ref = p_refs[c % len(p_refs)]
                m, l, alpha = softmax(s_refs[c % ns], p_ref, m, l)
                add_values(base + c, p_ref[...], alpha)
            return m, l

        carry = (jnp.full((1, 2 * tq), -jnp.inf, F32), jnp.zeros((8, 2 * tq), F32))
        finish(run_groups(one_group, carry)[1])

    if fixed:
        lane = lax.broadcasted_iota(jnp.int32, (1, 2 * tq), 1)
        qsq = jnp.sum(jnp.square(qt2.astype(F32)), axis=0, keepdims=True)
        fixed_shift(jnp.sqrt(qsq * jnp.where(lane < tq, kmax2[0], kmax2[1])))
    else:
        running_max()


def _max_sq_norm(k_ref, ones_ref, lanes):
    k = k_ref[...].astype(F32)
    n = _dot((k * k).astype(BF16), ones_ref[...])
    return jnp.max(n[:, lanes])


def _max_query_sq_norm(q):
    q = q.astype(F32)
    return jnp.max(jnp.sum(q * q, axis=0, keepdims=True))


def _attention_tiles(q_pair, k0_ref, k1_ref, vt_ref, o_ref, scratch, qmax2, kmax2, tq, tk):
    s_refs, p_refs, acc_ref = scratch[:ATTN_S_BUFS], scratch[ATTN_S_BUFS:-1], scratch[-1]
    bound2 = jnp.maximum(qmax2[0] * kmax2[0], qmax2[1] * kmax2[1])
    small = bound2 <= ATTN_FIXED_SHIFT_MAX * ATTN_FIXED_SHIFT_MAX

    def walk(fixed):
        @pl.loop(0, o_ref.shape[1] // tq)
        def _(j):
            lanes = pl.ds(pl.multiple_of(j * tq, tq), tq)
            q0t, q1t = q_pair(lanes)
            _flash_pair(q0t, q1t, k0_ref, k1_ref, vt_ref, o_ref.at[:, lanes], s_refs, p_refs, acc_ref, kmax2, fixed, tk)

    pl.when(small)(lambda: walk(True))
    pl.when(jnp.logical_not(small))(lambda: walk(False))


def _attn_a_body(qt_ref, k_ref, vt_ref, ones_ref, o_ref, *scratch, tq, tk):
    kmax2 = (_max_sq_norm(k_ref, ones_ref, slice(0, A_HEAD_DIM)), _max_sq_norm(k_ref, ones_ref, slice(A_HEAD_DIM, LANES)))
    qmax2 = (_max_query_sq_norm(qt_ref[:A_HEAD_DIM, :]), _max_query_sq_norm(qt_ref[A_HEAD_DIM:, :]))

    def q_pair(lanes):
        qt = qt_ref[:, lanes]
        zero = jnp.zeros((A_HEAD_DIM, tq), BF16)
        return jnp.concatenate([qt[:A_HEAD_DIM], zero], axis=0), jnp.concatenate([zero, qt[A_HEAD_DIM:]], axis=0)

    _attention_tiles(q_pair, k_ref, None, vt_ref, o_ref, scratch, qmax2, kmax2, tq, tk)


def _attn_b_body(q0t_ref, q1t_ref, k0_ref, k1_ref, vt_ref, ones_ref, o_ref, *scratch, tq, tk):
    kmax2 = (_max_sq_norm(k0_ref, ones_ref, slice(0, LANES)), _max_sq_norm(k1_ref, ones_ref, slice(0, LANES)))
    qmax2 = (_max_query_sq_norm(q0t_ref[...]), _max_query_sq_norm(q1t_ref[...]))
    q_pair = lambda lanes: (q0t_ref[:, lanes], q1t_ref[:, lanes])
    _attention_tiles(q_pair, k0_ref, k1_ref, vt_ref, o_ref, scratch, qmax2, kmax2, tq, tk)


def _attn_scratch(tq, tk):
    return ([pltpu.VMEM((tk, 2 * tq), F32)] * ATTN_S_BUFS + [pltpu.VMEM((tk, 2 * tq), BF16)] * ATTN_P_BUFS
            + [pltpu.VMEM((LANES, tq), F32)])


def _head_ones(head_dim):
    group = jnp.arange(LANES) // head_dim
    return (group[:, None] == group[None, :]).astype(BF16)


def _attention_a(qat, ka, vat, nseq, s, tq, tk):
    qspec = pl.BlockSpec((LANES, s), lambda b, h: (h, b))
    kspec = pl.BlockSpec((s, LANES), lambda b, h: (b, 0))
    vspec = pl.BlockSpec((LANES, s), lambda b, h: (0, b))
    ones = pl.BlockSpec((LANES, LANES), lambda b, h: (0, 0))
    return pl.pallas_call(
        functools.partial(_attn_a_body, tq=tq, tk=tk),
        grid=(nseq, A_Q_COLS // LANES),
        in_specs=[qspec, kspec, vspec, ones],
        out_specs=qspec,
        out_shape=jax.ShapeDtypeStruct(qat.shape, BF16),
        scratch_shapes=_attn_scratch(tq, tk),
        compiler_params=_cparams(("parallel", "parallel")),
        name="attn_a",
    )(qat, ka, vat, _head_ones(A_HEAD_DIM))


def _attention_b(qbt, kb, vbt, nseq, s, tq, tk):
    q0 = pl.BlockSpec((LANES, s), lambda b, h: (2 * h, b))
    q1 = pl.BlockSpec((LANES, s), lambda b, h: (2 * h + 1, b))
    k0 = pl.BlockSpec((s, LANES), lambda b, h: (b, 2 * h))
    k1 = pl.BlockSpec((s, LANES), lambda b, h: (b, 2 * h + 1))
    v = pl.BlockSpec((LANES, s), lambda b, h: (h, b))
    o = pl.BlockSpec((LANES, s), lambda b, h: (h, b))
    ones = pl.BlockSpec((LANES, LANES), lambda b, h: (0, 0))
    return pl.pallas_call(
        functools.partial(_attn_b_body, tq=tq, tk=tk),
        grid=(nseq, B_O_COLS // LANES),
        in_specs=[q0, q1, k0, k1, v, ones],
        out_specs=o,
        out_shape=jax.ShapeDtypeStruct(vbt.shape, BF16),
        scratch_shapes=_attn_scratch(tq, tk),
        compiler_params=_cparams(("parallel", "parallel")),
        name="attn_b",
    )(qbt, qbt, kb, kb, vbt, _head_ones(LANES))


def _within(x, d, period, n):
    row = lax.broadcasted_iota(jnp.int32, x.shape, 0)
    return jnp.where((row % period) + d < period, pltpu.roll(x, n - d, 0), pltpu.roll(x, period - d, 0))


def _route(logits_t, bias):
    n = N_EXPERTS
    scores = jax.nn.sigmoid(logits_t)
    biased = scores + bias
    row = lax.broadcasted_iota(jnp.int32, biased.shape, 0)
    pos = row % EXPERTS_PER_GROUP
    rank = jnp.zeros(biased.shape, jnp.int32)
    for d in range(1, EXPERTS_PER_GROUP):
        other = _within(biased, d, EXPERTS_PER_GROUP, n)
        other_pos = (pos + d) % EXPERTS_PER_GROUP
        ahead = (other > biased) | ((other == biased) & (other_pos < pos))
        rank = rank + ahead.astype(jnp.int32)
    top2 = rank < 2
    kept = jnp.where(top2, biased, 0.0)
    gscore = kept
    for d in range(1, EXPERTS_PER_GROUP):
        gscore = gscore + _within(kept, d, EXPERTS_PER_GROUP, n)
    grp = row // EXPERTS_PER_GROUP
    win = jnp.ones(biased.shape, jnp.bool_)
    for d in range(1, N_GROUPS):
        other = pltpu.roll(gscore, n - d * EXPERTS_PER_GROUP, 0)
        other_grp = (grp + d) % N_GROUPS
        win = win & ((other < gscore) | ((other == gscore) & (other_grp > grp)))
    sel = top2 & win
    w = jnp.where(sel, scores, 0.0)
    return w / jnp.sum(w, axis=0, keepdims=True), sel


def _pack_pairs(v):
    n = v.shape[1] // 2
    vb = v.astype(BF16).astype(F32)
    hi = pltpu.bitcast(vb[:, :n], jnp.int32)
    lo = pltpu.bitcast(vb[:, n:], jnp.int32)
    return hi | lax.shift_right_logical(lo, 16)


def _unpack_pairs(w):
    hi = pltpu.bitcast(w & jnp.int32(-65536), F32)
    lo = pltpu.bitcast(lax.shift_left(w, 16), F32)
    return jnp.concatenate([hi, lo], axis=1)


R_E0, R_E1, R_RANK0, R_RANK1, R_W0, R_W1 = range(6)


def _mixer_out_body(oa_ref, ob_ref, ga_ref, gb_ref, x_ref, wba_ref, wbb_ref, wout_ref, gffn_ref, wrh_ref, wrl_ref,
                    br_ref, tri_ref, h_ref, tp_ref, route_ref, route_t_ref, count_ref):
    @pl.when(pl.program_id(0) == 0)
    def _():
        count_ref[...] = jnp.zeros_like(count_ref)

    ma = _dot_tn(oa_ref[...], wba_ref[...])
    mb = _dot_tn(ob_ref[...], wbb_ref[...])
    merged = ga_ref[...].astype(F32) * ma + gb_ref[...].astype(F32) * mb
    h = x_ref[...] + _dot(merged.astype(BF16), wout_ref[...])
    h_ref[...] = h
    t = _rms(h) * gffn_ref[...]
    t_hi, t_lo = _split_bf16(t)
    tp_ref[...] = _pack_pairs(t)
    wrh, wrl = wrh_ref[...], wrl_ref[...]
    logits_t = _dot_nt(wrh, t_hi) + _dot_nt(wrh, t_lo) + _dot_nt(wrl, t_hi)
    gates_t, sel = _route(logits_t, br_ref[...])

    tm = gates_t.shape[1]
    onehot = jnp.where(sel, 1.0, 0.0)
    before = _dot(onehot.astype(BF16), tri_ref[...])
    rank_t = count_ref[:, 0:1] + before
    count_ref[...] = count_ref[...] + jnp.sum(onehot, axis=1, keepdims=True)

    row = lax.broadcasted_iota(jnp.int32, sel.shape, 0).astype(F32)
    e0 = jnp.min(jnp.where(sel, row, float(N_EXPERTS)), axis=0, keepdims=True)
    e1 = jnp.max(jnp.where(sel, row, -1.0), axis=0, keepdims=True)
    pick = lambda v, e: jnp.sum(jnp.where(sel & (row == e), v, 0.0), axis=0, keepdims=True)
    rec = jnp.concatenate([e0, e1, pick(rank_t, e0), pick(rank_t, e1), pick(gates_t, e0), pick(gates_t, e1),
                           jnp.zeros((LANES - 6, tm), F32)], axis=0)
    route_ref[...] = rec.T
    route_t_ref[...] = rec[:8]


def _mixer_out(oa, ob, ga, gb, x, lw, l, tm):
    t = x.shape[0]
    row = lambda n: pl.BlockSpec((tm, n), lambda i: (i, 0))
    full = lambda a: pl.BlockSpec((None,) + a.shape[1:], lambda i: (l,) + (0,) * (a.ndim - 1))
    const = lambda a: pl.BlockSpec(a.shape, lambda i: (0,) * a.ndim)
    col = lambda n: pl.BlockSpec((n, tm), lambda i: (0, i))
    tri = (jnp.arange(tm)[:, None] < jnp.arange(tm)[None, :]).astype(BF16)
    ins = [oa, ob, ga, gb, x, lw["wba"], lw["wbb"], lw["wout"], lw["gffn"], lw["wrh"], lw["wrl"], lw["br"], tri]
    in_specs = [col(A_Q_COLS), col(B_O_COLS), row(D_MODEL), row(D_MODEL), row(D_MODEL),
                full(lw["wba"]), full(lw["wbb"]), full(lw["wout"]), full(lw["gffn"]),
                const(lw["wrh"]), const(lw["wrl"]), const(lw["br"]), const(tri)]
    return pl.pallas_call(
        _mixer_out_body,
        grid=(t // tm,),
        in_specs=in_specs,
        out_specs=[row(D_MODEL), row(D_MODEL // 2), row(LANES), col(8),
                   pl.BlockSpec((N_EXPERTS, LANES), lambda i: (0, 0))],
        out_shape=[jax.ShapeDtypeStruct((t, D_MODEL), F32), jax.ShapeDtypeStruct((t, D_MODEL // 2), jnp.int32),
                   jax.ShapeDtypeStruct((t, LANES), F32), jax.ShapeDtypeStruct((8, t), F32),
                   jax.ShapeDtypeStruct((N_EXPERTS, LANES), F32)],
        compiler_params=_cparams(("arbitrary",)),
        name="mixer_out",
    )(*ins)


def _sc_plan(nrows):
    info = plsc.get_sparse_core_info()
    workers = info.num_cores * info.num_subcores
    per_worker = nrows // workers
    chunk = min(SC_CHUNK, per_worker)
    assert per_worker * workers == nrows and per_worker % chunk == 0 and chunk % 8 == 0, (nrows, workers, chunk)
    return info.num_cores, per_worker, chunk


def _sc_mesh():
    return plsc.VectorSubcoreMesh(core_axis_name="core", subcore_axis_name="subcore")


def _scatter_rows(x, idx0, idx1, nrows):
    t, d = x.shape
    ncores, per_worker, chunk = _sc_plan(t)

    @functools.partial(
        pl.kernel, out_type=jax.ShapeDtypeStruct((nrows, d), x.dtype), mesh=_sc_mesh(), name="moe_dispatch",
        scratch_types=[pltpu.VMEM((chunk,), jnp.int32), pltpu.VMEM((chunk,), jnp.int32), pltpu.VMEM((chunk, d), x.dtype)])
    def run(x_hbm, i0_hbm, i1_hbm, o_hbm, i0_v, i1_v, rows_v):
        worker = lax.axis_index("subcore") * ncores + lax.axis_index("core")

        @pl.loop(0, per_worker // chunk)
        def _(c):
            base = pl.multiple_of(worker * per_worker + c * chunk, chunk)
            pltpu.sync_copy(x_hbm.at[pl.ds(base, chunk)], rows_v)
            pltpu.sync_copy(i0_hbm.at[pl.ds(base, chunk)], i0_v)
            pltpu.sync_copy(i1_hbm.at[pl.ds(base, chunk)], i1_v)
            pltpu.sync_copy(rows_v, o_hbm.at[i0_v])
            pltpu.sync_copy(rows_v, o_hbm.at[i1_v])

    return run(x, idx0, idx1)


def _gather_rows(table, idx):
    m = idx.shape[0]
    d = table.shape[1]
    ncores, per_worker, chunk = _sc_plan(m)

    @functools.partial(
        pl.kernel, out_type=jax.ShapeDtypeStruct((m, d), table.dtype), mesh=_sc_mesh(), name="moe_collect",
        scratch_types=[pltpu.VMEM((chunk,), jnp.int32), pltpu.VMEM((chunk, d), table.dtype)])
    def run(x_hbm, i_hbm, o_hbm, i_v, rows_v):
        worker = lax.axis_index("subcore") * ncores + lax.axis_index("core")

        @pl.loop(0, per_worker // chunk)
        def _(c):
            base = pl.multiple_of(worker * per_worker + c * chunk, chunk)
            pltpu.sync_copy(i_hbm.at[pl.ds(base, chunk)], i_v)
            pltpu.sync_copy(x_hbm.at[i_v], rows_v)
            pltpu.sync_copy(rows_v, o_hbm.at[pl.ds(base, chunk)])

    return run(table, idx)


def _routing_tables(route_t, counts, tr):
    t = route_t.shape[1]
    experts = jnp.arange(N_EXPERTS, dtype=jnp.int32)
    cnt = counts[:, 0].astype(jnp.int32)
    seg_end = jnp.cumsum(cnt)
    seg_start = seg_end - cnt
    lookup = lambda table, idx: jnp.sum(jnp.where(idx[None, :] == experts[:, None], table[:, None], 0), axis=0)
    e = route_t[R_E0:R_E1 + 1].astype(jnp.int32).reshape(2 * t)
    rank = route_t[R_RANK0:R_RANK1 + 1].astype(jnp.int32).reshape(2 * t)
    pos = lookup(seg_start, e) + rank

    n_tiles = 2 * t // tr
    n_visits = n_tiles + N_EXPERTS - 1
    first_tile = seg_start // tr
    last_tile = jnp.maximum(seg_end - 1, 0) // tr
    visits = jnp.where(cnt > 0, last_tile - first_tile + 1, 0)
    visit_end = jnp.cumsum(visits)
    visit_start = visit_end - visits
    g = jnp.arange(n_visits, dtype=jnp.int32)
    valid = g < visit_end[-1]
    ex = jnp.minimum(jnp.sum((g[None, :] >= visit_end[:, None]).astype(jnp.int32), axis=0), N_EXPERTS - 1)
    tile = lookup(first_tile - visit_start, ex) + g
    lo = jnp.clip(lookup(seg_start, ex) - tile * tr, 0, tr)
    hi = jnp.clip(lookup(seg_end, ex) - tile * tr, 0, tr)
    last_ex = jnp.max(jnp.where(cnt > 0, experts, 0))
    tile = jnp.where(valid, tile, n_tiles - 1)
    ex = jnp.where(valid, ex, last_ex)
    lo = jnp.where(valid, lo, 0)
    hi = jnp.where(valid, hi, 0)
    first = jnp.concatenate([jnp.ones((1,), jnp.int32), (tile[1:] != tile[:-1]).astype(jnp.int32)])
    first = jnp.where(valid, first, 0)
    return pos, (tile, ex, lo, hi, first)


def _experts_body(tile_ref, ex_ref, lo_ref, hi_ref, first_ref, xs_ref, wg_ref, wu_ref, wd_ref, ys_ref, acc_ref):
    g = pl.program_id(0)
    lo, hi = lo_ref[g], hi_ref[g]

    @pl.when(hi > lo)
    def _():
        x = _unpack_pairs(xs_ref[...]).astype(BF16)
        a = jax.nn.silu(_dot(x, wg_ref[...].astype(BF16))) * _dot(x, wu_ref[...].astype(BF16))
        row = lax.broadcasted_iota(jnp.int32, (a.shape[0], 1), 0)
        a = jnp.where((row >= lo) & (row < hi), a, 0.0)
        y = _dot(a.astype(BF16), wd_ref[...].astype(BF16))

        @pl.when(first_ref[g] == 1)
        def _():
            acc_ref[...] = y

        @pl.when(first_ref[g] == 0)
        def _():
            acc_ref[...] += y

        ys_ref[...] = _pack_pairs(acc_ref[...])


def _experts(xs, visits, lw, l, tr):
    n_visits = visits[0].shape[0]
    rows = pl.BlockSpec((tr, D_MODEL // 2), lambda g, tile, ex, lo, hi, first: (tile[g], 0))
    wspec = lambda a: pl.BlockSpec((None, None) + a.shape[2:], lambda g, tile, ex, lo, hi, first: (l, ex[g], 0, 0))
    return pl.pallas_call(
        _experts_body,
        grid_spec=pltpu.PrefetchScalarGridSpec(
            num_scalar_prefetch=5,
            grid=(n_visits,),
            in_specs=[rows, wspec(lw["wg"]), wspec(lw["wu"]), wspec(lw["wd"])],
            out_specs=rows,
            scratch_shapes=[pltpu.VMEM((tr, D_MODEL), F32)],
        ),
        out_shape=jax.ShapeDtypeStruct(xs.shape, jnp.int32),
        compiler_params=_cparams(("arbitrary",)),
        name="experts",
    )(*visits, xs, lw["wg"], lw["wu"], lw["wd"])


def _moe_residual(h_ref, y0_ref, y1_ref, route_ref):
    r = route_ref[...]
    return (h_ref[...] + r[:, R_W0:R_W0 + 1] * _unpack_pairs(y0_ref[...])
            + r[:, R_W1:R_W1 + 1] * _unpack_pairs(y1_ref[...]))


def _final_body(h_ref, y0_ref, y1_ref, route_ref, gfin_ref, o_ref):
    o_ref[...] = _rms(_moe_residual(h_ref, y0_ref, y1_ref, route_ref)) * gfin_ref[...]


def _final(h, y, route, gfin, tm):
    t = h.shape[0]
    row = lambda n: pl.BlockSpec((tm, n), lambda i: (i, 0))
    second = pl.BlockSpec((tm, D_MODEL // 2), lambda i: (i + t // tm, 0))
    return pl.pallas_call(
        _final_body,
        grid=(t // tm,),
        in_specs=[row(D_MODEL), row(D_MODEL // 2), second, row(LANES), pl.BlockSpec(gfin.shape, lambda i: (0, 0))],
        out_specs=row(D_MODEL),
        out_shape=jax.ShapeDtypeStruct(h.shape, F32),
        compiler_params=_cparams(("parallel",)),
        name="final",
    )(h, y, y, route, gfin)


def _moe_rows(tp, route_t, counts, lw, l, tr):
    t = tp.shape[0]
    pos, visits = _routing_tables(route_t, counts, tr)
    xs = _scatter_rows(tp, pos[:t], pos[t:], 2 * t)
    ys = _experts(xs, visits, lw, l, tr)
    return _gather_rows(ys, pos)


def _rope_angles(seq_len, rot_dim):
    rows = seq_len // GRID_W
    row = jnp.broadcast_to(jnp.arange(rows)[:, None], (rows, GRID_W)).reshape(-1).astype(F32)
    col = jnp.broadcast_to(jnp.arange(GRID_W)[None, :], (rows, GRID_W)).reshape(-1).astype(F32)
    axis_dim = rot_dim // 2
    inv_freq = jnp.power(jnp.float32(ROPE_THETA), -jnp.arange(0, axis_dim, 2, dtype=F32) / axis_dim)
    ang = jnp.concatenate([row[:, None] * inv_freq[None, :], col[:, None] * inv_freq[None, :]], axis=-1)
    return jnp.cos(ang), jnp.sin(ang)


def _rope_tables(nseq, seq_len):
    c, s = _rope_angles(seq_len, A_HEAD_DIM)
    ca = jnp.concatenate([c, c, c, c], axis=-1)
    sa = jnp.concatenate([-s, s, -s, s], axis=-1)
    c, s = _rope_angles(seq_len, B_ROPE)
    ones = jnp.ones((seq_len, B_NOPE), F32)
    zeros = jnp.zeros((seq_len, B_NOPE), F32)
    tail = LANES - B_NOPE - B_ROPE
    cb = jnp.concatenate([ones, c, c, ones[:, :tail]], axis=-1)
    sb = jnp.concatenate([zeros, -s, s, zeros[:, :tail]], axis=-1)
    return tuple(jnp.tile(a, (nseq, 1)) for a in (ca, sa, cb, sb))


def _prepare_weights(norm_mix, w_in, a_q_norm, a_k_norm, b_q_norm, b_kv_norm, w_q_up, w_kv_up, w_branch_a,
                     w_branch_b, w_out, norm_ffn, w_router, b_router, w_gate, w_up, w_down):
    depth = w_in.shape[0]
    order = jnp.array(A_HEAD_ORDER)
    parts = []
    start = 0
    for n in (A_Q_COLS, A_KV_COLS, A_KV_COLS, B_Q_RANK, B_KV_RANK, B_ROPE, D_MODEL, D_MODEL):
        parts.append(w_in[..., start:start + n])
        start += n
    qa, ka, va, cq, ckv, kr, ga, gb = parts
    qa = qa.reshape(depth, D_MODEL, A_HEADS, A_HEAD_DIM)[:, :, order].reshape(depth, D_MODEL, A_Q_COLS)
    kr = jnp.pad(kr, ((0, 0), (0, 0), (B_NOPE, LANES - B_NOPE - B_ROPE)))
    win = jnp.concatenate([qa, ka, va, cq, ckv, kr, ga, gb], axis=-1).astype(BF16)

    wq = w_q_up.reshape(depth, B_Q_RANK, B_HEADS, B_QK_DIM)
    wq = jnp.pad(wq, ((0, 0), (0, 0), (0, 0), (0, LANES - B_QK_DIM))).reshape(depth, B_Q_RANK, B_PAD_COLS)
    wkv = w_kv_up.reshape(depth, B_KV_RANK, B_HEADS, B_NOPE + B_V)
    wk = jnp.pad(wkv[..., :B_NOPE], ((0, 0), (0, 0), (0, 0), (0, LANES - B_NOPE))).reshape(depth, B_KV_RANK, B_PAD_COLS)
    wv = wkv[..., B_NOPE:].reshape(depth, B_KV_RANK, B_O_COLS)
    wba = w_branch_a.reshape(depth, A_HEADS, A_HEAD_DIM, D_MODEL)[:, order].reshape(depth, A_Q_COLS, D_MODEL)

    group = jnp.arange(A_Q_COLS) // A_HEAD_DIM
    bd = (group[:, None] == group[None, :]).astype(BF16)
    wr_t = w_router.T
    wrh, wrl = _split_bf16(wr_t)
    vec = lambda a: a[:, None, :]
    return dict(
        gmix=vec(norm_mix), win=win,
        gq=vec(jnp.tile(a_q_norm, (1, A_HEADS)) * (A_HEAD_DIM ** -0.5 * LOG2E)), gk=vec(jnp.tile(a_k_norm, (1, A_KV_HEADS))),
        gbq=vec(b_q_norm), gbkv=vec(b_kv_norm),
        wq=wq.astype(BF16), wk=wk.astype(BF16), wv=wv.astype(BF16), bd=bd,
        wba=wba.astype(BF16), wbb=w_branch_b.astype(BF16), wout=w_out.astype(BF16), gffn=vec(norm_ffn),
        wrh=wrh, wrl=wrl, br=b_router[:, None].astype(F32),
        wg=w_gate, wu=w_up, wd=w_down,
    )


def _pick(n, candidates):
    for c in candidates:
        if n % c == 0:
            return c
    raise ValueError(f"no tile in {candidates} divides {n}")


def _trunk(x3, lw, norm_final):
    nseq, s, _ = x3.shape
    t = nseq * s
    x = x3.reshape(t, D_MODEL)
    tabs = _rope_tables(nseq, s)
    depth = lw["win"].shape[0]
    tm = _pick(t, (256,))
    tm_out = _pick(t, (512, 256))
    tq = _pick(s, (512, 256))
    tk = _pick(s, (512,))
    moe_out = None
    for l in range(depth):
        x_new, qa, ka, va, qb, kb, vb, ga, gb = _mixer_in(x, moe_out, lw, l, tabs, tm_out)
        x = x if x_new is None else x_new
        oa = _attention_a(qa, ka, va, nseq, s, tq, tk)
        ob = _attention_b(qb, kb, vb, nseq, s, tq, tk)
        h, tp, route, route_t, counts = _mixer_out(oa, ob, ga, gb, x, lw, l, _pick(t, (1024, 512, 256)))
        moe_out = (h, _moe_rows(tp, route_t, counts, lw, l, EXPERT_ROWS), route)
        x = None
    return _final(*moe_out, norm_final[None, :], tm).reshape(x3.shape)


def kernel(x_prompt, x_sample, norm_mix, w_in, a_q_norm, a_k_norm, b_q_norm, b_kv_norm, w_q_up, w_kv_up, w_branch_a,
           w_branch_b, w_out, norm_ffn, w_router, b_router, w_gate, w_up, w_down, norm_final):
    lw = _prepare_weights(norm_mix, w_in, a_q_norm, a_k_norm, b_q_norm, b_kv_norm, w_q_up, w_kv_up, w_branch_a,
                          w_branch_b, w_out, norm_ffn, w_router, b_router, w_gate, w_up, w_down)
    return _trunk(x_prompt, lw, norm_final), _trunk(x_sample, lw, norm_final)
```

```python
import functools

import jax
import jax.numpy as jnp
from jax import lax
from jax.experimental import pallas as pl
from jax.experimental.pallas import tpu as pltpu
from jax.experimental.pallas import tpu_sc as plsc

F32 = jnp.float32
BF16 = jnp.bfloat16

D_MODEL = 1024
GRID_W = 64
ROPE_THETA = 10000.0
EPS = 1e-6
A_HEADS = 8
A_KV_HEADS = 2
A_HEAD_DIM = 64
B_HEADS = 8
B_Q_RANK = 384
B_KV_RANK = 256
B_NOPE = 64
B_ROPE = 32
B_V = 64
N_EXPERTS = 16
N_GROUPS = 4
EXPERTS_PER_GROUP = N_EXPERTS // N_GROUPS
D_EXPERT = 512
A_Q_COLS = A_HEADS * A_HEAD_DIM
A_KV_COLS = A_KV_HEADS * A_HEAD_DIM
B_QK_DIM = B_NOPE + B_ROPE
B_O_COLS = B_HEADS * B_V

LANES = 128
LOG2E = 1.4426950408889634

C_QA = 0
C_KA = C_QA + A_Q_COLS
C_VA = C_KA + A_KV_COLS
C_CQ = C_VA + A_KV_COLS
C_CKV = C_CQ + B_Q_RANK
C_KR = C_CKV + B_KV_RANK
C_GA = C_KR + LANES
C_GB = C_GA + D_MODEL
C_END = C_GB + D_MODEL
B_PAD_COLS = B_HEADS * LANES

A_HEAD_ORDER = (0, 4, 1, 5, 2, 6, 3, 7)

VMEM_LIMIT = 56 * 1024 * 1024

ATTN_GROUP = 8
ATTN_S_BUFS = 3
ATTN_P_BUFS = 2
ATTN_SLAB = 32
ATTN_FIXED_SHIFT_MAX = 48.0
EXPERT_ROWS = 512
SC_CHUNK = 128


def _cparams(sem):
    return pltpu.CompilerParams(dimension_semantics=sem, vmem_limit_bytes=VMEM_LIMIT)


def _dot(a, b):
    return jnp.dot(a, b, preferred_element_type=F32)


def _dot_nt(a, b):
    return lax.dot_general(a, b, (((1,), (1,)), ((), ())), preferred_element_type=F32)


def _dot_tn(a, b):
    return lax.dot_general(a, b, (((0,), (0,)), ((), ())), preferred_element_type=F32)


def _rms(x):
    return x * lax.rsqrt(jnp.mean(x * x, axis=-1, keepdims=True) + EPS)


def _split_bf16(x):
    hi = x.astype(BF16)
    lo = (x - hi.astype(F32)).astype(BF16)
    return hi, lo


def _group_mean_sq(v, bd):
    return _dot((v * v).astype(BF16), bd) * (1.0 / A_HEAD_DIM)


def _rope_a(v, cos, sin):
    n = v.shape[-1]
    lane = lax.broadcasted_iota(jnp.int32, v.shape, 1)
    low = (lane % A_HEAD_DIM) < (A_HEAD_DIM // 2)
    swapped = jnp.where(low, pltpu.roll(v, n - A_HEAD_DIM // 2, 1), pltpu.roll(v, A_HEAD_DIM // 2, 1))
    return v * cos + swapped * sin


def _rope_b(v, cos, sin):
    n = v.shape[-1]
    lane = lax.broadcasted_iota(jnp.int32, v.shape, 1)
    low = (lane % LANES) < (B_NOPE + B_ROPE // 2)
    swapped = jnp.where(low, pltpu.roll(v, n - B_ROPE // 2, 1), pltpu.roll(v, B_ROPE // 2, 1))
    return v * cos + swapped * sin


def _mixer_in_body(*refs, fused):
    if fused:
        h_ref, y0_ref, y1_ref, route_ref, *refs = refs
    else:
        x_ref, *refs = refs
    (gmix_ref, win_ref, ca_ref, sa_ref, cb_ref, sb_ref, gq_ref, gk_ref, gbq_ref, gbkv_ref, wq_ref, wk_ref, wv_ref,
     bd_ref, *outs) = refs
    if fused:
        x_out_ref, *outs = outs
        x = _moe_residual(h_ref, y0_ref, y1_ref, route_ref)
        x_out_ref[...] = x
    else:
        x = x_ref[...]
    qat_ref, ka_ref, vat_ref, qbt_ref, kb_ref, vbt_ref, ga_ref, gb_ref = outs
    hb = (_rms(x) * gmix_ref[...]).astype(BF16)
    u = _dot(hb, win_ref[:, C_QA:C_GA])
    bd = bd_ref[...]
    ca, sa = ca_ref[...], sa_ref[...]
    cb, sb = cb_ref[...], sb_ref[...]

    qa = u[:, C_QA:C_KA]
    qa = qa * lax.rsqrt(_group_mean_sq(qa, bd) + EPS) * gq_ref[...]
    qa = _rope_a(qa, jnp.concatenate([ca] * (A_Q_COLS // LANES), axis=1),
                 jnp.concatenate([sa] * (A_Q_COLS // LANES), axis=1))
    qat_ref[...] = qa.T.astype(BF16)
    ka = u[:, C_KA:C_VA]
    ka = ka * lax.rsqrt(_group_mean_sq(ka, bd[:A_KV_COLS, :A_KV_COLS]) + EPS) * gk_ref[...]
    ka_ref[...] = _rope_a(ka, ca, sa).astype(BF16)
    vat_ref[...] = u[:, C_VA:C_CQ].T.astype(BF16)

    cq = (_rms(u[:, C_CQ:C_CKV]) * gbq_ref[...]).astype(BF16)
    qb = _dot(cq, wq_ref[...])
    qb = _rope_b(qb, jnp.concatenate([cb] * B_HEADS, axis=1), jnp.concatenate([sb] * B_HEADS, axis=1))
    qbt_ref[...] = (qb * (B_QK_DIM ** -0.5 * LOG2E)).T.astype(BF16)
    ckv = (_rms(u[:, C_CKV:C_KR]) * gbkv_ref[...]).astype(BF16)
    kr = _rope_b(u[:, C_KR:C_GA], cb, sb)
    kb_ref[...] = (_dot(ckv, wk_ref[...]) + jnp.concatenate([kr] * B_HEADS, axis=1)).astype(BF16)
    vbt_ref[...] = _dot(ckv, wv_ref[...]).T.astype(BF16)

    g = jax.nn.sigmoid(_dot(hb, win_ref[:, C_GA:C_END]))
    ga_ref[...] = g[:, :D_MODEL].astype(BF16)
    gb_ref[...] = g[:, D_MODEL:].astype(BF16)


def _mixer_in(x, moe_out, lw, l, tabs, tm):
    fused = x is None
    t = moe_out[0].shape[0] if fused else x.shape[0]
    row = lambda n: pl.BlockSpec((tm, n), lambda i: (i, 0))
    full = lambda a: pl.BlockSpec((None,) + a.shape[1:], lambda i: (l,) + (0,) * (a.ndim - 1))
    const = lambda a: pl.BlockSpec(a.shape, lambda i: (0,) * a.ndim)
    col = lambda n: pl.BlockSpec((n, tm), lambda i: (0, i))
    ca, sa, cb, sb = tabs
    if fused:
        h, y, route = moe_out
        second = pl.BlockSpec((tm, D_MODEL // 2), lambda i: (i + t // tm, 0))
        ins, in_specs = [h, y, y, route], [row(D_MODEL), row(D_MODEL // 2), second, row(LANES)]
    else:
        ins, in_specs = [x], [row(D_MODEL)]
    ins += [lw["gmix"], lw["win"], ca, sa, cb, sb, lw["gq"], lw["gk"], lw["gbq"], lw["gbkv"],
            lw["wq"], lw["wk"], lw["wv"], lw["bd"]]
    in_specs += [full(lw["gmix"]), full(lw["win"]), row(LANES), row(LANES), row(LANES), row(LANES),
                 full(lw["gq"]), full(lw["gk"]), full(lw["gbq"]), full(lw["gbkv"]),
                 full(lw["wq"]), full(lw["wk"]), full(lw["wv"]), const(lw["bd"])]
    widths = [A_Q_COLS, A_KV_COLS, A_KV_COLS, B_PAD_COLS, B_PAD_COLS, B_O_COLS, D_MODEL, D_MODEL]
    transposed = [True, False, True, True, False, True, False, False]
    out_specs = [col(n) if tr else row(n) for n, tr in zip(widths, transposed)]
    out_shape = [jax.ShapeDtypeStruct((n, t) if tr else (t, n), BF16) for n, tr in zip(widths, transposed)]
    if fused:
        out_specs.insert(0, row(D_MODEL))
        out_shape.insert(0, jax.ShapeDtypeStruct((t, D_MODEL), F32))
    outs = pl.pallas_call(
        functools.partial(_mixer_in_body, fused=fused),
        grid=(t // tm,),
        in_specs=in_specs,
        out_specs=out_specs,
        out_shape=out_shape,
        compiler_params=_cparams(("parallel",)),
        name="mixer_in",
    )(*ins)
    return tuple(outs) if fused else (None,) + tuple(outs)


def _flash_pair(q0t, q1t, k0_ref, k1_ref, vt_ref, o_ref, s_refs, p_refs, acc_ref, kmax2, fixed, tk):
    tq = q0t.shape[1]
    nk = vt_ref.shape[1] // tk
    group = min(nk, ATTN_GROUP)
    ns = len(s_refs)
    qt2 = jnp.concatenate([q0t, q1t], axis=1)

    def score_values(j):
        off = pl.multiple_of(j * tk, tk)
        if k1_ref is None:
            return _dot(k0_ref[pl.ds(off, tk), :], qt2)
        return jnp.concatenate([_dot(k0_ref[pl.ds(off, tk), :], q0t), _dot(k1_ref[pl.ds(off, tk), :], q1t)], axis=1)

    def add_values(j, p, alpha):
        off = pl.multiple_of(j * tk, tk)
        for rows, lanes in ((slice(0, B_V), slice(0, tq)), (slice(B_V, LANES), slice(tq, 2 * tq))):
            prev = acc_ref[rows, :] if alpha is None else alpha[:, lanes] * acc_ref[rows, :]
            acc_ref[rows, :] = prev + _dot(vt_ref[rows, pl.ds(off, tk)], p[:, lanes])

    def finish(l):
        l = jnp.sum(l, axis=0, keepdims=True)
        o_ref[:B_V, :] = (acc_ref[:B_V, :] / l[:, :tq]).astype(o_ref.dtype)
        o_ref[B_V:, :] = (acc_ref[B_V:, :] / l[:, tq:]).astype(o_ref.dtype)

    def run_groups(one_group, carry):
        acc_ref[...] = jnp.zeros_like(acc_ref)
        return one_group(0, carry) if nk == group else lax.fori_loop(0, nk // group, one_group, carry)

    def fixed_shift(shift):
        def one_group(g, l):
            for c in range(group):
                p = jnp.exp2(score_values(g * group + c) - shift)
                l = l + p.reshape(tk // 8, 8, 2 * tq).sum(axis=0)
                add_values(g * group + c, p.astype(BF16), None)
            return l

        finish(run_groups(one_group, jnp.zeros((8, 2 * tq), F32)))

    def softmax(s_ref, p_ref, m, l):
        slabs = [pl.ds(r, ATTN_SLAB) for r in range(0, tk, ATTN_SLAB)]
        fold = lambda x: x.reshape(ATTN_SLAB // 8, 8, 2 * tq)
        mx = fold(s_ref[slabs[0], :]).max(axis=0)
        for sl in slabs[1:]:
            mx = jnp.maximum(mx, fold(s_ref[sl, :]).max(axis=0))
        m_new = jnp.maximum(m, jnp.max(mx, axis=0, keepdims=True))
        alpha = jnp.exp2(m - m_new)
        l = alpha * l
        for sl in slabs:
            p = jnp.exp2(s_ref[sl, :] - m_new)
            p_ref[sl, :] = p.astype(BF16)
            l = l + fold(p).sum(axis=0)
        return m_new, l, alpha

    def running_max():
        def one_group(g, carry):
            m, l = carry
            base = g * group
            for c in range(min(ns - 1, group)):
                s_refs[c % ns][...] = score_values(base + c)
            for c in range(group):
                if c + ns - 1 < group:
                    s_refs[(c + ns - 1) % ns][...] = score_values(base + c + ns - 1)
                p_ref = p_refs[c % len(p_refs)]
                m, l, alpha = softmax(s_refs[c % ns], p_ref, m, l)
                add_values(base + c, p_ref[...], alpha)
            return m, l

        carry = (jnp.full((1, 2 * tq), -jnp.inf, F32), jnp.zeros((8, 2 * tq), F32))
        finish(run_groups(one_group, carry)[1])

    if fixed:
        lane = lax.broadcasted_iota(jnp.int32, (1, 2 * tq), 1)
        qsq = jnp.sum(jnp.square(qt2.astype(F32)), axis=0, keepdims=True)
        fixed_shift(jnp.sqrt(qsq * jnp.where(lane < tq, kmax2[0], kmax2[1])))
    else:
        running_max()


def _max_sq_norm(k_ref, ones_ref, lanes):
    k = k_ref[...].astype(F32)
    n = _dot((k * k).astype(BF16), ones_ref[...])
    return jnp.max(n[:, lanes])


def _max_query_sq_norm(q):
    q = q.astype(F32)
    return jnp.max(jnp.sum(q * q, axis=0, keepdims=True))


def _attention_tiles(q_pair, k0_ref, k1_ref, vt_ref, o_ref, scratch, qmax2, kmax2, tq, tk):
    s_refs, p_refs, acc_ref = scratch[:ATTN_S_BUFS], scratch[ATTN_S_BUFS:-1], scratch[-1]
    bound2 = jnp.maximum(qmax2[0] * kmax2[0], qmax2[1] * kmax2[1])
    small = bound2 <= ATTN_FIXED_SHIFT_MAX * ATTN_FIXED_SHIFT_MAX

    def walk(fixed):
        @pl.loop(0, o_ref.shape[1] // tq)
        def _(j):
            lanes = pl.ds(pl.multiple_of(j * tq, tq), tq)
            q0t, q1t = q_pair(lanes)
            _flash_pair(q0t, q1t, k0_ref, k1_ref, vt_ref, o_ref.at[:, lanes], s_refs, p_refs, acc_ref, kmax2, fixed, tk)

    pl.when(small)(lambda: walk(True))
    pl.when(jnp.logical_not(small))(lambda: walk(False))


def _attn_a_body(qt_ref, k_ref, vt_ref, ones_ref, o_ref, *scratch, tq, tk):
    kmax2 = (_max_sq_norm(k_ref, ones_ref, slice(0, A_HEAD_DIM)), _max_sq_norm(k_ref, ones_ref, slice(A_HEAD_DIM, LANES)))
    qmax2 = (_max_query_sq_norm(qt_ref[:A_HEAD_DIM, :]), _max_query_sq_norm(qt_ref[A_HEAD_DIM:, :]))

    def q_pair(lanes):
        qt = qt_ref[:, lanes]
        zero = jnp.zeros((A_HEAD_DIM, tq), BF16)
        return jnp.concatenate([qt[:A_HEAD_DIM], zero], axis=0), jnp.concatenate([zero, qt[A_HEAD_DIM:]], axis=0)

    _attention_tiles(q_pair, k_ref, None, vt_ref, o_ref, scratch, qmax2, kmax2, tq, tk)


def _attn_b_body(q0t_ref, q1t_ref, k0_ref, k1_ref, vt_ref, ones_ref, o_ref, *scratch, tq, tk):
    kmax2 = (_max_sq_norm(k0_ref, ones_ref, slice(0, LANES)), _max_sq_norm(k1_ref, ones_ref, slice(0, LANES)))
    qmax2 = (_max_query_sq_norm(q0t_ref[...]), _max_query_sq_norm(q1t_ref[...]))
    q_pair = lambda lanes: (q0t_ref[:, lanes], q1t_ref[:, lanes])
    _attention_tiles(q_pair, k0_ref, k1_ref, vt_ref, o_ref, scratch, qmax2, kmax2, tq, tk)


def _attn_scratch(tq, tk):
    return ([pltpu.VMEM((tk, 2 * tq), F32)] * ATTN_S_BUFS + [pltpu.VMEM((tk, 2 * tq), BF16)] * ATTN_P_BUFS
            + [pltpu.VMEM((LANES, tq), F32)])


def _head_ones(head_dim):
    group = jnp.arange(LANES) // head_dim
    return (group[:, None] == group[None, :]).astype(BF16)


def _attention_a(qat, ka, vat, nseq, s, tq, tk):
    qspec = pl.BlockSpec((LANES, s), lambda b, h: (h, b))
    kspec = pl.BlockSpec((s, LANES), lambda b, h: (b, 0))
    vspec = pl.BlockSpec((LANES, s), lambda b, h: (0, b))
    ones = pl.BlockSpec((LANES, LANES), lambda b, h: (0, 0))
    return pl.pallas_call(
        functools.partial(_attn_a_body, tq=tq, tk=tk),
        grid=(nseq, A_Q_COLS // LANES),
        in_specs=[qspec, kspec, vspec, ones],
        out_specs=qspec,
        out_shape=jax.ShapeDtypeStruct(qat.shape, BF16),
        scratch_shapes=_attn_scratch(tq, tk),
        compiler_params=_cparams(("parallel", "parallel")),
        name="attn_a",
    )(qat, ka, vat, _head_ones(A_HEAD_DIM))


def _attention_b(qbt, kb, vbt, nseq, s, tq, tk):
    q0 = pl.BlockSpec((LANES, s), lambda b, h: (2 * h, b))
    q1 = pl.BlockSpec((LANES, s), lambda b, h: (2 * h + 1, b))
    k0 = pl.BlockSpec((s, LANES), lambda b, h: (b, 2 * h))
    k1 = pl.BlockSpec((s, LANES), lambda b, h: (b, 2 * h + 1))
    v = pl.BlockSpec((LANES, s), lambda b, h: (h, b))
    o = pl.BlockSpec((LANES, s), lambda b, h: (h, b))
    ones = pl.BlockSpec((LANES, LANES), lambda b, h: (0, 0))
    return pl.pallas_call(
        functools.partial(_attn_b_body, tq=tq, tk=tk),
        grid=(nseq, B_O_COLS // LANES),
        in_specs=[q0, q1, k0, k1, v, ones],
        out_specs=o,
        out_shape=jax.ShapeDtypeStruct(vbt.shape, BF16),
        scratch_shapes=_attn_scratch(tq, tk),
        compiler_params=_cparams(("parallel", "parallel")),
        name="attn_b",
    )(qbt, qbt, kb, kb, vbt, _head_ones(LANES))


def _within(x, d, period, n):
    row = lax.broadcasted_iota(jnp.int32, x.shape, 0)
    return jnp.where((row % period) + d < period, pltpu.roll(x, n - d, 0), pltpu.roll(x, period - d, 0))


def _route(logits_t, bias):
    n = N_EXPERTS
    scores = jax.nn.sigmoid(logits_t)
    biased = scores + bias
    row = lax.broadcasted_iota(jnp.int32, biased.shape, 0)
    pos = row % EXPERTS_PER_GROUP
    rank = jnp.zeros(biased.shape, jnp.int32)
    for d in range(1, EXPERTS_PER_GROUP):
        other = _within(biased, d, EXPERTS_PER_GROUP, n)
        other_pos = (pos + d) % EXPERTS_PER_GROUP
        ahead = (other > biased) | ((other == biased) & (other_pos < pos))
        rank = rank + ahead.astype(jnp.int32)
    top2 = rank < 2
    kept = jnp.where(top2, biased, 0.0)
    gscore = kept
    for d in range(1, EXPERTS_PER_GROUP):
        gscore = gscore + _within(kept, d, EXPERTS_PER_GROUP, n)
    grp = row // EXPERTS_PER_GROUP
    win = jnp.ones(biased.shape, jnp.bool_)
    for d in range(1, N_GROUPS):
        other = pltpu.roll(gscore, n - d * EXPERTS_PER_GROUP, 0)
        other_grp = (grp + d) % N_GROUPS
        win = win & ((other < gscore) | ((other == gscore) & (other_grp > grp)))
    sel = top2 & win
    w = jnp.where(sel, scores, 0.0)
    return w / jnp.sum(w, axis=0, keepdims=True), sel


def _pack_pairs(v):
    n = v.shape[1] // 2
    vb = v.astype(BF16).astype(F32)
    hi = pltpu.bitcast(vb[:, :n], jnp.int32)
    lo = pltpu.bitcast(vb[:, n:], jnp.int32)
    return hi | lax.shift_right_logical(lo, 16)


def _unpack_pairs(w):
    hi = pltpu.bitcast(w & jnp.int32(-65536), F32)
    lo = pltpu.bitcast(lax.shift_left(w, 16), F32)
    return jnp.concatenate([hi, lo], axis=1)


R_E0, R_E1, R_RANK0, R_RANK1, R_W0, R_W1 = range(6)


def _mixer_out_body(oa_ref, ob_ref, ga_ref, gb_ref, x_ref, wba_ref, wbb_ref, wout_ref, gffn_ref, wrh_ref, wrl_ref,
                    br_ref, tri_ref, h_ref, tp_ref, route_ref, route_t_ref, count_ref):
    @pl.when(pl.program_id(0) == 0)
    def _():
        count_ref[...] = jnp.zeros_like(count_ref)

    ma = _dot_tn(oa_ref[...], wba_ref[...])
    mb = _dot_tn(ob_ref[...], wbb_ref[...])
    merged = ga_ref[...].astype(F32) * ma + gb_ref[...].astype(F32) * mb
    h = x_ref[...] + _dot(merged.astype(BF16), wout_ref[...])
    h_ref[...] = h
    t = _rms(h) * gffn_ref[...]
    t_hi, t_lo = _split_bf16(t)
    tp_ref[...] = _pack_pairs(t)
    wrh, wrl = wrh_ref[...], wrl_ref[...]
    logits_t = _dot_nt(wrh, t_hi) + _dot_nt(wrh, t_lo) + _dot_nt(wrl, t_hi)
    gates_t, sel = _route(logits_t, br_ref[...])

    tm = gates_t.shape[1]
    onehot = jnp.where(sel, 1.0, 0.0)
    before = _dot(onehot.astype(BF16), tri_ref[...])
    rank_t = count_ref[:, 0:1] + before
    count_ref[...] = count_ref[...] + jnp.sum(onehot, axis=1, keepdims=True)

    row = lax.broadcasted_iota(jnp.int32, sel.shape, 0).astype(F32)
    e0 = jnp.min(jnp.where(sel, row, float(N_EXPERTS)), axis=0, keepdims=True)
    e1 = jnp.max(jnp.where(sel, row, -1.0), axis=0, keepdims=True)
    pick = lambda v, e: jnp.sum(jnp.where(sel & (row == e), v, 0.0), axis=0, keepdims=True)
    rec = jnp.concatenate([e0, e1, pick(rank_t, e0), pick(rank_t, e1), pick(gates_t, e0), pick(gates_t, e1),
                           jnp.zeros((LANES - 6, tm), F32)], axis=0)
    route_ref[...] = rec.T
    route_t_ref[...] = rec[:8]


def _mixer_out(oa, ob, ga, gb, x, lw, l, tm):
    t = x.shape[0]
    row = lambda n: pl.BlockSpec((tm, n), lambda i: (i, 0))
    full = lambda a: pl.BlockSpec((None,) + a.shape[1:], lambda i: (l,) + (0,) * (a.ndim - 1))
    const = lambda a: pl.BlockSpec(a.shape, lambda i: (0,) * a.ndim)
    col = lambda n: pl.BlockSpec((n, tm), lambda i: (0, i))
    tri = (jnp.arange(tm)[:, None] < jnp.arange(tm)[None, :]).astype(BF16)
    ins = [oa, ob, ga, gb, x, lw["wba"], lw["wbb"], lw["wout"], lw["gffn"], lw["wrh"], lw["wrl"], lw["br"], tri]
    in_specs = [col(A_Q_COLS), col(B_O_COLS), row(D_MODEL), row(D_MODEL), row(D_MODEL),
                full(lw["wba"]), full(lw["wbb"]), full(lw["wout"]), full(lw["gffn"]),
                const(lw["wrh"]), const(lw["wrl"]), const(lw["br"]), const(tri)]
    return pl.pallas_call(
        _mixer_out_body,
        grid=(t // tm,),
        in_specs=in_specs,
        out_specs=[row(D_MODEL), row(D_MODEL // 2), row(LANES), col(8),
                   pl.BlockSpec((N_EXPERTS, LANES), lambda i: (0, 0))],
        out_shape=[jax.ShapeDtypeStruct((t, D_MODEL), F32), jax.ShapeDtypeStruct((t, D_MODEL // 2), jnp.int32),
                   jax.ShapeDtypeStruct((t, LANES), F32), jax.ShapeDtypeStruct((8, t), F32),
                   jax.ShapeDtypeStruct((N_EXPERTS, LANES), F32)],
        compiler_params=_cparams(("arbitrary",)),
        name="mixer_out",
    )(*ins)


def _sc_plan(nrows):
    info = plsc.get_sparse_core_info()
    workers = info.num_cores * info.num_subcores
    per_worker = nrows // workers
    chunk = min(SC_CHUNK, per_worker)
    assert per_worker * workers == nrows and per_worker % chunk == 0 and chunk % 8 == 0, (nrows, workers, chunk)
    return info.num_cores, per_worker, chunk


def _sc_mesh():
    return plsc.VectorSubcoreMesh(core_axis_name="core", subcore_axis_name="subcore")


def _scatter_rows(x, idx0, idx1, nrows):
    t, d = x.shape
    ncores, per_worker, chunk = _sc_plan(t)

    @functools.partial(
        pl.kernel, out_type=jax.ShapeDtypeStruct((nrows, d), x.dtype), mesh=_sc_mesh(), name="moe_dispatch",
        scratch_types=[pltpu.VMEM((chunk,), jnp.int32), pltpu.VMEM((chunk,), jnp.int32), pltpu.VMEM((chunk, d), x.dtype)])
    def run(x_hbm, i0_hbm, i1_hbm, o_hbm, i0_v, i1_v, rows_v):
        worker = lax.axis_index("subcore") * ncores + lax.axis_index("core")

        @pl.loop(0, per_worker // chunk)
        def _(c):
            base = pl.multiple_of(worker * per_worker + c * chunk, chunk)
            pltpu.sync_copy(x_hbm.at[pl.ds(base, chunk)], rows_v)
            pltpu.sync_copy(i0_hbm.at[pl.ds(base, chunk)], i0_v)
            pltpu.sync_copy(i1_hbm.at[pl.ds(base, chunk)], i1_v)
            pltpu.sync_copy(rows_v, o_hbm.at[i0_v])
            pltpu.sync_copy(rows_v, o_hbm.at[i1_v])

    return run(x, idx0, idx1)


def _gather_rows(table, idx):
    m = idx.shape[0]
    d = table.shape[1]
    ncores, per_worker, chunk = _sc_plan(m)

    @functools.partial(
        pl.kernel, out_type=jax.ShapeDtypeStruct((m, d), table.dtype), mesh=_sc_mesh(), name="moe_collect",
        scratch_types=[pltpu.VMEM((chunk,), jnp.int32), pltpu.VMEM((chunk, d), table.dtype)])
    def run(x_hbm, i_hbm, o_hbm, i_v, rows_v):
        worker = lax.axis_index("subcore") * ncores + lax.axis_index("core")

        @pl.loop(0, per_worker // chunk)
        def _(c):
            base = pl.multiple_of(worker * per_worker + c * chunk, chunk)
            pltpu.sync_copy(i_hbm.at[pl.ds(base, chunk)], i_v)
            pltpu.sync_copy(x_hbm.at[i_v], rows_v)
            pltpu.sync_copy(rows_v, o_hbm.at[pl.ds(base, chunk)])

    return run(table, idx)


def _routing_tables(route_t, counts, tr):
    t = route_t.shape[1]
    experts = jnp.arange(N_EXPERTS, dtype=jnp.int32)
    cnt = counts[:, 0].astype(jnp.int32)
    seg_end = jnp.cumsum(cnt)
    seg_start = seg_end - cnt
    lookup = lambda table, idx: jnp.sum(jnp.where(idx[None, :] == experts[:, None], table[:, None], 0), axis=0)
    e = route_t[R_E0:R_E1 + 1].astype(jnp.int32).reshape(2 * t)
    rank = route_t[R_RANK0:R_RANK1 + 1].astype(jnp.int32).reshape(2 * t)
    pos = lookup(seg_start, e) + rank

    n_tiles = 2 * t // tr
    n_visits = n_tiles + N_EXPERTS - 1
    first_tile = seg_start // tr
    last_tile = jnp.maximum(seg_end - 1, 0) // tr
    visits = jnp.where(cnt > 0, last_tile - first_tile + 1, 0)
    visit_end = jnp.cumsum(visits)
    visit_start = visit_end - visits
    g = jnp.arange(n_visits, dtype=jnp.int32)
    valid = g < visit_end[-1]
    ex = jnp.minimum(jnp.sum((g[None, :] >= visit_end[:, None]).astype(jnp.int32), axis=0), N_EXPERTS - 1)
    tile = lookup(first_tile - visit_start, ex) + g
    lo = jnp.clip(lookup(seg_start, ex) - tile * tr, 0, tr)
    hi = jnp.clip(lookup(seg_end, ex) - tile * tr, 0, tr)
    last_ex = jnp.max(jnp.where(cnt > 0, experts, 0))
    tile = jnp.where(valid, tile, n_tiles - 1)
    ex = jnp.where(valid, ex, last_ex)
    lo = jnp.where(valid, lo, 0)
    hi = jnp.where(valid, hi, 0)
    first = jnp.concatenate([jnp.ones((1,), jnp.int32), (tile[1:] != tile[:-1]).astype(jnp.int32)])
    first = jnp.where(valid, first, 0)
    fresh = jnp.concatenate([jnp.ones((1,), jnp.int32), (ex[1:] != ex[:-1]).astype(jnp.int32)])
    return pos, (tile, ex, lo, hi, first, fresh)


def _experts_body(tile_ref, ex_ref, lo_ref, hi_ref, first_ref, fresh_ref, xs_ref, wg_ref, wu_ref, wd_ref, ys_ref,
                  acc_ref, wg_bf, wu_bf, wd_bf):
    g = pl.program_id(0)
    lo, hi = lo_ref[g], hi_ref[g]

    @pl.when(fresh_ref[g] == 1)
    def _():
        wg_bf[...] = wg_ref[...].astype(BF16)
        wu_bf[...] = wu_ref[...].astype(BF16)
        wd_bf[...] = wd_ref[...].astype(BF16)

    @pl.when(hi > lo)
    def _():
        x = _unpack_pairs(xs_ref[...]).astype(BF16)
        a = jax.nn.silu(_dot(x, wg_bf[...])) * _dot(x, wu_bf[...])
        row = lax.broadcasted_iota(jnp.int32, (a.shape[0], 1), 0)
        a = jnp.where((row >= lo) & (row < hi), a, 0.0)
        y = _dot(a.astype(BF16), wd_bf[...])

        @pl.when(first_ref[g] == 1)
        def _():
            acc_ref[...] = y

        @pl.when(first_ref[g] == 0)
        def _():
            acc_ref[...] += y

        ys_ref[...] = _pack_pairs(acc_ref[...])


def _experts(xs, visits, lw, l, tr):
    n_visits = visits[0].shape[0]
    rows = pl.BlockSpec((tr, D_MODEL // 2), lambda g, tile, *_: (tile[g], 0))
    wspec = lambda a: pl.BlockSpec((None, None) + a.shape[2:], lambda g, tile, ex, *_: (l, ex[g], 0, 0))
    return pl.pallas_call(
        _experts_body,
        grid_spec=pltpu.PrefetchScalarGridSpec(
            num_scalar_prefetch=len(visits),
            grid=(n_visits,),
            in_specs=[rows, wspec(lw["wg"]), wspec(lw["wu"]), wspec(lw["wd"])],
            out_specs=rows,
            scratch_shapes=[pltpu.VMEM((tr, D_MODEL), F32)] + [pltpu.VMEM(lw[k].shape[2:], BF16) for k in ("wg", "wu", "wd")],
        ),
        out_shape=jax.ShapeDtypeStruct(xs.shape, jnp.int32),
        compiler_params=_cparams(("arbitrary",)),
        name="experts",
    )(*visits, xs, lw["wg"], lw["wu"], lw["wd"])


def _moe_residual(h_ref, y0_ref, y1_ref, route_ref):
    r = route_ref[...]
    return (h_ref[...] + r[:, R_W0:R_W0 + 1] * _unpack_pairs(y0_ref[...])
            + r[:, R_W1:R_W1 + 1] * _unpack_pairs(y1_ref[...]))


def _final_body(h_ref, y0_ref, y1_ref, route_ref, gfin_ref, o_ref):
    o_ref[...] = _rms(_moe_residual(h_ref, y0_ref, y1_ref, route_ref)) * gfin_ref[...]


def _final(h, y, route, gfin, tm):
    t = h.shape[0]
    row = lambda n: pl.BlockSpec((tm, n), lambda i: (i, 0))
    second = pl.BlockSpec((tm, D_MODEL // 2), lambda i: (i + t // tm, 0))
    return pl.pallas_call(
        _final_body,
        grid=(t // tm,),
        in_specs=[row(D_MODEL), row(D_MODEL // 2), second, row(LANES), pl.BlockSpec(gfin.shape, lambda i: (0, 0))],
        out_specs=row(D_MODEL),
        out_shape=jax.ShapeDtypeStruct(h.shape, F32),
        compiler_params=_cparams(("parallel",)),
        name="final",
    )(h, y, y, route, gfin)


def _moe_rows(tp, route_t, counts, lw, l, tr):
    t = tp.shape[0]
    pos, visits = _routing_tables(route_t, counts, tr)
    xs = _scatter_rows(tp, pos[:t], pos[t:], 2 * t)
    ys = _experts(xs, visits, lw, l, tr)
    return _gather_rows(ys, pos)


def _rope_angles(seq_len, rot_dim):
    rows = seq_len // GRID_W
    row = jnp.broadcast_to(jnp.arange(rows)[:, None], (rows, GRID_W)).reshape(-1).astype(F32)
    col = jnp.broadcast_to(jnp.arange(GRID_W)[None, :], (rows, GRID_W)).reshape(-1).astype(F32)
    axis_dim = rot_dim // 2
    inv_freq = jnp.power(jnp.float32(ROPE_THETA), -jnp.arange(0, axis_dim, 2, dtype=F32) / axis_dim)
    ang = jnp.concatenate([row[:, None] * inv_freq[None, :], col[:, None] * inv_freq[None, :]], axis=-1)
    return jnp.cos(ang), jnp.sin(ang)


def _rope_tables(nseq, seq_len):
    c, s = _rope_angles(seq_len, A_HEAD_DIM)
    ca = jnp.concatenate([c, c, c, c], axis=-1)
    sa = jnp.concatenate([-s, s, -s, s], axis=-1)
    c, s = _rope_angles(seq_len, B_ROPE)
    ones = jnp.ones((seq_len, B_NOPE), F32)
    zeros = jnp.zeros((seq_len, B_NOPE), F32)
    tail = LANES - B_NOPE - B_ROPE
    cb = jnp.concatenate([ones, c, c, ones[:, :tail]], axis=-1)
    sb = jnp.concatenate([zeros, -s, s, zeros[:, :tail]], axis=-1)
    return tuple(jnp.tile(a, (nseq, 1)) for a in (ca, sa, cb, sb))


def _prepare_weights(norm_mix, w_in, a_q_norm, a_k_norm, b_q_norm, b_kv_norm, w_q_up, w_kv_up, w_branch_a,
                     w_branch_b, w_out, norm_ffn, w_router, b_router, w_gate, w_up, w_down):
    depth = w_in.shape[0]
    order = jnp.array(A_HEAD_ORDER)
    parts = []
    start = 0
    for n in (A_Q_COLS, A_KV_COLS, A_KV_COLS, B_Q_RANK, B_KV_RANK, B_ROPE, D_MODEL, D_MODEL):
        parts.append(w_in[..., start:start + n])
        start += n
    qa, ka, va, cq, ckv, kr, ga, gb = parts
    qa = qa.reshape(depth, D_MODEL, A_HEADS, A_HEAD_DIM)[:, :, order].reshape(depth, D_MODEL, A_Q_COLS)
    kr = jnp.pad(kr, ((0, 0), (0, 0), (B_NOPE, LANES - B_NOPE - B_ROPE)))
    win = jnp.concatenate([qa, ka, va, cq, ckv, kr, ga, gb], axis=-1).astype(BF16)

    wq = w_q_up.reshape(depth, B_Q_RANK, B_HEADS, B_QK_DIM)
    wq = jnp.pad(wq, ((0, 0), (0, 0), (0, 0), (0, LANES - B_QK_DIM))).reshape(depth, B_Q_RANK, B_PAD_COLS)
    wkv = w_kv_up.reshape(depth, B_KV_RANK, B_HEADS, B_NOPE + B_V)
    wk = jnp.pad(wkv[..., :B_NOPE], ((0, 0), (0, 0), (0, 0), (0, LANES - B_NOPE))).reshape(depth, B_KV_RANK, B_PAD_COLS)
    wv = wkv[..., B_NOPE:].reshape(depth, B_KV_RANK, B_O_COLS)
    wba = w_branch_a.reshape(depth, A_HEADS, A_HEAD_DIM, D_MODEL)[:, order].reshape(depth, A_Q_COLS, D_MODEL)

    group = jnp.arange(A_Q_COLS) // A_HEAD_DIM
    bd = (group[:, None] == group[None, :]).astype(BF16)
    wr_t = w_router.T
    wrh, wrl = _split_bf16(wr_t)
    vec = lambda a: a[:, None, :]
    return dict(
        gmix=vec(norm_mix), win=win,
        gq=vec(jnp.tile(a_q_norm, (1, A_HEADS)) * (A_HEAD_DIM ** -0.5 * LOG2E)), gk=vec(jnp.tile(a_k_norm, (1, A_KV_HEADS))),
        gbq=vec(b_q_norm), gbkv=vec(b_kv_norm),
        wq=wq.astype(BF16), wk=wk.astype(BF16), wv=wv.astype(BF16), bd=bd,
        wba=wba.astype(BF16), wbb=w_branch_b.astype(BF16), wout=w_out.astype(BF16), gffn=vec(norm_ffn),
        wrh=wrh, wrl=wrl, br=b_router[:, None].astype(F32),
        wg=w_gate, wu=w_up, wd=w_down,
    )


def _pick(n, candidates):
    for c in candidates:
        if n % c == 0:
            return c
    raise ValueError(f"no tile in {candidates} divides {n}")


def _trunk(x3, lw, norm_final):
    nseq, s, _ = x3.shape
    t = nseq * s
    x = x3.reshape(t, D_MODEL)
    tabs = _rope_tables(nseq, s)
    depth = lw["win"].shape[0]
    tm = _pick(t, (256,))
    tm_out = _pick(t, (512, 256))
    tq = _pick(s, (512, 256))
    tk = _pick(s, (512,))
    moe_out = None
    for l in range(depth):
        x_new, qa, ka, va, qb, kb, vb, ga, gb = _mixer_in(x, moe_out, lw, l, tabs, tm_out)
        x = x if x_new is None else x_new
        oa = _attention_a(qa, ka, va, nseq, s, tq, tk)
        ob = _attention_b(qb, kb, vb, nseq, s, tq, tk)
        h, tp, route, route_t, counts = _mixer_out(oa, ob, ga, gb, x, lw, l, _pick(t, (1024, 512, 256)))
        moe_out = (h, _moe_rows(tp, route_t, counts, lw, l, EXPERT_ROWS), route)
        x = None
    return _final(*moe_out, norm_final[None, :], tm).reshape(x3.shape)


def kernel(x_prompt, x_sample, norm_mix, w_in, a_q_norm, a_k_norm, b_q_norm, b_kv_norm, w_q_up, w_kv_up, w_branch_a,
           w_branch_b, w_out, norm_ffn, w_router, b_router, w_gate, w_up, w_down, norm_final):
    lw = _prepare_weights(norm_mix, w_in, a_q_norm, a_k_norm, b_q_norm, b_kv_norm, w_q_up, w_kv_up, w_branch_a,
                          w_branch_b, w_out, norm_ffn, w_router, b_router, w_gate, w_up, w_down)
    return _trunk(x_prompt, lw, norm_final), _trunk(x_sample, lw, norm_final)
```

```python
import functools

import jax
import jax.numpy as jnp
from jax import lax
from jax.experimental import pallas as pl
from jax.experimental.pallas import tpu as pltpu
from jax.experimental.pallas import tpu_sc as plsc

F32 = jnp.float32
BF16 = jnp.bfloat16

D_MODEL = 1024
GRID_W = 64
ROPE_THETA = 10000.0
EPS = 1e-6
A_HEADS = 8
A_KV_HEADS = 2
A_HEAD_DIM = 64
B_HEADS = 8
B_Q_RANK = 384
B_KV_RANK = 256
B_NOPE = 64
B_ROPE = 32
B_V = 64
N_EXPERTS = 16
N_GROUPS = 4
EXPERTS_PER_GROUP = N_EXPERTS // N_GROUPS
D_EXPERT = 512
A_Q_COLS = A_HEADS * A_HEAD_DIM
A_KV_COLS = A_KV_HEADS * A_HEAD_DIM
B_QK_DIM = B_NOPE + B_ROPE
B_O_COLS = B_HEADS * B_V

LANES = 128
LOG2E = 1.4426950408889634

C_QA = 0
C_KA = C_QA + A_Q_COLS
C_VA = C_KA + A_KV_COLS
C_CQ = C_VA + A_KV_COLS
C_CKV = C_CQ + B_Q_RANK
C_KR = C_CKV + B_KV_RANK
C_GA = C_KR + LANES
C_GB = C_GA + D_MODEL
C_END = C_GB + D_MODEL
B_PAD_COLS = B_HEADS * LANES

A_HEAD_ORDER = (0, 4, 1, 5, 2, 6, 3, 7)

VMEM_LIMIT = 56 * 1024 * 1024

ATTN_GROUP = 8
ATTN_S_BUFS = 3
ATTN_P_BUFS = 2
ATTN_SLAB = 32
ATTN_FIXED_SHIFT_MAX = 48.0
EXPERT_ROWS = 512
SC_CHUNK = 128


def _cparams(sem):
    return pltpu.CompilerParams(dimension_semantics=sem, vmem_limit_bytes=VMEM_LIMIT)


def _dot(a, b):
    return jnp.dot(a, b, preferred_element_type=F32)


def _dot_nt(a, b):
    return lax.dot_general(a, b, (((1,), (1,)), ((), ())), preferred_element_type=F32)


def _dot_tn(a, b):
    return lax.dot_general(a, b, (((0,), (0,)), ((), ())), preferred_element_type=F32)


def _rms(x):
    return x * lax.rsqrt(jnp.mean(x * x, axis=-1, keepdims=True) + EPS)


def _split_bf16(x):
    hi = x.astype(BF16)
    lo = (x - hi.astype(F32)).astype(BF16)
    return hi, lo


def _group_mean_sq(v, bd):
    return _dot((v * v).astype(BF16), bd) * (1.0 / A_HEAD_DIM)


def _rope_a(v, cos, sin):
    n = v.shape[-1]
    lane = lax.broadcasted_iota(jnp.int32, v.shape, 1)
    low = (lane % A_HEAD_DIM) < (A_HEAD_DIM // 2)
    swapped = jnp.where(low, pltpu.roll(v, n - A_HEAD_DIM // 2, 1), pltpu.roll(v, A_HEAD_DIM // 2, 1))
    return v * cos + swapped * sin


def _rope_b(v, cos, sin):
    n = v.shape[-1]
    lane = lax.broadcasted_iota(jnp.int32, v.shape, 1)
    low = (lane % LANES) < (B_NOPE + B_ROPE // 2)
    swapped = jnp.where(low, pltpu.roll(v, n - B_ROPE // 2, 1), pltpu.roll(v, B_ROPE // 2, 1))
    return v * cos + swapped * sin


def _mixer_in_body(*refs, fused):
    if fused:
        h_ref, y0_ref, y1_ref, route_ref, *refs = refs
    else:
        x_ref, *refs = refs
    (gmix_ref, win_ref, ca_ref, sa_ref, cb_ref, sb_ref, gq_ref, gk_ref, gbq_ref, gbkv_ref, wq_ref, wk_ref, wv_ref,
     bd_ref, *outs) = refs
    if fused:
        x_out_ref, *outs = outs
        x = _moe_residual(h_ref, y0_ref, y1_ref, route_ref)
        x_out_ref[...] = x
    else:
        x = x_ref[...]
    qat_ref, ka_ref, vat_ref, qbt_ref, kb_ref, vbt_ref, ga_ref, gb_ref = outs
    hb = (_rms(x) * gmix_ref[...]).astype(BF16)
    u = _dot(hb, win_ref[:, C_QA:C_GA])
    bd = bd_ref[...]
    ca, sa = ca_ref[...], sa_ref[...]
    cb, sb = cb_ref[...], sb_ref[...]

    qa = u[:, C_QA:C_KA]
    qa = qa * lax.rsqrt(_group_mean_sq(qa, bd) + EPS) * gq_ref[...]
    qa = _rope_a(qa, jnp.concatenate([ca] * (A_Q_COLS // LANES), axis=1),
                 jnp.concatenate([sa] * (A_Q_COLS // LANES), axis=1))
    qat_ref[...] = qa.T.astype(BF16)
    ka = u[:, C_KA:C_VA]
    ka = ka * lax.rsqrt(_group_mean_sq(ka, bd[:A_KV_COLS, :A_KV_COLS]) + EPS) * gk_ref[...]
    ka_ref[...] = _rope_a(ka, ca, sa).astype(BF16)
    vat_ref[...] = u[:, C_VA:C_CQ].T.astype(BF16)

    cq = (_rms(u[:, C_CQ:C_CKV]) * gbq_ref[...]).astype(BF16)
    qb = _dot(cq, wq_ref[...])
    qb = _rope_b(qb, jnp.concatenate([cb] * B_HEADS, axis=1), jnp.concatenate([sb] * B_HEADS, axis=1))
    qbt_ref[...] = (qb * (B_QK_DIM ** -0.5 * LOG2E)).T.astype(BF16)
    ckv = (_rms(u[:, C_CKV:C_KR]) * gbkv_ref[...]).astype(BF16)
    kr = _rope_b(u[:, C_KR:C_GA], cb, sb)
    kb_ref[...] = (_dot(ckv, wk_ref[...]) + jnp.concatenate([kr] * B_HEADS, axis=1)).astype(BF16)
    vbt_ref[...] = _dot(ckv, wv_ref[...]).T.astype(BF16)

    g = jax.nn.sigmoid(_dot(hb, win_ref[:, C_GA:C_END]))
    ga_ref[...] = g[:, :D_MODEL].astype(BF16)
    gb_ref[...] = g[:, D_MODEL:].astype(BF16)


def _mixer_in(x, moe_out, lw, l, tabs, tm):
    fused = x is None
    t = moe_out[0].shape[0] if fused else x.shape[0]
    row = lambda n: pl.BlockSpec((tm, n), lambda i: (i, 0))
    full = lambda a: pl.BlockSpec((None,) + a.shape[1:], lambda i: (l,) + (0,) * (a.ndim - 1))
    const = lambda a: pl.BlockSpec(a.shape, lambda i: (0,) * a.ndim)
    col = lambda n: pl.BlockSpec((n, tm), lambda i: (0, i))
    ca, sa, cb, sb = tabs
    if fused:
        h, y, route = moe_out
        second = pl.BlockSpec((tm, D_MODEL // 2), lambda i: (i + t // tm, 0))
        ins, in_specs = [h, y, y, route], [row(D_MODEL), row(D_MODEL // 2), second, row(LANES)]
    else:
        ins, in_specs = [x], [row(D_MODEL)]
    ins += [lw["gmix"], lw["win"], ca, sa, cb, sb, lw["gq"], lw["gk"], lw["gbq"], lw["gbkv"],
            lw["wq"], lw["wk"], lw["wv"], lw["bd"]]
    in_specs += [full(lw["gmix"]), full(lw["win"]), row(LANES), row(LANES), row(LANES), row(LANES),
                 full(lw["gq"]), full(lw["gk"]), full(lw["gbq"]), full(lw["gbkv"]),
                 full(lw["wq"]), full(lw["wk"]), full(lw["wv"]), const(lw["bd"])]
    widths = [A_Q_COLS, A_KV_COLS, A_KV_COLS, B_PAD_COLS, B_PAD_COLS, B_O_COLS, D_MODEL, D_MODEL]
    transposed = [True, False, True, True, False, True, False, False]
    out_specs = [col(n) if tr else row(n) for n, tr in zip(widths, transposed)]
    out_shape = [jax.ShapeDtypeStruct((n, t) if tr else (t, n), BF16) for n, tr in zip(widths, transposed)]
    if fused:
        out_specs.insert(0, row(D_MODEL))
        out_shape.insert(0, jax.ShapeDtypeStruct((t, D_MODEL), F32))
    outs = pl.pallas_call(
        functools.partial(_mixer_in_body, fused=fused),
        grid=(t // tm,),
        in_specs=in_specs,
        out_specs=out_specs,
        out_shape=out_shape,
        compiler_params=_cparams(("parallel",)),
        name="mixer_in",
    )(*ins)
    return tuple(outs) if fused else (None,) + tuple(outs)


def _flash_pair(q0t, q1t, k0_ref, k1_ref, vt_ref, o_ref, s_refs, p_refs, acc_ref, kmax2, fixed, tk):
    tq = q0t.shape[1]
    nk = vt_ref.shape[1] // tk
    group = min(nk, ATTN_GROUP)
    ns = len(s_refs)
    qt2 = jnp.concatenate([q0t, q1t], axis=1)

    def score_values(j):
        off = pl.multiple_of(j * tk, tk)
        if k1_ref is None:
            return _dot(k0_ref[pl.ds(off, tk), :], qt2)
        return jnp.concatenate([_dot(k0_ref[pl.ds(off, tk), :], q0t), _dot(k1_ref[pl.ds(off, tk), :], q1t)], axis=1)

    def add_values(j, p, alpha):
        off = pl.multiple_of(j * tk, tk)
        for rows, lanes in ((slice(0, B_V), slice(0, tq)), (slice(B_V, LANES), slice(tq, 2 * tq))):
            prev = acc_ref[rows, :] if alpha is None else alpha[:, lanes] * acc_ref[rows, :]
            acc_ref[rows, :] = prev + _dot(vt_ref[rows, pl.ds(off, tk)], p[:, lanes])

    def finish(l):
        l = jnp.sum(l, axis=0, keepdims=True)
        o_ref[:B_V, :] = (acc_ref[:B_V, :] / l[:, :tq]).astype(o_ref.dtype)
        o_ref[B_V:, :] = (acc_ref[B_V:, :] / l[:, tq:]).astype(o_ref.dtype)

    def run_groups(one_group, carry):
        acc_ref[...] = jnp.zeros_like(acc_ref)
        return one_group(0, carry) if nk == group else lax.fori_loop(0, nk // group, one_group, carry)

    def fixed_shift(shift):
        wide = 2 * LANES
        tiles = [(n * wide, (n * wide) // tq) for n in range(2 * tq // wide)]

        def value_tile(j, p, lane0, head):
            off = pl.multiple_of(j * tk, tk)
            rows = slice(head * B_V, (head + 1) * B_V)
            lanes = slice(lane0 - head * tq, lane0 - head * tq + wide)
            acc_ref[rows, lanes] += _dot(vt_ref[rows, pl.ds(off, tk)], p)

        def one_group(g, l):
            l = list(l)
            prev = None
            for c in range(group):
                j = g * group + c
                off = pl.multiple_of(j * tk, tk)
                cur = []
                for n, (lane0, head) in enumerate(tiles):
                    k_ref = k0_ref if (k1_ref is None or head == 0) else k1_ref
                    s = _dot(k_ref[pl.ds(off, tk), :], qt2[:, lane0:lane0 + wide])
                    if prev is not None:
                        value_tile(j - 1, prev[n], lane0, head)
                    p = jnp.exp2(s - shift[:, lane0:lane0 + wide])
                    l[n] = l[n] + p.reshape(tk // 8, 8, wide).sum(axis=0)
                    cur.append(p.astype(BF16))
                prev = cur
            for n, (lane0, head) in enumerate(tiles):
                value_tile(g * group + group - 1, prev[n], lane0, head)
            return tuple(l)

        l = run_groups(one_group, tuple(jnp.zeros((8, wide), F32) for _ in tiles))
        finish(jnp.concatenate(l, axis=1))

    def softmax(s_ref, p_ref, m, l):
        slabs = [pl.ds(r, ATTN_SLAB) for r in range(0, tk, ATTN_SLAB)]
        fold = lambda x: x.reshape(ATTN_SLAB // 8, 8, 2 * tq)
        mx = fold(s_ref[slabs[0], :]).max(axis=0)
        for sl in slabs[1:]:
            mx = jnp.maximum(mx, fold(s_ref[sl, :]).max(axis=0))
        m_new = jnp.maximum(m, jnp.max(mx, axis=0, keepdims=True))
        alpha = jnp.exp2(m - m_new)
        l = alpha * l
        for sl in slabs:
            p = jnp.exp2(s_ref[sl, :] - m_new)
            p_ref[sl, :] = p.astype(BF16)
            l = l + fold(p).sum(axis=0)
        return m_new, l, alpha

    def running_max():
        def one_group(g, carry):
            m, l = carry
            base = g * group
            for c in range(min(ns - 1, group)):
                s_refs[c % ns][...] = score_values(base + c)
            for c in range(group):
                if c + ns - 1 < group:
                    s_refs[(c + ns - 1) % ns][...] = score_values(base + c + ns - 1)
                p_ref = p_refs[c % len(p_refs)]
                m, l, alpha = softmax(s_refs[c % ns], p_ref, m, l)
                add_values(base + c, p_ref[...], alpha)
            return m, l

        carry = (jnp.full((1, 2 * tq), -jnp.inf, F32), jnp.zeros((8, 2 * tq), F32))
        finish(run_groups(one_group, carry)[1])

    if fixed:
        lane = lax.broadcasted_iota(jnp.int32, (1, 2 * tq), 1)
        qsq = jnp.sum(jnp.square(qt2.astype(F32)), axis=0, keepdims=True)
        fixed_shift(jnp.sqrt(qsq * jnp.where(lane < tq, kmax2[0], kmax2[1])))
    else:
        running_max()


def _max_sq_norm(k_ref, ones_ref, lanes):
    k = k_ref[...].astype(F32)
    n = _dot((k * k).astype(BF16), ones_ref[...])
    return jnp.max(n[:, lanes])


def _max_query_sq_norm(q):
    q = q.astype(F32)
    return jnp.max(jnp.sum(q * q, axis=0, keepdims=True))


def _attention_tiles(q_pair, k0_ref, k1_ref, vt_ref, o_ref, scratch, qmax2, kmax2, tq, tk):
    s_refs, p_refs, acc_ref = scratch[:ATTN_S_BUFS], scratch[ATTN_S_BUFS:-1], scratch[-1]
    bound2 = jnp.maximum(qmax2[0] * kmax2[0], qmax2[1] * kmax2[1])
    small = bound2 <= ATTN_FIXED_SHIFT_MAX * ATTN_FIXED_SHIFT_MAX

    def walk(fixed):
        @pl.loop(0, o_ref.shape[1] // tq)
        def _(j):
            lanes = pl.ds(pl.multiple_of(j * tq, tq), tq)
            q0t, q1t = q_pair(lanes)
            _flash_pair(q0t, q1t, k0_ref, k1_ref, vt_ref, o_ref.at[:, lanes], s_refs, p_refs, acc_ref, kmax2, fixed, tk)

    pl.when(small)(lambda: walk(True))
    pl.when(jnp.logical_not(small))(lambda: walk(False))


def _attn_a_body(qt_ref, k_ref, vt_ref, ones_ref, o_ref, *scratch, tq, tk):
    kmax2 = (_max_sq_norm(k_ref, ones_ref, slice(0, A_HEAD_DIM)), _max_sq_norm(k_ref, ones_ref, slice(A_HEAD_DIM, LANES)))
    qmax2 = (_max_query_sq_norm(qt_ref[:A_HEAD_DIM, :]), _max_query_sq_norm(qt_ref[A_HEAD_DIM:, :]))

    def q_pair(lanes):
        qt = qt_ref[:, lanes]
        zero = jnp.zeros((A_HEAD_DIM, tq), BF16)
        return jnp.concatenate([qt[:A_HEAD_DIM], zero], axis=0), jnp.concatenate([zero, qt[A_HEAD_DIM:]], axis=0)

    _attention_tiles(q_pair, k_ref, None, vt_ref, o_ref, scratch, qmax2, kmax2, tq, tk)


def _attn_b_body(q0t_ref, q1t_ref, k0_ref, k1_ref, vt_ref, ones_ref, o_ref, *scratch, tq, tk):
    kmax2 = (_max_sq_norm(k0_ref, ones_ref, slice(0, LANES)), _max_sq_norm(k1_ref, ones_ref, slice(0, LANES)))
    qmax2 = (_max_query_sq_norm(q0t_ref[...]), _max_query_sq_norm(q1t_ref[...]))
    q_pair = lambda lanes: (q0t_ref[:, lanes], q1t_ref[:, lanes])
    _attention_tiles(q_pair, k0_ref, k1_ref, vt_ref, o_ref, scratch, qmax2, kmax2, tq, tk)


def _attn_scratch(tq, tk):
    return ([pltpu.VMEM((tk, 2 * tq), F32)] * ATTN_S_BUFS + [pltpu.VMEM((tk, 2 * tq), BF16)] * ATTN_P_BUFS
            + [pltpu.VMEM((LANES, tq), F32)])


def _head_ones(head_dim):
    group = jnp.arange(LANES) // head_dim
    return (group[:, None] == group[None, :]).astype(BF16)


def _attention_a(qat, ka, vat, nseq, s, tq, tk):
    qspec = pl.BlockSpec((LANES, s), lambda b, h: (h, b))
    kspec = pl.BlockSpec((s, LANES), lambda b, h: (b, 0))
    vspec = pl.BlockSpec((LANES, s), lambda b, h: (0, b))
    ones = pl.BlockSpec((LANES, LANES), lambda b, h: (0, 0))
    return pl.pallas_call(
        functools.partial(_attn_a_body, tq=tq, tk=tk),
        grid=(nseq, A_Q_COLS // LANES),
        in_specs=[qspec, kspec, vspec, ones],
        out_specs=qspec,
        out_shape=jax.ShapeDtypeStruct(qat.shape, BF16),
        scratch_shapes=_attn_scratch(tq, tk),
        compiler_params=_cparams(("parallel", "parallel")),
        name="attn_a",
    )(qat, ka, vat, _head_ones(A_HEAD_DIM))


def _attention_b(qbt, kb, vbt, nseq, s, tq, tk):
    q0 = pl.BlockSpec((LANES, s), lambda b, h: (2 * h, b))
    q1 = pl.BlockSpec((LANES, s), lambda b, h: (2 * h + 1, b))
    k0 = pl.BlockSpec((s, LANES), lambda b, h: (b, 2 * h))
    k1 = pl.BlockSpec((s, LANES), lambda b, h: (b, 2 * h + 1))
    v = pl.BlockSpec((LANES, s), lambda b, h: (h, b))
    o = pl.BlockSpec((LANES, s), lambda b, h: (h, b))
    ones = pl.BlockSpec((LANES, LANES), lambda b, h: (0, 0))
    return pl.pallas_call(
        functools.partial(_attn_b_body, tq=tq, tk=tk),
        grid=(nseq, B_O_COLS // LANES),
        in_specs=[q0, q1, k0, k1, v, ones],
        out_specs=o,
        out_shape=jax.ShapeDtypeStruct(vbt.shape, BF16),
        scratch_shapes=_attn_scratch(tq, tk),
        compiler_params=_cparams(("parallel", "parallel")),
        name="attn_b",
    )(qbt, qbt, kb, kb, vbt, _head_ones(LANES))


def _within(x, d, period, n):
    row = lax.broadcasted_iota(jnp.int32, x.shape, 0)
    return jnp.where((row % period) + d < period, pltpu.roll(x, n - d, 0), pltpu.roll(x, period - d, 0))


def _route(logits_t, bias):
    n = N_EXPERTS
    scores = jax.nn.sigmoid(logits_t)
    biased = scores + bias
    row = lax.broadcasted_iota(jnp.int32, biased.shape, 0)
    pos = row % EXPERTS_PER_GROUP
    rank = jnp.zeros(biased.shape, jnp.int32)
    for d in range(1, EXPERTS_PER_GROUP):
        other = _within(biased, d, EXPERTS_PER_GROUP, n)
        other_pos = (pos + d) % EXPERTS_PER_GROUP
        ahead = (other > biased) | ((other == biased) & (other_pos < pos))
        rank = rank + ahead.astype(jnp.int32)
    top2 = rank < 2
    kept = jnp.where(top2, biased, 0.0)
    gscore = kept
    for d in range(1, EXPERTS_PER_GROUP):
        gscore = gscore + _within(kept, d, EXPERTS_PER_GROUP, n)
    grp = row // EXPERTS_PER_GROUP
    win = jnp.ones(biased.shape, jnp.bool_)
    for d in range(1, N_GROUPS):
        other = pltpu.roll(gscore, n - d * EXPERTS_PER_GROUP, 0)
        other_grp = (grp + d) % N_GROUPS
        win = win & ((other < gscore) | ((other == gscore) & (other_grp > grp)))
    sel = top2 & win
    w = jnp.where(sel, scores, 0.0)
    return w / jnp.sum(w, axis=0, keepdims=True), sel


def _pack_pairs(v):
    n = v.shape[1] // 2
    vb = v.astype(BF16).astype(F32)
    hi = pltpu.bitcast(vb[:, :n], jnp.int32)
    lo = pltpu.bitcast(vb[:, n:], jnp.int32)
    return hi | lax.shift_right_logical(lo, 16)


def _unpack_pairs(w):
    hi = pltpu.bitcast(w & jnp.int32(-65536), F32)
    lo = pltpu.bitcast(lax.shift_left(w, 16), F32)
    return jnp.concatenate([hi, lo], axis=1)


R_E0, R_E1, R_RANK0, R_RANK1, R_W0, R_W1 = range(6)


def _mixer_out_body(oa_ref, ob_ref, ga_ref, gb_ref, x_ref, wba_ref, wbb_ref, wout_ref, gffn_ref, wrh_ref, wrl_ref,
                    br_ref, tri_ref, h_ref, tp_ref, route_ref, route_t_ref, count_ref):
    @pl.when(pl.program_id(0) == 0)
    def _():
        count_ref[...] = jnp.zeros_like(count_ref)

    ma = _dot_tn(oa_ref[...], wba_ref[...])
    mb = _dot_tn(ob_ref[...], wbb_ref[...])
    merged = ga_ref[...].astype(F32) * ma + gb_ref[...].astype(F32) * mb
    h = x_ref[...] + _dot(merged.astype(BF16), wout_ref[...])
    h_ref[...] = h
    t = _rms(h) * gffn_ref[...]
    t_hi, t_lo = _split_bf16(t)
    tp_ref[...] = _pack_pairs(t)
    wrh, wrl = wrh_ref[...], wrl_ref[...]
    logits_t = _dot_nt(wrh, t_hi) + _dot_nt(wrh, t_lo) + _dot_nt(wrl, t_hi)
    gates_t, sel = _route(logits_t, br_ref[...])

    tm = gates_t.shape[1]
    onehot = jnp.where(sel, 1.0, 0.0)
    before = _dot(onehot.astype(BF16), tri_ref[...])
    rank_t = count_ref[:, 0:1] + before
    count_ref[...] = count_ref[...] + jnp.sum(onehot, axis=1, keepdims=True)

    row = lax.broadcasted_iota(jnp.int32, sel.shape, 0).astype(F32)
    e0 = jnp.min(jnp.where(sel, row, float(N_EXPERTS)), axis=0, keepdims=True)
    e1 = jnp.max(jnp.where(sel, row, -1.0), axis=0, keepdims=True)
    pick = lambda v, e: jnp.sum(jnp.where(sel & (row == e), v, 0.0), axis=0, keepdims=True)
    rec = jnp.concatenate([e0, e1, pick(rank_t, e0), pick(rank_t, e1), pick(gates_t, e0), pick(gates_t, e1),
                           jnp.zeros((LANES - 6, tm), F32)], axis=0)
    route_ref[...] = rec.T
    route_t_ref[...] = rec[:8]


def _mixer_out(oa, ob, ga, gb, x, lw, l, tm):
    t = x.shape[0]
    row = lambda n: pl.BlockSpec((tm, n), lambda i: (i, 0))
    full = lambda a: pl.BlockSpec((None,) + a.shape[1:], lambda i: (l,) + (0,) * (a.ndim - 1))
    const = lambda a: pl.BlockSpec(a.shape, lambda i: (0,) * a.ndim)
    col = lambda n: pl.BlockSpec((n, tm), lambda i: (0, i))
    tri = (jnp.arange(tm)[:, None] < jnp.arange(tm)[None, :]).astype(BF16)
    ins = [oa, ob, ga, gb, x, lw["wba"], lw["wbb"], lw["wout"], lw["gffn"], lw["wrh"], lw["wrl"], lw["br"], tri]
    in_specs = [col(A_Q_COLS), col(B_O_COLS), row(D_MODEL), row(D_MODEL), row(D_MODEL),
                full(lw["wba"]), full(lw["wbb"]), full(lw["wout"]), full(lw["gffn"]),
                const(lw["wrh"]), const(lw["wrl"]), const(lw["br"]), const(tri)]
    return pl.pallas_call(
        _mixer_out_body,
        grid=(t // tm,),
        in_specs=in_specs,
        out_specs=[row(D_MODEL), row(D_MODEL // 2), row(LANES), col(8),
                   pl.BlockSpec((N_EXPERTS, LANES), lambda i: (0, 0))],
        out_shape=[jax.ShapeDtypeStruct((t, D_MODEL), F32), jax.ShapeDtypeStruct((t, D_MODEL // 2), jnp.int32),
                   jax.ShapeDtypeStruct((t, LANES), F32), jax.ShapeDtypeStruct((8, t), F32),
                   jax.ShapeDtypeStruct((N_EXPERTS, LANES), F32)],
        compiler_params=_cparams(("arbitrary",)),
        name="mixer_out",
    )(*ins)


def _sc_plan(nrows):
    info = plsc.get_sparse_core_info()
    workers = info.num_cores * info.num_subcores
    per_worker = nrows // workers
    chunk = min(SC_CHUNK, per_worker)
    assert per_worker * workers == nrows and per_worker % chunk == 0 and chunk % 8 == 0, (nrows, workers, chunk)
    return info.num_cores, per_worker, chunk


def _sc_mesh():
    return plsc.VectorSubcoreMesh(core_axis_name="core", subcore_axis_name="subcore")


def _scatter_rows(x, idx0, idx1, nrows):
    t, d = x.shape
    ncores, per_worker, chunk = _sc_plan(t)

    @functools.partial(
        pl.kernel, out_type=jax.ShapeDtypeStruct((nrows, d), x.dtype), mesh=_sc_mesh(), name="moe_dispatch",
        scratch_types=[pltpu.VMEM((chunk,), jnp.int32), pltpu.VMEM((chunk,), jnp.int32), pltpu.VMEM((chunk, d), x.dtype)])
    def run(x_hbm, i0_hbm, i1_hbm, o_hbm, i0_v, i1_v, rows_v):
        worker = lax.axis_index("subcore") * ncores + lax.axis_index("core")

        @pl.loop(0, per_worker // chunk)
        def _(c):
            base = pl.multiple_of(worker * per_worker + c * chunk, chunk)
            pltpu.sync_copy(x_hbm.at[pl.ds(base, chunk)], rows_v)
            pltpu.sync_copy(i0_hbm.at[pl.ds(base, chunk)], i0_v)
            pltpu.sync_copy(i1_hbm.at[pl.ds(base, chunk)], i1_v)
            pltpu.sync_copy(rows_v, o_hbm.at[i0_v])
            pltpu.sync_copy(rows_v, o_hbm.at[i1_v])

    return run(x, idx0, idx1)


def _gather_rows(table, idx):
    m = idx.shape[0]
    d = table.shape[1]
    ncores, per_worker, chunk = _sc_plan(m)

    @functools.partial(
        pl.kernel, out_type=jax.ShapeDtypeStruct((m, d), table.dtype), mesh=_sc_mesh(), name="moe_collect",
        scratch_types=[pltpu.VMEM((chunk,), jnp.int32), pltpu.VMEM((chunk, d), table.dtype)])
    def run(x_hbm, i_hbm, o_hbm, i_v, rows_v):
        worker = lax.axis_index("subcore") * ncores + lax.axis_index("core")

        @pl.loop(0, per_worker // chunk)
        def _(c):
            base = pl.multiple_of(worker * per_worker + c * chunk, chunk)
            pltpu.sync_copy(i_hbm.at[pl.ds(base, chunk)], i_v)
            pltpu.sync_copy(x_hbm.at[i_v], rows_v)
            pltpu.sync_copy(rows_v, o_hbm.at[pl.ds(base, chunk)])

    return run(table, idx)


def _routing_tables(route_t, counts, tr):
    t = route_t.shape[1]
    experts = jnp.arange(N_EXPERTS, dtype=jnp.int32)
    cnt = counts[:, 0].astype(jnp.int32)
    seg_end = jnp.cumsum(cnt)
    seg_start = seg_end - cnt
    lookup = lambda table, idx: jnp.sum(jnp.where(idx[None, :] == experts[:, None], table[:, None], 0), axis=0)
    e = route_t[R_E0:R_E1 + 1].astype(jnp.int32).reshape(2 * t)
    rank = route_t[R_RANK0:R_RANK1 + 1].astype(jnp.int32).reshape(2 * t)
    pos = lookup(seg_start, e) + rank

    n_tiles = 2 * t // tr
    n_visits = n_tiles + N_EXPERTS - 1
    first_tile = seg_start // tr
    last_tile = jnp.maximum(seg_end - 1, 0) // tr
    visits = jnp.where(cnt > 0, last_tile - first_tile + 1, 0)
    visit_end = jnp.cumsum(visits)
    visit_start = visit_end - visits
    g = jnp.arange(n_visits, dtype=jnp.int32)
    valid = g < visit_end[-1]
    ex = jnp.minimum(jnp.sum((g[None, :] >= visit_end[:, None]).astype(jnp.int32), axis=0), N_EXPERTS - 1)
    tile = lookup(first_tile - visit_start, ex) + g
    lo = jnp.clip(lookup(seg_start, ex) - tile * tr, 0, tr)
    hi = jnp.clip(lookup(seg_end, ex) - tile * tr, 0, tr)
    last_ex = jnp.max(jnp.where(cnt > 0, experts, 0))
    tile = jnp.where(valid, tile, n_tiles - 1)
    ex = jnp.where(valid, ex, last_ex)
    lo = jnp.where(valid, lo, 0)
    hi = jnp.where(valid, hi, 0)
    first = jnp.concatenate([jnp.ones((1,), jnp.int32), (tile[1:] != tile[:-1]).astype(jnp.int32)])
    first = jnp.where(valid, first, 0)
    fresh = jnp.concatenate([jnp.ones((1,), jnp.int32), (ex[1:] != ex[:-1]).astype(jnp.int32)])
    return pos, (tile, ex, lo, hi, first, fresh)


def _experts_body(tile_ref, ex_ref, lo_ref, hi_ref, first_ref, fresh_ref, xs_ref, wg_ref, wu_ref, wd_ref, ys_ref,
                  acc_ref, wg_bf, wu_bf, wd_bf):
    g = pl.program_id(0)
    lo, hi = lo_ref[g], hi_ref[g]

    @pl.when(fresh_ref[g] == 1)
    def _():
        wg_bf[...] = wg_ref[...].astype(BF16)
        wu_bf[...] = wu_ref[...].astype(BF16)
        wd_bf[...] = wd_ref[...].astype(BF16)

    @pl.when(hi > lo)
    def _():
        x = _unpack_pairs(xs_ref[...]).astype(BF16)
        a = jax.nn.silu(_dot(x, wg_bf[...])) * _dot(x, wu_bf[...])
        row = lax.broadcasted_iota(jnp.int32, (a.shape[0], 1), 0)
        a = jnp.where((row >= lo) & (row < hi), a, 0.0)
        y = _dot(a.astype(BF16), wd_bf[...])

        @pl.when(first_ref[g] == 1)
        def _():
            acc_ref[...] = y

        @pl.when(first_ref[g] == 0)
        def _():
            acc_ref[...] += y

        ys_ref[...] = _pack_pairs(acc_ref[...])


def _experts(xs, visits, lw, l, tr):
    n_visits = visits[0].shape[0]
    rows = pl.BlockSpec((tr, D_MODEL // 2), lambda g, tile, *_: (tile[g], 0))
    wspec = lambda a: pl.BlockSpec((None, None) + a.shape[2:], lambda g, tile, ex, *_: (l, ex[g], 0, 0))
    return pl.pallas_call(
        _experts_body,
        grid_spec=pltpu.PrefetchScalarGridSpec(
            num_scalar_prefetch=len(visits),
            grid=(n_visits,),
            in_specs=[rows, wspec(lw["wg"]), wspec(lw["wu"]), wspec(lw["wd"])],
            out_specs=rows,
            scratch_shapes=[pltpu.VMEM((tr, D_MODEL), F32)] + [pltpu.VMEM(lw[k].shape[2:], BF16) for k in ("wg", "wu", "wd")],
        ),
        out_shape=jax.ShapeDtypeStruct(xs.shape, jnp.int32),
        compiler_params=_cparams(("arbitrary",)),
        name="experts",
    )(*visits, xs, lw["wg"], lw["wu"], lw["wd"])


def _moe_residual(h_ref, y0_ref, y1_ref, route_ref):
    r = route_ref[...]
    return (h_ref[...] + r[:, R_W0:R_W0 + 1] * _unpack_pairs(y0_ref[...])
            + r[:, R_W1:R_W1 + 1] * _unpack_pairs(y1_ref[...]))


def _final_body(h_ref, y0_ref, y1_ref, route_ref, gfin_ref, o_ref):
    o_ref[...] = _rms(_moe_residual(h_ref, y0_ref, y1_ref, route_ref)) * gfin_ref[...]


def _final(h, y, route, gfin, tm):
    t = h.shape[0]
    row = lambda n: pl.BlockSpec((tm, n), lambda i: (i, 0))
    second = pl.BlockSpec((tm, D_MODEL // 2), lambda i: (i + t // tm, 0))
    return pl.pallas_call(
        _final_body,
        grid=(t // tm,),
        in_specs=[row(D_MODEL), row(D_MODEL // 2), second, row(LANES), pl.BlockSpec(gfin.shape, lambda i: (0, 0))],
        out_specs=row(D_MODEL),
        out_shape=jax.ShapeDtypeStruct(h.shape, F32),
        compiler_params=_cparams(("parallel",)),
        name="final",
    )(h, y, y, route, gfin)


def _moe_rows(tp, route_t, counts, lw, l, tr):
    t = tp.shape[0]
    pos, visits = _routing_tables(route_t, counts, tr)
    xs = _scatter_rows(tp, pos[:t], pos[t:], 2 * t)
    ys = _experts(xs, visits, lw, l, tr)
    return _gather_rows(ys, pos)


def _rope_angles(seq_len, rot_dim):
    rows = seq_len // GRID_W
    row = jnp.broadcast_to(jnp.arange(rows)[:, None], (rows, GRID_W)).reshape(-1).astype(F32)
    col = jnp.broadcast_to(jnp.arange(GRID_W)[None, :], (rows, GRID_W)).reshape(-1).astype(F32)
    axis_dim = rot_dim // 2
    inv_freq = jnp.power(jnp.float32(ROPE_THETA), -jnp.arange(0, axis_dim, 2, dtype=F32) / axis_dim)
    ang = jnp.concatenate([row[:, None] * inv_freq[None, :], col[:, None] * inv_freq[None, :]], axis=-1)
    return jnp.cos(ang), jnp.sin(ang)


def _rope_tables(nseq, seq_len):
    c, s = _rope_angles(seq_len, A_HEAD_DIM)
    ca = jnp.concatenate([c, c, c, c], axis=-1)
    sa = jnp.concatenate([-s, s, -s, s], axis=-1)
    c, s = _rope_angles(seq_len, B_ROPE)
    ones = jnp.ones((seq_len, B_NOPE), F32)
    zeros = jnp.zeros((seq_len, B_NOPE), F32)
    tail = LANES - B_NOPE - B_ROPE
    cb = jnp.concatenate([ones, c, c, ones[:, :tail]], axis=-1)
    sb = jnp.concatenate([zeros, -s, s, zeros[:, :tail]], axis=-1)
    return tuple(jnp.tile(a, (nseq, 1)) for a in (ca, sa, cb, sb))


def _prepare_weights(norm_mix, w_in, a_q_norm, a_k_norm, b_q_norm, b_kv_norm, w_q_up, w_kv_up, w_branch_a,
                     w_branch_b, w_out, norm_ffn, w_router, b_router, w_gate, w_up, w_down):
    depth = w_in.shape[0]
    order = jnp.array(A_HEAD_ORDER)
    parts = []
    start = 0
    for n in (A_Q_COLS, A_KV_COLS, A_KV_COLS, B_Q_RANK, B_KV_RANK, B_ROPE, D_MODEL, D_MODEL):
        parts.append(w_in[..., start:start + n])
        start += n
    qa, ka, va, cq, ckv, kr, ga, gb = parts
    qa = qa.reshape(depth, D_MODEL, A_HEADS, A_HEAD_DIM)[:, :, order].reshape(depth, D_MODEL, A_Q_COLS)
    kr = jnp.pad(kr, ((0, 0), (0, 0), (B_NOPE, LANES - B_NOPE - B_ROPE)))
    win = jnp.concatenate([qa, ka, va, cq, ckv, kr, ga, gb], axis=-1).astype(BF16)

    wq = w_q_up.reshape(depth, B_Q_RANK, B_HEADS, B_QK_DIM)
    wq = jnp.pad(wq, ((0, 0), (0, 0), (0, 0), (0, LANES - B_QK_DIM))).reshape(depth, B_Q_RANK, B_PAD_COLS)
    wkv = w_kv_up.reshape(depth, B_KV_RANK, B_HEADS, B_NOPE + B_V)
    wk = jnp.pad(wkv[..., :B_NOPE], ((0, 0), (0, 0), (0, 0), (0, LANES - B_NOPE))).reshape(depth, B_KV_RANK, B_PAD_COLS)
    wv = wkv[..., B_NOPE:].reshape(depth, B_KV_RANK, B_O_COLS)
    wba = w_branch_a.reshape(depth, A_HEADS, A_HEAD_DIM, D_MODEL)[:, order].reshape(depth, A_Q_COLS, D_MODEL)

    group = jnp.arange(A_Q_COLS) // A_HEAD_DIM
    bd = (group[:, None] == group[None, :]).astype(BF16)
    wr_t = w_router.T
    wrh, wrl = _split_bf16(wr_t)
    vec = lambda a: a[:, None, :]
    return dict(
        gmix=vec(norm_mix), win=win,
        gq=vec(jnp.tile(a_q_norm, (1, A_HEADS)) * (A_HEAD_DIM ** -0.5 * LOG2E)), gk=vec(jnp.tile(a_k_norm, (1, A_KV_HEADS))),
        gbq=vec(b_q_norm), gbkv=vec(b_kv_norm),
        wq=wq.astype(BF16), wk=wk.astype(BF16), wv=wv.astype(BF16), bd=bd,
        wba=wba.astype(BF16), wbb=w_branch_b.astype(BF16), wout=w_out.astype(BF16), gffn=vec(norm_ffn),
        wrh=wrh, wrl=wrl, br=b_router[:, None].astype(F32),
        wg=w_gate, wu=w_up, wd=w_down,
    )


def _pick(n, candidates):
    for c in candidates:
        if n % c == 0:
            return c
    raise ValueError(f"no tile in {candidates} divides {n}")


def _trunk(x3, lw, norm_final):
    nseq, s, _ = x3.shape
    t = nseq * s
    x = x3.reshape(t, D_MODEL)
    tabs = _rope_tables(nseq, s)
    depth = lw["win"].shape[0]
    tm = _pick(t, (256,))
    tm_out = _pick(t, (512, 256))
    tq = _pick(s, (512, 256))
    tk = _pick(s, (512,))
    moe_out = None
    for l in range(depth):
        x_new, qa, ka, va, qb, kb, vb, ga, gb = _mixer_in(x, moe_out, lw, l, tabs, tm_out)
        x = x if x_new is None else x_new
        oa = _attention_a(qa, ka, va, nseq, s, tq, tk)
        ob = _attention_b(qb, kb, vb, nseq, s, tq, tk)
        h, tp, route, route_t, counts = _mixer_out(oa, ob, ga, gb, x, lw, l, _pick(t, (1024, 512, 256)))
        moe_out = (h, _moe_rows(tp, route_t, counts, lw, l, EXPERT_ROWS), route)
        x = None
    return _final(*moe_out, norm_final[None, :], tm).reshape(x3.shape)


def kernel(x_prompt, x_sample, norm_mix, w_in, a_q_norm, a_k_norm, b_q_norm, b_kv_norm, w_q_up, w_kv_up, w_branch_a,
           w_branch_b, w_out, norm_ffn, w_router, b_router, w_gate, w_up, w_down, norm_final):
    lw = _prepare_weights(norm_mix, w_in, a_q_norm, a_k_norm, b_q_norm, b_kv_norm, w_q_up, w_kv_up, w_branch_a,
                          w_branch_b, w_out, norm_ffn, w_router, b_router, w_gate, w_up, w_down)
    return _trunk(x_prompt, lw, norm_final), _trunk(x_sample, lw, norm_final)
```

```python
import functools

import jax
import jax.numpy as jnp
from jax import lax
from jax.experimental import pallas as pl
from jax.experimental.pallas import tpu as pltpu
from jax.experimental.pallas import tpu_sc as plsc

F32 = jnp.float32
BF16 = jnp.bfloat16

D_MODEL = 1024
GRID_W = 64
ROPE_THETA = 10000.0
EPS = 1e-6
A_HEADS = 8
A_KV_HEADS = 2
A_HEAD_DIM = 64
B_HEADS = 8
B_Q_RANK = 384
B_KV_RANK = 256
B_NOPE = 64
B_ROPE = 32
B_V = 64
N_EXPERTS = 16
N_GROUPS = 4
EXPERTS_PER_GROUP = N_EXPERTS // N_GROUPS
D_EXPERT = 512
A_Q_COLS = A_HEADS * A_HEAD_DIM
A_KV_COLS = A_KV_HEADS * A_HEAD_DIM
B_QK_DIM = B_NOPE + B_ROPE
B_O_COLS = B_HEADS * B_V

LANES = 128
LOG2E = 1.4426950408889634

C_QA = 0
C_KA = C_QA + A_Q_COLS
C_VA = C_KA + A_KV_COLS
C_CQ = C_VA + A_KV_COLS
C_CKV = C_CQ + B_Q_RANK
C_KR = C_CKV + B_KV_RANK
C_GA = C_KR + LANES
C_GB = C_GA + D_MODEL
C_END = C_GB + D_MODEL
B_PAD_COLS = B_HEADS * LANES

A_HEAD_ORDER = (0, 4, 1, 5, 2, 6, 3, 7)

VMEM_LIMIT = 56 * 1024 * 1024

ATTN_GROUP = 16
ATTN_S_BUFS = 3
ATTN_P_BUFS = 2
ATTN_VALUE_LAG = 4
ATTN_SLAB = 32
ATTN_FIXED_SHIFT_MAX = 48.0
EXPERT_ROWS = 512
SC_CHUNK = 128


def _cparams(sem):
    return pltpu.CompilerParams(dimension_semantics=sem, vmem_limit_bytes=VMEM_LIMIT)


def _dot(a, b):
    return jnp.dot(a, b, preferred_element_type=F32)


def _dot_nt(a, b):
    return lax.dot_general(a, b, (((1,), (1,)), ((), ())), preferred_element_type=F32)


def _dot_tn(a, b):
    return lax.dot_general(a, b, (((0,), (0,)), ((), ())), preferred_element_type=F32)


def _rms(x):
    return x * lax.rsqrt(jnp.mean(x * x, axis=-1, keepdims=True) + EPS)


def _split_bf16(x):
    hi = x.astype(BF16)
    lo = (x - hi.astype(F32)).astype(BF16)
    return hi, lo


def _group_mean_sq(v, bd):
    return _dot((v * v).astype(BF16), bd) * (1.0 / A_HEAD_DIM)


def _rope_a(v, cos, sin):
    n = v.shape[-1]
    lane = lax.broadcasted_iota(jnp.int32, v.shape, 1)
    low = (lane % A_HEAD_DIM) < (A_HEAD_DIM // 2)
    swapped = jnp.where(low, pltpu.roll(v, n - A_HEAD_DIM // 2, 1), pltpu.roll(v, A_HEAD_DIM // 2, 1))
    return v * cos + swapped * sin


def _rope_b(v, cos, sin):
    n = v.shape[-1]
    lane = lax.broadcasted_iota(jnp.int32, v.shape, 1)
    low = (lane % LANES) < (B_NOPE + B_ROPE // 2)
    swapped = jnp.where(low, pltpu.roll(v, n - B_ROPE // 2, 1), pltpu.roll(v, B_ROPE // 2, 1))
    return v * cos + swapped * sin


def _mixer_in_body(*refs, fused):
    if fused:
        h_ref, y0_ref, y1_ref, route_ref, *refs = refs
    else:
        x_ref, *refs = refs
    (gmix_ref, win_ref, ca_ref, sa_ref, cb_ref, sb_ref, gq_ref, gk_ref, gbq_ref, gbkv_ref, wq_ref, wk_ref, wv_ref,
     bd_ref, *outs) = refs
    if fused:
        x_out_ref, *outs = outs
        x = _moe_residual(h_ref, y0_ref, y1_ref, route_ref)
        x_out_ref[...] = x
    else:
        x = x_ref[...]
    qat_ref, ka_ref, vat_ref, qbt_ref, kb_ref, vbt_ref, ga_ref, gb_ref = outs
    hb = (_rms(x) * gmix_ref[...]).astype(BF16)
    u = _dot(hb, win_ref[:, C_QA:C_GA])
    bd = bd_ref[...]
    ca, sa = ca_ref[...], sa_ref[...]
    cb, sb = cb_ref[...], sb_ref[...]

    qa = u[:, C_QA:C_KA]
    qa = qa * lax.rsqrt(_group_mean_sq(qa, bd) + EPS) * gq_ref[...]
    qa = _rope_a(qa, jnp.concatenate([ca] * (A_Q_COLS // LANES), axis=1),
                 jnp.concatenate([sa] * (A_Q_COLS // LANES), axis=1))
    qat_ref[...] = qa.T.astype(BF16)
    ka = u[:, C_KA:C_VA]
    ka = ka * lax.rsqrt(_group_mean_sq(ka, bd[:A_KV_COLS, :A_KV_COLS]) + EPS) * gk_ref[...]
    ka_ref[...] = _rope_a(ka, ca, sa).astype(BF16)
    vat_ref[...] = u[:, C_VA:C_CQ].T.astype(BF16)

    cq = (_rms(u[:, C_CQ:C_CKV]) * gbq_ref[...]).astype(BF16)
    qb = _dot(cq, wq_ref[...])
    qb = _rope_b(qb, jnp.concatenate([cb] * B_HEADS, axis=1), jnp.concatenate([sb] * B_HEADS, axis=1))
    qbt_ref[...] = (qb * (B_QK_DIM ** -0.5 * LOG2E)).T.astype(BF16)
    ckv = (_rms(u[:, C_CKV:C_KR]) * gbkv_ref[...]).astype(BF16)
    kr = _rope_b(u[:, C_KR:C_GA], cb, sb)
    kb_ref[...] = (_dot(ckv, wk_ref[...]) + jnp.concatenate([kr] * B_HEADS, axis=1)).astype(BF16)
    vbt_ref[...] = _dot(ckv, wv_ref[...]).T.astype(BF16)

    g = jax.nn.sigmoid(_dot(hb, win_ref[:, C_GA:C_END]))
    ga_ref[...] = g[:, :D_MODEL].astype(BF16)
    gb_ref[...] = g[:, D_MODEL:].astype(BF16)


def _mixer_in(x, moe_out, lw, l, tabs, tm):
    fused = x is None
    t = moe_out[0].shape[0] if fused else x.shape[0]
    row = lambda n: pl.BlockSpec((tm, n), lambda i: (i, 0))
    full = lambda a: pl.BlockSpec((None,) + a.shape[1:], lambda i: (l,) + (0,) * (a.ndim - 1))
    const = lambda a: pl.BlockSpec(a.shape, lambda i: (0,) * a.ndim)
    col = lambda n: pl.BlockSpec((n, tm), lambda i: (0, i))
    ca, sa, cb, sb = tabs
    if fused:
        h, y, route = moe_out
        second = pl.BlockSpec((tm, D_MODEL // 2), lambda i: (i + t // tm, 0))
        ins, in_specs = [h, y, y, route], [row(D_MODEL), row(D_MODEL // 2), second, row(LANES)]
    else:
        ins, in_specs = [x], [row(D_MODEL)]
    ins += [lw["gmix"], lw["win"], ca, sa, cb, sb, lw["gq"], lw["gk"], lw["gbq"], lw["gbkv"],
            lw["wq"], lw["wk"], lw["wv"], lw["bd"]]
    in_specs += [full(lw["gmix"]), full(lw["win"]), row(LANES), row(LANES), row(LANES), row(LANES),
                 full(lw["gq"]), full(lw["gk"]), full(lw["gbq"]), full(lw["gbkv"]),
                 full(lw["wq"]), full(lw["wk"]), full(lw["wv"]), const(lw["bd"])]
    widths = [A_Q_COLS, A_KV_COLS, A_KV_COLS, B_PAD_COLS, B_PAD_COLS, B_O_COLS, D_MODEL, D_MODEL]
    transposed = [True, False, True, True, False, True, False, False]
    out_specs = [col(n) if tr else row(n) for n, tr in zip(widths, transposed)]
    out_shape = [jax.ShapeDtypeStruct((n, t) if tr else (t, n), BF16) for n, tr in zip(widths, transposed)]
    if fused:
        out_specs.insert(0, row(D_MODEL))
        out_shape.insert(0, jax.ShapeDtypeStruct((t, D_MODEL), F32))
    outs = pl.pallas_call(
        functools.partial(_mixer_in_body, fused=fused),
        grid=(t // tm,),
        in_specs=in_specs,
        out_specs=out_specs,
        out_shape=out_shape,
        compiler_params=_cparams(("parallel",)),
        name="mixer_in",
    )(*ins)
    return tuple(outs) if fused else (None,) + tuple(outs)


def _flash_pair(q0t, q1t, k0_ref, k1_ref, vt_ref, o_ref, s_refs, p_refs, acc_ref, kmax2, fixed, tk):
    tq = q0t.shape[1]
    nk = vt_ref.shape[1] // tk
    group = min(nk, ATTN_GROUP)
    ns = len(s_refs)
    qt2 = jnp.concatenate([q0t, q1t], axis=1)

    def score_values(j):
        off = pl.multiple_of(j * tk, tk)
        if k1_ref is None:
            return _dot(k0_ref[pl.ds(off, tk), :], qt2)
        return jnp.concatenate([_dot(k0_ref[pl.ds(off, tk), :], q0t), _dot(k1_ref[pl.ds(off, tk), :], q1t)], axis=1)

    def add_values(j, p, alpha):
        off = pl.multiple_of(j * tk, tk)
        for rows, lanes in ((slice(0, B_V), slice(0, tq)), (slice(B_V, LANES), slice(tq, 2 * tq))):
            prev = acc_ref[rows, :] if alpha is None else alpha[:, lanes] * acc_ref[rows, :]
            acc_ref[rows, :] = prev + _dot(vt_ref[rows, pl.ds(off, tk)], p[:, lanes])

    def finish(l):
        l = jnp.sum(l, axis=0, keepdims=True)
        o_ref[:B_V, :] = (acc_ref[:B_V, :] / l[:, :tq]).astype(o_ref.dtype)
        o_ref[B_V:, :] = (acc_ref[B_V:, :] / l[:, tq:]).astype(o_ref.dtype)

    def run_groups(one_group, carry):
        acc_ref[...] = jnp.zeros_like(acc_ref)
        return one_group(0, carry) if nk == group else lax.fori_loop(0, nk // group, one_group, carry)

    def fixed_shift(shift):
        wide = 2 * LANES
        tiles = [(n * wide, (n * wide) // tq) for n in range(2 * tq // wide)]

        def value_tile(j, p, lane0, head):
            off = pl.multiple_of(j * tk, tk)
            rows = slice(head * B_V, (head + 1) * B_V)
            lanes = slice(lane0 - head * tq, lane0 - head * tq + wide)
            acc_ref[rows, lanes] += _dot(vt_ref[rows, pl.ds(off, tk)], p)

        def one_group(g, l):
            l = list(l)
            pending = []
            for c in range(group):
                j = g * group + c
                off = pl.multiple_of(j * tk, tk)
                for n, (lane0, head) in enumerate(tiles):
                    k_ref = k0_ref if (k1_ref is None or head == 0) else k1_ref
                    s = _dot(k_ref[pl.ds(off, tk), :], qt2[:, lane0:lane0 + wide])
                    if len(pending) >= ATTN_VALUE_LAG:
                        value_tile(*pending.pop(0))
                    p = jnp.exp2(s - shift[:, lane0:lane0 + wide])
                    l[n] = l[n] + p.reshape(tk // 8, 8, wide).sum(axis=0)
                    pending.append((j, p.astype(BF16), lane0, head))
            for item in pending:
                value_tile(*item)
            return tuple(l)

        l = run_groups(one_group, tuple(jnp.zeros((8, wide), F32) for _ in tiles))
        finish(jnp.concatenate(l, axis=1))

    def softmax(s_ref, p_ref, m, l):
        slabs = [pl.ds(r, ATTN_SLAB) for r in range(0, tk, ATTN_SLAB)]
        fold = lambda x: x.reshape(ATTN_SLAB // 8, 8, 2 * tq)
        mx = fold(s_ref[slabs[0], :]).max(axis=0)
        for sl in slabs[1:]:
            mx = jnp.maximum(mx, fold(s_ref[sl, :]).max(axis=0))
        m_new = jnp.maximum(m, jnp.max(mx, axis=0, keepdims=True))
        alpha = jnp.exp2(m - m_new)
        l = alpha * l
        for sl in slabs:
            p = jnp.exp2(s_ref[sl, :] - m_new)
            p_ref[sl, :] = p.astype(BF16)
            l = l + fold(p).sum(axis=0)
        return m_new, l, alpha

    def running_max():
        def one_group(g, carry):
            m, l = carry
            base = g * group
            for c in range(min(ns - 1, group)):
                s_refs[c % ns][...] = score_values(base + c)
            for c in range(group):
                if c + ns - 1 < group:
                    s_refs[(c + ns - 1) % ns][...] = score_values(base + c + ns - 1)
                p_ref = p_refs[c % len(p_refs)]
                m, l, alpha = softmax(s_refs[c % ns], p_ref, m, l)
                add_values(base + c, p_ref[...], alpha)
            return m, l

        carry = (jnp.full((1, 2 * tq), -jnp.inf, F32), jnp.zeros((8, 2 * tq), F32))
        finish(run_groups(one_group, carry)[1])

    if fixed:
        lane = lax.broadcasted_iota(jnp.int32, (1, 2 * tq), 1)
        qsq = jnp.sum(jnp.square(qt2.astype(F32)), axis=0, keepdims=True)
        fixed_shift(jnp.sqrt(qsq * jnp.where(lane < tq, kmax2[0], kmax2[1])))
    else:
        running_max()


def _max_sq_norm(k_ref, ones_ref, lanes):
    k = k_ref[...].astype(F32)
    n = _dot((k * k).astype(BF16), ones_ref[...])
    return jnp.max(n[:, lanes])


def _max_query_sq_norm(q):
    q = q.astype(F32)
    return jnp.max(jnp.sum(q * q, axis=0, keepdims=True))


def _attention_tiles(q_pair, k0_ref, k1_ref, vt_ref, o_ref, scratch, qmax2, kmax2, tq, tk):
    s_refs, p_refs, acc_ref = scratch[:ATTN_S_BUFS], scratch[ATTN_S_BUFS:-1], scratch[-1]
    bound2 = jnp.maximum(qmax2[0] * kmax2[0], qmax2[1] * kmax2[1])
    small = bound2 <= ATTN_FIXED_SHIFT_MAX * ATTN_FIXED_SHIFT_MAX

    def walk(fixed):
        @pl.loop(0, o_ref.shape[1] // tq)
        def _(j):
            lanes = pl.ds(pl.multiple_of(j * tq, tq), tq)
            q0t, q1t = q_pair(lanes)
            _flash_pair(q0t, q1t, k0_ref, k1_ref, vt_ref, o_ref.at[:, lanes], s_refs, p_refs, acc_ref, kmax2, fixed, tk)

    pl.when(small)(lambda: walk(True))
    pl.when(jnp.logical_not(small))(lambda: walk(False))


def _attn_a_body(qt_ref, k_ref, vt_ref, ones_ref, o_ref, *scratch, tq, tk):
    kmax2 = (_max_sq_norm(k_ref, ones_ref, slice(0, A_HEAD_DIM)), _max_sq_norm(k_ref, ones_ref, slice(A_HEAD_DIM, LANES)))
    qmax2 = (_max_query_sq_norm(qt_ref[:A_HEAD_DIM, :]), _max_query_sq_norm(qt_ref[A_HEAD_DIM:, :]))

    def q_pair(lanes):
        qt = qt_ref[:, lanes]
        zero = jnp.zeros((A_HEAD_DIM, tq), BF16)
        return jnp.concatenate([qt[:A_HEAD_DIM], zero], axis=0), jnp.concatenate([zero, qt[A_HEAD_DIM:]], axis=0)

    _attention_tiles(q_pair, k_ref, None, vt_ref, o_ref, scratch, qmax2, kmax2, tq, tk)


def _attn_b_body(q0t_ref, q1t_ref, k0_ref, k1_ref, vt_ref, ones_ref, o_ref, *scratch, tq, tk):
    kmax2 = (_max_sq_norm(k0_ref, ones_ref, slice(0, LANES)), _max_sq_norm(k1_ref, ones_ref, slice(0, LANES)))
    qmax2 = (_max_query_sq_norm(q0t_ref[...]), _max_query_sq_norm(q1t_ref[...]))
    q_pair = lambda lanes: (q0t_ref[:, lanes], q1t_ref[:, lanes])
    _attention_tiles(q_pair, k0_ref, k1_ref, vt_ref, o_ref, scratch, qmax2, kmax2, tq, tk)


def _attn_scratch(tq, tk):
    return ([pltpu.VMEM((tk, 2 * tq), F32)] * ATTN_S_BUFS + [pltpu.VMEM((tk, 2 * tq), BF16)] * ATTN_P_BUFS
            + [pltpu.VMEM((LANES, tq), F32)])


def _head_ones(head_dim):
    group = jnp.arange(LANES) // head_dim
    return (group[:, None] == group[None, :]).astype(BF16)


def _attention_a(qat, ka, vat, nseq, s, tq, tk):
    qspec = pl.BlockSpec((LANES, s), lambda b, h: (h, b))
    kspec = pl.BlockSpec((s, LANES), lambda b, h: (b, 0))
    vspec = pl.BlockSpec((LANES, s), lambda b, h: (0, b))
    ones = pl.BlockSpec((LANES, LANES), lambda b, h: (0, 0))
    return pl.pallas_call(
        functools.partial(_attn_a_body, tq=tq, tk=tk),
        grid=(nseq, A_Q_COLS // LANES),
        in_specs=[qspec, kspec, vspec, ones],
        out_specs=qspec,
        out_shape=jax.ShapeDtypeStruct(qat.shape, BF16),
        scratch_shapes=_attn_scratch(tq, tk),
        compiler_params=_cparams(("parallel", "parallel")),
        name="attn_a",
    )(qat, ka, vat, _head_ones(A_HEAD_DIM))


def _attention_b(qbt, kb, vbt, nseq, s, tq, tk):
    q0 = pl.BlockSpec((LANES, s), lambda b, h: (2 * h, b))
    q1 = pl.BlockSpec((LANES, s), lambda b, h: (2 * h + 1, b))
    k0 = pl.BlockSpec((s, LANES), lambda b, h: (b, 2 * h))
    k1 = pl.BlockSpec((s, LANES), lambda b, h: (b, 2 * h + 1))
    v = pl.BlockSpec((LANES, s), lambda b, h: (h, b))
    o = pl.BlockSpec((LANES, s), lambda b, h: (h, b))
    ones = pl.BlockSpec((LANES, LANES), lambda b, h: (0, 0))
    return pl.pallas_call(
        functools.partial(_attn_b_body, tq=tq, tk=tk),
        grid=(nseq, B_O_COLS // LANES),
        in_specs=[q0, q1, k0, k1, v, ones],
        out_specs=o,
        out_shape=jax.ShapeDtypeStruct(vbt.shape, BF16),
        scratch_shapes=_attn_scratch(tq, tk),
        compiler_params=_cparams(("parallel", "parallel")),
        name="attn_b",
    )(qbt, qbt, kb, kb, vbt, _head_ones(LANES))


def _within(x, d, period, n):
    row = lax.broadcasted_iota(jnp.int32, x.shape, 0)
    return jnp.where((row % period) + d < period, pltpu.roll(x, n - d, 0), pltpu.roll(x, period - d, 0))


def _route(logits_t, bias):
    n = N_EXPERTS
    scores = jax.nn.sigmoid(logits_t)
    biased = scores + bias
    row = lax.broadcasted_iota(jnp.int32, biased.shape, 0)
    pos = row % EXPERTS_PER_GROUP
    rank = jnp.zeros(biased.shape, jnp.int32)
    for d in range(1, EXPERTS_PER_GROUP):
        other = _within(biased, d, EXPERTS_PER_GROUP, n)
        other_pos = (pos + d) % EXPERTS_PER_GROUP
        ahead = (other > biased) | ((other == biased) & (other_pos < pos))
        rank = rank + ahead.astype(jnp.int32)
    top2 = rank < 2
    kept = jnp.where(top2, biased, 0.0)
    gscore = kept
    for d in range(1, EXPERTS_PER_GROUP):
        gscore = gscore + _within(kept, d, EXPERTS_PER_GROUP, n)
    grp = row // EXPERTS_PER_GROUP
    win = jnp.ones(biased.shape, jnp.bool_)
    for d in range(1, N_GROUPS):
        other = pltpu.roll(gscore, n - d * EXPERTS_PER_GROUP, 0)
        other_grp = (grp + d) % N_GROUPS
        win = win & ((other < gscore) | ((other == gscore) & (other_grp > grp)))
    sel = top2 & win
    w = jnp.where(sel, scores, 0.0)
    return w / jnp.sum(w, axis=0, keepdims=True), sel


def _pack_pairs(v):
    n = v.shape[1] // 2
    vb = v.astype(BF16).astype(F32)
    hi = pltpu.bitcast(vb[:, :n], jnp.int32)
    lo = pltpu.bitcast(vb[:, n:], jnp.int32)
    return hi | lax.shift_right_logical(lo, 16)


def _unpack_pairs(w):
    hi = pltpu.bitcast(w & jnp.int32(-65536), F32)
    lo = pltpu.bitcast(lax.shift_left(w, 16), F32)
    return jnp.concatenate([hi, lo], axis=1)


R_E0, R_E1, R_RANK0, R_RANK1, R_W0, R_W1 = range(6)


def _mixer_out_body(oa_ref, ob_ref, ga_ref, gb_ref, x_ref, wba_ref, wbb_ref, wout_ref, gffn_ref, wrh_ref, wrl_ref,
                    br_ref, tri_ref, h_ref, tp_ref, route_ref, route_t_ref, count_ref):
    @pl.when(pl.program_id(0) == 0)
    def _():
        count_ref[...] = jnp.zeros_like(count_ref)

    ma = _dot_tn(oa_ref[...], wba_ref[...])
    mb = _dot_tn(ob_ref[...], wbb_ref[...])
    merged = ga_ref[...].astype(F32) * ma + gb_ref[...].astype(F32) * mb
    h = x_ref[...] + _dot(merged.astype(BF16), wout_ref[...])
    h_ref[...] = h
    t = _rms(h) * gffn_ref[...]
    t_hi, t_lo = _split_bf16(t)
    tp_ref[...] = _pack_pairs(t)
    wrh, wrl = wrh_ref[...], wrl_ref[...]
    logits_t = _dot_nt(wrh, t_hi) + _dot_nt(wrh, t_lo) + _dot_nt(wrl, t_hi)
    gates_t, sel = _route(logits_t, br_ref[...])

    tm = gates_t.shape[1]
    onehot = jnp.where(sel, 1.0, 0.0)
    before = _dot(onehot.astype(BF16), tri_ref[...])
    rank_t = count_ref[:, 0:1] + before
    count_ref[...] = count_ref[...] + jnp.sum(onehot, axis=1, keepdims=True)

    row = lax.broadcasted_iota(jnp.int32, sel.shape, 0).astype(F32)
    e0 = jnp.min(jnp.where(sel, row, float(N_EXPERTS)), axis=0, keepdims=True)
    e1 = jnp.max(jnp.where(sel, row, -1.0), axis=0, keepdims=True)
    pick = lambda v, e: jnp.sum(jnp.where(sel & (row == e), v, 0.0), axis=0, keepdims=True)
    rec = jnp.concatenate([e0, e1, pick(rank_t, e0), pick(rank_t, e1), pick(gates_t, e0), pick(gates_t, e1),
                           jnp.zeros((LANES - 6, tm), F32)], axis=0)
    route_ref[...] = rec.T
    route_t_ref[...] = rec[:8]


def _mixer_out(oa, ob, ga, gb, x, lw, l, tm):
    t = x.shape[0]
    row = lambda n: pl.BlockSpec((tm, n), lambda i: (i, 0))
    full = lambda a: pl.BlockSpec((None,) + a.shape[1:], lambda i: (l,) + (0,) * (a.ndim - 1))
    const = lambda a: pl.BlockSpec(a.shape, lambda i: (0,) * a.ndim)
    col = lambda n: pl.BlockSpec((n, tm), lambda i: (0, i))
    tri = (jnp.arange(tm)[:, None] < jnp.arange(tm)[None, :]).astype(BF16)
    ins = [oa, ob, ga, gb, x, lw["wba"], lw["wbb"], lw["wout"], lw["gffn"], lw["wrh"], lw["wrl"], lw["br"], tri]
    in_specs = [col(A_Q_COLS), col(B_O_COLS), row(D_MODEL), row(D_MODEL), row(D_MODEL),
                full(lw["wba"]), full(lw["wbb"]), full(lw["wout"]), full(lw["gffn"]),
                const(lw["wrh"]), const(lw["wrl"]), const(lw["br"]), const(tri)]
    return pl.pallas_call(
        _mixer_out_body,
        grid=(t // tm,),
        in_specs=in_specs,
        out_specs=[row(D_MODEL), row(D_MODEL // 2), row(LANES), col(8),
                   pl.BlockSpec((N_EXPERTS, LANES), lambda i: (0, 0))],
        out_shape=[jax.ShapeDtypeStruct((t, D_MODEL), F32), jax.ShapeDtypeStruct((t, D_MODEL // 2), jnp.int32),
                   jax.ShapeDtypeStruct((t, LANES), F32), jax.ShapeDtypeStruct((8, t), F32),
                   jax.ShapeDtypeStruct((N_EXPERTS, LANES), F32)],
        compiler_params=_cparams(("arbitrary",)),
        name="mixer_out",
    )(*ins)


def _sc_plan(nrows):
    info = plsc.get_sparse_core_info()
    workers = info.num_cores * info.num_subcores
    per_worker = nrows // workers
    chunk = min(SC_CHUNK, per_worker)
    assert per_worker * workers == nrows and per_worker % chunk == 0 and chunk % 8 == 0, (nrows, workers, chunk)
    return info.num_cores, per_worker, chunk


def _sc_mesh():
    return plsc.VectorSubcoreMesh(core_axis_name="core", subcore_axis_name="subcore")


def _scatter_rows(x, idx0, idx1, nrows):
    t, d = x.shape
    ncores, per_worker, chunk = _sc_plan(t)

    @functools.partial(
        pl.kernel, out_type=jax.ShapeDtypeStruct((nrows, d), x.dtype), mesh=_sc_mesh(), name="moe_dispatch",
        scratch_types=[pltpu.VMEM((chunk,), jnp.int32), pltpu.VMEM((chunk,), jnp.int32), pltpu.VMEM((chunk, d), x.dtype)])
    def run(x_hbm, i0_hbm, i1_hbm, o_hbm, i0_v, i1_v, rows_v):
        worker = lax.axis_index("subcore") * ncores + lax.axis_index("core")

        @pl.loop(0, per_worker // chunk)
        def _(c):
            base = pl.multiple_of(worker * per_worker + c * chunk, chunk)
            pltpu.sync_copy(x_hbm.at[pl.ds(base, chunk)], rows_v)
            pltpu.sync_copy(i0_hbm.at[pl.ds(base, chunk)], i0_v)
            pltpu.sync_copy(i1_hbm.at[pl.ds(base, chunk)], i1_v)
            pltpu.sync_copy(rows_v, o_hbm.at[i0_v])
            pltpu.sync_copy(rows_v, o_hbm.at[i1_v])

    return run(x, idx0, idx1)


def _gather_rows(table, idx):
    m = idx.shape[0]
    d = table.shape[1]
    ncores, per_worker, chunk = _sc_plan(m)

    @functools.partial(
        pl.kernel, out_type=jax.ShapeDtypeStruct((m, d), table.dtype), mesh=_sc_mesh(), name="moe_collect",
        scratch_types=[pltpu.VMEM((chunk,), jnp.int32), pltpu.VMEM((chunk, d), table.dtype)])
    def run(x_hbm, i_hbm, o_hbm, i_v, rows_v):
        worker = lax.axis_index("subcore") * ncores + lax.axis_index("core")

        @pl.loop(0, per_worker // chunk)
        def _(c):
            base = pl.multiple_of(worker * per_worker + c * chunk, chunk)
            pltpu.sync_copy(i_hbm.at[pl.ds(base, chunk)], i_v)
            pltpu.sync_copy(x_hbm.at[i_v], rows_v)
            pltpu.sync_copy(rows_v, o_hbm.at[pl.ds(base, chunk)])

    return run(table, idx)


def _routing_tables(route_t, counts, tr):
    t = route_t.shape[1]
    experts = jnp.arange(N_EXPERTS, dtype=jnp.int32)
    cnt = counts[:, 0].astype(jnp.int32)
    seg_end = jnp.cumsum(cnt)
    seg_start = seg_end - cnt
    lookup = lambda table, idx: jnp.sum(jnp.where(idx[None, :] == experts[:, None], table[:, None], 0), axis=0)
    e = route_t[R_E0:R_E1 + 1].astype(jnp.int32).reshape(2 * t)
    rank = route_t[R_RANK0:R_RANK1 + 1].astype(jnp.int32).reshape(2 * t)
    pos = lookup(seg_start, e) + rank

    n_tiles = 2 * t // tr
    n_visits = n_tiles + N_EXPERTS - 1
    first_tile = seg_start // tr
    last_tile = jnp.maximum(seg_end - 1, 0) // tr
    visits = jnp.where(cnt > 0, last_tile - first_tile + 1, 0)
    visit_end = jnp.cumsum(visits)
    visit_start = visit_end - visits
    g = jnp.arange(n_visits, dtype=jnp.int32)
    valid = g < visit_end[-1]
    ex = jnp.minimum(jnp.sum((g[None, :] >= visit_end[:, None]).astype(jnp.int32), axis=0), N_EXPERTS - 1)
    tile = lookup(first_tile - visit_start, ex) + g
    lo = jnp.clip(lookup(seg_start, ex) - tile * tr, 0, tr)
    hi = jnp.clip(lookup(seg_end, ex) - tile * tr, 0, tr)
    last_ex = jnp.max(jnp.where(cnt > 0, experts, 0))
    tile = jnp.where(valid, tile, n_tiles - 1)
    ex = jnp.where(valid, ex, last_ex)
    lo = jnp.where(valid, lo, 0)
    hi = jnp.where(valid, hi, 0)
    first = jnp.concatenate([jnp.ones((1,), jnp.int32), (tile[1:] != tile[:-1]).astype(jnp.int32)])
    first = jnp.where(valid, first, 0)
    fresh = jnp.concatenate([jnp.ones((1,), jnp.int32), (ex[1:] != ex[:-1]).astype(jnp.int32)])
    return pos, (tile, ex, lo, hi, first, fresh)


def _experts_body(tile_ref, ex_ref, lo_ref, hi_ref, first_ref, fresh_ref, xs_ref, wg_ref, wu_ref, wd_ref, ys_ref,
                  acc_ref, wg_bf, wu_bf, wd_bf):
    g = pl.program_id(0)
    lo, hi = lo_ref[g], hi_ref[g]

    @pl.when(fresh_ref[g] == 1)
    def _():
        wg_bf[...] = wg_ref[...].astype(BF16)
        wu_bf[...] = wu_ref[...].astype(BF16)
        wd_bf[...] = wd_ref[...].astype(BF16)

    @pl.when(hi > lo)
    def _():
        x = _unpack_pairs(xs_ref[...]).astype(BF16)
        a = jax.nn.silu(_dot(x, wg_bf[...])) * _dot(x, wu_bf[...])
        row = lax.broadcasted_iota(jnp.int32, (a.shape[0], 1), 0)
        a = jnp.where((row >= lo) & (row < hi), a, 0.0)
        y = _dot(a.astype(BF16), wd_bf[...])

        @pl.when(first_ref[g] == 1)
        def _():
            acc_ref[...] = y

        @pl.when(first_ref[g] == 0)
        def _():
            acc_ref[...] += y

        ys_ref[...] = _pack_pairs(acc_ref[...])


def _experts(xs, visits, lw, l, tr):
    n_visits = visits[0].shape[0]
    rows = pl.BlockSpec((tr, D_MODEL // 2), lambda g, tile, *_: (tile[g], 0))
    wspec = lambda a: pl.BlockSpec((None, None) + a.shape[2:], lambda g, tile, ex, *_: (l, ex[g], 0, 0))
    return pl.pallas_call(
        _experts_body,
        grid_spec=pltpu.PrefetchScalarGridSpec(
            num_scalar_prefetch=len(visits),
            grid=(n_visits,),
            in_specs=[rows, wspec(lw["wg"]), wspec(lw["wu"]), wspec(lw["wd"])],
            out_specs=rows,
            scratch_shapes=[pltpu.VMEM((tr, D_MODEL), F32)] + [pltpu.VMEM(lw[k].shape[2:], BF16) for k in ("wg", "wu", "wd")],
        ),
        out_shape=jax.ShapeDtypeStruct(xs.shape, jnp.int32),
        compiler_params=_cparams(("arbitrary",)),
        name="experts",
    )(*visits, xs, lw["wg"], lw["wu"], lw["wd"])


def _moe_residual(h_ref, y0_ref, y1_ref, route_ref):
    r = route_ref[...]
    return (h_ref[...] + r[:, R_W0:R_W0 + 1] * _unpack_pairs(y0_ref[...])
            + r[:, R_W1:R_W1 + 1] * _unpack_pairs(y1_ref[...]))


def _final_body(h_ref, y0_ref, y1_ref, route_ref, gfin_ref, o_ref):
    o_ref[...] = _rms(_moe_residual(h_ref, y0_ref, y1_ref, route_ref)) * gfin_ref[...]


def _final(h, y, route, gfin, tm):
    t = h.shape[0]
    row = lambda n: pl.BlockSpec((tm, n), lambda i: (i, 0))
    second = pl.BlockSpec((tm, D_MODEL // 2), lambda i: (i + t // tm, 0))
    return pl.pallas_call(
        _final_body,
        grid=(t // tm,),
        in_specs=[row(D_MODEL), row(D_MODEL // 2), second, row(LANES), pl.BlockSpec(gfin.shape, lambda i: (0, 0))],
        out_specs=row(D_MODEL),
        out_shape=jax.ShapeDtypeStruct(h.shape, F32),
        compiler_params=_cparams(("parallel",)),
        name="final",
    )(h, y, y, route, gfin)


def _moe_rows(tp, route_t, counts, lw, l, tr):
    t = tp.shape[0]
    pos, visits = _routing_tables(route_t, counts, tr)
    xs = _scatter_rows(tp, pos[:t], pos[t:], 2 * t)
    ys = _experts(xs, visits, lw, l, tr)
    return _gather_rows(ys, pos)


def _rope_angles(seq_len, rot_dim):
    rows = seq_len // GRID_W
    row = jnp.broadcast_to(jnp.arange(rows)[:, None], (rows, GRID_W)).reshape(-1).astype(F32)
    col = jnp.broadcast_to(jnp.arange(GRID_W)[None, :], (rows, GRID_W)).reshape(-1).astype(F32)
    axis_dim = rot_dim // 2
    inv_freq = jnp.power(jnp.float32(ROPE_THETA), -jnp.arange(0, axis_dim, 2, dtype=F32) / axis_dim)
    ang = jnp.concatenate([row[:, None] * inv_freq[None, :], col[:, None] * inv_freq[None, :]], axis=-1)
    return jnp.cos(ang), jnp.sin(ang)


def _rope_tables(nseq, seq_len):
    c, s = _rope_angles(seq_len, A_HEAD_DIM)
    ca = jnp.concatenate([c, c, c, c], axis=-1)
    sa = jnp.concatenate([-s, s, -s, s], axis=-1)
    c, s = _rope_angles(seq_len, B_ROPE)
    ones = jnp.ones((seq_len, B_NOPE), F32)
    zeros = jnp.zeros((seq_len, B_NOPE), F32)
    tail = LANES - B_NOPE - B_ROPE
    cb = jnp.concatenate([ones, c, c, ones[:, :tail]], axis=-1)
    sb = jnp.concatenate([zeros, -s, s, zeros[:, :tail]], axis=-1)
    return tuple(jnp.tile(a, (nseq, 1)) for a in (ca, sa, cb, sb))


def _prepare_weights(norm_mix, w_in, a_q_norm, a_k_norm, b_q_norm, b_kv_norm, w_q_up, w_kv_up, w_branch_a,
                     w_branch_b, w_out, norm_ffn, w_router, b_router, w_gate, w_up, w_down):
    depth = w_in.shape[0]
    order = jnp.array(A_HEAD_ORDER)
    parts = []
    start = 0
    for n in (A_Q_COLS, A_KV_COLS, A_KV_COLS, B_Q_RANK, B_KV_RANK, B_ROPE, D_MODEL, D_MODEL):
        parts.append(w_in[..., start:start + n])
        start += n
    qa, ka, va, cq, ckv, kr, ga, gb = parts
    qa = qa.reshape(depth, D_MODEL, A_HEADS, A_HEAD_DIM)[:, :, order].reshape(depth, D_MODEL, A_Q_COLS)
    kr = jnp.pad(kr, ((0, 0), (0, 0), (B_NOPE, LANES - B_NOPE - B_ROPE)))
    win = jnp.concatenate([qa, ka, va, cq, ckv, kr, ga, gb], axis=-1).astype(BF16)

    wq = w_q_up.reshape(depth, B_Q_RANK, B_HEADS, B_QK_DIM)
    wq = jnp.pad(wq, ((0, 0), (0, 0), (0, 0), (0, LANES - B_QK_DIM))).reshape(depth, B_Q_RANK, B_PAD_COLS)
    wkv = w_kv_up.reshape(depth, B_KV_RANK, B_HEADS, B_NOPE + B_V)
    wk = jnp.pad(wkv[..., :B_NOPE], ((0, 0), (0, 0), (0, 0), (0, LANES - B_NOPE))).reshape(depth, B_KV_RANK, B_PAD_COLS)
    wv = wkv[..., B_NOPE:].reshape(depth, B_KV_RANK, B_O_COLS)
    wba = w_branch_a.reshape(depth, A_HEADS, A_HEAD_DIM, D_MODEL)[:, order].reshape(depth, A_Q_COLS, D_MODEL)

    group = jnp.arange(A_Q_COLS) // A_HEAD_DIM
    bd = (group[:, None] == group[None, :]).astype(BF16)
    wr_t = w_router.T
    wrh, wrl = _split_bf16(wr_t)
    vec = lambda a: a[:, None, :]
    return dict(
        gmix=vec(norm_mix), win=win,
        gq=vec(jnp.tile(a_q_norm, (1, A_HEADS)) * (A_HEAD_DIM ** -0.5 * LOG2E)), gk=vec(jnp.tile(a_k_norm, (1, A_KV_HEADS))),
        gbq=vec(b_q_norm), gbkv=vec(b_kv_norm),
        wq=wq.astype(BF16), wk=wk.astype(BF16), wv=wv.astype(BF16), bd=bd,
        wba=wba.astype(BF16), wbb=w_branch_b.astype(BF16), wout=w_out.astype(BF16), gffn=vec(norm_ffn),
        wrh=wrh, wrl=wrl, br=b_router[:, None].astype(F32),
        wg=w_gate, wu=w_up, wd=w_down,
    )


def _pick(n, candidates):
    for c in candidates:
        if n % c == 0:
            return c
    raise ValueError(f"no tile in {candidates} divides {n}")


def _trunk(x3, lw, norm_final):
    nseq, s, _ = x3.shape
    t = nseq * s
    x = x3.reshape(t, D_MODEL)
    tabs = _rope_tables(nseq, s)
    depth = lw["win"].shape[0]
    tm = _pick(t, (256,))
    tm_out = _pick(t, (512, 256))
    tq = _pick(s, (512, 256))
    tk = _pick(s, (512,))
    moe_out = None
    for l in range(depth):
        x_new, qa, ka, va, qb, kb, vb, ga, gb = _mixer_in(x, moe_out, lw, l, tabs, tm_out)
        x = x if x_new is None else x_new
        oa = _attention_a(qa, ka, va, nseq, s, tq, tk)
        ob = _attention_b(qb, kb, vb, nseq, s, tq, tk)
        h, tp, route, route_t, counts = _mixer_out(oa, ob, ga, gb, x, lw, l, _pick(t, (1024, 512, 256)))
        moe_out = (h, _moe_rows(tp, route_t, counts, lw, l, EXPERT_ROWS), route)
        x = None
    return _final(*moe_out, norm_final[None, :], tm).reshape(x3.shape)


def kernel(x_prompt, x_sample, norm_mix, w_in, a_q_norm, a_k_norm, b_q_norm, b_kv_norm, w_q_up, w_kv_up, w_branch_a,
           w_branch_b, w_out, norm_ffn, w_router, b_router, w_gate, w_up, w_down, norm_final):
    lw = _prepare_weights(norm_mix, w_in, a_q_norm, a_k_norm, b_q_norm, b_kv_norm, w_q_up, w_kv_up, w_branch_a,
                          w_branch_b, w_out, norm_ffn, w_router, b_router, w_gate, w_up, w_down)
    return _trunk(x_prompt, lw, norm_final), _trunk(x_sample, lw, norm_final)
```

```python
import functools

import jax
import jax.numpy as jnp
from jax import lax
from jax.experimental import pallas as pl
from jax.experimental.pallas import tpu as pltpu
from jax.experimental.pallas import tpu_sc as plsc

F32 = jnp.float32
BF16 = jnp.bfloat16

D_MODEL = 1024
GRID_W = 64
ROPE_THETA = 10000.0
EPS = 1e-6
A_HEADS = 8
A_KV_HEADS = 2
A_HEAD_DIM = 64
B_HEADS = 8
B_Q_RANK = 384
B_KV_RANK = 256
B_NOPE = 64
B_ROPE = 32
B_V = 64
N_EXPERTS = 16
N_GROUPS = 4
EXPERTS_PER_GROUP = N_EXPERTS // N_GROUPS
D_EXPERT = 512
A_Q_COLS = A_HEADS * A_HEAD_DIM
A_KV_COLS = A_KV_HEADS * A_HEAD_DIM
B_QK_DIM = B_NOPE + B_ROPE
B_O_COLS = B_HEADS * B_V

LANES = 128
LOG2E = 1.4426950408889634

C_QA = 0
C_KA = C_QA + A_Q_COLS
C_VA = C_KA + A_KV_COLS
C_CQ = C_VA + A_KV_COLS
C_CKV = C_CQ + B_Q_RANK
C_KR = C_CKV + B_KV_RANK
C_GA = C_KR + LANES
C_GB = C_GA + D_MODEL
C_END = C_GB + D_MODEL
B_PAD_COLS = B_HEADS * LANES

A_HEAD_ORDER = (0, 4, 1, 5, 2, 6, 3, 7)

VMEM_LIMIT = 56 * 1024 * 1024

ATTN_GROUP = 16
ATTN_S_BUFS = 3
ATTN_P_BUFS = 2
ATTN_VALUE_LAG = 4
ATTN_SLAB = 32
ATTN_FIXED_SHIFT_MAX = 48.0
EXPERT_ROWS = 512
SC_CHUNK = 128


def _cparams(sem):
    return pltpu.CompilerParams(dimension_semantics=sem, vmem_limit_bytes=VMEM_LIMIT)


def _dot(a, b):
    return jnp.dot(a, b, preferred_element_type=F32)


def _dot_nt(a, b):
    return lax.dot_general(a, b, (((1,), (1,)), ((), ())), preferred_element_type=F32)


def _dot_tn(a, b):
    return lax.dot_general(a, b, (((0,), (0,)), ((), ())), preferred_element_type=F32)


def _rms(x):
    return x * lax.rsqrt(jnp.mean(x * x, axis=-1, keepdims=True) + EPS)


def _split_bf16(x):
    hi = x.astype(BF16)
    lo = (x - hi.astype(F32)).astype(BF16)
    return hi, lo


def _group_mean_sq(v, bd):
    return _dot((v * v).astype(BF16), bd) * (1.0 / A_HEAD_DIM)


def _rope_a(v, cos, sin):
    n = v.shape[-1]
    lane = lax.broadcasted_iota(jnp.int32, v.shape, 1)
    low = (lane % A_HEAD_DIM) < (A_HEAD_DIM // 2)
    swapped = jnp.where(low, pltpu.roll(v, n - A_HEAD_DIM // 2, 1), pltpu.roll(v, A_HEAD_DIM // 2, 1))
    return v * cos + swapped * sin


def _rope_b(v, cos, sin):
    n = v.shape[-1]
    lane = lax.broadcasted_iota(jnp.int32, v.shape, 1)
    low = (lane % LANES) < (B_NOPE + B_ROPE // 2)
    swapped = jnp.where(low, pltpu.roll(v, n - B_ROPE // 2, 1), pltpu.roll(v, B_ROPE // 2, 1))
    return v * cos + swapped * sin


def _mixer_in_body(*refs, fused):
    if fused:
        h_ref, y0_ref, y1_ref, route_ref, *refs = refs
    else:
        x_ref, *refs = refs
    (gmix_ref, win_ref, ca_ref, sa_ref, cb_ref, sb_ref, gq_ref, gk_ref, gbq_ref, gbkv_ref, wq_ref, wk_ref, wv_ref,
     bd_ref, *outs) = refs
    if fused:
        x_out_ref, *outs = outs
        x = _moe_residual(h_ref, y0_ref, y1_ref, route_ref)
        x_out_ref[...] = x
    else:
        x = x_ref[...]
    qat_ref, ka_ref, vat_ref, qbt_ref, kb_ref, vbt_ref, ga_ref, gb_ref = outs
    hb = (_rms(x) * gmix_ref[...]).astype(BF16)
    u = _dot(hb, win_ref[:, C_QA:C_GA])
    bd = bd_ref[...]
    ca, sa = ca_ref[...], sa_ref[...]
    cb, sb = cb_ref[...], sb_ref[...]

    qa = u[:, C_QA:C_KA]
    qa = qa * lax.rsqrt(_group_mean_sq(qa, bd) + EPS) * gq_ref[...]
    qa = _rope_a(qa, jnp.concatenate([ca] * (A_Q_COLS // LANES), axis=1),
                 jnp.concatenate([sa] * (A_Q_COLS // LANES), axis=1))
    qat_ref[...] = qa.T.astype(BF16)
    ka = u[:, C_KA:C_VA]
    ka = ka * lax.rsqrt(_group_mean_sq(ka, bd[:A_KV_COLS, :A_KV_COLS]) + EPS) * gk_ref[...]
    ka_ref[...] = _rope_a(ka, ca, sa).astype(BF16)
    vat_ref[...] = u[:, C_VA:C_CQ].T.astype(BF16)

    cq = (_rms(u[:, C_CQ:C_CKV]) * gbq_ref[...]).astype(BF16)
    qb = _dot(cq, wq_ref[...])
    qb = _rope_b(qb, jnp.concatenate([cb] * B_HEADS, axis=1), jnp.concatenate([sb] * B_HEADS, axis=1))
    qbt_ref[...] = (qb * (B_QK_DIM ** -0.5 * LOG2E)).T.astype(BF16)
    ckv = (_rms(u[:, C_CKV:C_KR]) * gbkv_ref[...]).astype(BF16)
    kr = _rope_b(u[:, C_KR:C_GA], cb, sb)
    kb_ref[...] = (_dot(ckv, wk_ref[...]) + jnp.concatenate([kr] * B_HEADS, axis=1)).astype(BF16)
    vbt_ref[...] = _dot(ckv, wv_ref[...]).T.astype(BF16)

    g = jax.nn.sigmoid(_dot(hb, win_ref[:, C_GA:C_END]))
    ga_ref[...] = g[:, :D_MODEL].astype(BF16)
    gb_ref[...] = g[:, D_MODEL:].astype(BF16)


def _mixer_in(x, moe_out, lw, l, tabs, tm):
    fused = x is None
    t = moe_out[0].shape[0] if fused else x.shape[0]
    row = lambda n: pl.BlockSpec((tm, n), lambda i: (i, 0))
    full = lambda a: pl.BlockSpec((None,) + a.shape[1:], lambda i: (l,) + (0,) * (a.ndim - 1))
    const = lambda a: pl.BlockSpec(a.shape, lambda i: (0,) * a.ndim)
    col = lambda n: pl.BlockSpec((n, tm), lambda i: (0, i))
    ca, sa, cb, sb = tabs
    if fused:
        h, y, route = moe_out
        second = pl.BlockSpec((tm, D_MODEL // 2), lambda i: (i + t // tm, 0))
        ins, in_specs = [h, y, y, route], [row(D_MODEL), row(D_MODEL // 2), second, row(LANES)]
    else:
        ins, in_specs = [x], [row(D_MODEL)]
    ins += [lw["gmix"], lw["win"], ca, sa, cb, sb, lw["gq"], lw["gk"], lw["gbq"], lw["gbkv"],
            lw["wq"], lw["wk"], lw["wv"], lw["bd"]]
    in_specs += [full(lw["gmix"]), full(lw["win"]), row(LANES), row(LANES), row(LANES), row(LANES),
                 full(lw["gq"]), full(lw["gk"]), full(lw["gbq"]), full(lw["gbkv"]),
                 full(lw["wq"]), full(lw["wk"]), full(lw["wv"]), const(lw["bd"])]
    widths = [A_Q_COLS, A_KV_COLS, A_KV_COLS, B_PAD_COLS, B_PAD_COLS, B_O_COLS, D_MODEL, D_MODEL]
    transposed = [True, False, True, True, False, True, False, False]
    out_specs = [col(n) if tr else row(n) for n, tr in zip(widths, transposed)]
    out_shape = [jax.ShapeDtypeStruct((n, t) if tr else (t, n), BF16) for n, tr in zip(widths, transposed)]
    if fused:
        out_specs.insert(0, row(D_MODEL))
        out_shape.insert(0, jax.ShapeDtypeStruct((t, D_MODEL), F32))
    outs = pl.pallas_call(
        functools.partial(_mixer_in_body, fused=fused),
        grid=(t // tm,),
        in_specs=in_specs,
        out_specs=out_specs,
        out_shape=out_shape,
        compiler_params=_cparams(("parallel",)),
        name="mixer_in",
    )(*ins)
    return tuple(outs) if fused else (None,) + tuple(outs)


def _flash_pair(q0t, q1t, k0_ref, k1_ref, vt_ref, o_ref, s_refs, p_refs, acc_ref, kmax2, fixed, tk):
    tq = q0t.shape[1]
    nk = vt_ref.shape[1] // tk
    group = min(nk, ATTN_GROUP)
    ns = len(s_refs)
    qt2 = jnp.concatenate([q0t, q1t], axis=1)

    def score_values(j):
        off = pl.multiple_of(j * tk, tk)
        if k1_ref is None:
            return _dot(k0_ref[pl.ds(off, tk), :], qt2)
        return jnp.concatenate([_dot(k0_ref[pl.ds(off, tk), :], q0t), _dot(k1_ref[pl.ds(off, tk), :], q1t)], axis=1)

    def add_values(j, p, alpha):
        off = pl.multiple_of(j * tk, tk)
        for rows, lanes in ((slice(0, B_V), slice(0, tq)), (slice(B_V, LANES), slice(tq, 2 * tq))):
            prev = acc_ref[rows, :] if alpha is None else alpha[:, lanes] * acc_ref[rows, :]
            acc_ref[rows, :] = prev + _dot(vt_ref[rows, pl.ds(off, tk)], p[:, lanes])

    def finish(l):
        l = jnp.sum(l, axis=0, keepdims=True)
        o_ref[:B_V, :] = (acc_ref[:B_V, :] / l[:, :tq]).astype(o_ref.dtype)
        o_ref[B_V:, :] = (acc_ref[B_V:, :] / l[:, tq:]).astype(o_ref.dtype)

    def run_groups(one_group, carry):
        acc_ref[...] = jnp.zeros_like(acc_ref)
        return one_group(0, carry) if nk == group else lax.fori_loop(0, nk // group, one_group, carry)

    def fixed_shift(shift):
        wide = 2 * LANES
        tiles = [(n * wide, (n * wide) // tq) for n in range(2 * tq // wide)]

        def value_tile(j, p, lane0, head):
            off = pl.multiple_of(j * tk, tk)
            rows = slice(head * B_V, (head + 1) * B_V)
            lanes = slice(lane0 - head * tq, lane0 - head * tq + wide)
            acc_ref[rows, lanes] += _dot(vt_ref[rows, pl.ds(off, tk)], p)

        def one_group(g, l):
            l = list(l)
            pending = []
            for c in range(group):
                j = g * group + c
                off = pl.multiple_of(j * tk, tk)
                for n, (lane0, head) in enumerate(tiles):
                    k_ref = k0_ref if (k1_ref is None or head == 0) else k1_ref
                    s = _dot(k_ref[pl.ds(off, tk), :], qt2[:, lane0:lane0 + wide])
                    if len(pending) >= ATTN_VALUE_LAG:
                        value_tile(*pending.pop(0))
                    p = jnp.exp2(s - shift[:, lane0:lane0 + wide])
                    l[n] = l[n] + p.reshape(tk // 8, 8, wide).sum(axis=0)
                    pending.append((j, p.astype(BF16), lane0, head))
            for item in pending:
                value_tile(*item)
            return tuple(l)

        l = run_groups(one_group, tuple(jnp.zeros((8, wide), F32) for _ in tiles))
        finish(jnp.concatenate(l, axis=1))

    def softmax(s_ref, p_ref, m, l):
        slabs = [pl.ds(r, ATTN_SLAB) for r in range(0, tk, ATTN_SLAB)]
        fold = lambda x: x.reshape(ATTN_SLAB // 8, 8, 2 * tq)
        mx = fold(s_ref[slabs[0], :]).max(axis=0)
        for sl in slabs[1:]:
            mx = jnp.maximum(mx, fold(s_ref[sl, :]).max(axis=0))
        m_new = jnp.maximum(m, jnp.max(mx, axis=0, keepdims=True))
        alpha = jnp.exp2(m - m_new)
        l = alpha * l
        for sl in slabs:
            p = jnp.exp2(s_ref[sl, :] - m_new)
            p_ref[sl, :] = p.astype(BF16)
            l = l + fold(p).sum(axis=0)
        return m_new, l, alpha

    def running_max():
        def one_group(g, carry):
            m, l = carry
            base = g * group
            for c in range(min(ns - 1, group)):
                s_refs[c % ns][...] = score_values(base + c)
            for c in range(group):
                if c + ns - 1 < group:
                    s_refs[(c + ns - 1) % ns][...] = score_values(base + c + ns - 1)
                p_ref = p_refs[c % len(p_refs)]
                m, l, alpha = softmax(s_refs[c % ns], p_ref, m, l)
                add_values(base + c, p_ref[...], alpha)
            return m, l

        carry = (jnp.full((1, 2 * tq), -jnp.inf, F32), jnp.zeros((8, 2 * tq), F32))
        finish(run_groups(one_group, carry)[1])

    if fixed:
        lane = lax.broadcasted_iota(jnp.int32, (1, 2 * tq), 1)
        qsq = jnp.sum(jnp.square(qt2.astype(F32)), axis=0, keepdims=True)
        fixed_shift(jnp.sqrt(qsq * jnp.where(lane < tq, kmax2[0], kmax2[1])))
    else:
        running_max()


def _max_sq_norm(k_ref, ones_ref, lanes):
    k = k_ref[...].astype(F32)
    n = _dot((k * k).astype(BF16), ones_ref[...])
    return jnp.max(n[:, lanes])


def _max_query_sq_norm(q):
    q = q.astype(F32)
    return jnp.max(jnp.sum(q * q, axis=0, keepdims=True))


def _attention_tiles(q_pair, k0_ref, k1_ref, vt_ref, o_ref, scratch, qmax2, kmax2, tq, tk):
    s_refs, p_refs, acc_ref = scratch[:ATTN_S_BUFS], scratch[ATTN_S_BUFS:-1], scratch[-1]
    bound2 = jnp.maximum(qmax2[0] * kmax2[0], qmax2[1] * kmax2[1])
    small = bound2 <= ATTN_FIXED_SHIFT_MAX * ATTN_FIXED_SHIFT_MAX

    def walk(fixed):
        @pl.loop(0, o_ref.shape[1] // tq)
        def _(j):
            lanes = pl.ds(pl.multiple_of(j * tq, tq), tq)
            q0t, q1t = q_pair(lanes)
            _flash_pair(q0t, q1t, k0_ref, k1_ref, vt_ref, o_ref.at[:, lanes], s_refs, p_refs, acc_ref, kmax2, fixed, tk)

    pl.when(small)(lambda: walk(True))
    pl.when(jnp.logical_not(small))(lambda: walk(False))


def _attn_a_body(qt_ref, k_ref, vt_ref, bound_ref, o_ref, *scratch, layer, tq, tk):
    qmax2 = (bound_ref[layer, 0], bound_ref[layer, 0])
    kmax2 = (bound_ref[layer, 1], bound_ref[layer, 1])

    def q_pair(lanes):
        qt = qt_ref[:, lanes]
        zero = jnp.zeros((A_HEAD_DIM, tq), BF16)
        return jnp.concatenate([qt[:A_HEAD_DIM], zero], axis=0), jnp.concatenate([zero, qt[A_HEAD_DIM:]], axis=0)

    _attention_tiles(q_pair, k_ref, None, vt_ref, o_ref, scratch, qmax2, kmax2, tq, tk)


def _attn_b_body(q0t_ref, q1t_ref, k0_ref, k1_ref, vt_ref, ones_ref, o_ref, *scratch, tq, tk):
    kmax2 = (_max_sq_norm(k0_ref, ones_ref, slice(0, LANES)), _max_sq_norm(k1_ref, ones_ref, slice(0, LANES)))
    qmax2 = (_max_query_sq_norm(q0t_ref[...]), _max_query_sq_norm(q1t_ref[...]))
    q_pair = lambda lanes: (q0t_ref[:, lanes], q1t_ref[:, lanes])
    _attention_tiles(q_pair, k0_ref, k1_ref, vt_ref, o_ref, scratch, qmax2, kmax2, tq, tk)


def _attn_scratch(tq, tk):
    return ([pltpu.VMEM((tk, 2 * tq), F32)] * ATTN_S_BUFS + [pltpu.VMEM((tk, 2 * tq), BF16)] * ATTN_P_BUFS
            + [pltpu.VMEM((LANES, tq), F32)])


def _head_ones(head_dim):
    group = jnp.arange(LANES) // head_dim
    return (group[:, None] == group[None, :]).astype(BF16)


def _attention_a(qat, ka, vat, bounds, layer, nseq, s, tq, tk):
    qspec = pl.BlockSpec((LANES, s), lambda b, h: (h, b))
    kspec = pl.BlockSpec((s, LANES), lambda b, h: (b, 0))
    vspec = pl.BlockSpec((LANES, s), lambda b, h: (0, b))
    return pl.pallas_call(
        functools.partial(_attn_a_body, layer=layer, tq=tq, tk=tk),
        grid=(nseq, A_Q_COLS // LANES),
        in_specs=[qspec, kspec, vspec, pl.BlockSpec(memory_space=pltpu.SMEM)],
        out_specs=qspec,
        out_shape=jax.ShapeDtypeStruct(qat.shape, BF16),
        scratch_shapes=_attn_scratch(tq, tk),
        compiler_params=_cparams(("parallel", "parallel")),
        name="attn_a",
    )(qat, ka, vat, bounds)


def _attention_b(qbt, kb, vbt, nseq, s, tq, tk):
    q0 = pl.BlockSpec((LANES, s), lambda b, h: (2 * h, b))
    q1 = pl.BlockSpec((LANES, s), lambda b, h: (2 * h + 1, b))
    k0 = pl.BlockSpec((s, LANES), lambda b, h: (b, 2 * h))
    k1 = pl.BlockSpec((s, LANES), lambda b, h: (b, 2 * h + 1))
    v = pl.BlockSpec((LANES, s), lambda b, h: (h, b))
    o = pl.BlockSpec((LANES, s), lambda b, h: (h, b))
    ones = pl.BlockSpec((LANES, LANES), lambda b, h: (0, 0))
    return pl.pallas_call(
        functools.partial(_attn_b_body, tq=tq, tk=tk),
        grid=(nseq, B_O_COLS // LANES),
        in_specs=[q0, q1, k0, k1, v, ones],
        out_specs=o,
        out_shape=jax.ShapeDtypeStruct(vbt.shape, BF16),
        scratch_shapes=_attn_scratch(tq, tk),
        compiler_params=_cparams(("parallel", "parallel")),
        name="attn_b",
    )(qbt, qbt, kb, kb, vbt, _head_ones(LANES))


def _within(x, d, period, n):
    row = lax.broadcasted_iota(jnp.int32, x.shape, 0)
    return jnp.where((row % period) + d < period, pltpu.roll(x, n - d, 0), pltpu.roll(x, period - d, 0))


def _route(logits_t, bias):
    n = N_EXPERTS
    scores = jax.nn.sigmoid(logits_t)
    biased = scores + bias
    row = lax.broadcasted_iota(jnp.int32, biased.shape, 0)
    pos = row % EXPERTS_PER_GROUP
    rank = jnp.zeros(biased.shape, jnp.int32)
    for d in range(1, EXPERTS_PER_GROUP):
        other = _within(biased, d, EXPERTS_PER_GROUP, n)
        other_pos = (pos + d) % EXPERTS_PER_GROUP
        ahead = (other > biased) | ((other == biased) & (other_pos < pos))
        rank = rank + ahead.astype(jnp.int32)
    top2 = rank < 2
    kept = jnp.where(top2, biased, 0.0)
    gscore = kept
    for d in range(1, EXPERTS_PER_GROUP):
        gscore = gscore + _within(kept, d, EXPERTS_PER_GROUP, n)
    grp = row // EXPERTS_PER_GROUP
    win = jnp.ones(biased.shape, jnp.bool_)
    for d in range(1, N_GROUPS):
        other = pltpu.roll(gscore, n - d * EXPERTS_PER_GROUP, 0)
        other_grp = (grp + d) % N_GROUPS
        win = win & ((other < gscore) | ((other == gscore) & (other_grp > grp)))
    sel = top2 & win
    w = jnp.where(sel, scores, 0.0)
    return w / jnp.sum(w, axis=0, keepdims=True), sel


def _pack_pairs(v):
    n = v.shape[1] // 2
    vb = v.astype(BF16).astype(F32)
    hi = pltpu.bitcast(vb[:, :n], jnp.int32)
    lo = pltpu.bitcast(vb[:, n:], jnp.int32)
    return hi | lax.shift_right_logical(lo, 16)


def _unpack_pairs(w):
    hi = pltpu.bitcast(w & jnp.int32(-65536), F32)
    lo = pltpu.bitcast(lax.shift_left(w, 16), F32)
    return jnp.concatenate([hi, lo], axis=1)


R_E0, R_E1, R_RANK0, R_RANK1, R_W0, R_W1 = range(6)


def _mixer_out_body(oa_ref, ob_ref, ga_ref, gb_ref, x_ref, wba_ref, wbb_ref, wout_ref, gffn_ref, wrh_ref, wrl_ref,
                    br_ref, tri_ref, h_ref, tp_ref, route_ref, route_t_ref, count_ref):
    @pl.when(pl.program_id(0) == 0)
    def _():
        count_ref[...] = jnp.zeros_like(count_ref)

    ma = _dot_tn(oa_ref[...], wba_ref[...])
    mb = _dot_tn(ob_ref[...], wbb_ref[...])
    merged = ga_ref[...].astype(F32) * ma + gb_ref[...].astype(F32) * mb
    h = x_ref[...] + _dot(merged.astype(BF16), wout_ref[...])
    h_ref[...] = h
    t = _rms(h) * gffn_ref[...]
    t_hi, t_lo = _split_bf16(t)
    tp_ref[...] = _pack_pairs(t)
    wrh, wrl = wrh_ref[...], wrl_ref[...]
    logits_t = _dot_nt(wrh, t_hi) + _dot_nt(wrh, t_lo) + _dot_nt(wrl, t_hi)
    gates_t, sel = _route(logits_t, br_ref[...])

    tm = gates_t.shape[1]
    onehot = jnp.where(sel, 1.0, 0.0)
    before = _dot(onehot.astype(BF16), tri_ref[...])
    rank_t = count_ref[:, 0:1] + before
    count_ref[...] = count_ref[...] + jnp.sum(onehot, axis=1, keepdims=True)

    row = lax.broadcasted_iota(jnp.int32, sel.shape, 0).astype(F32)
    e0 = jnp.min(jnp.where(sel, row, float(N_EXPERTS)), axis=0, keepdims=True)
    e1 = jnp.max(jnp.where(sel, row, -1.0), axis=0, keepdims=True)
    pick = lambda v, e: jnp.sum(jnp.where(sel & (row == e), v, 0.0), axis=0, keepdims=True)
    rec = jnp.concatenate([e0, e1, pick(rank_t, e0), pick(rank_t, e1), pick(gates_t, e0), pick(gates_t, e1),
                           jnp.zeros((LANES - 6, tm), F32)], axis=0)
    route_ref[...] = rec.T
    route_t_ref[...] = rec[:8]


def _mixer_out(oa, ob, ga, gb, x, lw, l, tm):
    t = x.shape[0]
    row = lambda n: pl.BlockSpec((tm, n), lambda i: (i, 0))
    full = lambda a: pl.BlockSpec((None,) + a.shape[1:], lambda i: (l,) + (0,) * (a.ndim - 1))
    const = lambda a: pl.BlockSpec(a.shape, lambda i: (0,) * a.ndim)
    col = lambda n: pl.BlockSpec((n, tm), lambda i: (0, i))
    tri = (jnp.arange(tm)[:, None] < jnp.arange(tm)[None, :]).astype(BF16)
    ins = [oa, ob, ga, gb, x, lw["wba"], lw["wbb"], lw["wout"], lw["gffn"], lw["wrh"], lw["wrl"], lw["br"], tri]
    in_specs = [col(A_Q_COLS), col(B_O_COLS), row(D_MODEL), row(D_MODEL), row(D_MODEL),
                full(lw["wba"]), full(lw["wbb"]), full(lw["wout"]), full(lw["gffn"]),
                const(lw["wrh"]), const(lw["wrl"]), const(lw["br"]), const(tri)]
    return pl.pallas_call(
        _mixer_out_body,
        grid=(t // tm,),
        in_specs=in_specs,
        out_specs=[row(D_MODEL), row(D_MODEL // 2), row(LANES), col(8),
                   pl.BlockSpec((N_EXPERTS, LANES), lambda i: (0, 0))],
        out_shape=[jax.ShapeDtypeStruct((t, D_MODEL), F32), jax.ShapeDtypeStruct((t, D_MODEL // 2), jnp.int32),
                   jax.ShapeDtypeStruct((t, LANES), F32), jax.ShapeDtypeStruct((8, t), F32),
                   jax.ShapeDtypeStruct((N_EXPERTS, LANES), F32)],
        compiler_params=_cparams(("arbitrary",)),
        name="mixer_out",
    )(*ins)


def _sc_plan(nrows):
    info = plsc.get_sparse_core_info()
    workers = info.num_cores * info.num_subcores
    per_worker = nrows // workers
    chunk = min(SC_CHUNK, per_worker)
    assert per_worker * workers == nrows and per_worker % chunk == 0 and chunk % 8 == 0, (nrows, workers, chunk)
    return info.num_cores, per_worker, chunk


def _sc_mesh():
    return plsc.VectorSubcoreMesh(core_axis_name="core", subcore_axis_name="subcore")


def _scatter_rows(x, idx0, idx1, nrows):
    t, d = x.shape
    ncores, per_worker, chunk = _sc_plan(t)

    @functools.partial(
        pl.kernel, out_type=jax.ShapeDtypeStruct((nrows, d), x.dtype), mesh=_sc_mesh(), name="moe_dispatch",
        scratch_types=[pltpu.VMEM((chunk,), jnp.int32), pltpu.VMEM((chunk,), jnp.int32), pltpu.VMEM((chunk, d), x.dtype)])
    def run(x_hbm, i0_hbm, i1_hbm, o_hbm, i0_v, i1_v, rows_v):
        worker = lax.axis_index("subcore") * ncores + lax.axis_index("core")

        @pl.loop(0, per_worker // chunk)
        def _(c):
            base = pl.multiple_of(worker * per_worker + c * chunk, chunk)
            pltpu.sync_copy(x_hbm.at[pl.ds(base, chunk)], rows_v)
            pltpu.sync_copy(i0_hbm.at[pl.ds(base, chunk)], i0_v)
            pltpu.sync_copy(i1_hbm.at[pl.ds(base, chunk)], i1_v)
            pltpu.sync_copy(rows_v, o_hbm.at[i0_v])
            pltpu.sync_copy(rows_v, o_hbm.at[i1_v])

    return run(x, idx0, idx1)


def _gather_rows(table, idx):
    m = idx.shape[0]
    d = table.shape[1]
    ncores, per_worker, chunk = _sc_plan(m)

    @functools.partial(
        pl.kernel, out_type=jax.ShapeDtypeStruct((m, d), table.dtype), mesh=_sc_mesh(), name="moe_collect",
        scratch_types=[pltpu.VMEM((chunk,), jnp.int32), pltpu.VMEM((chunk, d), table.dtype)])
    def run(x_hbm, i_hbm, o_hbm, i_v, rows_v):
        worker = lax.axis_index("subcore") * ncores + lax.axis_index("core")

        @pl.loop(0, per_worker // chunk)
        def _(c):
            base = pl.multiple_of(worker * per_worker + c * chunk, chunk)
            pltpu.sync_copy(i_hbm.at[pl.ds(base, chunk)], i_v)
            pltpu.sync_copy(x_hbm.at[i_v], rows_v)
            pltpu.sync_copy(rows_v, o_hbm.at[pl.ds(base, chunk)])

    return run(table, idx)


def _routing_tables(route_t, counts, tr):
    t = route_t.shape[1]
    experts = jnp.arange(N_EXPERTS, dtype=jnp.int32)
    cnt = counts[:, 0].astype(jnp.int32)
    seg_end = jnp.cumsum(cnt)
    seg_start = seg_end - cnt
    lookup = lambda table, idx: jnp.sum(jnp.where(idx[None, :] == experts[:, None], table[:, None], 0), axis=0)
    e = route_t[R_E0:R_E1 + 1].astype(jnp.int32).reshape(2 * t)
    rank = route_t[R_RANK0:R_RANK1 + 1].astype(jnp.int32).reshape(2 * t)
    pos = lookup(seg_start, e) + rank

    n_tiles = 2 * t // tr
    n_visits = n_tiles + N_EXPERTS - 1
    first_tile = seg_start // tr
    last_tile = jnp.maximum(seg_end - 1, 0) // tr
    visits = jnp.where(cnt > 0, last_tile - first_tile + 1, 0)
    visit_end = jnp.cumsum(visits)
    visit_start = visit_end - visits
    g = jnp.arange(n_visits, dtype=jnp.int32)
    valid = g < visit_end[-1]
    ex = jnp.minimum(jnp.sum((g[None, :] >= visit_end[:, None]).astype(jnp.int32), axis=0), N_EXPERTS - 1)
    tile = lookup(first_tile - visit_start, ex) + g
    lo = jnp.clip(lookup(seg_start, ex) - tile * tr, 0, tr)
    hi = jnp.clip(lookup(seg_end, ex) - tile * tr, 0, tr)
    last_ex = jnp.max(jnp.where(cnt > 0, experts, 0))
    tile = jnp.where(valid, tile, n_tiles - 1)
    ex = jnp.where(valid, ex, last_ex)
    lo = jnp.where(valid, lo, 0)
    hi = jnp.where(valid, hi, 0)
    first = jnp.concatenate([jnp.ones((1,), jnp.int32), (tile[1:] != tile[:-1]).astype(jnp.int32)])
    first = jnp.where(valid, first, 0)
    fresh = jnp.concatenate([jnp.ones((1,), jnp.int32), (ex[1:] != ex[:-1]).astype(jnp.int32)])
    return pos, (tile, ex, lo, hi, first, fresh)


def _experts_body(tile_ref, ex_ref, lo_ref, hi_ref, first_ref, fresh_ref, xs_ref, wg_ref, wu_ref, wd_ref, ys_ref,
                  acc_ref, wg_bf, wu_bf, wd_bf):
    g = pl.program_id(0)
    lo, hi = lo_ref[g], hi_ref[g]

    @pl.when(fresh_ref[g] == 1)
    def _():
        wg_bf[...] = wg_ref[...].astype(BF16)
        wu_bf[...] = wu_ref[...].astype(BF16)
        wd_bf[...] = wd_ref[...].astype(BF16)

    @pl.when(hi > lo)
    def _():
        x = _unpack_pairs(xs_ref[...]).astype(BF16)
        a = jax.nn.silu(_dot(x, wg_bf[...])) * _dot(x, wu_bf[...])
        row = lax.broadcasted_iota(jnp.int32, (a.shape[0], 1), 0)
        a = jnp.where((row >= lo) & (row < hi), a, 0.0)
        y = _dot(a.astype(BF16), wd_bf[...])

        @pl.when(first_ref[g] == 1)
        def _():
            acc_ref[...] = y

        @pl.when(first_ref[g] == 0)
        def _():
            acc_ref[...] += y

        ys_ref[...] = _pack_pairs(acc_ref[...])


def _experts(xs, visits, lw, l, tr):
    n_visits = visits[0].shape[0]
    rows = pl.BlockSpec((tr, D_MODEL // 2), lambda g, tile, *_: (tile[g], 0))
    wspec = lambda a: pl.BlockSpec((None, None) + a.shape[2:], lambda g, tile, ex, *_: (l, ex[g], 0, 0))
    return pl.pallas_call(
        _experts_body,
        grid_spec=pltpu.PrefetchScalarGridSpec(
            num_scalar_prefetch=len(visits),
            grid=(n_visits,),
            in_specs=[rows, wspec(lw["wg"]), wspec(lw["wu"]), wspec(lw["wd"])],
            out_specs=rows,
            scratch_shapes=[pltpu.VMEM((tr, D_MODEL), F32)] + [pltpu.VMEM(lw[k].shape[2:], BF16) for k in ("wg", "wu", "wd")],
        ),
        out_shape=jax.ShapeDtypeStruct(xs.shape, jnp.int32),
        compiler_params=_cparams(("arbitrary",)),
        name="experts",
    )(*visits, xs, lw["wg"], lw["wu"], lw["wd"])


def _moe_residual(h_ref, y0_ref, y1_ref, route_ref):
    r = route_ref[...]
    return (h_ref[...] + r[:, R_W0:R_W0 + 1] * _unpack_pairs(y0_ref[...])
            + r[:, R_W1:R_W1 + 1] * _unpack_pairs(y1_ref[...]))


def _final_body(h_ref, y0_ref, y1_ref, route_ref, gfin_ref, o_ref):
    o_ref[...] = _rms(_moe_residual(h_ref, y0_ref, y1_ref, route_ref)) * gfin_ref[...]


def _final(h, y, route, gfin, tm):
    t = h.shape[0]
    row = lambda n: pl.BlockSpec((tm, n), lambda i: (i, 0))
    second = pl.BlockSpec((tm, D_MODEL // 2), lambda i: (i + t // tm, 0))
    return pl.pallas_call(
        _final_body,
        grid=(t // tm,),
        in_specs=[row(D_MODEL), row(D_MODEL // 2), second, row(LANES), pl.BlockSpec(gfin.shape, lambda i: (0, 0))],
        out_specs=row(D_MODEL),
        out_shape=jax.ShapeDtypeStruct(h.shape, F32),
        compiler_params=_cparams(("parallel",)),
        name="final",
    )(h, y, y, route, gfin)


def _moe_rows(tp, route_t, counts, lw, l, tr):
    t = tp.shape[0]
    pos, visits = _routing_tables(route_t, counts, tr)
    xs = _scatter_rows(tp, pos[:t], pos[t:], 2 * t)
    ys = _experts(xs, visits, lw, l, tr)
    return _gather_rows(ys, pos)


def _rope_angles(seq_len, rot_dim):
    rows = seq_len // GRID_W
    row = jnp.broadcast_to(jnp.arange(rows)[:, None], (rows, GRID_W)).reshape(-1).astype(F32)
    col = jnp.broadcast_to(jnp.arange(GRID_W)[None, :], (rows, GRID_W)).reshape(-1).astype(F32)
    axis_dim = rot_dim // 2
    inv_freq = jnp.power(jnp.float32(ROPE_THETA), -jnp.arange(0, axis_dim, 2, dtype=F32) / axis_dim)
    ang = jnp.concatenate([row[:, None] * inv_freq[None, :], col[:, None] * inv_freq[None, :]], axis=-1)
    return jnp.cos(ang), jnp.sin(ang)


def _rope_tables(nseq, seq_len):
    c, s = _rope_angles(seq_len, A_HEAD_DIM)
    ca = jnp.concatenate([c, c, c, c], axis=-1)
    sa = jnp.concatenate([-s, s, -s, s], axis=-1)
    c, s = _rope_angles(seq_len, B_ROPE)
    ones = jnp.ones((seq_len, B_NOPE), F32)
    zeros = jnp.zeros((seq_len, B_NOPE), F32)
    tail = LANES - B_NOPE - B_ROPE
    cb = jnp.concatenate([ones, c, c, ones[:, :tail]], axis=-1)
    sb = jnp.concatenate([zeros, -s, s, zeros[:, :tail]], axis=-1)
    return tuple(jnp.tile(a, (nseq, 1)) for a in (ca, sa, cb, sb))


def _prepare_weights(norm_mix, w_in, a_q_norm, a_k_norm, b_q_norm, b_kv_norm, w_q_up, w_kv_up, w_branch_a,
                     w_branch_b, w_out, norm_ffn, w_router, b_router, w_gate, w_up, w_down):
    depth = w_in.shape[0]
    order = jnp.array(A_HEAD_ORDER)
    parts = []
    start = 0
    for n in (A_Q_COLS, A_KV_COLS, A_KV_COLS, B_Q_RANK, B_KV_RANK, B_ROPE, D_MODEL, D_MODEL):
        parts.append(w_in[..., start:start + n])
        start += n
    qa, ka, va, cq, ckv, kr, ga, gb = parts
    qa = qa.reshape(depth, D_MODEL, A_HEADS, A_HEAD_DIM)[:, :, order].reshape(depth, D_MODEL, A_Q_COLS)
    kr = jnp.pad(kr, ((0, 0), (0, 0), (B_NOPE, LANES - B_NOPE - B_ROPE)))
    win = jnp.concatenate([qa, ka, va, cq, ckv, kr, ga, gb], axis=-1).astype(BF16)

    wq = w_q_up.reshape(depth, B_Q_RANK, B_HEADS, B_QK_DIM)
    wq = jnp.pad(wq, ((0, 0), (0, 0), (0, 0), (0, LANES - B_QK_DIM))).reshape(depth, B_Q_RANK, B_PAD_COLS)
    wkv = w_kv_up.reshape(depth, B_KV_RANK, B_HEADS, B_NOPE + B_V)
    wk = jnp.pad(wkv[..., :B_NOPE], ((0, 0), (0, 0), (0, 0), (0, LANES - B_NOPE))).reshape(depth, B_KV_RANK, B_PAD_COLS)
    wv = wkv[..., B_NOPE:].reshape(depth, B_KV_RANK, B_O_COLS)
    wba = w_branch_a.reshape(depth, A_HEADS, A_HEAD_DIM, D_MODEL)[:, order].reshape(depth, A_Q_COLS, D_MODEL)

    group = jnp.arange(A_Q_COLS) // A_HEAD_DIM
    bd = (group[:, None] == group[None, :]).astype(BF16)
    wr_t = w_router.T
    wrh, wrl = _split_bf16(wr_t)
    vec = lambda a: a[:, None, :]
    gq = jnp.tile(a_q_norm, (1, A_HEADS)) * (A_HEAD_DIM ** -0.5 * LOG2E)
    norm2_bound = lambda g: 1.02 * A_HEAD_DIM * jnp.max(g * g, axis=-1)
    bounds_a = jnp.stack([norm2_bound(gq), norm2_bound(a_k_norm)], axis=-1).astype(F32)
    return dict(
        bounds_a=bounds_a,
        gmix=vec(norm_mix), win=win,
        gq=vec(gq), gk=vec(jnp.tile(a_k_norm, (1, A_KV_HEADS))),
        gbq=vec(b_q_norm), gbkv=vec(b_kv_norm),
        wq=wq.astype(BF16), wk=wk.astype(BF16), wv=wv.astype(BF16), bd=bd,
        wba=wba.astype(BF16), wbb=w_branch_b.astype(BF16), wout=w_out.astype(BF16), gffn=vec(norm_ffn),
        wrh=wrh, wrl=wrl, br=b_router[:, None].astype(F32),
        wg=w_gate, wu=w_up, wd=w_down,
    )


def _pick(n, candidates):
    for c in candidates:
        if n % c == 0:
            return c
    raise ValueError(f"no tile in {candidates} divides {n}")


def _trunk(x3, lw, norm_final):
    nseq, s, _ = x3.shape
    t = nseq * s
    x = x3.reshape(t, D_MODEL)
    tabs = _rope_tables(nseq, s)
    depth = lw["win"].shape[0]
    tm = _pick(t, (256,))
    tm_out = _pick(t, (512, 256))
    tq = _pick(s, (512, 256))
    tk = _pick(s, (512,))
    moe_out = None
    for l in range(depth):
        x_new, qa, ka, va, qb, kb, vb, ga, gb = _mixer_in(x, moe_out, lw, l, tabs, tm_out)
        x = x if x_new is None else x_new
        oa = _attention_a(qa, ka, va, lw["bounds_a"], l, nseq, s, tq, tk)
        ob = _attention_b(qb, kb, vb, nseq, s, tq, tk)
        h, tp, route, route_t, counts = _mixer_out(oa, ob, ga, gb, x, lw, l, _pick(t, (1024, 512, 256)))
        moe_out = (h, _moe_rows(tp, route_t, counts, lw, l, EXPERT_ROWS), route)
        x = None
    return _final(*moe_out, norm_final[None, :], tm).reshape(x3.shape)


def kernel(x_prompt, x_sample, norm_mix, w_in, a_q_norm, a_k_norm, b_q_norm, b_kv_norm, w_q_up, w_kv_up, w_branch_a,
           w_branch_b, w_out, norm_ffn, w_router, b_router, w_gate, w_up, w_down, norm_final):
    lw = _prepare_weights(norm_mix, w_in, a_q_norm, a_k_norm, b_q_norm, b_kv_norm, w_q_up, w_kv_up, w_branch_a,
                          w_branch_b, w_out, norm_ffn, w_router, b_router, w_gate, w_up, w_down)
    return _trunk(x_prompt, lw, norm_final), _trunk(x_sample, lw, norm_final)
```

```python
import functools

import jax
import jax.numpy as jnp
from jax import lax
from jax.experimental import pallas as pl
from jax.experimental.pallas import tpu as pltpu
from jax.experimental.pallas import tpu_sc as plsc

F32 = jnp.float32
BF16 = jnp.bfloat16

D_MODEL = 1024
GRID_W = 64
ROPE_THETA = 10000.0
EPS = 1e-6
A_HEADS = 8
A_KV_HEADS = 2
A_HEAD_DIM = 64
B_HEADS = 8
B_Q_RANK = 384
B_KV_RANK = 256
B_NOPE = 64
B_ROPE = 32
B_V = 64
N_EXPERTS = 16
N_GROUPS = 4
EXPERTS_PER_GROUP = N_EXPERTS // N_GROUPS
D_EXPERT = 512
A_Q_COLS = A_HEADS * A_HEAD_DIM
A_KV_COLS = A_KV_HEADS * A_HEAD_DIM
B_QK_DIM = B_NOPE + B_ROPE
B_O_COLS = B_HEADS * B_V

LANES = 128
LOG2E = 1.4426950408889634

C_QA = 0
C_KA = C_QA + A_Q_COLS
C_VA = C_KA + A_KV_COLS
C_CQ = C_VA + A_KV_COLS
C_CKV = C_CQ + B_Q_RANK
C_KR = C_CKV + B_KV_RANK
C_GA = C_KR + LANES
C_GB = C_GA + D_MODEL
C_END = C_GB + D_MODEL
B_PAD_COLS = B_HEADS * LANES

A_HEAD_ORDER = (0, 4, 1, 5, 2, 6, 3, 7)

VMEM_LIMIT = 56 * 1024 * 1024

ATTN_GROUP = 16
ATTN_UNROLL_BUDGET = 8192
ATTN_S_BUFS = 3
ATTN_P_BUFS = 2
ATTN_VALUE_LAG = 4
ATTN_SLAB = 32
ATTN_FIXED_SHIFT_MAX = 48.0
EXPERT_ROWS = 512
SC_CHUNK = 128


def _cparams(sem):
    return pltpu.CompilerParams(dimension_semantics=sem, vmem_limit_bytes=VMEM_LIMIT)


def _dot(a, b):
    return jnp.dot(a, b, preferred_element_type=F32)


def _dot_nt(a, b):
    return lax.dot_general(a, b, (((1,), (1,)), ((), ())), preferred_element_type=F32)


def _dot_tn(a, b):
    return lax.dot_general(a, b, (((0,), (0,)), ((), ())), preferred_element_type=F32)


def _rms(x):
    return x * lax.rsqrt(jnp.mean(x * x, axis=-1, keepdims=True) + EPS)


def _split_bf16(x):
    hi = x.astype(BF16)
    lo = (x - hi.astype(F32)).astype(BF16)
    return hi, lo


def _group_mean_sq(v, bd):
    return _dot((v * v).astype(BF16), bd) * (1.0 / A_HEAD_DIM)


def _rope_a(v, cos, sin):
    n = v.shape[-1]
    lane = lax.broadcasted_iota(jnp.int32, v.shape, 1)
    low = (lane % A_HEAD_DIM) < (A_HEAD_DIM // 2)
    swapped = jnp.where(low, pltpu.roll(v, n - A_HEAD_DIM // 2, 1), pltpu.roll(v, A_HEAD_DIM // 2, 1))
    return v * cos + swapped * sin


def _rope_b(v, cos, sin):
    n = v.shape[-1]
    lane = lax.broadcasted_iota(jnp.int32, v.shape, 1)
    low = (lane % LANES) < (B_NOPE + B_ROPE // 2)
    swapped = jnp.where(low, pltpu.roll(v, n - B_ROPE // 2, 1), pltpu.roll(v, B_ROPE // 2, 1))
    return v * cos + swapped * sin


def _mixer_in_body(*refs, fused):
    if fused:
        h_ref, y0_ref, y1_ref, route_ref, *refs = refs
    else:
        x_ref, *refs = refs
    (gmix_ref, win_ref, ca_ref, sa_ref, cb_ref, sb_ref, gq_ref, gk_ref, gbq_ref, gbkv_ref, wq_ref, wk_ref, wv_ref,
     bd_ref, *outs) = refs
    if fused:
        x_out_ref, *outs = outs
        x = _moe_residual(h_ref, y0_ref, y1_ref, route_ref)
        x_out_ref[...] = x
    else:
        x = x_ref[...]
    qat_ref, ka_ref, vat_ref, qbt_ref, kb_ref, vbt_ref, ga_ref, gb_ref = outs
    hb = (_rms(x) * gmix_ref[...]).astype(BF16)
    u = _dot(hb, win_ref[:, C_QA:C_GA])
    bd = bd_ref[...]
    ca, sa = ca_ref[...], sa_ref[...]
    cb, sb = cb_ref[...], sb_ref[...]

    qa = u[:, C_QA:C_KA]
    qa = qa * lax.rsqrt(_group_mean_sq(qa, bd) + EPS) * gq_ref[...]
    qa = _rope_a(qa, jnp.concatenate([ca] * (A_Q_COLS // LANES), axis=1),
                 jnp.concatenate([sa] * (A_Q_COLS // LANES), axis=1))
    qat_ref[...] = qa.T.astype(BF16)
    ka = u[:, C_KA:C_VA]
    ka = ka * lax.rsqrt(_group_mean_sq(ka, bd[:A_KV_COLS, :A_KV_COLS]) + EPS) * gk_ref[...]
    ka_ref[...] = _rope_a(ka, ca, sa).astype(BF16)
    vat_ref[...] = u[:, C_VA:C_CQ].T.astype(BF16)

    cq = (_rms(u[:, C_CQ:C_CKV]) * gbq_ref[...]).astype(BF16)
    qb = _dot(cq, wq_ref[...])
    qb = _rope_b(qb, jnp.concatenate([cb] * B_HEADS, axis=1), jnp.concatenate([sb] * B_HEADS, axis=1))
    qbt_ref[...] = (qb * (B_QK_DIM ** -0.5 * LOG2E)).T.astype(BF16)
    ckv = (_rms(u[:, C_CKV:C_KR]) * gbkv_ref[...]).astype(BF16)
    kr = _rope_b(u[:, C_KR:C_GA], cb, sb)
    kb_ref[...] = (_dot(ckv, wk_ref[...]) + jnp.concatenate([kr] * B_HEADS, axis=1)).astype(BF16)
    vbt_ref[...] = _dot(ckv, wv_ref[...]).T.astype(BF16)

    g = jax.nn.sigmoid(_dot(hb, win_ref[:, C_GA:C_END]))
    ga_ref[...] = g[:, :D_MODEL].astype(BF16)
    gb_ref[...] = g[:, D_MODEL:].astype(BF16)


def _mixer_in(x, moe_out, lw, l, tabs, tm):
    fused = x is None
    t = moe_out[0].shape[0] if fused else x.shape[0]
    row = lambda n: pl.BlockSpec((tm, n), lambda i: (i, 0))
    full = lambda a: pl.BlockSpec((None,) + a.shape[1:], lambda i: (l,) + (0,) * (a.ndim - 1))
    const = lambda a: pl.BlockSpec(a.shape, lambda i: (0,) * a.ndim)
    col = lambda n: pl.BlockSpec((n, tm), lambda i: (0, i))
    ca, sa, cb, sb = tabs
    if fused:
        h, y, route = moe_out
        second = pl.BlockSpec((tm, D_MODEL // 2), lambda i: (i + t // tm, 0))
        ins, in_specs = [h, y, y, route], [row(D_MODEL), row(D_MODEL // 2), second, row(LANES)]
    else:
        ins, in_specs = [x], [row(D_MODEL)]
    ins += [lw["gmix"], lw["win"], ca, sa, cb, sb, lw["gq"], lw["gk"], lw["gbq"], lw["gbkv"],
            lw["wq"], lw["wk"], lw["wv"], lw["bd"]]
    in_specs += [full(lw["gmix"]), full(lw["win"]), row(LANES), row(LANES), row(LANES), row(LANES),
                 full(lw["gq"]), full(lw["gk"]), full(lw["gbq"]), full(lw["gbkv"]),
                 full(lw["wq"]), full(lw["wk"]), full(lw["wv"]), const(lw["bd"])]
    widths = [A_Q_COLS, A_KV_COLS, A_KV_COLS, B_PAD_COLS, B_PAD_COLS, B_O_COLS, D_MODEL, D_MODEL]
    transposed = [True, False, True, True, False, True, False, False]
    out_specs = [col(n) if tr else row(n) for n, tr in zip(widths, transposed)]
    out_shape = [jax.ShapeDtypeStruct((n, t) if tr else (t, n), BF16) for n, tr in zip(widths, transposed)]
    if fused:
        out_specs.insert(0, row(D_MODEL))
        out_shape.insert(0, jax.ShapeDtypeStruct((t, D_MODEL), F32))
    outs = pl.pallas_call(
        functools.partial(_mixer_in_body, fused=fused),
        grid=(t // tm,),
        in_specs=in_specs,
        out_specs=out_specs,
        out_shape=out_shape,
        compiler_params=_cparams(("parallel",)),
        name="mixer_in",
    )(*ins)
    return tuple(outs) if fused else (None,) + tuple(outs)


def _flash_pair(q0t, q1t, k0_ref, k1_ref, vt_ref, o_ref, s_refs, p_refs, acc_ref, kmax2, fixed, tk):
    tq = q0t.shape[1]
    nk = vt_ref.shape[1] // tk
    group = min(nk, ATTN_GROUP)
    ns = len(s_refs)
    qt2 = jnp.concatenate([q0t, q1t], axis=1)

    def score_values(j):
        off = pl.multiple_of(j * tk, tk)
        if k1_ref is None:
            return _dot(k0_ref[pl.ds(off, tk), :], qt2)
        return jnp.concatenate([_dot(k0_ref[pl.ds(off, tk), :], q0t), _dot(k1_ref[pl.ds(off, tk), :], q1t)], axis=1)

    def add_values(j, p, alpha):
        off = pl.multiple_of(j * tk, tk)
        for rows, lanes in ((slice(0, B_V), slice(0, tq)), (slice(B_V, LANES), slice(tq, 2 * tq))):
            prev = acc_ref[rows, :] if alpha is None else alpha[:, lanes] * acc_ref[rows, :]
            acc_ref[rows, :] = prev + _dot(vt_ref[rows, pl.ds(off, tk)], p[:, lanes])

    def finish(l):
        l = jnp.sum(l, axis=0, keepdims=True)
        o_ref[:B_V, :] = (acc_ref[:B_V, :] / l[:, :tq]).astype(o_ref.dtype)
        o_ref[B_V:, :] = (acc_ref[B_V:, :] / l[:, tq:]).astype(o_ref.dtype)

    def run_groups(one_group, carry):
        acc_ref[...] = jnp.zeros_like(acc_ref)
        return one_group(0, carry) if nk == group else lax.fori_loop(0, nk // group, one_group, carry)

    def fixed_shift(shift):
        wide = 2 * LANES
        tiles = [(n * wide, (n * wide) // tq) for n in range(2 * tq // wide)]

        def value_tile(j, p, lane0, head):
            off = pl.multiple_of(j * tk, tk)
            rows = slice(head * B_V, (head + 1) * B_V)
            lanes = slice(lane0 - head * tq, lane0 - head * tq + wide)
            acc_ref[rows, lanes] += _dot(vt_ref[rows, pl.ds(off, tk)], p)

        def one_group(g, l):
            l = list(l)
            pending = []
            for c in range(group):
                j = g * group + c
                off = pl.multiple_of(j * tk, tk)
                for n, (lane0, head) in enumerate(tiles):
                    k_ref = k0_ref if (k1_ref is None or head == 0) else k1_ref
                    s = _dot(k_ref[pl.ds(off, tk), :], qt2[:, lane0:lane0 + wide])
                    if len(pending) >= ATTN_VALUE_LAG:
                        value_tile(*pending.pop(0))
                    p = jnp.exp2(s - shift[:, lane0:lane0 + wide])
                    l[n] = l[n] + p.reshape(tk // 8, 8, wide).sum(axis=0)
                    pending.append((j, p.astype(BF16), lane0, head))
            for item in pending:
                value_tile(*item)
            return tuple(l)

        l = run_groups(one_group, tuple(jnp.zeros((8, wide), F32) for _ in tiles))
        finish(jnp.concatenate(l, axis=1))

    def softmax(s_ref, p_ref, m, l):
        slabs = [pl.ds(r, ATTN_SLAB) for r in range(0, tk, ATTN_SLAB)]
        fold = lambda x: x.reshape(ATTN_SLAB // 8, 8, 2 * tq)
        mx = fold(s_ref[slabs[0], :]).max(axis=0)
        for sl in slabs[1:]:
            mx = jnp.maximum(mx, fold(s_ref[sl, :]).max(axis=0))
        m_new = jnp.maximum(m, jnp.max(mx, axis=0, keepdims=True))
        alpha = jnp.exp2(m - m_new)
        l = alpha * l
        for sl in slabs:
            p = jnp.exp2(s_ref[sl, :] - m_new)
            p_ref[sl, :] = p.astype(BF16)
            l = l + fold(p).sum(axis=0)
        return m_new, l, alpha

    def running_max():
        def one_group(g, carry):
            m, l = carry
            base = g * group
            for c in range(min(ns - 1, group)):
                s_refs[c % ns][...] = score_values(base + c)
            for c in range(group):
                if c + ns - 1 < group:
                    s_refs[(c + ns - 1) % ns][...] = score_values(base + c + ns - 1)
                p_ref = p_refs[c % len(p_refs)]
                m, l, alpha = softmax(s_refs[c % ns], p_ref, m, l)
                add_values(base + c, p_ref[...], alpha)
            return m, l

        carry = (jnp.full((1, 2 * tq), -jnp.inf, F32), jnp.zeros((8, 2 * tq), F32))
        finish(run_groups(one_group, carry)[1])

    if fixed:
        lane = lax.broadcasted_iota(jnp.int32, (1, 2 * tq), 1)
        qsq = jnp.sum(jnp.square(qt2.astype(F32)), axis=0, keepdims=True)
        fixed_shift(jnp.sqrt(qsq * jnp.where(lane < tq, kmax2[0], kmax2[1])))
    else:
        running_max()


def _max_sq_norm(k_ref, ones_ref, lanes):
    k = k_ref[...].astype(F32)
    n = _dot((k * k).astype(BF16), ones_ref[...])
    return jnp.max(n[:, lanes])


def _max_query_sq_norm(q):
    q = q.astype(F32)
    return jnp.max(jnp.sum(q * q, axis=0, keepdims=True))


def _attention_tiles(q_pair, k0_ref, k1_ref, vt_ref, o_ref, scratch, qmax2, kmax2, tq, tk):
    s_refs, p_refs, acc_ref = scratch[:ATTN_S_BUFS], scratch[ATTN_S_BUFS:-1], scratch[-1]
    bound2 = jnp.maximum(qmax2[0] * kmax2[0], qmax2[1] * kmax2[1])
    small = bound2 <= ATTN_FIXED_SHIFT_MAX * ATTN_FIXED_SHIFT_MAX

    def walk(fixed):
        @pl.loop(0, o_ref.shape[1] // tq)
        def _(j):
            lanes = pl.ds(pl.multiple_of(j * tq, tq), tq)
            q0t, q1t = q_pair(lanes)
            _flash_pair(q0t, q1t, k0_ref, k1_ref, vt_ref, o_ref.at[:, lanes], s_refs, p_refs, acc_ref, kmax2, fixed, tk)

    pl.when(small)(lambda: walk(True))
    pl.when(jnp.logical_not(small))(lambda: walk(False))


def _attn_a_body(qt_ref, k_ref, vt_ref, bound_ref, o_ref, *scratch, layer, tq, tk):
    qmax2 = (bound_ref[layer, 0], bound_ref[layer, 0])
    kmax2 = (bound_ref[layer, 1], bound_ref[layer, 1])

    def q_pair(lanes):
        qt = qt_ref[:, lanes]
        zero = jnp.zeros((A_HEAD_DIM, tq), BF16)
        return jnp.concatenate([qt[:A_HEAD_DIM], zero], axis=0), jnp.concatenate([zero, qt[A_HEAD_DIM:]], axis=0)

    _attention_tiles(q_pair, k_ref, None, vt_ref, o_ref, scratch, qmax2, kmax2, tq, tk)


def _attn_b_body(q0t_ref, q1t_ref, k0_ref, k1_ref, vt_ref, ones_ref, o_ref, *scratch, tq, tk):
    kmax2 = (_max_sq_norm(k0_ref, ones_ref, slice(0, LANES)), _max_sq_norm(k1_ref, ones_ref, slice(0, LANES)))
    qmax2 = (_max_query_sq_norm(q0t_ref[...]), _max_query_sq_norm(q1t_ref[...]))
    q_pair = lambda lanes: (q0t_ref[:, lanes], q1t_ref[:, lanes])
    _attention_tiles(q_pair, k0_ref, k1_ref, vt_ref, o_ref, scratch, qmax2, kmax2, tq, tk)


def _attn_scratch(tq, tk):
    return ([pltpu.VMEM((tk, 2 * tq), F32)] * ATTN_S_BUFS + [pltpu.VMEM((tk, 2 * tq), BF16)] * ATTN_P_BUFS
            + [pltpu.VMEM((LANES, tq), F32)])


def _head_ones(head_dim):
    group = jnp.arange(LANES) // head_dim
    return (group[:, None] == group[None, :]).astype(BF16)


def _attention_a(qat, ka, vat, bounds, layer, nseq, s, tq, tk):
    qspec = pl.BlockSpec((LANES, s), lambda b, h: (h, b))
    kspec = pl.BlockSpec((s, LANES), lambda b, h: (b, 0))
    vspec = pl.BlockSpec((LANES, s), lambda b, h: (0, b))
    return pl.pallas_call(
        functools.partial(_attn_a_body, layer=layer, tq=tq, tk=tk),
        grid=(nseq, A_Q_COLS // LANES),
        in_specs=[qspec, kspec, vspec, pl.BlockSpec(memory_space=pltpu.SMEM)],
        out_specs=qspec,
        out_shape=jax.ShapeDtypeStruct(qat.shape, BF16),
        scratch_shapes=_attn_scratch(tq, tk),
        compiler_params=_cparams(("parallel", "parallel")),
        name="attn_a",
    )(qat, ka, vat, bounds)


def _attention_b(qbt, kb, vbt, nseq, s, tq, tk):
    q0 = pl.BlockSpec((LANES, s), lambda b, h: (2 * h, b))
    q1 = pl.BlockSpec((LANES, s), lambda b, h: (2 * h + 1, b))
    k0 = pl.BlockSpec((s, LANES), lambda b, h: (b, 2 * h))
    k1 = pl.BlockSpec((s, LANES), lambda b, h: (b, 2 * h + 1))
    v = pl.BlockSpec((LANES, s), lambda b, h: (h, b))
    o = pl.BlockSpec((LANES, s), lambda b, h: (h, b))
    ones = pl.BlockSpec((LANES, LANES), lambda b, h: (0, 0))
    return pl.pallas_call(
        functools.partial(_attn_b_body, tq=tq, tk=tk),
        grid=(nseq, B_O_COLS // LANES),
        in_specs=[q0, q1, k0, k1, v, ones],
        out_specs=o,
        out_shape=jax.ShapeDtypeStruct(vbt.shape, BF16),
        scratch_shapes=_attn_scratch(tq, tk),
        compiler_params=_cparams(("parallel", "parallel")),
        name="attn_b",
    )(qbt, qbt, kb, kb, vbt, _head_ones(LANES))


def _within(x, d, period, n):
    row = lax.broadcasted_iota(jnp.int32, x.shape, 0)
    return jnp.where((row % period) + d < period, pltpu.roll(x, n - d, 0), pltpu.roll(x, period - d, 0))


def _route(logits_t, bias):
    n = N_EXPERTS
    scores = jax.nn.sigmoid(logits_t)
    biased = scores + bias
    row = lax.broadcasted_iota(jnp.int32, biased.shape, 0)
    pos = row % EXPERTS_PER_GROUP
    rank = jnp.zeros(biased.shape, jnp.int32)
    for d in range(1, EXPERTS_PER_GROUP):
        other = _within(biased, d, EXPERTS_PER_GROUP, n)
        other_pos = (pos + d) % EXPERTS_PER_GROUP
        ahead = (other > biased) | ((other == biased) & (other_pos < pos))
        rank = rank + ahead.astype(jnp.int32)
    top2 = rank < 2
    kept = jnp.where(top2, biased, 0.0)
    gscore = kept
    for d in range(1, EXPERTS_PER_GROUP):
        gscore = gscore + _within(kept, d, EXPERTS_PER_GROUP, n)
    grp = row // EXPERTS_PER_GROUP
    win = jnp.ones(biased.shape, jnp.bool_)
    for d in range(1, N_GROUPS):
        other = pltpu.roll(gscore, n - d * EXPERTS_PER_GROUP, 0)
        other_grp = (grp + d) % N_GROUPS
        win = win & ((other < gscore) | ((other == gscore) & (other_grp > grp)))
    sel = top2 & win
    w = jnp.where(sel, scores, 0.0)
    return w / jnp.sum(w, axis=0, keepdims=True), sel


def _pack_pairs(v):
    n = v.shape[1] // 2
    vb = v.astype(BF16).astype(F32)
    hi = pltpu.bitcast(vb[:, :n], jnp.int32)
    lo = pltpu.bitcast(vb[:, n:], jnp.int32)
    return hi | lax.shift_right_logical(lo, 16)


def _unpack_pairs(w):
    hi = pltpu.bitcast(w & jnp.int32(-65536), F32)
    lo = pltpu.bitcast(lax.shift_left(w, 16), F32)
    return jnp.concatenate([hi, lo], axis=1)


R_E0, R_E1, R_RANK0, R_RANK1, R_W0, R_W1 = range(6)


def _mixer_out_body(oa_ref, ob_ref, ga_ref, gb_ref, x_ref, wba_ref, wbb_ref, wout_ref, gffn_ref, wrh_ref, wrl_ref,
                    br_ref, tri_ref, h_ref, tp_ref, route_ref, route_t_ref, count_ref):
    @pl.when(pl.program_id(0) == 0)
    def _():
        count_ref[...] = jnp.zeros_like(count_ref)

    ma = _dot_tn(oa_ref[...], wba_ref[...])
    mb = _dot_tn(ob_ref[...], wbb_ref[...])
    merged = ga_ref[...].astype(F32) * ma + gb_ref[...].astype(F32) * mb
    h = x_ref[...] + _dot(merged.astype(BF16), wout_ref[...])
    h_ref[...] = h
    t = _rms(h) * gffn_ref[...]
    t_hi, t_lo = _split_bf16(t)
    tp_ref[...] = _pack_pairs(t)
    wrh, wrl = wrh_ref[...], wrl_ref[...]
    logits_t = _dot_nt(wrh, t_hi) + _dot_nt(wrh, t_lo) + _dot_nt(wrl, t_hi)
    gates_t, sel = _route(logits_t, br_ref[...])

    tm = gates_t.shape[1]
    onehot = jnp.where(sel, 1.0, 0.0)
    before = _dot(onehot.astype(BF16), tri_ref[...])
    rank_t = count_ref[:, 0:1] + before
    count_ref[...] = count_ref[...] + jnp.sum(onehot, axis=1, keepdims=True)

    row = lax.broadcasted_iota(jnp.int32, sel.shape, 0).astype(F32)
    e0 = jnp.min(jnp.where(sel, row, float(N_EXPERTS)), axis=0, keepdims=True)
    e1 = jnp.max(jnp.where(sel, row, -1.0), axis=0, keepdims=True)
    pick = lambda v, e: jnp.sum(jnp.where(sel & (row == e), v, 0.0), axis=0, keepdims=True)
    rec = jnp.concatenate([e0, e1, pick(rank_t, e0), pick(rank_t, e1), pick(gates_t, e0), pick(gates_t, e1),
                           jnp.zeros((LANES - 6, tm), F32)], axis=0)
    route_ref[...] = rec.T
    route_t_ref[...] = rec[:8]


def _mixer_out(oa, ob, ga, gb, x, lw, l, tm):
    t = x.shape[0]
    row = lambda n: pl.BlockSpec((tm, n), lambda i: (i, 0))
    full = lambda a: pl.BlockSpec((None,) + a.shape[1:], lambda i: (l,) + (0,) * (a.ndim - 1))
    const = lambda a: pl.BlockSpec(a.shape, lambda i: (0,) * a.ndim)
    col = lambda n: pl.BlockSpec((n, tm), lambda i: (0, i))
    tri = (jnp.arange(tm)[:, None] < jnp.arange(tm)[None, :]).astype(BF16)
    ins = [oa, ob, ga, gb, x, lw["wba"], lw["wbb"], lw["wout"], lw["gffn"], lw["wrh"], lw["wrl"], lw["br"], tri]
    in_specs = [col(A_Q_COLS), col(B_O_COLS), row(D_MODEL), row(D_MODEL), row(D_MODEL),
                full(lw["wba"]), full(lw["wbb"]), full(lw["wout"]), full(lw["gffn"]),
                const(lw["wrh"]), const(lw["wrl"]), const(lw["br"]), const(tri)]
    return pl.pallas_call(
        _mixer_out_body,
        grid=(t // tm,),
        in_specs=in_specs,
        out_specs=[row(D_MODEL), row(D_MODEL // 2), row(LANES), col(8),
                   pl.BlockSpec((N_EXPERTS, LANES), lambda i: (0, 0))],
        out_shape=[jax.ShapeDtypeStruct((t, D_MODEL), F32), jax.ShapeDtypeStruct((t, D_MODEL // 2), jnp.int32),
                   jax.ShapeDtypeStruct((t, LANES), F32), jax.ShapeDtypeStruct((8, t), F32),
                   jax.ShapeDtypeStruct((N_EXPERTS, LANES), F32)],
        compiler_params=_cparams(("arbitrary",)),
        name="mixer_out",
    )(*ins)


def _sc_plan(nrows):
    info = plsc.get_sparse_core_info()
    workers = info.num_cores * info.num_subcores
    per_worker = nrows // workers
    chunk = min(SC_CHUNK, per_worker)
    assert per_worker * workers == nrows and per_worker % chunk == 0 and chunk % 8 == 0, (nrows, workers, chunk)
    return info.num_cores, per_worker, chunk


def _sc_mesh():
    return plsc.VectorSubcoreMesh(core_axis_name="core", subcore_axis_name="subcore")


def _scatter_rows(x, idx0, idx1, nrows):
    t, d = x.shape
    ncores, per_worker, chunk = _sc_plan(t)

    @functools.partial(
        pl.kernel, out_type=jax.ShapeDtypeStruct((nrows, d), x.dtype), mesh=_sc_mesh(), name="moe_dispatch",
        scratch_types=[pltpu.VMEM((chunk,), jnp.int32), pltpu.VMEM((chunk,), jnp.int32), pltpu.VMEM((chunk, d), x.dtype)])
    def run(x_hbm, i0_hbm, i1_hbm, o_hbm, i0_v, i1_v, rows_v):
        worker = lax.axis_index("subcore") * ncores + lax.axis_index("core")

        @pl.loop(0, per_worker // chunk)
        def _(c):
            base = pl.multiple_of(worker * per_worker + c * chunk, chunk)
            pltpu.sync_copy(x_hbm.at[pl.ds(base, chunk)], rows_v)
            pltpu.sync_copy(i0_hbm.at[pl.ds(base, chunk)], i0_v)
            pltpu.sync_copy(i1_hbm.at[pl.ds(base, chunk)], i1_v)
            pltpu.sync_copy(rows_v, o_hbm.at[i0_v])
            pltpu.sync_copy(rows_v, o_hbm.at[i1_v])

    return run(x, idx0, idx1)


def _gather_rows(table, idx):
    m = idx.shape[0]
    d = table.shape[1]
    ncores, per_worker, chunk = _sc_plan(m)

    @functools.partial(
        pl.kernel, out_type=jax.ShapeDtypeStruct((m, d), table.dtype), mesh=_sc_mesh(), name="moe_collect",
        scratch_types=[pltpu.VMEM((chunk,), jnp.int32), pltpu.VMEM((chunk, d), table.dtype)])
    def run(x_hbm, i_hbm, o_hbm, i_v, rows_v):
        worker = lax.axis_index("subcore") * ncores + lax.axis_index("core")

        @pl.loop(0, per_worker // chunk)
        def _(c):
            base = pl.multiple_of(worker * per_worker + c * chunk, chunk)
            pltpu.sync_copy(i_hbm.at[pl.ds(base, chunk)], i_v)
            pltpu.sync_copy(x_hbm.at[i_v], rows_v)
            pltpu.sync_copy(rows_v, o_hbm.at[pl.ds(base, chunk)])

    return run(table, idx)


def _routing_tables(route_t, counts, tr):
    t = route_t.shape[1]
    experts = jnp.arange(N_EXPERTS, dtype=jnp.int32)
    cnt = counts[:, 0].astype(jnp.int32)
    seg_end = jnp.cumsum(cnt)
    seg_start = seg_end - cnt
    lookup = lambda table, idx: jnp.sum(jnp.where(idx[None, :] == experts[:, None], table[:, None], 0), axis=0)
    e = route_t[R_E0:R_E1 + 1].astype(jnp.int32).reshape(2 * t)
    rank = route_t[R_RANK0:R_RANK1 + 1].astype(jnp.int32).reshape(2 * t)
    pos = lookup(seg_start, e) + rank

    n_tiles = 2 * t // tr
    n_visits = n_tiles + N_EXPERTS - 1
    first_tile = seg_start // tr
    last_tile = jnp.maximum(seg_end - 1, 0) // tr
    visits = jnp.where(cnt > 0, last_tile - first_tile + 1, 0)
    visit_end = jnp.cumsum(visits)
    visit_start = visit_end - visits
    g = jnp.arange(n_visits, dtype=jnp.int32)
    valid = g < visit_end[-1]
    ex = jnp.minimum(jnp.sum((g[None, :] >= visit_end[:, None]).astype(jnp.int32), axis=0), N_EXPERTS - 1)
    tile = lookup(first_tile - visit_start, ex) + g
    lo = jnp.clip(lookup(seg_start, ex) - tile * tr, 0, tr)
    hi = jnp.clip(lookup(seg_end, ex) - tile * tr, 0, tr)
    last_ex = jnp.max(jnp.where(cnt > 0, experts, 0))
    tile = jnp.where(valid, tile, n_tiles - 1)
    ex = jnp.where(valid, ex, last_ex)
    lo = jnp.where(valid, lo, 0)
    hi = jnp.where(valid, hi, 0)
    first = jnp.concatenate([jnp.ones((1,), jnp.int32), (tile[1:] != tile[:-1]).astype(jnp.int32)])
    first = jnp.where(valid, first, 0)
    fresh = jnp.concatenate([jnp.ones((1,), jnp.int32), (ex[1:] != ex[:-1]).astype(jnp.int32)])
    return pos, (tile, ex, lo, hi, first, fresh)


def _experts_body(tile_ref, ex_ref, lo_ref, hi_ref, first_ref, fresh_ref, xs_ref, wg_ref, wu_ref, wd_ref, ys_ref,
                  acc_ref, wg_bf, wu_bf, wd_bf):
    g = pl.program_id(0)
    lo, hi = lo_ref[g], hi_ref[g]

    @pl.when(fresh_ref[g] == 1)
    def _():
        wg_bf[...] = wg_ref[...].astype(BF16)
        wu_bf[...] = wu_ref[...].astype(BF16)
        wd_bf[...] = wd_ref[...].astype(BF16)

    @pl.when(hi > lo)
    def _():
        x = _unpack_pairs(xs_ref[...]).astype(BF16)
        a = jax.nn.silu(_dot(x, wg_bf[...])) * _dot(x, wu_bf[...])
        row = lax.broadcasted_iota(jnp.int32, (a.shape[0], 1), 0)
        a = jnp.where((row >= lo) & (row < hi), a, 0.0)
        y = _dot(a.astype(BF16), wd_bf[...])

        @pl.when(first_ref[g] == 1)
        def _():
            acc_ref[...] = y

        @pl.when(first_ref[g] == 0)
        def _():
            acc_ref[...] += y

        ys_ref[...] = _pack_pairs(acc_ref[...])


def _experts(xs, visits, lw, l, tr):
    n_visits = visits[0].shape[0]
    rows = pl.BlockSpec((tr, D_MODEL // 2), lambda g, tile, *_: (tile[g], 0))
    wspec = lambda a: pl.BlockSpec((None, None) + a.shape[2:], lambda g, tile, ex, *_: (l, ex[g], 0, 0))
    return pl.pallas_call(
        _experts_body,
        grid_spec=pltpu.PrefetchScalarGridSpec(
            num_scalar_prefetch=len(visits),
            grid=(n_visits,),
            in_specs=[rows, wspec(lw["wg"]), wspec(lw["wu"]), wspec(lw["wd"])],
            out_specs=rows,
            scratch_shapes=[pltpu.VMEM((tr, D_MODEL), F32)] + [pltpu.VMEM(lw[k].shape[2:], BF16) for k in ("wg", "wu", "wd")],
        ),
        out_shape=jax.ShapeDtypeStruct(xs.shape, jnp.int32),
        compiler_params=_cparams(("arbitrary",)),
        name="experts",
    )(*visits, xs, lw["wg"], lw["wu"], lw["wd"])


def _moe_residual(h_ref, y0_ref, y1_ref, route_ref):
    r = route_ref[...]
    return (h_ref[...] + r[:, R_W0:R_W0 + 1] * _unpack_pairs(y0_ref[...])
            + r[:, R_W1:R_W1 + 1] * _unpack_pairs(y1_ref[...]))


def _final_body(h_ref, y0_ref, y1_ref, route_ref, gfin_ref, o_ref):
    o_ref[...] = _rms(_moe_residual(h_ref, y0_ref, y1_ref, route_ref)) * gfin_ref[...]


def _final(h, y, route, gfin, tm):
    t = h.shape[0]
    row = lambda n: pl.BlockSpec((tm, n), lambda i: (i, 0))
    second = pl.BlockSpec((tm, D_MODEL // 2), lambda i: (i + t // tm, 0))
    return pl.pallas_call(
        _final_body,
        grid=(t // tm,),
        in_specs=[row(D_MODEL), row(D_MODEL // 2), second, row(LANES), pl.BlockSpec(gfin.shape, lambda i: (0, 0))],
        out_specs=row(D_MODEL),
        out_shape=jax.ShapeDtypeStruct(h.shape, F32),
        compiler_params=_cparams(("parallel",)),
        name="final",
    )(h, y, y, route, gfin)


def _moe_rows(tp, route_t, counts, lw, l, tr):
    t = tp.shape[0]
    pos, visits = _routing_tables(route_t, counts, tr)
    xs = _scatter_rows(tp, pos[:t], pos[t:], 2 * t)
    ys = _experts(xs, visits, lw, l, tr)
    return _gather_rows(ys, pos)


def _rope_angles(seq_len, rot_dim):
    rows = seq_len // GRID_W
    row = jnp.broadcast_to(jnp.arange(rows)[:, None], (rows, GRID_W)).reshape(-1).astype(F32)
    col = jnp.broadcast_to(jnp.arange(GRID_W)[None, :], (rows, GRID_W)).reshape(-1).astype(F32)
    axis_dim = rot_dim // 2
    inv_freq = jnp.power(jnp.float32(ROPE_THETA), -jnp.arange(0, axis_dim, 2, dtype=F32) / axis_dim)
    ang = jnp.concatenate([row[:, None] * inv_freq[None, :], col[:, None] * inv_freq[None, :]], axis=-1)
    return jnp.cos(ang), jnp.sin(ang)


def _rope_tables(nseq, seq_len):
    c, s = _rope_angles(seq_len, A_HEAD_DIM)
    ca = jnp.concatenate([c, c, c, c], axis=-1)
    sa = jnp.concatenate([-s, s, -s, s], axis=-1)
    c, s = _rope_angles(seq_len, B_ROPE)
    ones = jnp.ones((seq_len, B_NOPE), F32)
    zeros = jnp.zeros((seq_len, B_NOPE), F32)
    tail = LANES - B_NOPE - B_ROPE
    cb = jnp.concatenate([ones, c, c, ones[:, :tail]], axis=-1)
    sb = jnp.concatenate([zeros, -s, s, zeros[:, :tail]], axis=-1)
    return tuple(jnp.tile(a, (nseq, 1)) for a in (ca, sa, cb, sb))


def _prepare_weights(norm_mix, w_in, a_q_norm, a_k_norm, b_q_norm, b_kv_norm, w_q_up, w_kv_up, w_branch_a,
                     w_branch_b, w_out, norm_ffn, w_router, b_router, w_gate, w_up, w_down):
    depth = w_in.shape[0]
    order = jnp.array(A_HEAD_ORDER)
    parts = []
    start = 0
    for n in (A_Q_COLS, A_KV_COLS, A_KV_COLS, B_Q_RANK, B_KV_RANK, B_ROPE, D_MODEL, D_MODEL):
        parts.append(w_in[..., start:start + n])
        start += n
    qa, ka, va, cq, ckv, kr, ga, gb = parts
    qa = qa.reshape(depth, D_MODEL, A_HEADS, A_HEAD_DIM)[:, :, order].reshape(depth, D_MODEL, A_Q_COLS)
    kr = jnp.pad(kr, ((0, 0), (0, 0), (B_NOPE, LANES - B_NOPE - B_ROPE)))
    win = jnp.concatenate([qa, ka, va, cq, ckv, kr, ga, gb], axis=-1).astype(BF16)

    wq = w_q_up.reshape(depth, B_Q_RANK, B_HEADS, B_QK_DIM)
    wq = jnp.pad(wq, ((0, 0), (0, 0), (0, 0), (0, LANES - B_QK_DIM))).reshape(depth, B_Q_RANK, B_PAD_COLS)
    wkv = w_kv_up.reshape(depth, B_KV_RANK, B_HEADS, B_NOPE + B_V)
    wk = jnp.pad(wkv[..., :B_NOPE], ((0, 0), (0, 0), (0, 0), (0, LANES - B_NOPE))).reshape(depth, B_KV_RANK, B_PAD_COLS)
    wv = wkv[..., B_NOPE:].reshape(depth, B_KV_RANK, B_O_COLS)
    wba = w_branch_a.reshape(depth, A_HEADS, A_HEAD_DIM, D_MODEL)[:, order].reshape(depth, A_Q_COLS, D_MODEL)

    group = jnp.arange(A_Q_COLS) // A_HEAD_DIM
    bd = (group[:, None] == group[None, :]).astype(BF16)
    wr_t = w_router.T
    wrh, wrl = _split_bf16(wr_t)
    vec = lambda a: a[:, None, :]
    gq = jnp.tile(a_q_norm, (1, A_HEADS)) * (A_HEAD_DIM ** -0.5 * LOG2E)
    norm2_bound = lambda g: 1.02 * A_HEAD_DIM * jnp.max(g * g, axis=-1)
    bounds_a = jnp.stack([norm2_bound(gq), norm2_bound(a_k_norm)], axis=-1).astype(F32)
    return dict(
        bounds_a=bounds_a,
        gmix=vec(norm_mix), win=win,
        gq=vec(gq), gk=vec(jnp.tile(a_k_norm, (1, A_KV_HEADS))),
        gbq=vec(b_q_norm), gbkv=vec(b_kv_norm),
        wq=wq.astype(BF16), wk=wk.astype(BF16), wv=wv.astype(BF16), bd=bd,
        wba=wba.astype(BF16), wbb=w_branch_b.astype(BF16), wout=w_out.astype(BF16), gffn=vec(norm_ffn),
        wrh=wrh, wrl=wrl, br=b_router[:, None].astype(F32),
        wg=w_gate, wu=w_up, wd=w_down,
    )


def _pick(n, candidates):
    for c in candidates:
        if n % c == 0:
            return c
    raise ValueError(f"no tile in {candidates} divides {n}")


def _trunk(x3, lw, norm_final):
    nseq, s, _ = x3.shape
    t = nseq * s
    x = x3.reshape(t, D_MODEL)
    tabs = _rope_tables(nseq, s)
    depth = lw["win"].shape[0]
    tm = _pick(t, (256,))
    tm_out = _pick(t, (512, 256))
    tk = _pick(s, (512,))
    tq = _pick(s, tuple(c for c in (1024, 512, 256) if c * (s // tk) <= ATTN_UNROLL_BUDGET) or (256,))
    moe_out = None
    for l in range(depth):
        x_new, qa, ka, va, qb, kb, vb, ga, gb = _mixer_in(x, moe_out, lw, l, tabs, tm_out)
        x = x if x_new is None else x_new
        oa = _attention_a(qa, ka, va, lw["bounds_a"], l, nseq, s, tq, tk)
        ob = _attention_b(qb, kb, vb, nseq, s, tq, tk)
        h, tp, route, route_t, counts = _mixer_out(oa, ob, ga, gb, x, lw, l, _pick(t, (1024, 512, 256)))
        moe_out = (h, _moe_rows(tp, route_t, counts, lw, l, EXPERT_ROWS), route)
        x = None
    return _final(*moe_out, norm_final[None, :], tm).reshape(x3.shape)


def kernel(x_prompt, x_sample, norm_mix, w_in, a_q_norm, a_k_norm, b_q_norm, b_kv_norm, w_q_up, w_kv_up, w_branch_a,
           w_branch_b, w_out, norm_ffn, w_router, b_router, w_gate, w_up, w_down, norm_final):
    lw = _prepare_weights(norm_mix, w_in, a_q_norm, a_k_norm, b_q_norm, b_kv_norm, w_q_up, w_kv_up, w_branch_a,
                          w_branch_b, w_out, norm_ffn, w_router, b_router, w_gate, w_up, w_down)
    return _trunk(x_prompt, lw, norm_final), _trunk(x_sample, lw, norm_final)
```

```python
import functools

import jax
import jax.numpy as jnp
from jax import lax
from jax.experimental import pallas as pl
from jax.experimental.pallas import tpu as pltpu
from jax.experimental.pallas import tpu_sc as plsc

F32 = jnp.float32
BF16 = jnp.bfloat16

D_MODEL = 1024
GRID_W = 64
ROPE_THETA = 10000.0
EPS = 1e-6
A_HEADS = 8
A_KV_HEADS = 2
A_HEAD_DIM = 64
B_HEADS = 8
B_Q_RANK = 384
B_KV_RANK = 256
B_NOPE = 64
B_ROPE = 32
B_V = 64
N_EXPERTS = 16
N_GROUPS = 4
EXPERTS_PER_GROUP = N_EXPERTS // N_GROUPS
D_EXPERT = 512
A_Q_COLS = A_HEADS * A_HEAD_DIM
A_KV_COLS = A_KV_HEADS * A_HEAD_DIM
B_QK_DIM = B_NOPE + B_ROPE
B_O_COLS = B_HEADS * B_V

LANES = 128
LOG2E = 1.4426950408889634

C_QA = 0
C_KA = C_QA + A_Q_COLS
C_VA = C_KA + A_KV_COLS
C_CQ = C_VA + A_KV_COLS
C_CKV = C_CQ + B_Q_RANK
C_KR = C_CKV + B_KV_RANK
C_GA = C_KR + LANES
C_GB = C_GA + D_MODEL
C_END = C_GB + D_MODEL
B_PAD_COLS = B_HEADS * LANES

A_HEAD_ORDER = (0, 4, 1, 5, 2, 6, 3, 7)

VMEM_LIMIT = 56 * 1024 * 1024

ATTN_GROUP = 16
ATTN_UNROLL_BUDGET = 8192
ATTN_S_BUFS = 3
ATTN_P_BUFS = 2
ATTN_VALUE_LAG = 4
ATTN_SLAB = 32
ATTN_FIXED_SHIFT_MAX = 48.0
EXPERT_ROWS = 512
SC_CHUNK = 128


def _cparams(sem):
    return pltpu.CompilerParams(dimension_semantics=sem, vmem_limit_bytes=VMEM_LIMIT)


def _dot(a, b):
    return jnp.dot(a, b, preferred_element_type=F32)


def _dot_nt(a, b):
    return lax.dot_general(a, b, (((1,), (1,)), ((), ())), preferred_element_type=F32)


def _dot_tn(a, b):
    return lax.dot_general(a, b, (((0,), (0,)), ((), ())), preferred_element_type=F32)


def _rms(x):
    return x * lax.rsqrt(jnp.mean(x * x, axis=-1, keepdims=True) + EPS)


def _split_bf16(x):
    hi = x.astype(BF16)
    lo = (x - hi.astype(F32)).astype(BF16)
    return hi, lo


def _group_mean_sq(v, bd):
    return _dot((v * v).astype(BF16), bd) * (1.0 / A_HEAD_DIM)


def _rope_a(v, cos, sin):
    n = v.shape[-1]
    lane = lax.broadcasted_iota(jnp.int32, v.shape, 1)
    low = (lane % A_HEAD_DIM) < (A_HEAD_DIM // 2)
    swapped = jnp.where(low, pltpu.roll(v, n - A_HEAD_DIM // 2, 1), pltpu.roll(v, A_HEAD_DIM // 2, 1))
    return v * cos + swapped * sin


def _rope_b(v, cos, sin):
    n = v.shape[-1]
    lane = lax.broadcasted_iota(jnp.int32, v.shape, 1)
    low = (lane % LANES) < (B_NOPE + B_ROPE // 2)
    swapped = jnp.where(low, pltpu.roll(v, n - B_ROPE // 2, 1), pltpu.roll(v, B_ROPE // 2, 1))
    return v * cos + swapped * sin


def _mixer_in_body(*refs, fused):
    if fused:
        h_ref, y0_ref, y1_ref, route_ref, *refs = refs
    else:
        x_ref, *refs = refs
    (gmix_ref, win_ref, ca_ref, sa_ref, cb_ref, sb_ref, gq_ref, gk_ref, gbq_ref, gbkv_ref, wq_ref, wk_ref, wv_ref,
     bd_ref, *outs) = refs
    if fused:
        x_out_ref, *outs = outs
        x = _moe_residual(h_ref, y0_ref, y1_ref, route_ref)
        x_out_ref[...] = x
    else:
        x = x_ref[...]
    qat_ref, ka_ref, vat_ref, qbt_ref, kb_ref, vbt_ref, ga_ref, gb_ref = outs
    hb = (_rms(x) * gmix_ref[...]).astype(BF16)
    u = _dot(hb, win_ref[:, C_QA:C_GA])
    bd = bd_ref[...]
    ca, sa = ca_ref[...], sa_ref[...]
    cb, sb = cb_ref[...], sb_ref[...]

    qa = u[:, C_QA:C_KA]
    qa = qa * lax.rsqrt(_group_mean_sq(qa, bd) + EPS) * gq_ref[...]
    qa = _rope_a(qa, jnp.concatenate([ca] * (A_Q_COLS // LANES), axis=1),
                 jnp.concatenate([sa] * (A_Q_COLS // LANES), axis=1))
    qat_ref[...] = qa.T.astype(BF16)
    ka = u[:, C_KA:C_VA]
    ka = ka * lax.rsqrt(_group_mean_sq(ka, bd[:A_KV_COLS, :A_KV_COLS]) + EPS) * gk_ref[...]
    ka_ref[...] = _rope_a(ka, ca, sa).astype(BF16)
    vat_ref[...] = u[:, C_VA:C_CQ].T.astype(BF16)

    cq = (_rms(u[:, C_CQ:C_CKV]) * gbq_ref[...]).astype(BF16)
    qb = _dot(cq, wq_ref[...])
    qb = _rope_b(qb, jnp.concatenate([cb] * B_HEADS, axis=1), jnp.concatenate([sb] * B_HEADS, axis=1))
    qbt_ref[...] = (qb * (B_QK_DIM ** -0.5 * LOG2E)).T.astype(BF16)
    ckv = (_rms(u[:, C_CKV:C_KR]) * gbkv_ref[...]).astype(BF16)
    kr = _rope_b(u[:, C_KR:C_GA], cb, sb)
    kb_ref[...] = (_dot(ckv, wk_ref[...]) + jnp.concatenate([kr] * B_HEADS, axis=1)).astype(BF16)
    vbt_ref[...] = _dot(ckv, wv_ref[...]).T.astype(BF16)

    g = jax.nn.sigmoid(_dot(hb, win_ref[:, C_GA:C_END]))
    ga_ref[...] = g[:, :D_MODEL].astype(BF16)
    gb_ref[...] = g[:, D_MODEL:].astype(BF16)


def _mixer_in(x, moe_out, lw, l, tabs, tm):
    fused = x is None
    t = moe_out[0].shape[0] if fused else x.shape[0]
    row = lambda n: pl.BlockSpec((tm, n), lambda i: (i, 0))
    full = lambda a: pl.BlockSpec((None,) + a.shape[1:], lambda i: (l,) + (0,) * (a.ndim - 1))
    const = lambda a: pl.BlockSpec(a.shape, lambda i: (0,) * a.ndim)
    col = lambda n: pl.BlockSpec((n, tm), lambda i: (0, i))
    ca, sa, cb, sb = tabs
    if fused:
        h, y, route = moe_out
        second = pl.BlockSpec((tm, D_MODEL // 2), lambda i: (i + t // tm, 0))
        ins, in_specs = [h, y, y, route], [row(D_MODEL), row(D_MODEL // 2), second, row(LANES)]
    else:
        ins, in_specs = [x], [row(D_MODEL)]
    ins += [lw["gmix"], lw["win"], ca, sa, cb, sb, lw["gq"], lw["gk"], lw["gbq"], lw["gbkv"],
            lw["wq"], lw["wk"], lw["wv"], lw["bd"]]
    in_specs += [full(lw["gmix"]), full(lw["win"]), row(LANES), row(LANES), row(LANES), row(LANES),
                 full(lw["gq"]), full(lw["gk"]), full(lw["gbq"]), full(lw["gbkv"]),
                 full(lw["wq"]), full(lw["wk"]), full(lw["wv"]), const(lw["bd"])]
    widths = [A_Q_COLS, A_KV_COLS, A_KV_COLS, B_PAD_COLS, B_PAD_COLS, B_O_COLS, D_MODEL, D_MODEL]
    transposed = [True, False, True, True, False, True, False, False]
    out_specs = [col(n) if tr else row(n) for n, tr in zip(widths, transposed)]
    out_shape = [jax.ShapeDtypeStruct((n, t) if tr else (t, n), BF16) for n, tr in zip(widths, transposed)]
    if fused:
        out_specs.insert(0, row(D_MODEL))
        out_shape.insert(0, jax.ShapeDtypeStruct((t, D_MODEL), F32))
    outs = pl.pallas_call(
        functools.partial(_mixer_in_body, fused=fused),
        grid=(t // tm,),
        in_specs=in_specs,
        out_specs=out_specs,
        out_shape=out_shape,
        compiler_params=_cparams(("parallel",)),
        name="mixer_in",
    )(*ins)
    return tuple(outs) if fused else (None,) + tuple(outs)


def _flash_pair(q0t, q1t, k0_ref, k1_ref, vt_ref, o_ref, s_refs, p_refs, acc_ref, kmax2, fixed, tk):
    tq = q0t.shape[1]
    nk = vt_ref.shape[1] // tk
    group = min(nk, ATTN_GROUP)
    ns = len(s_refs)
    qt2 = jnp.concatenate([q0t, q1t], axis=1)

    def score_values(j):
        off = pl.multiple_of(j * tk, tk)
        if k1_ref is None:
            return _dot(k0_ref[pl.ds(off, tk), :], qt2)
        return jnp.concatenate([_dot(k0_ref[pl.ds(off, tk), :], q0t), _dot(k1_ref[pl.ds(off, tk), :], q1t)], axis=1)

    def add_values(j, p, alpha):
        off = pl.multiple_of(j * tk, tk)
        for rows, lanes in ((slice(0, B_V), slice(0, tq)), (slice(B_V, LANES), slice(tq, 2 * tq))):
            prev = acc_ref[rows, :] if alpha is None else alpha[:, lanes] * acc_ref[rows, :]
            acc_ref[rows, :] = prev + _dot(vt_ref[rows, pl.ds(off, tk)], p[:, lanes])

    def finish(l):
        l = jnp.sum(l, axis=0, keepdims=True)
        o_ref[:B_V, :] = (acc_ref[:B_V, :] / l[:, :tq]).astype(o_ref.dtype)
        o_ref[B_V:, :] = (acc_ref[B_V:, :] / l[:, tq:]).astype(o_ref.dtype)

    def run_groups(one_group, carry):
        acc_ref[...] = jnp.zeros_like(acc_ref)
        return one_group(0, carry) if nk == group else lax.fori_loop(0, nk // group, one_group, carry)

    def fixed_shift(shift):
        wide = 2 * LANES
        tiles = [(n * wide, (n * wide) // tq) for n in range(2 * tq // wide)]

        def value_tile(j, p, lane0, head):
            off = pl.multiple_of(j * tk, tk)
            rows = slice(head * B_V, (head + 1) * B_V)
            lanes = slice(lane0 - head * tq, lane0 - head * tq + wide)
            acc_ref[rows, lanes] += _dot(vt_ref[rows, pl.ds(off, tk)], p)

        def one_group(g, l):
            l = list(l)
            pending = []
            for c in range(group):
                j = g * group + c
                off = pl.multiple_of(j * tk, tk)
                for n, (lane0, head) in enumerate(tiles):
                    k_ref = k0_ref if (k1_ref is None or head == 0) else k1_ref
                    s = _dot(k_ref[pl.ds(off, tk), :], qt2[:, lane0:lane0 + wide])
                    if len(pending) >= ATTN_VALUE_LAG:
                        value_tile(*pending.pop(0))
                    p = jnp.exp2(s - shift[:, lane0:lane0 + wide])
                    l[n] = l[n] + p.reshape(tk // 8, 8, wide).sum(axis=0)
                    pending.append((j, p.astype(BF16), lane0, head))
            for item in pending:
                value_tile(*item)
            return tuple(l)

        l = run_groups(one_group, tuple(jnp.zeros((8, wide), F32) for _ in tiles))
        finish(jnp.concatenate(l, axis=1))

    def softmax(s_ref, p_ref, m, l):
        slabs = [pl.ds(r, ATTN_SLAB) for r in range(0, tk, ATTN_SLAB)]
        fold = lambda x: x.reshape(ATTN_SLAB // 8, 8, 2 * tq)
        mx = fold(s_ref[slabs[0], :]).max(axis=0)
        for sl in slabs[1:]:
            mx = jnp.maximum(mx, fold(s_ref[sl, :]).max(axis=0))
        m_new = jnp.maximum(m, jnp.max(mx, axis=0, keepdims=True))
        alpha = jnp.exp2(m - m_new)
        l = alpha * l
        for sl in slabs:
            p = jnp.exp2(s_ref[sl, :] - m_new)
            p_ref[sl, :] = p.astype(BF16)
            l = l + fold(p).sum(axis=0)
        return m_new, l, alpha

    def running_max():
        def one_group(g, carry):
            m, l = carry
            base = g * group
            for c in range(min(ns - 1, group)):
                s_refs[c % ns][...] = score_values(base + c)
            for c in range(group):
                if c + ns - 1 < group:
                    s_refs[(c + ns - 1) % ns][...] = score_values(base + c + ns - 1)
                p_ref = p_refs[c % len(p_refs)]
                m, l, alpha = softmax(s_refs[c % ns], p_ref, m, l)
                add_values(base + c, p_ref[...], alpha)
            return m, l

        carry = (jnp.full((1, 2 * tq), -jnp.inf, F32), jnp.zeros((8, 2 * tq), F32))
        finish(run_groups(one_group, carry)[1])

    if fixed:
        lane = lax.broadcasted_iota(jnp.int32, (1, 2 * tq), 1)
        qsq = jnp.sum(jnp.square(qt2.astype(F32)), axis=0, keepdims=True)
        fixed_shift(jnp.sqrt(qsq * jnp.where(lane < tq, kmax2[0], kmax2[1])))
    else:
        running_max()


def _max_sq_norm(k_ref, ones_ref, lanes):
    k = k_ref[...].astype(F32)
    n = _dot((k * k).astype(BF16), ones_ref[...])
    return jnp.max(n[:, lanes])


def _max_query_sq_norm(q):
    q = q.astype(F32)
    return jnp.max(jnp.sum(q * q, axis=0, keepdims=True))


def _attention_tiles(q_pair, k0_ref, k1_ref, vt_ref, o_ref, scratch, qmax2, kmax2, tq, tk):
    s_refs, p_refs, acc_ref = scratch[:ATTN_S_BUFS], scratch[ATTN_S_BUFS:-1], scratch[-1]
    bound2 = jnp.maximum(qmax2[0] * kmax2[0], qmax2[1] * kmax2[1])
    small = bound2 <= ATTN_FIXED_SHIFT_MAX * ATTN_FIXED_SHIFT_MAX

    def walk(fixed):
        @pl.loop(0, o_ref.shape[1] // tq)
        def _(j):
            lanes = pl.ds(pl.multiple_of(j * tq, tq), tq)
            q0t, q1t = q_pair(lanes)
            _flash_pair(q0t, q1t, k0_ref, k1_ref, vt_ref, o_ref.at[:, lanes], s_refs, p_refs, acc_ref, kmax2, fixed, tk)

    pl.when(small)(lambda: walk(True))
    pl.when(jnp.logical_not(small))(lambda: walk(False))


def _attn_a_body(qt_ref, k_ref, vt_ref, bound_ref, o_ref, *scratch, layer, tq, tk):
    qmax2 = (bound_ref[layer, 0], bound_ref[layer, 0])
    kmax2 = (bound_ref[layer, 1], bound_ref[layer, 1])

    def q_pair(lanes):
        qt = qt_ref[:, lanes]
        zero = jnp.zeros((A_HEAD_DIM, tq), BF16)
        return jnp.concatenate([qt[:A_HEAD_DIM], zero], axis=0), jnp.concatenate([zero, qt[A_HEAD_DIM:]], axis=0)

    _attention_tiles(q_pair, k_ref, None, vt_ref, o_ref, scratch, qmax2, kmax2, tq, tk)


def _attn_b_body(q0t_ref, q1t_ref, k0_ref, k1_ref, vt_ref, ones_ref, o_ref, *scratch, tq, tk):
    kmax2 = (_max_sq_norm(k0_ref, ones_ref, slice(0, LANES)), _max_sq_norm(k1_ref, ones_ref, slice(0, LANES)))
    qmax2 = (_max_query_sq_norm(q0t_ref[...]), _max_query_sq_norm(q1t_ref[...]))
    q_pair = lambda lanes: (q0t_ref[:, lanes], q1t_ref[:, lanes])
    _attention_tiles(q_pair, k0_ref, k1_ref, vt_ref, o_ref, scratch, qmax2, kmax2, tq, tk)


def _attn_scratch(tq, tk):
    return ([pltpu.VMEM((tk, 2 * tq), F32)] * ATTN_S_BUFS + [pltpu.VMEM((tk, 2 * tq), BF16)] * ATTN_P_BUFS
            + [pltpu.VMEM((LANES, tq), F32)])


def _head_ones(head_dim):
    group = jnp.arange(LANES) // head_dim
    return (group[:, None] == group[None, :]).astype(BF16)


def _attention_a(qat, ka, vat, bounds, layer, nseq, s, tq, tk):
    qspec = pl.BlockSpec((LANES, s), lambda b, h: (h, b))
    kspec = pl.BlockSpec((s, LANES), lambda b, h: (b, 0))
    vspec = pl.BlockSpec((LANES, s), lambda b, h: (0, b))
    return pl.pallas_call(
        functools.partial(_attn_a_body, layer=layer, tq=tq, tk=tk),
        grid=(nseq, A_Q_COLS // LANES),
        in_specs=[qspec, kspec, vspec, pl.BlockSpec(memory_space=pltpu.SMEM)],
        out_specs=qspec,
        out_shape=jax.ShapeDtypeStruct(qat.shape, BF16),
        scratch_shapes=_attn_scratch(tq, tk),
        compiler_params=_cparams(("parallel", "parallel")),
        name="attn_a",
    )(qat, ka, vat, bounds)


def _attention_b(qbt, kb, vbt, nseq, s, tq, tk):
    q0 = pl.BlockSpec((LANES, s), lambda b, h: (2 * h, b))
    q1 = pl.BlockSpec((LANES, s), lambda b, h: (2 * h + 1, b))
    k0 = pl.BlockSpec((s, LANES), lambda b, h: (b, 2 * h))
    k1 = pl.BlockSpec((s, LANES), lambda b, h: (b, 2 * h + 1))
    v = pl.BlockSpec((LANES, s), lambda b, h: (h, b))
    o = pl.BlockSpec((LANES, s), lambda b, h: (h, b))
    ones = pl.BlockSpec((LANES, LANES), lambda b, h: (0, 0))
    return pl.pallas_call(
        functools.partial(_attn_b_body, tq=tq, tk=tk),
        grid=(nseq, B_O_COLS // LANES),
        in_specs=[q0, q1, k0, k1, v, ones],
        out_specs=o,
        out_shape=jax.ShapeDtypeStruct(vbt.shape, BF16),
        scratch_shapes=_attn_scratch(tq, tk),
        compiler_params=_cparams(("parallel", "parallel")),
        name="attn_b",
    )(qbt, qbt, kb, kb, vbt, _head_ones(LANES))


def _within(x, d, period, n):
    row = lax.broadcasted_iota(jnp.int32, x.shape, 0)
    return jnp.where((row % period) + d < period, pltpu.roll(x, n - d, 0), pltpu.roll(x, period - d, 0))


def _route(logits_t, bias):
    n = N_EXPERTS
    scores = jax.nn.sigmoid(logits_t)
    biased = scores + bias
    row = lax.broadcasted_iota(jnp.int32, biased.shape, 0)
    pos = row % EXPERTS_PER_GROUP
    rank = jnp.zeros(biased.shape, jnp.int32)
    for d in range(1, EXPERTS_PER_GROUP):
        other = _within(biased, d, EXPERTS_PER_GROUP, n)
        other_pos = (pos + d) % EXPERTS_PER_GROUP
        ahead = (other > biased) | ((other == biased) & (other_pos < pos))
        rank = rank + ahead.astype(jnp.int32)
    top2 = rank < 2
    kept = jnp.where(top2, biased, 0.0)
    gscore = kept
    for d in range(1, EXPERTS_PER_GROUP):
        gscore = gscore + _within(kept, d, EXPERTS_PER_GROUP, n)
    grp = row // EXPERTS_PER_GROUP
    win = jnp.ones(biased.shape, jnp.bool_)
    for d in range(1, N_GROUPS):
        other = pltpu.roll(gscore, n - d * EXPERTS_PER_GROUP, 0)
        other_grp = (grp + d) % N_GROUPS
        win = win & ((other < gscore) | ((other == gscore) & (other_grp > grp)))
    sel = top2 & win
    w = jnp.where(sel, scores, 0.0)
    return w / jnp.sum(w, axis=0, keepdims=True), sel


def _pack_pairs(v):
    n = v.shape[1] // 2
    vb = v.astype(BF16).astype(F32)
    hi = pltpu.bitcast(vb[:, :n], jnp.int32)
    lo = pltpu.bitcast(vb[:, n:], jnp.int32)
    return hi | lax.shift_right_logical(lo, 16)


def _unpack_pairs(w):
    hi = pltpu.bitcast(w & jnp.int32(-65536), F32)
    lo = pltpu.bitcast(lax.shift_left(w, 16), F32)
    return jnp.concatenate([hi, lo], axis=1)


R_E0, R_E1, R_RANK0, R_RANK1, R_W0, R_W1 = range(6)


def _mixer_out_body(oa_ref, ob_ref, ga_ref, gb_ref, x_ref, wba_ref, wbb_ref, wout_ref, gffn_ref, wrh_ref, wrl_ref,
                    br_ref, tri_ref, h_ref, tp_ref, route_ref, route_t_ref, count_ref):
    @pl.when(pl.program_id(0) == 0)
    def _():
        count_ref[...] = jnp.zeros_like(count_ref)

    ma = _dot_tn(oa_ref[...], wba_ref[...])
    mb = _dot_tn(ob_ref[...], wbb_ref[...])
    merged = ga_ref[...].astype(F32) * ma + gb_ref[...].astype(F32) * mb
    h = x_ref[...] + _dot(merged.astype(BF16), wout_ref[...])
    h_ref[...] = h
    t = _rms(h) * gffn_ref[...]
    t_hi, t_lo = _split_bf16(t)
    tp_ref[...] = _pack_pairs(t)
    wrh, wrl = wrh_ref[...], wrl_ref[...]
    logits_t = _dot_nt(wrh, t_hi) + _dot_nt(wrh, t_lo) + _dot_nt(wrl, t_hi)
    gates_t, sel = _route(logits_t, br_ref[...])

    tm = gates_t.shape[1]
    onehot = jnp.where(sel, 1.0, 0.0)
    before = _dot(onehot.astype(BF16), tri_ref[...])
    rank_t = count_ref[:, 0:1] + before
    count_ref[...] = count_ref[...] + jnp.sum(onehot, axis=1, keepdims=True)

    row = lax.broadcasted_iota(jnp.int32, sel.shape, 0).astype(F32)
    e0 = jnp.min(jnp.where(sel, row, float(N_EXPERTS)), axis=0, keepdims=True)
    e1 = jnp.max(jnp.where(sel, row, -1.0), axis=0, keepdims=True)
    pick = lambda v, e: jnp.sum(jnp.where(sel & (row == e), v, 0.0), axis=0, keepdims=True)
    rec = jnp.concatenate([e0, e1, pick(rank_t, e0), pick(rank_t, e1), pick(gates_t, e0), pick(gates_t, e1),
                           jnp.zeros((LANES - 6, tm), F32)], axis=0)
    route_ref[...] = rec.T
    route_t_ref[...] = rec[:8]


def _mixer_out(oa, ob, ga, gb, x, lw, l, tm):
    t = x.shape[0]
    row = lambda n: pl.BlockSpec((tm, n), lambda i: (i, 0))
    full = lambda a: pl.BlockSpec((None,) + a.shape[1:], lambda i: (l,) + (0,) * (a.ndim - 1))
    const = lambda a: pl.BlockSpec(a.shape, lambda i: (0,) * a.ndim)
    col = lambda n: pl.BlockSpec((n, tm), lambda i: (0, i))
    tri = (jnp.arange(tm)[:, None] < jnp.arange(tm)[None, :]).astype(BF16)
    ins = [oa, ob, ga, gb, x, lw["wba"], lw["wbb"], lw["wout"], lw["gffn"], lw["wrh"], lw["wrl"], lw["br"], tri]
    in_specs = [col(A_Q_COLS), col(B_O_COLS), row(D_MODEL), row(D_MODEL), row(D_MODEL),
                full(lw["wba"]), full(lw["wbb"]), full(lw["wout"]), full(lw["gffn"]),
                const(lw["wrh"]), const(lw["wrl"]), const(lw["br"]), const(tri)]
    return pl.pallas_call(
        _mixer_out_body,
        grid=(t // tm,),
        in_specs=in_specs,
        out_specs=[row(D_MODEL), row(D_MODEL // 2), row(LANES), col(8),
                   pl.BlockSpec((N_EXPERTS, LANES), lambda i: (0, 0))],
        out_shape=[jax.ShapeDtypeStruct((t, D_MODEL), F32), jax.ShapeDtypeStruct((t, D_MODEL // 2), jnp.int32),
                   jax.ShapeDtypeStruct((t, LANES), F32), jax.ShapeDtypeStruct((8, t), F32),
                   jax.ShapeDtypeStruct((N_EXPERTS, LANES), F32)],
        compiler_params=_cparams(("arbitrary",)),
        name="mixer_out",
    )(*ins)


def _sc_plan(nrows):
    info = plsc.get_sparse_core_info()
    workers = info.num_cores * info.num_subcores
    per_worker = nrows // workers
    chunk = min(SC_CHUNK, per_worker)
    assert per_worker * workers == nrows and per_worker % chunk == 0 and chunk % 8 == 0, (nrows, workers, chunk)
    return info.num_cores, per_worker, chunk


def _sc_mesh():
    return plsc.VectorSubcoreMesh(core_axis_name="core", subcore_axis_name="subcore")


def _scatter_rows(x, idx0, idx1, nrows):
    t, d = x.shape
    ncores, per_worker, chunk = _sc_plan(t)

    @functools.partial(
        pl.kernel, out_type=jax.ShapeDtypeStruct((nrows, d), x.dtype), mesh=_sc_mesh(), name="moe_dispatch",
        scratch_types=[pltpu.VMEM((chunk,), jnp.int32), pltpu.VMEM((chunk,), jnp.int32), pltpu.VMEM((chunk, d), x.dtype)])
    def run(x_hbm, i0_hbm, i1_hbm, o_hbm, i0_v, i1_v, rows_v):
        worker = lax.axis_index("subcore") * ncores + lax.axis_index("core")

        @pl.loop(0, per_worker // chunk)
        def _(c):
            base = pl.multiple_of(worker * per_worker + c * chunk, chunk)
            pltpu.sync_copy(x_hbm.at[pl.ds(base, chunk)], rows_v)
            pltpu.sync_copy(i0_hbm.at[pl.ds(base, chunk)], i0_v)
            pltpu.sync_copy(i1_hbm.at[pl.ds(base, chunk)], i1_v)
            pltpu.sync_copy(rows_v, o_hbm.at[i0_v])
            pltpu.sync_copy(rows_v, o_hbm.at[i1_v])

    return run(x, idx0, idx1)


def _gather_rows(table, idx):
    m = idx.shape[0]
    d = table.shape[1]
    ncores, per_worker, chunk = _sc_plan(m)

    @functools.partial(
        pl.kernel, out_type=jax.ShapeDtypeStruct((m, d), table.dtype), mesh=_sc_mesh(), name="moe_collect",
        scratch_types=[pltpu.VMEM((chunk,), jnp.int32), pltpu.VMEM((chunk, d), table.dtype)])
    def run(x_hbm, i_hbm, o_hbm, i_v, rows_v):
        worker = lax.axis_index("subcore") * ncores + lax.axis_index("core")

        @pl.loop(0, per_worker // chunk)
        def _(c):
            base = pl.multiple_of(worker * per_worker + c * chunk, chunk)
            pltpu.sync_copy(i_hbm.at[pl.ds(base, chunk)], i_v)
            pltpu.sync_copy(x_hbm.at[i_v], rows_v)
            pltpu.sync_copy(rows_v, o_hbm.at[pl.ds(base, chunk)])

    return run(table, idx)


def _routing_tables(route_t, counts, tr):
    t = route_t.shape[1]
    experts = jnp.arange(N_EXPERTS, dtype=jnp.int32)
    cnt = counts[:, 0].astype(jnp.int32)
    seg_end = jnp.cumsum(cnt)
    seg_start = seg_end - cnt
    lookup = lambda table, idx: jnp.sum(jnp.where(idx[None, :] == experts[:, None], table[:, None], 0), axis=0)
    e = route_t[R_E0:R_E1 + 1].astype(jnp.int32).reshape(2 * t)
    rank = route_t[R_RANK0:R_RANK1 + 1].astype(jnp.int32).reshape(2 * t)
    pos = lookup(seg_start, e) + rank

    n_tiles = 2 * t // tr
    n_visits = n_tiles + N_EXPERTS - 1
    first_tile = seg_start // tr
    last_tile = jnp.maximum(seg_end - 1, 0) // tr
    visits = jnp.where(cnt > 0, last_tile - first_tile + 1, 0)
    visit_end = jnp.cumsum(visits)
    visit_start = visit_end - visits
    g = jnp.arange(n_visits, dtype=jnp.int32)
    valid = g < visit_end[-1]
    ex = jnp.minimum(jnp.sum((g[None, :] >= visit_end[:, None]).astype(jnp.int32), axis=0), N_EXPERTS - 1)
    tile = lookup(first_tile - visit_start, ex) + g
    lo = jnp.clip(lookup(seg_start, ex) - tile * tr, 0, tr)
    hi = jnp.clip(lookup(seg_end, ex) - tile * tr, 0, tr)
    last_ex = jnp.max(jnp.where(cnt > 0, experts, 0))
    tile = jnp.where(valid, tile, n_tiles - 1)
    ex = jnp.where(valid, ex, last_ex)
    lo = jnp.where(valid, lo, 0)
    hi = jnp.where(valid, hi, 0)
    first = jnp.concatenate([jnp.ones((1,), jnp.int32), (tile[1:] != tile[:-1]).astype(jnp.int32)])
    first = jnp.where(valid, first, 0)
    fresh = jnp.concatenate([jnp.ones((1,), jnp.int32), (ex[1:] != ex[:-1]).astype(jnp.int32)])
    return pos, (tile, ex, lo, hi, first, fresh)


def _experts_body(tile_ref, ex_ref, lo_ref, hi_ref, first_ref, fresh_ref, xs_ref, wg_ref, wu_ref, wd_ref, ys_ref,
                  acc_ref, wg_bf, wu_bf, wd_bf):
    g = pl.program_id(0)
    lo, hi = lo_ref[g], hi_ref[g]

    @pl.when(fresh_ref[g] == 1)
    def _():
        wg_bf[...] = wg_ref[...].astype(BF16)
        wu_bf[...] = wu_ref[...].astype(BF16)
        wd_bf[...] = wd_ref[...].astype(BF16)

    @pl.when(hi > lo)
    def _():
        x = _unpack_pairs(xs_ref[...]).astype(BF16)
        a = jax.nn.silu(_dot(x, wg_bf[...])) * _dot(x, wu_bf[...])
        row = lax.broadcasted_iota(jnp.int32, (a.shape[0], 1), 0)
        a = jnp.where((row >= lo) & (row < hi), a, 0.0)
        y = _dot(a.astype(BF16), wd_bf[...])

        @pl.when(first_ref[g] == 1)
        def _():
            acc_ref[...] = y

        @pl.when(first_ref[g] == 0)
        def _():
            acc_ref[...] += y

        ys_ref[...] = _pack_pairs(acc_ref[...])


def _experts(xs, visits, lw, l, tr):
    n_visits = visits[0].shape[0]
    rows = pl.BlockSpec((tr, D_MODEL // 2), lambda g, tile, *_: (tile[g], 0))
    wspec = lambda a: pl.BlockSpec((None, None) + a.shape[2:], lambda g, tile, ex, *_: (l, ex[g], 0, 0))
    return pl.pallas_call(
        _experts_body,
        grid_spec=pltpu.PrefetchScalarGridSpec(
            num_scalar_prefetch=len(visits),
            grid=(n_visits,),
            in_specs=[rows, wspec(lw["wg"]), wspec(lw["wu"]), wspec(lw["wd"])],
            out_specs=rows,
            scratch_shapes=[pltpu.VMEM((tr, D_MODEL), F32)] + [pltpu.VMEM(lw[k].shape[2:], BF16) for k in ("wg", "wu", "wd")],
        ),
        out_shape=jax.ShapeDtypeStruct(xs.shape, jnp.int32),
        compiler_params=_cparams(("arbitrary",)),
        name="experts",
    )(*visits, xs, lw["wg"], lw["wu"], lw["wd"])


def _moe_residual(h_ref, y0_ref, y1_ref, route_ref):
    r = route_ref[...]
    return (h_ref[...] + r[:, R_W0:R_W0 + 1] * _unpack_pairs(y0_ref[...])
            + r[:, R_W1:R_W1 + 1] * _unpack_pairs(y1_ref[...]))


def _final_body(h_ref, y0_ref, y1_ref, route_ref, gfin_ref, o_ref):
    o_ref[...] = _rms(_moe_residual(h_ref, y0_ref, y1_ref, route_ref)) * gfin_ref[...]


def _final(h, y, route, gfin, tm):
    t = h.shape[0]
    row = lambda n: pl.BlockSpec((tm, n), lambda i: (i, 0))
    second = pl.BlockSpec((tm, D_MODEL // 2), lambda i: (i + t // tm, 0))
    return pl.pallas_call(
        _final_body,
        grid=(t // tm,),
        in_specs=[row(D_MODEL), row(D_MODEL // 2), second, row(LANES), pl.BlockSpec(gfin.shape, lambda i: (0, 0))],
        out_specs=row(D_MODEL),
        out_shape=jax.ShapeDtypeStruct(h.shape, F32),
        compiler_params=_cparams(("parallel",)),
        name="final",
    )(h, y, y, route, gfin)


def _moe_rows(tp, route_t, counts, lw, l, tr):
    t = tp.shape[0]
    pos, visits = _routing_tables(route_t, counts, tr)
    xs = _scatter_rows(tp, pos[:t], pos[t:], 2 * t)
    ys = _experts(xs, visits, lw, l, tr)
    return _gather_rows(ys, pos)


def _rope_angles(seq_len, rot_dim):
    rows = seq_len // GRID_W
    row = jnp.broadcast_to(jnp.arange(rows)[:, None], (rows, GRID_W)).reshape(-1).astype(F32)
    col = jnp.broadcast_to(jnp.arange(GRID_W)[None, :], (rows, GRID_W)).reshape(-1).astype(F32)
    axis_dim = rot_dim // 2
    inv_freq = jnp.power(jnp.float32(ROPE_THETA), -jnp.arange(0, axis_dim, 2, dtype=F32) / axis_dim)
    ang = jnp.concatenate([row[:, None] * inv_freq[None, :], col[:, None] * inv_freq[None, :]], axis=-1)
    return jnp.cos(ang), jnp.sin(ang)


def _rope_tables(nseq, seq_len):
    c, s = _rope_angles(seq_len, A_HEAD_DIM)
    ca = jnp.concatenate([c, c, c, c], axis=-1)
    sa = jnp.concatenate([-s, s, -s, s], axis=-1)
    c, s = _rope_angles(seq_len, B_ROPE)
    ones = jnp.ones((seq_len, B_NOPE), F32)
    zeros = jnp.zeros((seq_len, B_NOPE), F32)
    tail = LANES - B_NOPE - B_ROPE
    cb = jnp.concatenate([ones, c, c, ones[:, :tail]], axis=-1)
    sb = jnp.concatenate([zeros, -s, s, zeros[:, :tail]], axis=-1)
    return tuple(jnp.tile(a, (nseq, 1)) for a in (ca, sa, cb, sb))


def _prepare_weights(norm_mix, w_in, a_q_norm, a_k_norm, b_q_norm, b_kv_norm, w_q_up, w_kv_up, w_branch_a,
                     w_branch_b, w_out, norm_ffn, w_router, b_router, w_gate, w_up, w_down):
    depth = w_in.shape[0]
    order = jnp.array(A_HEAD_ORDER)
    parts = []
    start = 0
    for n in (A_Q_COLS, A_KV_COLS, A_KV_COLS, B_Q_RANK, B_KV_RANK, B_ROPE, D_MODEL, D_MODEL):
        parts.append(w_in[..., start:start + n])
        start += n
    qa, ka, va, cq, ckv, kr, ga, gb = parts
    qa = qa.reshape(depth, D_MODEL, A_HEADS, A_HEAD_DIM)[:, :, order].reshape(depth, D_MODEL, A_Q_COLS)
    kr = jnp.pad(kr, ((0, 0), (0, 0), (B_NOPE, LANES - B_NOPE - B_ROPE)))
    win = jnp.concatenate([qa, ka, va, cq, ckv, kr, ga, gb], axis=-1).astype(BF16)

    wq = w_q_up.reshape(depth, B_Q_RANK, B_HEADS, B_QK_DIM)
    wq = jnp.pad(wq, ((0, 0), (0, 0), (0, 0), (0, LANES - B_QK_DIM))).reshape(depth, B_Q_RANK, B_PAD_COLS)
    wkv = w_kv_up.reshape(depth, B_KV_RANK, B_HEADS, B_NOPE + B_V)
    wk = jnp.pad(wkv[..., :B_NOPE], ((0, 0), (0, 0), (0, 0), (0, LANES - B_NOPE))).reshape(depth, B_KV_RANK, B_PAD_COLS)
    wv = wkv[..., B_NOPE:].reshape(depth, B_KV_RANK, B_O_COLS)
    wba = w_branch_a.reshape(depth, A_HEADS, A_HEAD_DIM, D_MODEL)[:, order].reshape(depth, A_Q_COLS, D_MODEL)

    group = jnp.arange(A_Q_COLS) // A_HEAD_DIM
    bd = (group[:, None] == group[None, :]).astype(BF16)
    wr_t = w_router.T
    wrh, wrl = _split_bf16(wr_t)
    vec = lambda a: a[:, None, :]
    gq = jnp.tile(a_q_norm, (1, A_HEADS)) * (A_HEAD_DIM ** -0.5 * LOG2E)
    norm2_bound = lambda g: 1.02 * A_HEAD_DIM * jnp.max(g * g, axis=-1)
    bounds_a = jnp.stack([norm2_bound(gq), norm2_bound(a_k_norm)], axis=-1).astype(F32)
    return dict(
        bounds_a=bounds_a,
        gmix=vec(norm_mix), win=win,
        gq=vec(gq), gk=vec(jnp.tile(a_k_norm, (1, A_KV_HEADS))),
        gbq=vec(b_q_norm), gbkv=vec(b_kv_norm),
        wq=wq.astype(BF16), wk=wk.astype(BF16), wv=wv.astype(BF16), bd=bd,
        wba=wba.astype(BF16), wbb=w_branch_b.astype(BF16), wout=w_out.astype(BF16), gffn=vec(norm_ffn),
        wrh=wrh, wrl=wrl, br=b_router[:, None].astype(F32),
        wg=w_gate, wu=w_up, wd=w_down,
    )


def _pick(n, candidates):
    for c in candidates:
        if n % c == 0:
            return c
    raise ValueError(f"no tile in {candidates} divides {n}")


def _trunk(x3, lw, norm_final):
    nseq, s, _ = x3.shape
    t = nseq * s
    x = x3.reshape(t, D_MODEL)
    tabs = _rope_tables(nseq, s)
    depth = lw["win"].shape[0]
    tm = _pick(t, (1024, 512, 256))
    tm_in = _pick(t, (512, 256))
    tk = _pick(s, (512,))
    tq = _pick(s, tuple(c for c in (1024, 512, 256) if c * (s // tk) <= ATTN_UNROLL_BUDGET) or (256,))
    moe_out = None
    for l in range(depth):
        x_new, qa, ka, va, qb, kb, vb, ga, gb = _mixer_in(x, moe_out, lw, l, tabs, tm_in)
        x = x if x_new is None else x_new
        oa = _attention_a(qa, ka, va, lw["bounds_a"], l, nseq, s, tq, tk)
        ob = _attention_b(qb, kb, vb, nseq, s, tq, tk)
        h, tp, route, route_t, counts = _mixer_out(oa, ob, ga, gb, x, lw, l, tm)
        moe_out = (h, _moe_rows(tp, route_t, counts, lw, l, EXPERT_ROWS), route)
        x = None
    return _final(*moe_out, norm_final[None, :], tm).reshape(x3.shape)


def kernel(x_prompt, x_sample, norm_mix, w_in, a_q_norm, a_k_norm, b_q_norm, b_kv_norm, w_q_up, w_kv_up, w_branch_a,
           w_branch_b, w_out, norm_ffn, w_router, b_router, w_gate, w_up, w_down, norm_final):
    lw = _prepare_weights(norm_mix, w_in, a_q_norm, a_k_norm, b_q_norm, b_kv_norm, w_q_up, w_kv_up, w_branch_a,
                          w_branch_b, w_out, norm_ffn, w_router, b_router, w_gate, w_up, w_down)
    return _trunk(x_prompt, lw, norm_final), _trunk(x_sample, lw, norm_final)
```

```python
import functools

import jax
import jax.numpy as jnp
from jax import lax
from jax.experimental import pallas as pl
from jax.experimental.pallas import tpu as pltpu
from jax.experimental.pallas import tpu_sc as plsc

F32 = jnp.float32
BF16 = jnp.bfloat16

D_MODEL = 1024
GRID_W = 64
ROPE_THETA = 10000.0
EPS = 1e-6
A_HEADS = 8
A_KV_HEADS = 2
A_HEAD_DIM = 64
B_HEADS = 8
B_Q_RANK = 384
B_KV_RANK = 256
B_NOPE = 64
B_ROPE = 32
B_V = 64
N_EXPERTS = 16
N_GROUPS = 4
EXPERTS_PER_GROUP = N_EXPERTS // N_GROUPS
D_EXPERT = 512
A_Q_COLS = A_HEADS * A_HEAD_DIM
A_KV_COLS = A_KV_HEADS * A_HEAD_DIM
B_QK_DIM = B_NOPE + B_ROPE
B_O_COLS = B_HEADS * B_V

LANES = 128
LOG2E = 1.4426950408889634

C_QA = 0
C_KA = C_QA + A_Q_COLS
C_VA = C_KA + A_KV_COLS
C_CQ = C_VA + A_KV_COLS
C_CKV = C_CQ + B_Q_RANK
C_KR = C_CKV + B_KV_RANK
C_GA = C_KR + LANES
C_GB = C_GA + D_MODEL
C_END = C_GB + D_MODEL
B_PAD_COLS = B_HEADS * LANES

A_HEAD_ORDER = (0, 4, 1, 5, 2, 6, 3, 7)

VMEM_LIMIT = 56 * 1024 * 1024

ATTN_GROUP = 16
ATTN_UNROLL_BUDGET = 8192
ATTN_S_BUFS = 3
ATTN_P_BUFS = 2
ATTN_VALUE_LAG = 2
ATTN_SLAB = 32
ATTN_FIXED_SHIFT_MAX = 48.0
EXPERT_ROWS = 512
SC_CHUNK = 128


def _cparams(sem):
    return pltpu.CompilerParams(dimension_semantics=sem, vmem_limit_bytes=VMEM_LIMIT)


def _dot(a, b):
    return jnp.dot(a, b, preferred_element_type=F32)


def _dot_nt(a, b):
    return lax.dot_general(a, b, (((1,), (1,)), ((), ())), preferred_element_type=F32)


def _dot_tn(a, b):
    return lax.dot_general(a, b, (((0,), (0,)), ((), ())), preferred_element_type=F32)


def _rms(x):
    return x * lax.rsqrt(jnp.mean(x * x, axis=-1, keepdims=True) + EPS)


def _split_bf16(x):
    hi = x.astype(BF16)
    lo = (x - hi.astype(F32)).astype(BF16)
    return hi, lo


def _group_mean_sq(v, bd):
    return _dot((v * v).astype(BF16), bd) * (1.0 / A_HEAD_DIM)


def _rope_a(v, cos, sin):
    n = v.shape[-1]
    lane = lax.broadcasted_iota(jnp.int32, v.shape, 1)
    low = (lane % A_HEAD_DIM) < (A_HEAD_DIM // 2)
    swapped = jnp.where(low, pltpu.roll(v, n - A_HEAD_DIM // 2, 1), pltpu.roll(v, A_HEAD_DIM // 2, 1))
    return v * cos + swapped * sin


def _rope_b(v, cos, sin):
    n = v.shape[-1]
    lane = lax.broadcasted_iota(jnp.int32, v.shape, 1)
    low = (lane % LANES) < (B_NOPE + B_ROPE // 2)
    swapped = jnp.where(low, pltpu.roll(v, n - B_ROPE // 2, 1), pltpu.roll(v, B_ROPE // 2, 1))
    return v * cos + swapped * sin


def _mixer_in_body(*refs, fused):
    if fused:
        h_ref, y0_ref, y1_ref, route_ref, *refs = refs
    else:
        x_ref, *refs = refs
    (gmix_ref, win_ref, ca_ref, sa_ref, cb_ref, sb_ref, gq_ref, gk_ref, gbq_ref, gbkv_ref, wq_ref, wk_ref, wv_ref,
     bd_ref, *outs) = refs
    if fused:
        x_out_ref, *outs = outs
        x = _moe_residual(h_ref, y0_ref, y1_ref, route_ref)
        x_out_ref[...] = x
    else:
        x = x_ref[...]
    qat_ref, ka_ref, vat_ref, qbt_ref, kb_ref, vbt_ref, ga_ref, gb_ref = outs
    hb = (_rms(x) * gmix_ref[...]).astype(BF16)
    u = _dot(hb, win_ref[:, C_QA:C_GA])
    bd = bd_ref[...]
    ca, sa = ca_ref[...], sa_ref[...]
    cb, sb = cb_ref[...], sb_ref[...]

    qa = u[:, C_QA:C_KA]
    qa = qa * lax.rsqrt(_group_mean_sq(qa, bd) + EPS) * gq_ref[...]
    qa = _rope_a(qa, jnp.concatenate([ca] * (A_Q_COLS // LANES), axis=1),
                 jnp.concatenate([sa] * (A_Q_COLS // LANES), axis=1))
    qat_ref[...] = qa.T.astype(BF16)
    ka = u[:, C_KA:C_VA]
    ka = ka * lax.rsqrt(_group_mean_sq(ka, bd[:A_KV_COLS, :A_KV_COLS]) + EPS) * gk_ref[...]
    ka_ref[...] = _rope_a(ka, ca, sa).astype(BF16)
    vat_ref[...] = u[:, C_VA:C_CQ].T.astype(BF16)

    cq = (_rms(u[:, C_CQ:C_CKV]) * gbq_ref[...]).astype(BF16)
    qb = _dot(cq, wq_ref[...])
    qb = _rope_b(qb, jnp.concatenate([cb] * B_HEADS, axis=1), jnp.concatenate([sb] * B_HEADS, axis=1))
    qbt_ref[...] = (qb * (B_QK_DIM ** -0.5 * LOG2E)).T.astype(BF16)
    ckv = (_rms(u[:, C_CKV:C_KR]) * gbkv_ref[...]).astype(BF16)
    kr = _rope_b(u[:, C_KR:C_GA], cb, sb)
    kb_ref[...] = (_dot(ckv, wk_ref[...]) + jnp.concatenate([kr] * B_HEADS, axis=1)).astype(BF16)
    vbt_ref[...] = _dot(ckv, wv_ref[...]).T.astype(BF16)

    g = jax.nn.sigmoid(_dot(hb, win_ref[:, C_GA:C_END]))
    ga_ref[...] = g[:, :D_MODEL].astype(BF16)
    gb_ref[...] = g[:, D_MODEL:].astype(BF16)


def _mixer_in(x, moe_out, lw, l, tabs, tm):
    fused = x is None
    t = moe_out[0].shape[0] if fused else x.shape[0]
    row = lambda n: pl.BlockSpec((tm, n), lambda i: (i, 0))
    full = lambda a: pl.BlockSpec((None,) + a.shape[1:], lambda i: (l,) + (0,) * (a.ndim - 1))
    const = lambda a: pl.BlockSpec(a.shape, lambda i: (0,) * a.ndim)
    col = lambda n: pl.BlockSpec((n, tm), lambda i: (0, i))
    ca, sa, cb, sb = tabs
    if fused:
        h, y, route = moe_out
        second = pl.BlockSpec((tm, D_MODEL // 2), lambda i: (i + t // tm, 0))
        ins, in_specs = [h, y, y, route], [row(D_MODEL), row(D_MODEL // 2), second, row(LANES)]
    else:
        ins, in_specs = [x], [row(D_MODEL)]
    ins += [lw["gmix"], lw["win"], ca, sa, cb, sb, lw["gq"], lw["gk"], lw["gbq"], lw["gbkv"],
            lw["wq"], lw["wk"], lw["wv"], lw["bd"]]
    in_specs += [full(lw["gmix"]), full(lw["win"]), row(LANES), row(LANES), row(LANES), row(LANES),
                 full(lw["gq"]), full(lw["gk"]), full(lw["gbq"]), full(lw["gbkv"]),
                 full(lw["wq"]), full(lw["wk"]), full(lw["wv"]), const(lw["bd"])]
    widths = [A_Q_COLS, A_KV_COLS, A_KV_COLS, B_PAD_COLS, B_PAD_COLS, B_O_COLS, D_MODEL, D_MODEL]
    transposed = [True, False, True, True, False, True, False, False]
    out_specs = [col(n) if tr else row(n) for n, tr in zip(widths, transposed)]
    out_shape = [jax.ShapeDtypeStruct((n, t) if tr else (t, n), BF16) for n, tr in zip(widths, transposed)]
    if fused:
        out_specs.insert(0, row(D_MODEL))
        out_shape.insert(0, jax.ShapeDtypeStruct((t, D_MODEL), F32))
    outs = pl.pallas_call(
        functools.partial(_mixer_in_body, fused=fused),
        grid=(t // tm,),
        in_specs=in_specs,
        out_specs=out_specs,
        out_shape=out_shape,
        compiler_params=_cparams(("parallel",)),
        name="mixer_in",
    )(*ins)
    return tuple(outs) if fused else (None,) + tuple(outs)


def _flash_pair(q0t, q1t, k0_ref, k1_ref, vt_ref, o_ref, s_refs, p_refs, acc_ref, kmax2, fixed, tk):
    tq = q0t.shape[1]
    nk = vt_ref.shape[1] // tk
    group = min(nk, ATTN_GROUP)
    ns = len(s_refs)
    qt2 = jnp.concatenate([q0t, q1t], axis=1)

    def score_values(j):
        off = pl.multiple_of(j * tk, tk)
        if k1_ref is None:
            return _dot(k0_ref[pl.ds(off, tk), :], qt2)
        return jnp.concatenate([_dot(k0_ref[pl.ds(off, tk), :], q0t), _dot(k1_ref[pl.ds(off, tk), :], q1t)], axis=1)

    def add_values(j, p, alpha):
        off = pl.multiple_of(j * tk, tk)
        for rows, lanes in ((slice(0, B_V), slice(0, tq)), (slice(B_V, LANES), slice(tq, 2 * tq))):
            prev = acc_ref[rows, :] if alpha is None else alpha[:, lanes] * acc_ref[rows, :]
            acc_ref[rows, :] = prev + _dot(vt_ref[rows, pl.ds(off, tk)], p[:, lanes])

    def finish(l):
        l = jnp.sum(l, axis=0, keepdims=True)
        o_ref[:B_V, :] = (acc_ref[:B_V, :] / l[:, :tq]).astype(o_ref.dtype)
        o_ref[B_V:, :] = (acc_ref[B_V:, :] / l[:, tq:]).astype(o_ref.dtype)

    def run_groups(one_group, carry):
        acc_ref[...] = jnp.zeros_like(acc_ref)
        return one_group(0, carry) if nk == group else lax.fori_loop(0, nk // group, one_group, carry)

    def fixed_shift(shift):
        wide = 2 * LANES
        tiles = [(n * wide, (n * wide) // tq) for n in range(2 * tq // wide)]

        def value_tile(j, p, lane0, head):
            off = pl.multiple_of(j * tk, tk)
            rows = slice(head * B_V, (head + 1) * B_V)
            lanes = slice(lane0 - head * tq, lane0 - head * tq + wide)
            acc_ref[rows, lanes] += _dot(vt_ref[rows, pl.ds(off, tk)], p)

        def one_group(g, l):
            l = list(l)
            pending = []
            for c in range(group):
                j = g * group + c
                off = pl.multiple_of(j * tk, tk)
                for n, (lane0, head) in enumerate(tiles):
                    k_ref = k0_ref if (k1_ref is None or head == 0) else k1_ref
                    s = _dot(k_ref[pl.ds(off, tk), :], qt2[:, lane0:lane0 + wide])
                    if len(pending) >= ATTN_VALUE_LAG:
                        value_tile(*pending.pop(0))
                    p = jnp.exp2(s - shift[:, lane0:lane0 + wide])
                    l[n] = l[n] + p.reshape(tk // 8, 8, wide).sum(axis=0)
                    pending.append((j, p.astype(BF16), lane0, head))
            for item in pending:
                value_tile(*item)
            return tuple(l)

        l = run_groups(one_group, tuple(jnp.zeros((8, wide), F32) for _ in tiles))
        finish(jnp.concatenate(l, axis=1))

    def softmax(s_ref, p_ref, m, l):
        slabs = [pl.ds(r, ATTN_SLAB) for r in range(0, tk, ATTN_SLAB)]
        fold = lambda x: x.reshape(ATTN_SLAB // 8, 8, 2 * tq)
        mx = fold(s_ref[slabs[0], :]).max(axis=0)
        for sl in slabs[1:]:
            mx = jnp.maximum(mx, fold(s_ref[sl, :]).max(axis=0))
        m_new = jnp.maximum(m, jnp.max(mx, axis=0, keepdims=True))
        alpha = jnp.exp2(m - m_new)
        l = alpha * l
        for sl in slabs:
            p = jnp.exp2(s_ref[sl, :] - m_new)
            p_ref[sl, :] = p.astype(BF16)
            l = l + fold(p).sum(axis=0)
        return m_new, l, alpha

    def running_max():
        def one_group(g, carry):
            m, l = carry
            base = g * group
            for c in range(min(ns - 1, group)):
                s_refs[c % ns][...] = score_values(base + c)
            for c in range(group):
                if c + ns - 1 < group:
                    s_refs[(c + ns - 1) % ns][...] = score_values(base + c + ns - 1)
                p_ref = p_refs[c % len(p_refs)]
                m, l, alpha = softmax(s_refs[c % ns], p_ref, m, l)
                add_values(base + c, p_ref[...], alpha)
            return m, l

        carry = (jnp.full((1, 2 * tq), -jnp.inf, F32), jnp.zeros((8, 2 * tq), F32))
        finish(run_groups(one_group, carry)[1])

    if fixed:
        lane = lax.broadcasted_iota(jnp.int32, (1, 2 * tq), 1)
        qsq = jnp.sum(jnp.square(qt2.astype(F32)), axis=0, keepdims=True)
        fixed_shift(jnp.sqrt(qsq * jnp.where(lane < tq, kmax2[0], kmax2[1])))
    else:
        running_max()


def _max_sq_norm(k_ref, ones_ref, lanes):
    k = k_ref[...].astype(F32)
    n = _dot((k * k).astype(BF16), ones_ref[...])
    return jnp.max(n[:, lanes])


def _max_query_sq_norm(q):
    q = q.astype(F32)
    return jnp.max(jnp.sum(q * q, axis=0, keepdims=True))


def _attention_tiles(q_pair, k0_ref, k1_ref, vt_ref, o_ref, scratch, qmax2, kmax2, tq, tk):
    s_refs, p_refs, acc_ref = scratch[:ATTN_S_BUFS], scratch[ATTN_S_BUFS:-1], scratch[-1]
    bound2 = jnp.maximum(qmax2[0] * kmax2[0], qmax2[1] * kmax2[1])
    small = bound2 <= ATTN_FIXED_SHIFT_MAX * ATTN_FIXED_SHIFT_MAX

    def walk(fixed):
        @pl.loop(0, o_ref.shape[1] // tq)
        def _(j):
            lanes = pl.ds(pl.multiple_of(j * tq, tq), tq)
            q0t, q1t = q_pair(lanes)
            _flash_pair(q0t, q1t, k0_ref, k1_ref, vt_ref, o_ref.at[:, lanes], s_refs, p_refs, acc_ref, kmax2, fixed, tk)

    pl.when(small)(lambda: walk(True))
    pl.when(jnp.logical_not(small))(lambda: walk(False))


def _attn_a_body(qt_ref, k_ref, vt_ref, bound_ref, o_ref, *scratch, layer, tq, tk):
    qmax2 = (bound_ref[layer, 0], bound_ref[layer, 0])
    kmax2 = (bound_ref[layer, 1], bound_ref[layer, 1])

    def q_pair(lanes):
        qt = qt_ref[:, lanes]
        zero = jnp.zeros((A_HEAD_DIM, tq), BF16)
        return jnp.concatenate([qt[:A_HEAD_DIM], zero], axis=0), jnp.concatenate([zero, qt[A_HEAD_DIM:]], axis=0)

    _attention_tiles(q_pair, k_ref, None, vt_ref, o_ref, scratch, qmax2, kmax2, tq, tk)


def _attn_b_body(q0t_ref, q1t_ref, k0_ref, k1_ref, vt_ref, ones_ref, o_ref, *scratch, tq, tk):
    kmax2 = (_max_sq_norm(k0_ref, ones_ref, slice(0, LANES)), _max_sq_norm(k1_ref, ones_ref, slice(0, LANES)))
    qmax2 = (_max_query_sq_norm(q0t_ref[...]), _max_query_sq_norm(q1t_ref[...]))
    q_pair = lambda lanes: (q0t_ref[:, lanes], q1t_ref[:, lanes])
    _attention_tiles(q_pair, k0_ref, k1_ref, vt_ref, o_ref, scratch, qmax2, kmax2, tq, tk)


def _attn_scratch(tq, tk):
    return ([pltpu.VMEM((tk, 2 * tq), F32)] * ATTN_S_BUFS + [pltpu.VMEM((tk, 2 * tq), BF16)] * ATTN_P_BUFS
            + [pltpu.VMEM((LANES, tq), F32)])


def _head_ones(head_dim):
    group = jnp.arange(LANES) // head_dim
    return (group[:, None] == group[None, :]).astype(BF16)


def _attention_a(qat, ka, vat, bounds, layer, nseq, s, tq, tk):
    qspec = pl.BlockSpec((LANES, s), lambda b, h: (h, b))
    kspec = pl.BlockSpec((s, LANES), lambda b, h: (b, 0))
    vspec = pl.BlockSpec((LANES, s), lambda b, h: (0, b))
    return pl.pallas_call(
        functools.partial(_attn_a_body, layer=layer, tq=tq, tk=tk),
        grid=(nseq, A_Q_COLS // LANES),
        in_specs=[qspec, kspec, vspec, pl.BlockSpec(memory_space=pltpu.SMEM)],
        out_specs=qspec,
        out_shape=jax.ShapeDtypeStruct(qat.shape, BF16),
        scratch_shapes=_attn_scratch(tq, tk),
        compiler_params=_cparams(("parallel", "parallel")),
        name="attn_a",
    )(qat, ka, vat, bounds)


def _attention_b(qbt, kb, vbt, nseq, s, tq, tk):
    q0 = pl.BlockSpec((LANES, s), lambda b, h: (2 * h, b))
    q1 = pl.BlockSpec((LANES, s), lambda b, h: (2 * h + 1, b))
    k0 = pl.BlockSpec((s, LANES), lambda b, h: (b, 2 * h))
    k1 = pl.BlockSpec((s, LANES), lambda b, h: (b, 2 * h + 1))
    v = pl.BlockSpec((LANES, s), lambda b, h: (h, b))
    o = pl.BlockSpec((LANES, s), lambda b, h: (h, b))
    ones = pl.BlockSpec((LANES, LANES), lambda b, h: (0, 0))
    return pl.pallas_call(
        functools.partial(_attn_b_body, tq=tq, tk=tk),
        grid=(nseq, B_O_COLS // LANES),
        in_specs=[q0, q1, k0, k1, v, ones],
        out_specs=o,
        out_shape=jax.ShapeDtypeStruct(vbt.shape, BF16),
        scratch_shapes=_attn_scratch(tq, tk),
        compiler_params=_cparams(("parallel", "parallel")),
        name="attn_b",
    )(qbt, qbt, kb, kb, vbt, _head_ones(LANES))


def _within(x, d, period, n):
    row = lax.broadcasted_iota(jnp.int32, x.shape, 0)
    return jnp.where((row % period) + d < period, pltpu.roll(x, n - d, 0), pltpu.roll(x, period - d, 0))


def _route(logits_t, bias):
    n = N_EXPERTS
    scores = jax.nn.sigmoid(logits_t)
    biased = scores + bias
    row = lax.broadcasted_iota(jnp.int32, biased.shape, 0)
    pos = row % EXPERTS_PER_GROUP
    rank = jnp.zeros(biased.shape, jnp.int32)
    for d in range(1, EXPERTS_PER_GROUP):
        other = _within(biased, d, EXPERTS_PER_GROUP, n)
        other_pos = (pos + d) % EXPERTS_PER_GROUP
        ahead = (other > biased) | ((other == biased) & (other_pos < pos))
        rank = rank + ahead.astype(jnp.int32)
    top2 = rank < 2
    kept = jnp.where(top2, biased, 0.0)
    gscore = kept
    for d in range(1, EXPERTS_PER_GROUP):
        gscore = gscore + _within(kept, d, EXPERTS_PER_GROUP, n)
    grp = row // EXPERTS_PER_GROUP
    win = jnp.ones(biased.shape, jnp.bool_)
    for d in range(1, N_GROUPS):
        other = pltpu.roll(gscore, n - d * EXPERTS_PER_GROUP, 0)
        other_grp = (grp + d) % N_GROUPS
        win = win & ((other < gscore) | ((other == gscore) & (other_grp > grp)))
    sel = top2 & win
    w = jnp.where(sel, scores, 0.0)
    return w / jnp.sum(w, axis=0, keepdims=True), sel


def _pack_pairs(v):
    n = v.shape[1] // 2
    vb = v.astype(BF16).astype(F32)
    hi = pltpu.bitcast(vb[:, :n], jnp.int32)
    lo = pltpu.bitcast(vb[:, n:], jnp.int32)
    return hi | lax.shift_right_logical(lo, 16)


def _unpack_pairs(w):
    hi = pltpu.bitcast(w & jnp.int32(-65536), F32)
    lo = pltpu.bitcast(lax.shift_left(w, 16), F32)
    return jnp.concatenate([hi, lo], axis=1)


R_E0, R_E1, R_RANK0, R_RANK1, R_W0, R_W1 = range(6)


def _mixer_out_body(oa_ref, ob_ref, ga_ref, gb_ref, x_ref, wba_ref, wbb_ref, wout_ref, gffn_ref, wrh_ref, wrl_ref,
                    br_ref, tri_ref, h_ref, tp_ref, route_ref, route_t_ref, count_ref):
    @pl.when(pl.program_id(0) == 0)
    def _():
        count_ref[...] = jnp.zeros_like(count_ref)

    ma = _dot_tn(oa_ref[...], wba_ref[...])
    mb = _dot_tn(ob_ref[...], wbb_ref[...])
    merged = ga_ref[...].astype(F32) * ma + gb_ref[...].astype(F32) * mb
    h = x_ref[...] + _dot(merged.astype(BF16), wout_ref[...])
    h_ref[...] = h
    t = _rms(h) * gffn_ref[...]
    t_hi, t_lo = _split_bf16(t)
    tp_ref[...] = _pack_pairs(t)
    wrh, wrl = wrh_ref[...], wrl_ref[...]
    logits_t = _dot_nt(wrh, t_hi) + _dot_nt(wrh, t_lo) + _dot_nt(wrl, t_hi)
    gates_t, sel = _route(logits_t, br_ref[...])

    tm = gates_t.shape[1]
    onehot = jnp.where(sel, 1.0, 0.0)
    before = _dot(onehot.astype(BF16), tri_ref[...])
    rank_t = count_ref[:, 0:1] + before
    count_ref[...] = count_ref[...] + jnp.sum(onehot, axis=1, keepdims=True)

    row = lax.broadcasted_iota(jnp.int32, sel.shape, 0).astype(F32)
    e0 = jnp.min(jnp.where(sel, row, float(N_EXPERTS)), axis=0, keepdims=True)
    e1 = jnp.max(jnp.where(sel, row, -1.0), axis=0, keepdims=True)
    pick = lambda v, e: jnp.sum(jnp.where(sel & (row == e), v, 0.0), axis=0, keepdims=True)
    rec = jnp.concatenate([e0, e1, pick(rank_t, e0), pick(rank_t, e1), pick(gates_t, e0), pick(gates_t, e1),
                           jnp.zeros((LANES - 6, tm), F32)], axis=0)
    route_ref[...] = rec.T
    route_t_ref[...] = rec[:8]


def _mixer_out(oa, ob, ga, gb, x, lw, l, tm):
    t = x.shape[0]
    row = lambda n: pl.BlockSpec((tm, n), lambda i: (i, 0))
    full = lambda a: pl.BlockSpec((None,) + a.shape[1:], lambda i: (l,) + (0,) * (a.ndim - 1))
    const = lambda a: pl.BlockSpec(a.shape, lambda i: (0,) * a.ndim)
    col = lambda n: pl.BlockSpec((n, tm), lambda i: (0, i))
    tri = (jnp.arange(tm)[:, None] < jnp.arange(tm)[None, :]).astype(BF16)
    ins = [oa, ob, ga, gb, x, lw["wba"], lw["wbb"], lw["wout"], lw["gffn"], lw["wrh"], lw["wrl"], lw["br"], tri]
    in_specs = [col(A_Q_COLS), col(B_O_COLS), row(D_MODEL), row(D_MODEL), row(D_MODEL),
                full(lw["wba"]), full(lw["wbb"]), full(lw["wout"]), full(lw["gffn"]),
                const(lw["wrh"]), const(lw["wrl"]), const(lw["br"]), const(tri)]
    return pl.pallas_call(
        _mixer_out_body,
        grid=(t // tm,),
        in_specs=in_specs,
        out_specs=[row(D_MODEL), row(D_MODEL // 2), row(LANES), col(8),
                   pl.BlockSpec((N_EXPERTS, LANES), lambda i: (0, 0))],
        out_shape=[jax.ShapeDtypeStruct((t, D_MODEL), F32), jax.ShapeDtypeStruct((t, D_MODEL // 2), jnp.int32),
                   jax.ShapeDtypeStruct((t, LANES), F32), jax.ShapeDtypeStruct((8, t), F32),
                   jax.ShapeDtypeStruct((N_EXPERTS, LANES), F32)],
        compiler_params=_cparams(("arbitrary",)),
        name="mixer_out",
    )(*ins)


def _sc_plan(nrows):
    info = plsc.get_sparse_core_info()
    workers = info.num_cores * info.num_subcores
    per_worker = nrows // workers
    chunk = min(SC_CHUNK, per_worker)
    assert per_worker * workers == nrows and per_worker % chunk == 0 and chunk % 8 == 0, (nrows, workers, chunk)
    return info.num_cores, per_worker, chunk


def _sc_mesh():
    return plsc.VectorSubcoreMesh(core_axis_name="core", subcore_axis_name="subcore")


def _scatter_rows(x, idx0, idx1, nrows):
    t, d = x.shape
    ncores, per_worker, chunk = _sc_plan(t)

    @functools.partial(
        pl.kernel, out_type=jax.ShapeDtypeStruct((nrows, d), x.dtype), mesh=_sc_mesh(), name="moe_dispatch",
        scratch_types=[pltpu.VMEM((chunk,), jnp.int32), pltpu.VMEM((chunk,), jnp.int32), pltpu.VMEM((chunk, d), x.dtype)])
    def run(x_hbm, i0_hbm, i1_hbm, o_hbm, i0_v, i1_v, rows_v):
        worker = lax.axis_index("subcore") * ncores + lax.axis_index("core")

        @pl.loop(0, per_worker // chunk)
        def _(c):
            base = pl.multiple_of(worker * per_worker + c * chunk, chunk)
            pltpu.sync_copy(x_hbm.at[pl.ds(base, chunk)], rows_v)
            pltpu.sync_copy(i0_hbm.at[pl.ds(base, chunk)], i0_v)
            pltpu.sync_copy(i1_hbm.at[pl.ds(base, chunk)], i1_v)
            pltpu.sync_copy(rows_v, o_hbm.at[i0_v])
            pltpu.sync_copy(rows_v, o_hbm.at[i1_v])

    return run(x, idx0, idx1)


def _gather_rows(table, idx):
    m = idx.shape[0]
    d = table.shape[1]
    ncores, per_worker, chunk = _sc_plan(m)

    @functools.partial(
        pl.kernel, out_type=jax.ShapeDtypeStruct((m, d), table.dtype), mesh=_sc_mesh(), name="moe_collect",
        scratch_types=[pltpu.VMEM((chunk,), jnp.int32), pltpu.VMEM((chunk, d), table.dtype)])
    def run(x_hbm, i_hbm, o_hbm, i_v, rows_v):
        worker = lax.axis_index("subcore") * ncores + lax.axis_index("core")

        @pl.loop(0, per_worker // chunk)
        def _(c):
            base = pl.multiple_of(worker * per_worker + c * chunk, chunk)
            pltpu.sync_copy(i_hbm.at[pl.ds(base, chunk)], i_v)
            pltpu.sync_copy(x_hbm.at[i_v], rows_v)
            pltpu.sync_copy(rows_v, o_hbm.at[pl.ds(base, chunk)])

    return run(table, idx)


def _routing_tables(route_t, counts, tr):
    t = route_t.shape[1]
    experts = jnp.arange(N_EXPERTS, dtype=jnp.int32)
    cnt = counts[:, 0].astype(jnp.int32)
    seg_end = jnp.cumsum(cnt)
    seg_start = seg_end - cnt
    lookup = lambda table, idx: jnp.sum(jnp.where(idx[None, :] == experts[:, None], table[:, None], 0), axis=0)
    e = route_t[R_E0:R_E1 + 1].astype(jnp.int32).reshape(2 * t)
    rank = route_t[R_RANK0:R_RANK1 + 1].astype(jnp.int32).reshape(2 * t)
    pos = lookup(seg_start, e) + rank

    n_tiles = 2 * t // tr
    n_visits = n_tiles + N_EXPERTS - 1
    first_tile = seg_start // tr
    last_tile = jnp.maximum(seg_end - 1, 0) // tr
    visits = jnp.where(cnt > 0, last_tile - first_tile + 1, 0)
    visit_end = jnp.cumsum(visits)
    visit_start = visit_end - visits
    g = jnp.arange(n_visits, dtype=jnp.int32)
    valid = g < visit_end[-1]
    ex = jnp.minimum(jnp.sum((g[None, :] >= visit_end[:, None]).astype(jnp.int32), axis=0), N_EXPERTS - 1)
    tile = lookup(first_tile - visit_start, ex) + g
    lo = jnp.clip(lookup(seg_start, ex) - tile * tr, 0, tr)
    hi = jnp.clip(lookup(seg_end, ex) - tile * tr, 0, tr)
    last_ex = jnp.max(jnp.where(cnt > 0, experts, 0))
    tile = jnp.where(valid, tile, n_tiles - 1)
    ex = jnp.where(valid, ex, last_ex)
    lo = jnp.where(valid, lo, 0)
    hi = jnp.where(valid, hi, 0)
    first = jnp.concatenate([jnp.ones((1,), jnp.int32), (tile[1:] != tile[:-1]).astype(jnp.int32)])
    first = jnp.where(valid, first, 0)
    fresh = jnp.concatenate([jnp.ones((1,), jnp.int32), (ex[1:] != ex[:-1]).astype(jnp.int32)])
    return pos, (tile, ex, lo, hi, first, fresh)


def _experts_body(tile_ref, ex_ref, lo_ref, hi_ref, first_ref, fresh_ref, xs_ref, wg_ref, wu_ref, wd_ref, ys_ref,
                  acc_ref, wg_bf, wu_bf, wd_bf):
    g = pl.program_id(0)
    lo, hi = lo_ref[g], hi_ref[g]

    @pl.when(fresh_ref[g] == 1)
    def _():
        wg_bf[...] = wg_ref[...].astype(BF16)
        wu_bf[...] = wu_ref[...].astype(BF16)
        wd_bf[...] = wd_ref[...].astype(BF16)

    @pl.when(hi > lo)
    def _():
        x = _unpack_pairs(xs_ref[...]).astype(BF16)
        a = jax.nn.silu(_dot(x, wg_bf[...])) * _dot(x, wu_bf[...])
        row = lax.broadcasted_iota(jnp.int32, (a.shape[0], 1), 0)
        a = jnp.where((row >= lo) & (row < hi), a, 0.0)
        y = _dot(a.astype(BF16), wd_bf[...])

        @pl.when(first_ref[g] == 1)
        def _():
            acc_ref[...] = y

        @pl.when(first_ref[g] == 0)
        def _():
            acc_ref[...] += y

        ys_ref[...] = _pack_pairs(acc_ref[...])


def _experts(xs, visits, lw, l, tr):
    n_visits = visits[0].shape[0]
    rows = pl.BlockSpec((tr, D_MODEL // 2), lambda g, tile, *_: (tile[g], 0))
    wspec = lambda a: pl.BlockSpec((None, None) + a.shape[2:], lambda g, tile, ex, *_: (l, ex[g], 0, 0))
    return pl.pallas_call(
        _experts_body,
        grid_spec=pltpu.PrefetchScalarGridSpec(
            num_scalar_prefetch=len(visits),
            grid=(n_visits,),
            in_specs=[rows, wspec(lw["wg"]), wspec(lw["wu"]), wspec(lw["wd"])],
            out_specs=rows,
            scratch_shapes=[pltpu.VMEM((tr, D_MODEL), F32)] + [pltpu.VMEM(lw[k].shape[2:], BF16) for k in ("wg", "wu", "wd")],
        ),
        out_shape=jax.ShapeDtypeStruct(xs.shape, jnp.int32),
        compiler_params=_cparams(("arbitrary",)),
        name="experts",
    )(*visits, xs, lw["wg"], lw["wu"], lw["wd"])


def _moe_residual(h_ref, y0_ref, y1_ref, route_ref):
    r = route_ref[...]
    return (h_ref[...] + r[:, R_W0:R_W0 + 1] * _unpack_pairs(y0_ref[...])
            + r[:, R_W1:R_W1 + 1] * _unpack_pairs(y1_ref[...]))


def _final_body(h_ref, y0_ref, y1_ref, route_ref, gfin_ref, o_ref):
    o_ref[...] = _rms(_moe_residual(h_ref, y0_ref, y1_ref, route_ref)) * gfin_ref[...]


def _final(h, y, route, gfin, tm):
    t = h.shape[0]
    row = lambda n: pl.BlockSpec((tm, n), lambda i: (i, 0))
    second = pl.BlockSpec((tm, D_MODEL // 2), lambda i: (i + t // tm, 0))
    return pl.pallas_call(
        _final_body,
        grid=(t // tm,),
        in_specs=[row(D_MODEL), row(D_MODEL // 2), second, row(LANES), pl.BlockSpec(gfin.shape, lambda i: (0, 0))],
        out_specs=row(D_MODEL),
        out_shape=jax.ShapeDtypeStruct(h.shape, F32),
        compiler_params=_cparams(("parallel",)),
        name="final",
    )(h, y, y, route, gfin)


def _moe_rows(tp, route_t, counts, lw, l, tr):
    t = tp.shape[0]
    pos, visits = _routing_tables(route_t, counts, tr)
    xs = _scatter_rows(tp, pos[:t], pos[t:], 2 * t)
    ys = _experts(xs, visits, lw, l, tr)
    return _gather_rows(ys, pos)


def _rope_angles(seq_len, rot_dim):
    rows = seq_len // GRID_W
    row = jnp.broadcast_to(jnp.arange(rows)[:, None], (rows, GRID_W)).reshape(-1).astype(F32)
    col = jnp.broadcast_to(jnp.arange(GRID_W)[None, :], (rows, GRID_W)).reshape(-1).astype(F32)
    axis_dim = rot_dim // 2
    inv_freq = jnp.power(jnp.float32(ROPE_THETA), -jnp.arange(0, axis_dim, 2, dtype=F32) / axis_dim)
    ang = jnp.concatenate([row[:, None] * inv_freq[None, :], col[:, None] * inv_freq[None, :]], axis=-1)
    return jnp.cos(ang), jnp.sin(ang)


def _rope_tables(nseq, seq_len):
    c, s = _rope_angles(seq_len, A_HEAD_DIM)
    ca = jnp.concatenate([c, c, c, c], axis=-1)
    sa = jnp.concatenate([-s, s, -s, s], axis=-1)
    c, s = _rope_angles(seq_len, B_ROPE)
    ones = jnp.ones((seq_len, B_NOPE), F32)
    zeros = jnp.zeros((seq_len, B_NOPE), F32)
    tail = LANES - B_NOPE - B_ROPE
    cb = jnp.concatenate([ones, c, c, ones[:, :tail]], axis=-1)
    sb = jnp.concatenate([zeros, -s, s, zeros[:, :tail]], axis=-1)
    return tuple(jnp.tile(a, (nseq, 1)) for a in (ca, sa, cb, sb))


def _prepare_weights(norm_mix, w_in, a_q_norm, a_k_norm, b_q_norm, b_kv_norm, w_q_up, w_kv_up, w_branch_a,
                     w_branch_b, w_out, norm_ffn, w_router, b_router, w_gate, w_up, w_down):
    depth = w_in.shape[0]
    order = jnp.array(A_HEAD_ORDER)
    parts = []
    start = 0
    for n in (A_Q_COLS, A_KV_COLS, A_KV_COLS, B_Q_RANK, B_KV_RANK, B_ROPE, D_MODEL, D_MODEL):
        parts.append(w_in[..., start:start + n])
        start += n
    qa, ka, va, cq, ckv, kr, ga, gb = parts
    qa = qa.reshape(depth, D_MODEL, A_HEADS, A_HEAD_DIM)[:, :, order].reshape(depth, D_MODEL, A_Q_COLS)
    kr = jnp.pad(kr, ((0, 0), (0, 0), (B_NOPE, LANES - B_NOPE - B_ROPE)))
    win = jnp.concatenate([qa, ka, va, cq, ckv, kr, ga, gb], axis=-1).astype(BF16)

    wq = w_q_up.reshape(depth, B_Q_RANK, B_HEADS, B_QK_DIM)
    wq = jnp.pad(wq, ((0, 0), (0, 0), (0, 0), (0, LANES - B_QK_DIM))).reshape(depth, B_Q_RANK, B_PAD_COLS)
    wkv = w_kv_up.reshape(depth, B_KV_RANK, B_HEADS, B_NOPE + B_V)
    wk = jnp.pad(wkv[..., :B_NOPE], ((0, 0), (0, 0), (0, 0), (0, LANES - B_NOPE))).reshape(depth, B_KV_RANK, B_PAD_COLS)
    wv = wkv[..., B_NOPE:].reshape(depth, B_KV_RANK, B_O_COLS)
    wba = w_branch_a.reshape(depth, A_HEADS, A_HEAD_DIM, D_MODEL)[:, order].reshape(depth, A_Q_COLS, D_MODEL)

    group = jnp.arange(A_Q_COLS) // A_HEAD_DIM
    bd = (group[:, None] == group[None, :]).astype(BF16)
    wr_t = w_router.T
    wrh, wrl = _split_bf16(wr_t)
    vec = lambda a: a[:, None, :]
    gq = jnp.tile(a_q_norm, (1, A_HEADS)) * (A_HEAD_DIM ** -0.5 * LOG2E)
    norm2_bound = lambda g: 1.02 * A_HEAD_DIM * jnp.max(g * g, axis=-1)
    bounds_a = jnp.stack([norm2_bound(gq), norm2_bound(a_k_norm)], axis=-1).astype(F32)
    return dict(
        bounds_a=bounds_a,
        gmix=vec(norm_mix), win=win,
        gq=vec(gq), gk=vec(jnp.tile(a_k_norm, (1, A_KV_HEADS))),
        gbq=vec(b_q_norm), gbkv=vec(b_kv_norm),
        wq=wq.astype(BF16), wk=wk.astype(BF16), wv=wv.astype(BF16), bd=bd,
        wba=wba.astype(BF16), wbb=w_branch_b.astype(BF16), wout=w_out.astype(BF16), gffn=vec(norm_ffn),
        wrh=wrh, wrl=wrl, br=b_router[:, None].astype(F32),
        wg=w_gate, wu=w_up, wd=w_down,
    )


def _pick(n, candidates):
    for c in candidates:
        if n % c == 0:
            return c
    raise ValueError(f"no tile in {candidates} divides {n}")


def _trunk(x3, lw, norm_final):
    nseq, s, _ = x3.shape
    t = nseq * s
    x = x3.reshape(t, D_MODEL)
    tabs = _rope_tables(nseq, s)
    depth = lw["win"].shape[0]
    tm = _pick(t, (1024, 512, 256))
    tm_in = _pick(t, (512, 256))
    tk = _pick(s, (512,))
    tq = _pick(s, tuple(c for c in (1024, 512, 256) if c * (s // tk) <= ATTN_UNROLL_BUDGET) or (256,))
    moe_out = None
    for l in range(depth):
        x_new, qa, ka, va, qb, kb, vb, ga, gb = _mixer_in(x, moe_out, lw, l, tabs, tm_in)
        x = x if x_new is None else x_new
        oa = _attention_a(qa, ka, va, lw["bounds_a"], l, nseq, s, tq, tk)
        ob = _attention_b(qb, kb, vb, nseq, s, tq, tk)
        h, tp, route, route_t, counts = _mixer_out(oa, ob, ga, gb, x, lw, l, tm)
        moe_out = (h, _moe_rows(tp, route_t, counts, lw, l, EXPERT_ROWS), route)
        x = None
    return _final(*moe_out, norm_final[None, :], tm).reshape(x3.shape)


def kernel(x_prompt, x_sample, norm_mix, w_in, a_q_norm, a_k_norm, b_q_norm, b_kv_norm, w_q_up, w_kv_up, w_branch_a,
           w_branch_b, w_out, norm_ffn, w_router, b_router, w_gate, w_up, w_down, norm_final):
    lw = _prepare_weights(norm_mix, w_in, a_q_norm, a_k_norm, b_q_norm, b_kv_norm, w_q_up, w_kv_up, w_branch_a,
                          w_branch_b, w_out, norm_ffn, w_router, b_router, w_gate, w_up, w_down)
    return _trunk(x_prompt, lw, norm_final), _trunk(x_sample, lw, norm_final)
```

```python
import functools

import jax
import jax.numpy as jnp
from jax import lax
from jax.experimental import pallas as pl
from jax.experimental.pallas import tpu as pltpu
from jax.experimental.pallas import tpu_sc as plsc

F32 = jnp.float32
BF16 = jnp.bfloat16

D_MODEL = 1024
GRID_W = 64
ROPE_THETA = 10000.0
EPS = 1e-6
A_HEADS = 8
A_KV_HEADS = 2
A_HEAD_DIM = 64
B_HEADS = 8
B_Q_RANK = 384
B_KV_RANK = 256
B_NOPE = 64
B_ROPE = 32
B_V = 64
N_EXPERTS = 16
N_GROUPS = 4
EXPERTS_PER_GROUP = N_EXPERTS // N_GROUPS
D_EXPERT = 512
A_Q_COLS = A_HEADS * A_HEAD_DIM
A_KV_COLS = A_KV_HEADS * A_HEAD_DIM
B_QK_DIM = B_NOPE + B_ROPE
B_O_COLS = B_HEADS * B_V

LANES = 128
LOG2E = 1.4426950408889634

C_QA = 0
C_KA = C_QA + A_Q_COLS
C_VA = C_KA + A_KV_COLS
C_CQ = C_VA + A_KV_COLS
C_CKV = C_CQ + B_Q_RANK
C_KR = C_CKV + B_KV_RANK
C_GA = C_KR + LANES
C_GB = C_GA + D_MODEL
C_END = C_GB + D_MODEL
B_PAD_COLS = B_HEADS * LANES

A_HEAD_ORDER = (0, 4, 1, 5, 2, 6, 3, 7)

VMEM_LIMIT = 56 * 1024 * 1024

ATTN_GROUP = 16
ATTN_UNROLL_BUDGET = 8192
ATTN_S_BUFS = 3
ATTN_P_BUFS = 2
ATTN_VALUE_LAG = 1
ATTN_SLAB = 32
ATTN_FIXED_SHIFT_MAX = 48.0
EXPERT_ROWS = 512
SC_CHUNK = 128


def _cparams(sem):
    return pltpu.CompilerParams(dimension_semantics=sem, vmem_limit_bytes=VMEM_LIMIT)


def _dot(a, b):
    return jnp.dot(a, b, preferred_element_type=F32)


def _dot_nt(a, b):
    return lax.dot_general(a, b, (((1,), (1,)), ((), ())), preferred_element_type=F32)


def _dot_tn(a, b):
    return lax.dot_general(a, b, (((0,), (0,)), ((), ())), preferred_element_type=F32)


def _rms(x):
    return x * lax.rsqrt(jnp.mean(x * x, axis=-1, keepdims=True) + EPS)


def _split_bf16(x):
    hi = x.astype(BF16)
    lo = (x - hi.astype(F32)).astype(BF16)
    return hi, lo


def _group_mean_sq(v, bd):
    return _dot((v * v).astype(BF16), bd) * (1.0 / A_HEAD_DIM)


def _rope_a(v, cos, sin):
    n = v.shape[-1]
    lane = lax.broadcasted_iota(jnp.int32, v.shape, 1)
    low = (lane % A_HEAD_DIM) < (A_HEAD_DIM // 2)
    swapped = jnp.where(low, pltpu.roll(v, n - A_HEAD_DIM // 2, 1), pltpu.roll(v, A_HEAD_DIM // 2, 1))
    return v * cos + swapped * sin


def _rope_b(v, cos, sin):
    n = v.shape[-1]
    lane = lax.broadcasted_iota(jnp.int32, v.shape, 1)
    low = (lane % LANES) < (B_NOPE + B_ROPE // 2)
    swapped = jnp.where(low, pltpu.roll(v, n - B_ROPE // 2, 1), pltpu.roll(v, B_ROPE // 2, 1))
    return v * cos + swapped * sin


def _mixer_in_body(*refs, fused):
    if fused:
        h_ref, y0_ref, y1_ref, route_ref, *refs = refs
    else:
        x_ref, *refs = refs
    (gmix_ref, win_ref, ca_ref, sa_ref, cb_ref, sb_ref, gq_ref, gk_ref, gbq_ref, gbkv_ref, wq_ref, wk_ref, wv_ref,
     bd_ref, *outs) = refs
    if fused:
        x_out_ref, *outs = outs
        x = _moe_residual(h_ref, y0_ref, y1_ref, route_ref)
        x_out_ref[...] = x
    else:
        x = x_ref[...]
    qat_ref, ka_ref, vat_ref, qbt_ref, kb_ref, vbt_ref, ga_ref, gb_ref = outs
    hb = (_rms(x) * gmix_ref[...]).astype(BF16)
    u = _dot(hb, win_ref[:, C_QA:C_GA])
    bd = bd_ref[...]
    ca, sa = ca_ref[...], sa_ref[...]
    cb, sb = cb_ref[...], sb_ref[...]

    qa = u[:, C_QA:C_KA]
    qa = qa * lax.rsqrt(_group_mean_sq(qa, bd) + EPS) * gq_ref[...]
    qa = _rope_a(qa, jnp.concatenate([ca] * (A_Q_COLS // LANES), axis=1),
                 jnp.concatenate([sa] * (A_Q_COLS // LANES), axis=1))
    qat_ref[...] = qa.T.astype(BF16)
    ka = u[:, C_KA:C_VA]
    ka = ka * lax.rsqrt(_group_mean_sq(ka, bd[:A_KV_COLS, :A_KV_COLS]) + EPS) * gk_ref[...]
    ka_ref[...] = _rope_a(ka, ca, sa).astype(BF16)
    vat_ref[...] = u[:, C_VA:C_CQ].T.astype(BF16)

    cq = (_rms(u[:, C_CQ:C_CKV]) * gbq_ref[...]).astype(BF16)
    qb = _dot(cq, wq_ref[...])
    qb = _rope_b(qb, jnp.concatenate([cb] * B_HEADS, axis=1), jnp.concatenate([sb] * B_HEADS, axis=1))
    qbt_ref[...] = (qb * (B_QK_DIM ** -0.5 * LOG2E)).T.astype(BF16)
    ckv = (_rms(u[:, C_CKV:C_KR]) * gbkv_ref[...]).astype(BF16)
    kr = _rope_b(u[:, C_KR:C_GA], cb, sb)
    kb_ref[...] = (_dot(ckv, wk_ref[...]) + jnp.concatenate([kr] * B_HEADS, axis=1)).astype(BF16)
    vbt_ref[...] = _dot(ckv, wv_ref[...]).T.astype(BF16)

    g = jax.nn.sigmoid(_dot(hb, win_ref[:, C_GA:C_END]))
    ga_ref[...] = g[:, :D_MODEL].astype(BF16)
    gb_ref[...] = g[:, D_MODEL:].astype(BF16)


def _mixer_in(x, moe_out, lw, l, tabs, tm):
    fused = x is None
    t = moe_out[0].shape[0] if fused else x.shape[0]
    row = lambda n: pl.BlockSpec((tm, n), lambda i: (i, 0))
    full = lambda a: pl.BlockSpec((None,) + a.shape[1:], lambda i: (l,) + (0,) * (a.ndim - 1))
    const = lambda a: pl.BlockSpec(a.shape, lambda i: (0,) * a.ndim)
    col = lambda n: pl.BlockSpec((n, tm), lambda i: (0, i))
    ca, sa, cb, sb = tabs
    if fused:
        h, y, route = moe_out
        second = pl.BlockSpec((tm, D_MODEL // 2), lambda i: (i + t // tm, 0))
        ins, in_specs = [h, y, y, route], [row(D_MODEL), row(D_MODEL // 2), second, row(LANES)]
    else:
        ins, in_specs = [x], [row(D_MODEL)]
    ins += [lw["gmix"], lw["win"], ca, sa, cb, sb, lw["gq"], lw["gk"], lw["gbq"], lw["gbkv"],
            lw["wq"], lw["wk"], lw["wv"], lw["bd"]]
    in_specs += [full(lw["gmix"]), full(lw["win"]), row(LANES), row(LANES), row(LANES), row(LANES),
                 full(lw["gq"]), full(lw["gk"]), full(lw["gbq"]), full(lw["gbkv"]),
                 full(lw["wq"]), full(lw["wk"]), full(lw["wv"]), const(lw["bd"])]
    widths = [A_Q_COLS, A_KV_COLS, A_KV_COLS, B_PAD_COLS, B_PAD_COLS, B_O_COLS, D_MODEL, D_MODEL]
    transposed = [True, False, True, True, False, True, False, False]
    out_specs = [col(n) if tr else row(n) for n, tr in zip(widths, transposed)]
    out_shape = [jax.ShapeDtypeStruct((n, t) if tr else (t, n), BF16) for n, tr in zip(widths, transposed)]
    if fused:
        out_specs.insert(0, row(D_MODEL))
        out_shape.insert(0, jax.ShapeDtypeStruct((t, D_MODEL), F32))
    outs = pl.pallas_call(
        functools.partial(_mixer_in_body, fused=fused),
        grid=(t // tm,),
        in_specs=in_specs,
        out_specs=out_specs,
        out_shape=out_shape,
        compiler_params=_cparams(("parallel",)),
        name="mixer_in",
    )(*ins)
    return tuple(outs) if fused else (None,) + tuple(outs)


def _flash_pair(q0t, q1t, k0_ref, k1_ref, vt_ref, o_ref, s_refs, p_refs, acc_ref, kmax2, fixed, tk):
    tq = q0t.shape[1]
    nk = vt_ref.shape[1] // tk
    group = min(nk, ATTN_GROUP)
    ns = len(s_refs)
    qt2 = jnp.concatenate([q0t, q1t], axis=1)

    def score_values(j):
        off = pl.multiple_of(j * tk, tk)
        if k1_ref is None:
            return _dot(k0_ref[pl.ds(off, tk), :], qt2)
        return jnp.concatenate([_dot(k0_ref[pl.ds(off, tk), :], q0t), _dot(k1_ref[pl.ds(off, tk), :], q1t)], axis=1)

    def add_values(j, p, alpha):
        off = pl.multiple_of(j * tk, tk)
        for rows, lanes in ((slice(0, B_V), slice(0, tq)), (slice(B_V, LANES), slice(tq, 2 * tq))):
            prev = acc_ref[rows, :] if alpha is None else alpha[:, lanes] * acc_ref[rows, :]
            acc_ref[rows, :] = prev + _dot(vt_ref[rows, pl.ds(off, tk)], p[:, lanes])

    def finish(l):
        l = jnp.sum(l, axis=0, keepdims=True)
        o_ref[:B_V, :] = (acc_ref[:B_V, :] / l[:, :tq]).astype(o_ref.dtype)
        o_ref[B_V:, :] = (acc_ref[B_V:, :] / l[:, tq:]).astype(o_ref.dtype)

    def run_groups(one_group, carry):
        acc_ref[...] = jnp.zeros_like(acc_ref)
        return one_group(0, carry) if nk == group else lax.fori_loop(0, nk // group, one_group, carry)

    def fixed_shift(shift):
        wide = 2 * LANES
        tiles = [(n * wide, (n * wide) // tq) for n in range(2 * tq // wide)]

        def value_tile(j, p, lane0, head):
            off = pl.multiple_of(j * tk, tk)
            rows = slice(head * B_V, (head + 1) * B_V)
            lanes = slice(lane0 - head * tq, lane0 - head * tq + wide)
            acc_ref[rows, lanes] += _dot(vt_ref[rows, pl.ds(off, tk)], p)

        def one_group(g, l):
            l = list(l)
            pending = []
            for c in range(group):
                j = g * group + c
                off = pl.multiple_of(j * tk, tk)
                for n, (lane0, head) in enumerate(tiles):
                    k_ref = k0_ref if (k1_ref is None or head == 0) else k1_ref
                    s = _dot(k_ref[pl.ds(off, tk), :], qt2[:, lane0:lane0 + wide])
                    if len(pending) >= ATTN_VALUE_LAG:
                        value_tile(*pending.pop(0))
                    p = jnp.exp2(s - shift[:, lane0:lane0 + wide])
                    l[n] = l[n] + p.reshape(tk // 8, 8, wide).sum(axis=0)
                    pending.append((j, p.astype(BF16), lane0, head))
            for item in pending:
                value_tile(*item)
            return tuple(l)

        l = run_groups(one_group, tuple(jnp.zeros((8, wide), F32) for _ in tiles))
        finish(jnp.concatenate(l, axis=1))

    def softmax(s_ref, p_ref, m, l):
        slabs = [pl.ds(r, ATTN_SLAB) for r in range(0, tk, ATTN_SLAB)]
        fold = lambda x: x.reshape(ATTN_SLAB // 8, 8, 2 * tq)
        mx = fold(s_ref[slabs[0], :]).max(axis=0)
        for sl in slabs[1:]:
            mx = jnp.maximum(mx, fold(s_ref[sl, :]).max(axis=0))
        m_new = jnp.maximum(m, jnp.max(mx, axis=0, keepdims=True))
        alpha = jnp.exp2(m - m_new)
        l = alpha * l
        for sl in slabs:
            p = jnp.exp2(s_ref[sl, :] - m_new)
            p_ref[sl, :] = p.astype(BF16)
            l = l + fold(p).sum(axis=0)
        return m_new, l, alpha

    def running_max():
        def one_group(g, carry):
            m, l = carry
            base = g * group
            for c in range(min(ns - 1, group)):
                s_refs[c % ns][...] = score_values(base + c)
            for c in range(group):
                if c + ns - 1 < group:
                    s_refs[(c + ns - 1) % ns][...] = score_values(base + c + ns - 1)
                p_ref = p_refs[c % len(p_refs)]
                m, l, alpha = softmax(s_refs[c % ns], p_ref, m, l)
                add_values(base + c, p_ref[...], alpha)
            return m, l

        carry = (jnp.full((1, 2 * tq), -jnp.inf, F32), jnp.zeros((8, 2 * tq), F32))
        finish(run_groups(one_group, carry)[1])

    if fixed:
        lane = lax.broadcasted_iota(jnp.int32, (1, 2 * tq), 1)
        qsq = jnp.sum(jnp.square(qt2.astype(F32)), axis=0, keepdims=True)
        fixed_shift(jnp.sqrt(qsq * jnp.where(lane < tq, kmax2[0], kmax2[1])))
    else:
        running_max()


def _max_sq_norm(k_ref, ones_ref, lanes):
    k = k_ref[...].astype(F32)
    n = _dot((k * k).astype(BF16), ones_ref[...])
    return jnp.max(n[:, lanes])


def _max_query_sq_norm(q):
    q = q.astype(F32)
    return jnp.max(jnp.sum(q * q, axis=0, keepdims=True))


def _attention_tiles(q_pair, k0_ref, k1_ref, vt_ref, o_ref, scratch, qmax2, kmax2, tq, tk):
    s_refs, p_refs, acc_ref = scratch[:ATTN_S_BUFS], scratch[ATTN_S_BUFS:-1], scratch[-1]
    bound2 = jnp.maximum(qmax2[0] * kmax2[0], qmax2[1] * kmax2[1])
    small = bound2 <= ATTN_FIXED_SHIFT_MAX * ATTN_FIXED_SHIFT_MAX

    def walk(fixed):
        @pl.loop(0, o_ref.shape[1] // tq)
        def _(j):
            lanes = pl.ds(pl.multiple_of(j * tq, tq), tq)
            q0t, q1t = q_pair(lanes)
            _flash_pair(q0t, q1t, k0_ref, k1_ref, vt_ref, o_ref.at[:, lanes], s_refs, p_refs, acc_ref, kmax2, fixed, tk)

    pl.when(small)(lambda: walk(True))
    pl.when(jnp.logical_not(small))(lambda: walk(False))


def _attn_a_body(qt_ref, k_ref, vt_ref, bound_ref, o_ref, *scratch, layer, tq, tk):
    qmax2 = (bound_ref[layer, 0], bound_ref[layer, 0])
    kmax2 = (bound_ref[layer, 1], bound_ref[layer, 1])

    def q_pair(lanes):
        qt = qt_ref[:, lanes]
        zero = jnp.zeros((A_HEAD_DIM, tq), BF16)
        return jnp.concatenate([qt[:A_HEAD_DIM], zero], axis=0), jnp.concatenate([zero, qt[A_HEAD_DIM:]], axis=0)

    _attention_tiles(q_pair, k_ref, None, vt_ref, o_ref, scratch, qmax2, kmax2, tq, tk)


def _attn_b_body(q0t_ref, q1t_ref, k0_ref, k1_ref, vt_ref, ones_ref, o_ref, *scratch, tq, tk):
    kmax2 = (_max_sq_norm(k0_ref, ones_ref, slice(0, LANES)), _max_sq_norm(k1_ref, ones_ref, slice(0, LANES)))
    qmax2 = (_max_query_sq_norm(q0t_ref[...]), _max_query_sq_norm(q1t_ref[...]))
    q_pair = lambda lanes: (q0t_ref[:, lanes], q1t_ref[:, lanes])
    _attention_tiles(q_pair, k0_ref, k1_ref, vt_ref, o_ref, scratch, qmax2, kmax2, tq, tk)


def _attn_scratch(tq, tk):
    return ([pltpu.VMEM((tk, 2 * tq), F32)] * ATTN_S_BUFS + [pltpu.VMEM((tk, 2 * tq), BF16)] * ATTN_P_BUFS
            + [pltpu.VMEM((LANES, tq), F32)])


def _head_ones(head_dim):
    group = jnp.arange(LANES) // head_dim
    return (group[:, None] == group[None, :]).astype(BF16)


def _attention_a(qat, ka, vat, bounds, layer, nseq, s, tq, tk):
    qspec = pl.BlockSpec((LANES, s), lambda b, h: (h, b))
    kspec = pl.BlockSpec((s, LANES), lambda b, h: (b, 0))
    vspec = pl.BlockSpec((LANES, s), lambda b, h: (0, b))
    return pl.pallas_call(
        functools.partial(_attn_a_body, layer=layer, tq=tq, tk=tk),
        grid=(nseq, A_Q_COLS // LANES),
        in_specs=[qspec, kspec, vspec, pl.BlockSpec(memory_space=pltpu.SMEM)],
        out_specs=qspec,
        out_shape=jax.ShapeDtypeStruct(qat.shape, BF16),
        scratch_shapes=_attn_scratch(tq, tk),
        compiler_params=_cparams(("parallel", "parallel")),
        name="attn_a",
    )(qat, ka, vat, bounds)


def _attention_b(qbt, kb, vbt, nseq, s, tq, tk):
    q0 = pl.BlockSpec((LANES, s), lambda b, h: (2 * h, b))
    q1 = pl.BlockSpec((LANES, s), lambda b, h: (2 * h + 1, b))
    k0 = pl.BlockSpec((s, LANES), lambda b, h: (b, 2 * h))
    k1 = pl.BlockSpec((s, LANES), lambda b, h: (b, 2 * h + 1))
    v = pl.BlockSpec((LANES, s), lambda b, h: (h, b))
    o = pl.BlockSpec((LANES, s), lambda b, h: (h, b))
    ones = pl.BlockSpec((LANES, LANES), lambda b, h: (0, 0))
    return pl.pallas_call(
        functools.partial(_attn_b_body, tq=tq, tk=tk),
        grid=(nseq, B_O_COLS // LANES),
        in_specs=[q0, q1, k0, k1, v, ones],
        out_specs=o,
        out_shape=jax.ShapeDtypeStruct(vbt.shape, BF16),
        scratch_shapes=_attn_scratch(tq, tk),
        compiler_params=_cparams(("parallel", "parallel")),
        name="attn_b",
    )(qbt, qbt, kb, kb, vbt, _head_ones(LANES))


def _within(x, d, period, n):
    row = lax.broadcasted_iota(jnp.int32, x.shape, 0)
    return jnp.where((row % period) + d < period, pltpu.roll(x, n - d, 0), pltpu.roll(x, period - d, 0))


def _route(logits_t, bias):
    n = N_EXPERTS
    scores = jax.nn.sigmoid(logits_t)
    biased = scores + bias
    row = lax.broadcasted_iota(jnp.int32, biased.shape, 0)
    pos = row % EXPERTS_PER_GROUP
    rank = jnp.zeros(biased.shape, jnp.int32)
    for d in range(1, EXPERTS_PER_GROUP):
        other = _within(biased, d, EXPERTS_PER_GROUP, n)
        other_pos = (pos + d) % EXPERTS_PER_GROUP
        ahead = (other > biased) | ((other == biased) & (other_pos < pos))
        rank = rank + ahead.astype(jnp.int32)
    top2 = rank < 2
    kept = jnp.where(top2, biased, 0.0)
    gscore = kept
    for d in range(1, EXPERTS_PER_GROUP):
        gscore = gscore + _within(kept, d, EXPERTS_PER_GROUP, n)
    grp = row // EXPERTS_PER_GROUP
    win = jnp.ones(biased.shape, jnp.bool_)
    for d in range(1, N_GROUPS):
        other = pltpu.roll(gscore, n - d * EXPERTS_PER_GROUP, 0)
        other_grp = (grp + d) % N_GROUPS
        win = win & ((other < gscore) | ((other == gscore) & (other_grp > grp)))
    sel = top2 & win
    w = jnp.where(sel, scores, 0.0)
    return w / jnp.sum(w, axis=0, keepdims=True), sel


def _pack_pairs(v):
    n = v.shape[1] // 2
    vb = v.astype(BF16).astype(F32)
    hi = pltpu.bitcast(vb[:, :n], jnp.int32)
    lo = pltpu.bitcast(vb[:, n:], jnp.int32)
    return hi | lax.shift_right_logical(lo, 16)


def _unpack_pairs(w):
    hi = pltpu.bitcast(w & jnp.int32(-65536), F32)
    lo = pltpu.bitcast(lax.shift_left(w, 16), F32)
    return jnp.concatenate([hi, lo], axis=1)


R_E0, R_E1, R_RANK0, R_RANK1, R_W0, R_W1 = range(6)


def _mixer_out_body(oa_ref, ob_ref, ga_ref, gb_ref, x_ref, wba_ref, wbb_ref, wout_ref, gffn_ref, wrh_ref, wrl_ref,
                    br_ref, tri_ref, h_ref, tp_ref, route_ref, route_t_ref, count_ref):
    @pl.when(pl.program_id(0) == 0)
    def _():
        count_ref[...] = jnp.zeros_like(count_ref)

    ma = _dot_tn(oa_ref[...], wba_ref[...])
    mb = _dot_tn(ob_ref[...], wbb_ref[...])
    merged = ga_ref[...].astype(F32) * ma + gb_ref[...].astype(F32) * mb
    h = x_ref[...] + _dot(merged.astype(BF16), wout_ref[...])
    h_ref[...] = h
    t = _rms(h) * gffn_ref[...]
    t_hi, t_lo = _split_bf16(t)
    tp_ref[...] = _pack_pairs(t)
    wrh, wrl = wrh_ref[...], wrl_ref[...]
    logits_t = _dot_nt(wrh, t_hi) + _dot_nt(wrh, t_lo) + _dot_nt(wrl, t_hi)
    gates_t, sel = _route(logits_t, br_ref[...])

    tm = gates_t.shape[1]
    onehot = jnp.where(sel, 1.0, 0.0)
    before = _dot(onehot.astype(BF16), tri_ref[...])
    rank_t = count_ref[:, 0:1] + before
    count_ref[...] = count_ref[...] + jnp.sum(onehot, axis=1, keepdims=True)

    row = lax.broadcasted_iota(jnp.int32, sel.shape, 0).astype(F32)
    e0 = jnp.min(jnp.where(sel, row, float(N_EXPERTS)), axis=0, keepdims=True)
    e1 = jnp.max(jnp.where(sel, row, -1.0), axis=0, keepdims=True)
    pick = lambda v, e: jnp.sum(jnp.where(sel & (row == e), v, 0.0), axis=0, keepdims=True)
    rec = jnp.concatenate([e0, e1, pick(rank_t, e0), pick(rank_t, e1), pick(gates_t, e0), pick(gates_t, e1),
                           jnp.zeros((LANES - 6, tm), F32)], axis=0)
    route_ref[...] = rec.T
    route_t_ref[...] = rec[:8]


def _mixer_out(oa, ob, ga, gb, x, lw, l, tm):
    t = x.shape[0]
    row = lambda n: pl.BlockSpec((tm, n), lambda i: (i, 0))
    full = lambda a: pl.BlockSpec((None,) + a.shape[1:], lambda i: (l,) + (0,) * (a.ndim - 1))
    const = lambda a: pl.BlockSpec(a.shape, lambda i: (0,) * a.ndim)
    col = lambda n: pl.BlockSpec((n, tm), lambda i: (0, i))
    tri = (jnp.arange(tm)[:, None] < jnp.arange(tm)[None, :]).astype(BF16)
    ins = [oa, ob, ga, gb, x, lw["wba"], lw["wbb"], lw["wout"], lw["gffn"], lw["wrh"], lw["wrl"], lw["br"], tri]
    in_specs = [col(A_Q_COLS), col(B_O_COLS), row(D_MODEL), row(D_MODEL), row(D_MODEL),
                full(lw["wba"]), full(lw["wbb"]), full(lw["wout"]), full(lw["gffn"]),
                const(lw["wrh"]), const(lw["wrl"]), const(lw["br"]), const(tri)]
    return pl.pallas_call(
        _mixer_out_body,
        grid=(t // tm,),
        in_specs=in_specs,
        out_specs=[row(D_MODEL), row(D_MODEL // 2), row(LANES), col(8),
                   pl.BlockSpec((N_EXPERTS, LANES), lambda i: (0, 0))],
        out_shape=[jax.ShapeDtypeStruct((t, D_MODEL), F32), jax.ShapeDtypeStruct((t, D_MODEL // 2), jnp.int32),
                   jax.ShapeDtypeStruct((t, LANES), F32), jax.ShapeDtypeStruct((8, t), F32),
                   jax.ShapeDtypeStruct((N_EXPERTS, LANES), F32)],
        compiler_params=_cparams(("arbitrary",)),
        name="mixer_out",
    )(*ins)


def _sc_plan(nrows):
    info = plsc.get_sparse_core_info()
    workers = info.num_cores * info.num_subcores
    per_worker = nrows // workers
    chunk = min(SC_CHUNK, per_worker)
    assert per_worker * workers == nrows and per_worker % chunk == 0 and chunk % 8 == 0, (nrows, workers, chunk)
    return info.num_cores, per_worker, chunk


def _sc_mesh():
    return plsc.VectorSubcoreMesh(core_axis_name="core", subcore_axis_name="subcore")


def _scatter_rows(x, idx0, idx1, nrows):
    t, d = x.shape
    ncores, per_worker, chunk = _sc_plan(t)

    @functools.partial(
        pl.kernel, out_type=jax.ShapeDtypeStruct((nrows, d), x.dtype), mesh=_sc_mesh(), name="moe_dispatch",
        scratch_types=[pltpu.VMEM((chunk,), jnp.int32), pltpu.VMEM((chunk,), jnp.int32), pltpu.VMEM((chunk, d), x.dtype)])
    def run(x_hbm, i0_hbm, i1_hbm, o_hbm, i0_v, i1_v, rows_v):
        worker = lax.axis_index("subcore") * ncores + lax.axis_index("core")

        @pl.loop(0, per_worker // chunk)
        def _(c):
            base = pl.multiple_of(worker * per_worker + c * chunk, chunk)
            pltpu.sync_copy(x_hbm.at[pl.ds(base, chunk)], rows_v)
            pltpu.sync_copy(i0_hbm.at[pl.ds(base, chunk)], i0_v)
            pltpu.sync_copy(i1_hbm.at[pl.ds(base, chunk)], i1_v)
            pltpu.sync_copy(rows_v, o_hbm.at[i0_v])
            pltpu.sync_copy(rows_v, o_hbm.at[i1_v])

    return run(x, idx0, idx1)


def _gather_rows(table, idx):
    m = idx.shape[0]
    d = table.shape[1]
    ncores, per_worker, chunk = _sc_plan(m)

    @functools.partial(
        pl.kernel, out_type=jax.ShapeDtypeStruct((m, d), table.dtype), mesh=_sc_mesh(), name="moe_collect",
        scratch_types=[pltpu.VMEM((chunk,), jnp.int32), pltpu.VMEM((chunk, d), table.dtype)])
    def run(x_hbm, i_hbm, o_hbm, i_v, rows_v):
        worker = lax.axis_index("subcore") * ncores + lax.axis_index("core")

        @pl.loop(0, per_worker // chunk)
        def _(c):
            base = pl.multiple_of(worker * per_worker + c * chunk, chunk)
            pltpu.sync_copy(i_hbm.at[pl.ds(base, chunk)], i_v)
            pltpu.sync_copy(x_hbm.at[i_v], rows_v)
            pltpu.sync_copy(rows_v, o_hbm.at[pl.ds(base, chunk)])

    return run(table, idx)


def _routing_tables(route_t, counts, tr):
    t = route_t.shape[1]
    experts = jnp.arange(N_EXPERTS, dtype=jnp.int32)
    cnt = counts[:, 0].astype(jnp.int32)
    seg_end = jnp.cumsum(cnt)
    seg_start = seg_end - cnt
    lookup = lambda table, idx: jnp.sum(jnp.where(idx[None, :] == experts[:, None], table[:, None], 0), axis=0)
    e = route_t[R_E0:R_E1 + 1].astype(jnp.int32).reshape(2 * t)
    rank = route_t[R_RANK0:R_RANK1 + 1].astype(jnp.int32).reshape(2 * t)
    pos = lookup(seg_start, e) + rank

    n_tiles = 2 * t // tr
    n_visits = n_tiles + N_EXPERTS - 1
    first_tile = seg_start // tr
    last_tile = jnp.maximum(seg_end - 1, 0) // tr
    visits = jnp.where(cnt > 0, last_tile - first_tile + 1, 0)
    visit_end = jnp.cumsum(visits)
    visit_start = visit_end - visits
    g = jnp.arange(n_visits, dtype=jnp.int32)
    valid = g < visit_end[-1]
    ex = jnp.minimum(jnp.sum((g[None, :] >= visit_end[:, None]).astype(jnp.int32), axis=0), N_EXPERTS - 1)
    tile = lookup(first_tile - visit_start, ex) + g
    lo = jnp.clip(lookup(seg_start, ex) - tile * tr, 0, tr)
    hi = jnp.clip(lookup(seg_end, ex) - tile * tr, 0, tr)
    last_ex = jnp.max(jnp.where(cnt > 0, experts, 0))
    tile = jnp.where(valid, tile, n_tiles - 1)
    ex = jnp.where(valid, ex, last_ex)
    lo = jnp.where(valid, lo, 0)
    hi = jnp.where(valid, hi, 0)
    first = jnp.concatenate([jnp.ones((1,), jnp.int32), (tile[1:] != tile[:-1]).astype(jnp.int32)])
    first = jnp.where(valid, first, 0)
    fresh = jnp.concatenate([jnp.ones((1,), jnp.int32), (ex[1:] != ex[:-1]).astype(jnp.int32)])
    return pos, (tile, ex, lo, hi, first, fresh)


def _experts_body(tile_ref, ex_ref, lo_ref, hi_ref, first_ref, fresh_ref, xs_ref, wg_ref, wu_ref, wd_ref, ys_ref,
                  acc_ref, wg_bf, wu_bf, wd_bf):
    g = pl.program_id(0)
    lo, hi = lo_ref[g], hi_ref[g]

    @pl.when(fresh_ref[g] == 1)
    def _():
        wg_bf[...] = wg_ref[...].astype(BF16)
        wu_bf[...] = wu_ref[...].astype(BF16)
        wd_bf[...] = wd_ref[...].astype(BF16)

    @pl.when(hi > lo)
    def _():
        x = _unpack_pairs(xs_ref[...]).astype(BF16)
        a = jax.nn.silu(_dot(x, wg_bf[...])) * _dot(x, wu_bf[...])
        row = lax.broadcasted_iota(jnp.int32, (a.shape[0], 1), 0)
        a = jnp.where((row >= lo) & (row < hi), a, 0.0)
        y = _dot(a.astype(BF16), wd_bf[...])

        @pl.when(first_ref[g] == 1)
        def _():
            acc_ref[...] = y

        @pl.when(first_ref[g] == 0)
        def _():
            acc_ref[...] += y

        ys_ref[...] = _pack_pairs(acc_ref[...])


def _experts(xs, visits, lw, l, tr):
    n_visits = visits[0].shape[0]
    rows = pl.BlockSpec((tr, D_MODEL // 2), lambda g, tile, *_: (tile[g], 0))
    wspec = lambda a: pl.BlockSpec((None, None) + a.shape[2:], lambda g, tile, ex, *_: (l, ex[g], 0, 0))
    return pl.pallas_call(
        _experts_body,
        grid_spec=pltpu.PrefetchScalarGridSpec(
            num_scalar_prefetch=len(visits),
            grid=(n_visits,),
            in_specs=[rows, wspec(lw["wg"]), wspec(lw["wu"]), wspec(lw["wd"])],
            out_specs=rows,
            scratch_shapes=[pltpu.VMEM((tr, D_MODEL), F32)] + [pltpu.VMEM(lw[k].shape[2:], BF16) for k in ("wg", "wu", "wd")],
        ),
        out_shape=jax.ShapeDtypeStruct(xs.shape, jnp.int32),
        compiler_params=_cparams(("arbitrary",)),
        name="experts",
    )(*visits, xs, lw["wg"], lw["wu"], lw["wd"])


def _moe_residual(h_ref, y0_ref, y1_ref, route_ref):
    r = route_ref[...]
    return (h_ref[...] + r[:, R_W0:R_W0 + 1] * _unpack_pairs(y0_ref[...])
            + r[:, R_W1:R_W1 + 1] * _unpack_pairs(y1_ref[...]))


def _final_body(h_ref, y0_ref, y1_ref, route_ref, gfin_ref, o_ref):
    o_ref[...] = _rms(_moe_residual(h_ref, y0_ref, y1_ref, route_ref)) * gfin_ref[...]


def _final(h, y, route, gfin, tm):
    t = h.shape[0]
    row = lambda n: pl.BlockSpec((tm, n), lambda i: (i, 0))
    second = pl.BlockSpec((tm, D_MODEL // 2), lambda i: (i + t // tm, 0))
    return pl.pallas_call(
        _final_body,
        grid=(t // tm,),
        in_specs=[row(D_MODEL), row(D_MODEL // 2), second, row(LANES), pl.BlockSpec(gfin.shape, lambda i: (0, 0))],
        out_specs=row(D_MODEL),
        out_shape=jax.ShapeDtypeStruct(h.shape, F32),
        compiler_params=_cparams(("parallel",)),
        name="final",
    )(h, y, y, route, gfin)


def _moe_rows(tp, route_t, counts, lw, l, tr):
    t = tp.shape[0]
    pos, visits = _routing_tables(route_t, counts, tr)
    xs = _scatter_rows(tp, pos[:t], pos[t:], 2 * t)
    ys = _experts(xs, visits, lw, l, tr)
    return _gather_rows(ys, pos)


def _rope_angles(seq_len, rot_dim):
    rows = seq_len // GRID_W
    row = jnp.broadcast_to(jnp.arange(rows)[:, None], (rows, GRID_W)).reshape(-1).astype(F32)
    col = jnp.broadcast_to(jnp.arange(GRID_W)[None, :], (rows, GRID_W)).reshape(-1).astype(F32)
    axis_dim = rot_dim // 2
    inv_freq = jnp.power(jnp.float32(ROPE_THETA), -jnp.arange(0, axis_dim, 2, dtype=F32) / axis_dim)
    ang = jnp.concatenate([row[:, None] * inv_freq[None, :], col[:, None] * inv_freq[None, :]], axis=-1)
    return jnp.cos(ang), jnp.sin(ang)


def _rope_tables(nseq, seq_len):
    c, s = _rope_angles(seq_len, A_HEAD_DIM)
    ca = jnp.concatenate([c, c, c, c], axis=-1)
    sa = jnp.concatenate([-s, s, -s, s], axis=-1)
    c, s = _rope_angles(seq_len, B_ROPE)
    ones = jnp.ones((seq_len, B_NOPE), F32)
    zeros = jnp.zeros((seq_len, B_NOPE), F32)
    tail = LANES - B_NOPE - B_ROPE
    cb = jnp.concatenate([ones, c, c, ones[:, :tail]], axis=-1)
    sb = jnp.concatenate([zeros, -s, s, zeros[:, :tail]], axis=-1)
    return tuple(jnp.tile(a, (nseq, 1)) for a in (ca, sa, cb, sb))


def _prepare_weights(norm_mix, w_in, a_q_norm, a_k_norm, b_q_norm, b_kv_norm, w_q_up, w_kv_up, w_branch_a,
                     w_branch_b, w_out, norm_ffn, w_router, b_router, w_gate, w_up, w_down):
    depth = w_in.shape[0]
    order = jnp.array(A_HEAD_ORDER)
    parts = []
    start = 0
    for n in (A_Q_COLS, A_KV_COLS, A_KV_COLS, B_Q_RANK, B_KV_RANK, B_ROPE, D_MODEL, D_MODEL):
        parts.append(w_in[..., start:start + n])
        start += n
    qa, ka, va, cq, ckv, kr, ga, gb = parts
    qa = qa.reshape(depth, D_MODEL, A_HEADS, A_HEAD_DIM)[:, :, order].reshape(depth, D_MODEL, A_Q_COLS)
    kr = jnp.pad(kr, ((0, 0), (0, 0), (B_NOPE, LANES - B_NOPE - B_ROPE)))
    win = jnp.concatenate([qa, ka, va, cq, ckv, kr, ga, gb], axis=-1).astype(BF16)

    wq = w_q_up.reshape(depth, B_Q_RANK, B_HEADS, B_QK_DIM)
    wq = jnp.pad(wq, ((0, 0), (0, 0), (0, 0), (0, LANES - B_QK_DIM))).reshape(depth, B_Q_RANK, B_PAD_COLS)
    wkv = w_kv_up.reshape(depth, B_KV_RANK, B_HEADS, B_NOPE + B_V)
    wk = jnp.pad(wkv[..., :B_NOPE], ((0, 0), (0, 0), (0, 0), (0, LANES - B_NOPE))).reshape(depth, B_KV_RANK, B_PAD_COLS)
    wv = wkv[..., B_NOPE:].reshape(depth, B_KV_RANK, B_O_COLS)
    wba = w_branch_a.reshape(depth, A_HEADS, A_HEAD_DIM, D_MODEL)[:, order].reshape(depth, A_Q_COLS, D_MODEL)

    group = jnp.arange(A_Q_COLS) // A_HEAD_DIM
    bd = (group[:, None] == group[None, :]).astype(BF16)
    wr_t = w_router.T
    wrh, wrl = _split_bf16(wr_t)
    vec = lambda a: a[:, None, :]
    gq = jnp.tile(a_q_norm, (1, A_HEADS)) * (A_HEAD_DIM ** -0.5 * LOG2E)
    norm2_bound = lambda g: 1.02 * A_HEAD_DIM * jnp.max(g * g, axis=-1)
    bounds_a = jnp.stack([norm2_bound(gq), norm2_bound(a_k_norm)], axis=-1).astype(F32)
    return dict(
        bounds_a=bounds_a,
        gmix=vec(norm_mix), win=win,
        gq=vec(gq), gk=vec(jnp.tile(a_k_norm, (1, A_KV_HEADS))),
        gbq=vec(b_q_norm), gbkv=vec(b_kv_norm),
        wq=wq.astype(BF16), wk=wk.astype(BF16), wv=wv.astype(BF16), bd=bd,
        wba=wba.astype(BF16), wbb=w_branch_b.astype(BF16), wout=w_out.astype(BF16), gffn=vec(norm_ffn),
        wrh=wrh, wrl=wrl, br=b_router[:, None].astype(F32),
        wg=w_gate, wu=w_up, wd=w_down,
    )


def _pick(n, candidates):
    for c in candidates:
        if n % c == 0:
            return c
    raise ValueError(f"no tile in {candidates} divides {n}")


def _trunk(x3, lw, norm_final):
    nseq, s, _ = x3.shape
    t = nseq * s
    x = x3.reshape(t, D_MODEL)
    tabs = _rope_tables(nseq, s)
    depth = lw["win"].shape[0]
    tm = _pick(t, (1024, 512, 256))
    tm_in = _pick(t, (512, 256))
    tk = _pick(s, (512,))
    tq = _pick(s, tuple(c for c in (1024, 512, 256) if c * (s // tk) <= ATTN_UNROLL_BUDGET) or (256,))
    moe_out = None
    for l in range(depth):
        x_new, qa, ka, va, qb, kb, vb, ga, gb = _mixer_in(x, moe_out, lw, l, tabs, tm_in)
        x = x if x_new is None else x_new
        oa = _attention_a(qa, ka, va, lw["bounds_a"], l, nseq, s, tq, tk)
        ob = _attention_b(qb, kb, vb, nseq, s, tq, tk)
        h, tp, route, route_t, counts = _mixer_out(oa, ob, ga, gb, x, lw, l, tm)
        moe_out = (h, _moe_rows(tp, route_t, counts, lw, l, EXPERT_ROWS), route)
        x = None
    return _final(*moe_out, norm_final[None, :], tm).reshape(x3.shape)


def kernel(x_prompt, x_sample, norm_mix, w_in, a_q_norm, a_k_norm, b_q_norm, b_kv_norm, w_q_up, w_kv_up, w_branch_a,
           w_branch_b, w_out, norm_ffn, w_router, b_router, w_gate, w_up, w_down, norm_final):
    lw = _prepare_weights(norm_mix, w_in, a_q_norm, a_k_norm, b_q_norm, b_kv_norm, w_q_up, w_kv_up, w_branch_a,
                          w_branch_b, w_out, norm_ffn, w_router, b_router, w_gate, w_up, w_down)
    return _trunk(x_prompt, lw, norm_final), _trunk(x_sample, lw, norm_final)
```
